```python
import math
import jax, jax.numpy as jnp
from jax import lax
import numpy as np


D_MODEL = 1024
BATCH = 32
SEQ = 256
DEPTH = 2
DEC_BATCH = 2
DEC_SEQ = 1024
PAST_LEN = 512

GRID_W = 64
SC_W = 256
HY_W = 256
ML_W = 512
ML_H = 4
ML_DH = ML_W // ML_H
D_FF = 4 * D_MODEL
HY_ORDER = 2
HY_BANDS = 16
HY_EMB = 1 + 2 * HY_BANDS
HY_HID = 64
HY_DECAY_SHORT = 0.3
HY_DECAY_LONG = 1.5
HY_DECAY_TARGET = 1e-2
CHUNK = 64
EPS = 1e-6
N_IN = 3 * SC_W + 3 * HY_W + 4 * ML_W + 4 * ML_H

kernel_name = "hybrid_conv_hyena_mlstm_prefix_dit_step"

F32 = jnp.float32


def rmsnorm(x, g):
    xf = x.astype(F32)
    y = xf * lax.rsqrt(jnp.mean(xf * xf, axis=-1, keepdims=True) + EPS)
    return (y * g.astype(F32)).astype(x.dtype)


def short_conv(x, w):
    xp = jnp.pad(x, ((0, 0), (1, 1), (0, 0)))
    return xp[:, :-2] * w[0] + xp[:, 1:-1] * w[1] + xp[:, 2:] * w[2]


def grid_pos_embed(L):
    rows = L // GRID_W
    r, cidx = jnp.meshgrid(jnp.arange(rows, dtype=F32), jnp.arange(GRID_W, dtype=F32), indexing='ij')
    r = r.reshape(-1)
    cidx = cidx.reshape(-1)
    quarter = D_MODEL // 4
    omega = 1.0 / (10000.0 ** (jnp.arange(quarter, dtype=F32) / quarter))
    ar = r[:, None] * omega[None, :]
    ac = cidx[:, None] * omega[None, :]
    return jnp.concatenate([jnp.sin(ar), jnp.cos(ar), jnp.sin(ac), jnp.cos(ac)], axis=-1)


def hyena_filters(L, w1, b1, w2, b2, w3, freq):
    t_idx = jnp.arange(L, dtype=F32)
    t = t_idx / (L - 1)
    bands = jnp.arange(1, HY_BANDS + 1, dtype=F32)
    ang = 2.0 * math.pi * t_idx[:, None] * bands[None, :] / L
    z = jnp.concatenate([t[:, None], jnp.cos(ang), -jnp.sin(ang)], axis=-1)
    f = freq.astype(F32)
    hdn = jnp.sin(f * (z @ w1.astype(F32) + b1.astype(F32)))
    hdn = jnp.sin(f * (hdn @ w2.astype(F32) + b2.astype(F32)))
    h = (hdn @ w3.astype(F32)).reshape(L, HY_ORDER, 2, HY_W)
    deltas = jnp.abs(jnp.linspace(math.log(HY_DECAY_TARGET) / HY_DECAY_LONG,
                                  math.log(HY_DECAY_TARGET) / HY_DECAY_SHORT, HY_W, dtype=F32))
    h = h * jnp.exp(-t[:, None] * deltas[None, :])[:, None, None, :]
    fwd = h[:, :, 0]
    bwd = h[1:, :, 1][::-1]
    full = jnp.concatenate([fwd, jnp.zeros((1, HY_ORDER, HY_W), F32), bwd], axis=0)
    return full / jnp.sum(jnp.abs(full), axis=0, keepdims=True)


def hyena_mix(u3, conv_w, w1, b1, w2, b2, w3, freq, skip):
    B, L, _ = u3.shape
    u3 = short_conv(u3, conv_w).astype(F32)
    v, x1, x2 = jnp.split(u3, 3, axis=-1)
    Hf = jnp.fft.rfft(hyena_filters(L, w1, b1, w2, b2, w3, freq), axis=0)
    z = v
    for o, gate in enumerate((x1, x2)):
        Z = jnp.fft.rfft(z, n=2 * L, axis=1)
        zc = jnp.fft.irfft(Z * Hf[None, :, o], n=2 * L, axis=1)[:, :L]
        z = gate * (zc + skip[o].astype(F32) * z)
    return z


def mlstm_chunkwise(q, k, v, i_pre, logf, C0, n0, m0):
    B, H, L, DH = q.shape
    nc = L // CHUNK

    def to_chunks(a):
        return jnp.moveaxis(a.reshape(B, H, nc, CHUNK, *a.shape[3:]), 2, 0)

    causal = jnp.tril(jnp.ones((CHUNK, CHUNK), bool))

    def step(carry, xs):
        C, n, m = carry
        qq, kk, vv, ii, ff = xs
        b = jnp.cumsum(ff, axis=-1)
        log_d = jnp.where(causal, b[..., :, None] - b[..., None, :] + ii[..., None, :], -jnp.inf)
        log_prev = b + m[..., None]
        m_t = jnp.maximum(log_prev, jnp.max(log_d, axis=-1))
        d = jnp.exp(log_d - m_t[..., None])
        w_prev = jnp.exp(log_prev - m_t)
        s = jnp.einsum('bhtd,bhsd->bhts', qq, kk) * d
        num = jnp.einsum('bhts,bhsd->bhtd', s, vv) + w_prev[..., None] * jnp.einsum('bhtk,bhkv->bhtv', qq, C)
        den = jnp.sum(s, axis=-1) + w_prev * jnp.einsum('bhtk,bhk->bht', qq, n)
        h = num / jnp.maximum(jnp.abs(den), jnp.exp(-m_t))[..., None]
        bL = b[..., -1]
        log_s = bL[..., None] - b + ii
        m_new = jnp.maximum(bL + m, jnp.max(log_s, axis=-1))
        w_s = jnp.exp(log_s - m_new[..., None])
        w_c = jnp.exp(bL + m - m_new)
        C_new = w_c[..., None, None] * C + jnp.einsum('bhs,bhsk,bhsv->bhkv', w_s, kk, vv)
        n_new = w_c[..., None] * n + jnp.einsum('bhs,bhsk->bhk', w_s, kk)
        return (C_new, n_new, m_new), h

    carry0 = (C0.astype(F32), n0.astype(F32), m0.astype(F32))
    xs = tuple(to_chunks(a) for a in (q, k, v, i_pre, logf))
    (C, n, m), hs = lax.scan(step, carry0, xs)
    h = jnp.moveaxis(hs, 0, 2).reshape(B, H, L, DH)
    return h, C, n, m


def mlstm_mix(qkvo, gates, conv_w, gate_b, norm_g, C0, n0, m0):
    B, L, _ = qkvo.shape
    qk = jax.nn.silu(short_conv(qkvo[..., :2 * ML_W], conv_w))
    q, k = jnp.split(qk, 2, axis=-1)
    v = qkvo[..., 2 * ML_W:3 * ML_W]
    o = qkvo[..., 3 * ML_W:]

    def heads(a):
        return a.astype(F32).reshape(B, L, ML_H, ML_DH).transpose(0, 2, 1, 3)

    q, k, v = heads(q), heads(k) * (ML_DH ** -0.5), heads(v)
    g = (gates.astype(F32) + gate_b.astype(F32)).reshape(B, L, 2, 2, ML_H).transpose(2, 3, 0, 4, 1)
    h_f, Cf, nf, mf = mlstm_chunkwise(q, k, v, g[0, 0], jax.nn.log_sigmoid(g[0, 1]), C0[:, 0], n0[:, 0], m0[:, 0])
    flip = lambda a: jnp.flip(a, axis=2)
    h_b, Cb, nb, mb = mlstm_chunkwise(flip(q), flip(k), flip(v), flip(g[1, 0]), flip(jax.nn.log_sigmoid(g[1, 1])),
                                      C0[:, 1], n0[:, 1], m0[:, 1])
    h = h_f + flip(h_b)
    h = h * lax.rsqrt(jnp.mean(h * h, axis=-1, keepdims=True) + EPS)
    h = h.transpose(0, 2, 1, 3).reshape(B, L, ML_W) * norm_g.astype(F32)
    h = jax.nn.sigmoid(o.astype(F32)) * h
    return h, jnp.stack([Cf, Cb], axis=1), jnp.stack([nf, nb], axis=1), jnp.stack([mf, mb], axis=1)


def layer(x, mod, norm1_g, w_in, sc_conv_w, hy_conv_w, hy_w1, hy_b1, hy_w2, hy_b2, hy_w3, hy_freq, hy_skip,
          ml_conv_w, ml_gate_b, ml_norm_g, w_out, norm2_g, mlp_w1, mlp_w2, C0, n0, m0):
    sh1, sc1, g1, sh2, sc2, g2 = jnp.split(mod[:, None, :], 6, axis=-1)
    u = rmsnorm(x, norm1_g) * (1 + sc1) + sh1
    proj = u @ w_in
    o1 = 3 * SC_W
    o2 = o1 + 3 * HY_W
    o3 = o2 + 4 * ML_W
    b_g, c_g, xin = jnp.split(proj[..., :o1], 3, axis=-1)
    y_sc = (b_g * short_conv(c_g * xin, sc_conv_w)).astype(F32)
    y_hy = hyena_mix(proj[..., o1:o2], hy_conv_w, hy_w1, hy_b1, hy_w2, hy_b2, hy_w3, hy_freq, hy_skip)
    y_ml, C, n, m = mlstm_mix(proj[..., o2:o3], proj[..., o3:], ml_conv_w, ml_gate_b, ml_norm_g, C0, n0, m0)
    y = jnp.concatenate([y_sc, y_hy, y_ml], axis=-1).astype(x.dtype)
    x = x + g1 * (y @ w_out)
    u = rmsnorm(x, norm2_g) * (1 + sc2) + sh2
    x = x + g2 * (jnp.square(jax.nn.relu(u @ mlp_w1)) @ mlp_w2)
    return x, C, n, m


def setup_inputs(seed: int = 0) -> dict:
    key = jax.random.key(seed)
    ks = jax.random.split(key, 32)
    nrm = lambda k, shape, s: jax.random.normal(k, shape, F32) * s
    f_bias = jnp.linspace(3.0, 6.0, ML_H, dtype=F32)[None, None, None, :] + nrm(ks[24], (DEPTH, 2, 1, ML_H), 0.1)
    i_bias = nrm(ks[25], (DEPTH, 2, 1, ML_H), 0.1)
    ml_gate_b = jnp.concatenate([i_bias, f_bias], axis=2).reshape(DEPTH, 4 * ML_H)
    return {
        "x_prompt": nrm(ks[0], (BATCH, SEQ, D_MODEL), 1.0),
        "x_sample": nrm(ks[1], (DEC_BATCH, DEC_SEQ, D_MODEL), 1.0),
        "state_C": nrm(ks[2], (DEC_BATCH, DEPTH, 2, ML_H, ML_DH, ML_DH), 0.3),
        "state_n": nrm(ks[3], (DEC_BATCH, DEPTH, 2, ML_H, ML_DH), 0.3),
        "state_m": nrm(ks[4], (DEC_BATCH, DEPTH, 2, ML_H), 0.5),
        "c": nrm(ks[5], (DEC_BATCH, D_MODEL), 1.0),
        "c_ctx": nrm(ks[6], (D_MODEL,), 1.0),
        "norm1_g": 1.0 + nrm(ks[7], (DEPTH, D_MODEL), 0.02),
        "ada_w": nrm(ks[8], (DEPTH, D_MODEL, 6 * D_MODEL), 0.5 * D_MODEL ** -0.5),
        "ada_b": nrm(ks[9], (DEPTH, 6 * D_MODEL), 0.02),
        "w_in": nrm(ks[10], (DEPTH, D_MODEL, N_IN), D_MODEL ** -0.5),
        "sc_conv_w": nrm(ks[11], (DEPTH, 3, SC_W), 0.5),
        "hy_conv_w": nrm(ks[12], (DEPTH, 3, 3 * HY_W), 0.5),
        "hy_w1": nrm(ks[13], (DEPTH, HY_EMB, HY_HID), HY_EMB ** -0.5),
        "hy_b1": nrm(ks[14], (DEPTH, HY_HID), 0.1),
        "hy_w2": nrm(ks[15], (DEPTH, HY_HID, HY_HID), HY_HID ** -0.5),
        "hy_b2": nrm(ks[16], (DEPTH, HY_HID), 0.1),
        "hy_w3": nrm(ks[17], (DEPTH, HY_HID, HY_ORDER * 2 * HY_W), HY_HID ** -0.5),
        "hy_freq": 1.0 + nrm(ks[18], (DEPTH, HY_HID), 0.1),
        "hy_skip": nrm(ks[19], (DEPTH, HY_ORDER, HY_W), 0.5),
        "ml_conv_w": nrm(ks[20], (DEPTH, 3, 2 * ML_W), 0.5),
        "ml_gate_b": ml_gate_b,
        "ml_norm_g": 1.0 + nrm(ks[21], (DEPTH, ML_W), 0.02),
        "w_out": nrm(ks[22], (DEPTH, D_MODEL, D_MODEL), D_MODEL ** -0.5),
        "norm2_g": 1.0 + nrm(ks[23], (DEPTH, D_MODEL), 0.02),
        "mlp_w1": nrm(ks[26], (DEPTH, D_MODEL, D_FF), D_MODEL ** -0.5),
        "mlp_w2": nrm(ks[27], (DEPTH, D_FF, D_MODEL), D_FF ** -0.5),
        "norm_f_g": 1.0 + nrm(ks[28], (D_MODEL,), 0.02),
    }


def reference(x_prompt, x_sample, state_C, state_n, state_m, c, c_ctx, norm1_g, ada_w, ada_b, w_in,
              sc_conv_w, hy_conv_w, hy_w1, hy_b1, hy_w2, hy_b2, hy_w3, hy_freq, hy_skip, ml_conv_w,
              ml_gate_b, ml_norm_g, w_out, norm2_g, mlp_w1, mlp_w2, norm_f_g):
    def run(x, mod, l, C0, n0, m0):
        return layer(x, mod, norm1_g[l], w_in[l], sc_conv_w[l], hy_conv_w[l], hy_w1[l], hy_b1[l], hy_w2[l],
                     hy_b2[l], hy_w3[l], hy_freq[l], hy_skip[l], ml_conv_w[l], ml_gate_b[l], ml_norm_g[l],
                     w_out[l], norm2_g[l], mlp_w1[l], mlp_w2[l], C0, n0, m0)

    B = x_prompt.shape[0]
    zC = jnp.zeros((B, 2, ML_H, ML_DH, ML_DH), F32)
    zn = jnp.zeros((B, 2, ML_H, ML_DH), F32)
    zm = jnp.zeros((B, 2, ML_H), F32)
    sc_ctx = jax.nn.silu(c_ctx)[None, :]
    xp = x_prompt
    Cs, ns, ms = [], [], []
    for l in range(DEPTH):
        mod = sc_ctx @ ada_w[l] + ada_b[l]
        xp, C, n, m = run(xp, mod, l, zC, zn, zm)
        Cs.append(C)
        ns.append(n)
        ms.append(m)
    y_prompt = rmsnorm(xp, norm_f_g)
    new_state_C = jnp.stack(Cs, axis=1)
    new_state_n = jnp.stack(ns, axis=1)
    new_state_m = jnp.stack(ms, axis=1)

    L = x_sample.shape[1]
    xs = x_sample + grid_pos_embed(L).astype(x_sample.dtype)[None]
    sc_lat = jax.nn.silu(c)
    for l in range(DEPTH):
        mod = sc_lat @ ada_w[l] + ada_b[l]
        xs, _, _, _ = run(xs, mod, l, state_C[:, l], state_n[:, l], state_m[:, l])
    y_sample = rmsnorm(xs, norm_f_g)
    return (y_prompt, y_sample, new_state_C, new_state_n, new_state_m)
```

```python
import functools
import math

import numpy as np
import jax
import jax.numpy as jnp
from jax import lax
from jax.experimental import pallas as pl
from jax.experimental.pallas import tpu as pltpu

F32 = jnp.float32
BF16 = jnp.bfloat16

D_MODEL = 1024
DEPTH = 2
GRID_W = 64
SC_W = 256
HY_W = 256
ML_W = 512
ML_H = 4
ML_DH = ML_W // ML_H
D_FF = 4 * D_MODEL
HY_ORDER = 2
HY_BANDS = 16
HY_EMB = 1 + 2 * HY_BANDS
HY_HID = 64
HY_DECAY_SHORT = 0.3
HY_DECAY_LONG = 1.5
HY_DECAY_TARGET = 1e-2
EPS = 1e-6
N_MAIN = 3 * SC_W + 3 * HY_W + 4 * ML_W
N_GATE = 4 * ML_H
LANES = 128
ROW_BLOCK = 256
ATT_BLOCK = 256
NEG_BIG = -1e30
VMEM_LIMIT = 60 * 1024 * 1024


def _dot(a, b):
    return jnp.dot(a, b, preferred_element_type=F32)


def _dot_nt(a, b):
    return lax.dot_general(a, b, (((1,), (1,)), ((), ())), preferred_element_type=F32)


def _split2(a):
    hi = a.astype(BF16)
    lo = (a - hi.astype(F32)).astype(BF16)
    return hi, lo


def _split3(a):
    hi = a.astype(BF16)
    r = a - hi.astype(F32)
    mid = r.astype(BF16)
    lo = (r - mid.astype(F32)).astype(BF16)
    return hi, mid, lo


def _dot3(a, b):
    ah, al = _split2(a)
    bh, bl = _split2(b)
    return _dot(ah, bh) + _dot(al, bh) + _dot(ah, bl)


def _dot3_tab(th, tl, b):
    bh, bl = _split2(b)
    return _dot(th, bh) + _dot(tl, bh) + _dot(th, bl)


def _sigmoid(x):
    return 1.0 / (1.0 + jnp.exp(-x))


def _silu(x):
    return x * _sigmoid(x)


def _log_sigmoid(x):
    return jnp.minimum(x, 0.0) - jnp.log(1.0 + jnp.exp(-jnp.abs(x)))


def _conv3(x, w):
    L = x.shape[0]
    row = lax.broadcasted_iota(jnp.int32, x.shape, 0)
    xm = jnp.where(row == 0, 0.0, pltpu.roll(x, 1, axis=0))
    xp = jnp.where(row == L - 1, 0.0, pltpu.roll(x, L - 1, axis=0))
    return xm * w[0:1, :] + x * w[1:2, :] + xp * w[2:3, :]


def _rms(x):
    return x * lax.rsqrt(jnp.mean(x * x, axis=-1, keepdims=True) + EPS)


def _hi_lo(a64):
    a = a64.astype(np.float32)
    hi = a.astype(BF16)
    lo = (a - hi.astype(np.float32)).astype(BF16)
    return jnp.asarray(hi), jnp.asarray(lo)


def _const_spec(shape):
    return pl.BlockSpec(shape, lambda *_: tuple(0 for _ in shape), pipeline_mode=pl.Buffered(1))


@functools.lru_cache(maxsize=None)
def _dft_tables_np(L):
    k = np.arange(L, dtype=np.int64)
    m1 = ((2 * k[:, None] + 1) * k[None, :]) % (4 * L)
    a1 = np.pi * m1.astype(np.float64) / (2 * L)
    m2 = ((2 * k[:, None] + 1) * (2 * k[None, :] + 1)) % (8 * L)
    a2 = np.pi * m2.astype(np.float64) / (4 * L)
    return np.cos(a1), np.sin(a1), np.cos(a2), np.sin(a2)


@functools.lru_cache(maxsize=None)
def _filter_consts_np(L):
    t_idx = np.arange(L, dtype=np.float64)
    t = t_idx / (L - 1)
    bands = np.arange(1, HY_BANDS + 1, dtype=np.float64)
    ang = 2.0 * math.pi * t_idx[:, None] * bands[None, :] / L
    z = np.zeros((L, LANES), np.float64)
    z[:, 0] = t
    z[:, 1:1 + HY_BANDS] = np.cos(ang)
    z[:, 1 + HY_BANDS:1 + 2 * HY_BANDS] = -np.sin(ang)
    lin = np.linspace(math.log(HY_DECAY_TARGET) / HY_DECAY_LONG,
                      math.log(HY_DECAY_TARGET) / HY_DECAY_SHORT, HY_W).astype(np.float32)
    deltas = np.abs(lin).astype(np.float64)
    decay = np.exp(-t[:, None] * deltas[None, :])
    return z.astype(np.float32), decay.astype(np.float32)


@functools.lru_cache(maxsize=None)
def _pos_embed_np(L):
    rows = L // GRID_W
    r, cidx = np.meshgrid(np.arange(rows, dtype=np.float64), np.arange(GRID_W, dtype=np.float64), indexing="ij")
    r = r.reshape(-1)
    cidx = cidx.reshape(-1)
    quarter = D_MODEL // 4
    omega = 1.0 / (10000.0 ** (np.arange(quarter, dtype=np.float64) / quarter))
    ar = r[:, None] * omega[None, :]
    ac = cidx[:, None] * omega[None, :]
    return np.concatenate([np.sin(ar), np.cos(ar), np.sin(ac), np.cos(ac)], axis=-1).astype(np.float32)


def _ada_kernel(c_ref, w_ref, b_ref, o_ref):
    sc = _silu(c_ref[...])
    o_ref[0] = _dot3(sc, w_ref[0]) + b_ref[0]


def _ada(cond, ada_w, ada_b):
    tn = 1536
    nd = 6 * D_MODEL
    return pl.pallas_call(
        _ada_kernel,
        out_shape=jax.ShapeDtypeStruct((DEPTH, 8, nd), F32),
        grid=(DEPTH, nd // tn),
        in_specs=[
            pl.BlockSpec((8, D_MODEL), lambda l, j: (0, 0)),
            pl.BlockSpec((1, D_MODEL, tn), lambda l, j: (l, 0, j)),
            pl.BlockSpec((1, 1, tn), lambda l, j: (l, 0, j)),
        ],
        out_specs=pl.BlockSpec((1, 8, tn), lambda l, j: (l, 0, j)),
        compiler_params=pltpu.CompilerParams(
            dimension_semantics=("arbitrary", "arbitrary"), vmem_limit_bytes=VMEM_LIMIT),
        name="ada_mod",
    )(cond, ada_w, ada_b.reshape(DEPTH, 1, nd))


def _inproj_kernel(*refs, with_pos):
    if with_pos:
        x_ref, pos_ref, mod_ref, g_ref, w_ref, wg_ref, proj_ref, gate_ref, xo_ref = refs
        x = x_ref[...] + pos_ref[...]
        xo_ref[...] = x
    else:
        x_ref, mod_ref, g_ref, w_ref, wg_ref, proj_ref, gate_ref = refs
        x = x_ref[...]
    mod = mod_ref[0]
    sh1 = mod[:, 0:D_MODEL]
    sc1 = mod[:, D_MODEL:2 * D_MODEL]
    u = (_rms(x) * g_ref[...]) * (1.0 + sc1) + sh1
    ub = u.astype(BF16)
    tn = 512
    for j in range(N_MAIN // tn):
        proj_ref[:, j * tn:(j + 1) * tn] = _dot(ub, w_ref[:, j * tn:(j + 1) * tn])
    gate_ref[...] = _dot(ub, wg_ref[...])


def _inproj(x2d, pos, mod, g, w_main, w_gate, *, L):
    T = x2d.shape[0]
    tm = ROW_BLOCK
    nblk = T // tm
    per_seq = L // tm
    with_pos = pos is not None
    row_spec = pl.BlockSpec((tm, D_MODEL), lambda i: (i, 0))
    in_specs = [row_spec]
    args = [x2d]
    if with_pos:
        in_specs.append(pl.BlockSpec((tm, D_MODEL), lambda i: (i % per_seq, 0)))
        args.append(pos)
    nmod = mod.shape[0]
    in_specs += [
        pl.BlockSpec((1, 1, 6 * D_MODEL), lambda i: ((i // per_seq) % nmod, 0, 0)),
        _const_spec((1, D_MODEL)),
        _const_spec((D_MODEL, N_MAIN)),
        _const_spec((D_MODEL, LANES)),
    ]
    args += [mod, g, w_main, w_gate]
    out_shape = [jax.ShapeDtypeStruct((T, N_MAIN), F32), jax.ShapeDtypeStruct((T, LANES), F32)]
    out_specs = [pl.BlockSpec((tm, N_MAIN), lambda i: (i, 0)), pl.BlockSpec((tm, LANES), lambda i: (i, 0))]
    if with_pos:
        out_shape.append(jax.ShapeDtypeStruct((T, D_MODEL), F32))
        out_specs.append(row_spec)
    return pl.pallas_call(
        functools.partial(_inproj_kernel, with_pos=with_pos),
        out_shape=out_shape,
        grid=(nblk,),
        in_specs=in_specs,
        out_specs=out_specs,
        compiler_params=pltpu.CompilerParams(
            dimension_semantics=("arbitrary",), vmem_limit_bytes=VMEM_LIMIT),
        name="inproj",
    )(*args)


def _filter_kernel(z_ref, dec_ref, w1_ref, b1_ref, w2_ref, b2_ref, w3_ref, f_ref,
                   ch_ref, cl_ref, sh_ref, sl_ref, hre_ref, him_ref, *, L):
    f = f_ref[...]
    hdn = jnp.sin(f * (_dot3(z_ref[...], w1_ref[...]) + b1_ref[...]))
    hdn = jnp.sin(f * (_dot3(hdn, w2_ref[...]) + b2_ref[...]))
    h = _dot3(hdn, w3_ref[...])
    dec = dec_ref[...]
    row = lax.broadcasted_iota(jnp.int32, (L, HY_W), 0)
    hp, hm = [], []
    for o in range(HY_ORDER):
        hf = h[:, (2 * o) * HY_W:(2 * o + 1) * HY_W] * dec
        hb = jnp.where(row == 0, 0.0, h[:, (2 * o + 1) * HY_W:(2 * o + 2) * HY_W] * dec)
        nrm = jnp.sum(jnp.abs(hf), axis=0, keepdims=True) + jnp.sum(jnp.abs(hb), axis=0, keepdims=True)
        inv = 1.0 / nrm
        hp.append((hf + hb) * inv)
        hm.append((hf - hb) * inv)
    hp = jnp.concatenate(hp, axis=1)
    hm = jnp.concatenate(hm, axis=1)
    hre_ref[...] = _dot3_tab(ch_ref[...], cl_ref[...], hp)
    him_ref[...] = -_dot3_tab(sh_ref[...], sl_ref[...], hm)


def _filter_spectrum(L, w1p, b1p, w2p, b2p, w3p, fp):
    z_np, dec_np = _filter_consts_np(L)
    c1, s1, _, _ = _dft_tables_np(L)
    ch, cl = _hi_lo(c1)
    sh, sl = _hi_lo(s1)
    n = HY_ORDER * HY_W
    full = lambda shape: pl.BlockSpec(shape, lambda i: tuple(0 for _ in shape))
    return pl.pallas_call(
        functools.partial(_filter_kernel, L=L),
        out_shape=[jax.ShapeDtypeStruct((L, n), F32), jax.ShapeDtypeStruct((L, n), F32)],
        grid=(1,),
        in_specs=[full((L, LANES)), full((L, HY_W)), full((LANES, LANES)), full((1, LANES)),
                  full((LANES, LANES)), full((1, LANES)), full((LANES, 2 * n)), full((1, LANES)),
                  full((L, L)), full((L, L)), full((L, L)), full((L, L))],
        out_specs=[full((L, n)), full((L, n))],
        compiler_params=pltpu.CompilerParams(
            dimension_semantics=("arbitrary",), vmem_limit_bytes=VMEM_LIMIT),
        name="hyena_filter",
    )(jnp.asarray(z_np), jnp.asarray(dec_np), w1p, b1p, w2p, b2p, w3p, fp, ch, cl, sh, sl)


def _schy_kernel(sc_ref, hy_ref, scw_ref, hyw_ref, skip_ref, ch_ref, cl_ref, sh_ref, sl_ref,
                 hre_ref, him_ref, out_ref, *, L):
    sc = sc_ref[...]
    b_g = sc[:, 0:SC_W]
    c_g = sc[:, SC_W:2 * SC_W]
    xin = sc[:, 2 * SC_W:3 * SC_W]
    out_ref[:, 0:SC_W] = b_g * _conv3(c_g * xin, scw_ref[...])

    u3 = _conv3(hy_ref[...], hyw_ref[...])
    z = u3[:, 0:HY_W]
    gates = (u3[:, HY_W:2 * HY_W], u3[:, 2 * HY_W:3 * HY_W])
    ch, cl, sh, sl = ch_ref[...], cl_ref[...], sh_ref[...], sl_ref[...]
    inv_l = 1.0 / L
    for o in range(HY_ORDER):
        zh, zl = _split2(z)
        zc = _dot(ch, zh) + _dot(cl, zh) + _dot(ch, zl)
        zs = _dot(sh, zh) + _dot(sl, zh) + _dot(sh, zl)
        hre = hre_ref[:, o * HY_W:(o + 1) * HY_W]
        him = him_ref[:, o * HY_W:(o + 1) * HY_W]
        yre = zc * hre + zs * him
        yim = zc * him - zs * hre
        y = (_dot3_tab(ch, cl, yre) - _dot3_tab(sh, sl, yim)) * inv_l
        z = gates[o] * (y + skip_ref[o:o + 1, :] * z)
    out_ref[:, SC_W:SC_W + HY_W] = z


def _schy(proj, sc_w, hy_w, skip, hre, him, *, B, L):
    _, _, c2, s2 = _dft_tables_np(L)
    ch, cl = _hi_lo(c2)
    sh, sl = _hi_lo(s2)
    n = HY_ORDER * HY_W
    const = _const_spec
    return pl.pallas_call(
        functools.partial(_schy_kernel, L=L),
        out_shape=jax.ShapeDtypeStruct((B * L, SC_W + HY_W), F32),
        grid=(B,),
        in_specs=[
            pl.BlockSpec((L, 3 * SC_W), lambda b: (b, 0)),
            pl.BlockSpec((L, 3 * HY_W), lambda b: (b, 1)),
            const((3, SC_W)), const((3, 3 * HY_W)), const((HY_ORDER, HY_W)),
            const((L, L)), const((L, L)), const((L, L)), const((L, L)),
            const((L, n)), const((L, n)),
        ],
        out_specs=pl.BlockSpec((L, SC_W + HY_W), lambda b: (b, 0)),
        compiler_params=pltpu.CompilerParams(
            dimension_semantics=("arbitrary",), vmem_limit_bytes=VMEM_LIMIT),
        name="sconv_hyena",
    )(proj, proj, sc_w, hy_w, skip, ch, cl, sh, sl, hre, him)


def _mlstm_kernel(*refs, L, with_state):
    if with_state:
        (q_ref, k_ref, v_ref, o_ref, g_ref, gb_ref, cw_ref, ng_ref,
         c0_ref, n0_ref, m0_ref, y_ref) = refs
    else:
        (q_ref, k_ref, v_ref, o_ref, g_ref, gb_ref, cw_ref, ng_ref,
         y_ref, cout_ref, nout_ref, mout_ref) = refs

    cw = cw_ref[...]
    q_all = _silu(_conv3(q_ref[...], cw[:, 0:ML_W]))
    k_all = _silu(_conv3(k_ref[...], cw[:, ML_W:2 * ML_W])) * (ML_DH ** -0.5)

    g = g_ref[...] + gb_ref[...]
    lane = lax.broadcasted_iota(jnp.int32, g.shape, 1)
    is_f = ((lane >= ML_H) & (lane < 2 * ML_H)) | ((lane >= 3 * ML_H) & (lane < 4 * ML_H))
    pk = jnp.where(is_f, _log_sigmoid(g), g)
    r_i = lax.broadcasted_iota(jnp.int32, (L, L), 0)
    c_i = lax.broadcasted_iota(jnp.int32, (L, L), 1)
    tri = jnp.where(c_i <= r_i, 1.0, 0.0).astype(BF16)
    p_hi, p_mid, p_lo = _split3(pk)
    cum = _dot(tri, p_hi) + _dot(tri, p_mid) + _dot(tri, p_lo)
    tot = cum[L - 1:L, :]
    suf = tot - cum + pk
    pk_t = pk.T
    cum_t = cum.T
    suf_t = suf.T

    tb = min(L, ATT_BLOCK)
    for h in range(ML_H):
        hs = slice(h * ML_DH, (h + 1) * ML_DH)
        q = q_all[:, hs]
        k = k_all[:, hs]
        v = v_ref[:, hs]
        qb = q.astype(BF16)
        kb = k.astype(BF16)
        vb = v.astype(BF16)
        lf, li_f = ML_H + h, h
        lb, li_b = 3 * ML_H + h, 2 * ML_H + h
        if with_state:
            m0f = m0_ref[0, h:h + 1, 0:1]
            m0b = m0_ref[0, ML_H + h:ML_H + h + 1, 0:1]
            c0f = c0_ref[0, 0, h].astype(BF16)
            c0b = c0_ref[0, 1, h].astype(BF16)
            n0f = n0_ref[0, h:h + 1, :]
            n0b = n0_ref[0, ML_H + h:ML_H + h + 1, :]
        else:
            m0f = m0b = 0.0
        for r0 in range(0, L, tb):
            rs = slice(r0, r0 + tb)
            s = _dot_nt(qb[rs], kb)
            t_idx = lax.broadcasted_iota(jnp.int32, (tb, L), 0) + r0
            s_idx = lax.broadcasted_iota(jnp.int32, (tb, L), 1)

            def direction(col, row_t, irow_t, mask, m0):
                bcol = col[rs]
                log_d = jnp.where(mask, bcol - row_t + irow_t, NEG_BIG)
                log_prev = bcol + m0
                m_t = jnp.maximum(log_prev, jnp.max(log_d, axis=-1, keepdims=True))
                p = s * jnp.exp(log_d - m_t)
                return p, m_t, jnp.exp(log_prev - m_t)

            p_f, m_f, wp_f = direction(cum[:, lf:lf + 1], cum_t[lf:lf + 1, :], pk_t[li_f:li_f + 1, :],
                                       s_idx <= t_idx, m0f)
            p_b, m_b, wp_b = direction(suf[:, lb:lb + 1], suf_t[lb:lb + 1, :], pk_t[li_b:li_b + 1, :],
                                       s_idx >= t_idx, m0b)
            den_f = jnp.sum(p_f, axis=-1, keepdims=True)
            den_b = jnp.sum(p_b, axis=-1, keepdims=True)
            if with_state:
                den_f = den_f + wp_f * jnp.sum(q[rs] * n0f, axis=-1, keepdims=True)
                den_b = den_b + wp_b * jnp.sum(q[rs] * n0b, axis=-1, keepdims=True)
            r_f = 1.0 / jnp.maximum(jnp.abs(den_f), jnp.exp(-m_f))
            r_b = 1.0 / jnp.maximum(jnp.abs(den_b), jnp.exp(-m_b))
            a = (p_f * r_f + p_b * r_b).astype(BF16)
            hout = _dot(a, vb)
            if with_state:
                hout = hout + (wp_f * r_f) * _dot(qb[rs], c0f) + (wp_b * r_b) * _dot(qb[rs], c0b)
            hn = hout * lax.rsqrt(jnp.mean(hout * hout, axis=-1, keepdims=True) + EPS)
            y_ref[rs, hs] = _sigmoid(o_ref[rs, hs]) * (hn * ng_ref[:, hs])

        if not with_state:
            for d, (col, icol_lane, b_last) in enumerate((
                    (cum[:, lf:lf + 1], li_f, cum[L - 1:L, lf:lf + 1]),
                    (suf[:, lb:lb + 1], li_b, suf[0:1, lb:lb + 1]))):
                log_s = b_last - col + pk[:, icol_lane:icol_lane + 1]
                m_new = jnp.maximum(b_last, jnp.max(log_s, axis=0, keepdims=True))
                w_s = jnp.exp(log_s - m_new)
                kw = k * w_s
                cout_ref[0, d, h] = _dot(kw.T.astype(BF16), vb)
                nout_ref[0, d * ML_H + h:d * ML_H + h + 1, :] = jnp.sum(kw, axis=0, keepdims=True)
                mout_ref[0, d * ML_H + h:d * ML_H + h + 1, :] = jnp.broadcast_to(m_new, (1, LANES))


def _mlstm(proj, gates, gate_b, conv_w, norm_g, state, *, B, L):
    with_state = state is not None
    col0 = (3 * SC_W + 3 * HY_W) // ML_W
    const = _const_spec
    in_specs = [
        pl.BlockSpec((L, ML_W), lambda b: (b, col0)),
        pl.BlockSpec((L, ML_W), lambda b: (b, col0 + 1)),
        pl.BlockSpec((L, ML_W), lambda b: (b, col0 + 2)),
        pl.BlockSpec((L, ML_W), lambda b: (b, col0 + 3)),
        pl.BlockSpec((L, LANES), lambda b: (b, 0)),
        const((1, LANES)), const((3, 2 * ML_W)), const((1, ML_W)),
    ]
    args = [proj, proj, proj, proj, gates, gate_b, conv_w, norm_g]
    y_shape = jax.ShapeDtypeStruct((B * L, ML_W), F32)
    y_spec = pl.BlockSpec((L, ML_W), lambda b: (b, 0))
    if with_state:
        c0, n0, m0 = state
        in_specs += [
            pl.BlockSpec((1, 2, ML_H, ML_DH, ML_DH), lambda b: (b, 0, 0, 0, 0)),
            pl.BlockSpec((1, 2 * ML_H, ML_DH), lambda b: (b, 0, 0)),
            pl.BlockSpec((1, 2 * ML_H, LANES), lambda b: (b, 0, 0)),
        ]
        args += [c0, n0, m0]
        out_shape, out_specs = y_shape, y_spec
    else:
        out_shape = [y_shape,
                     jax.ShapeDtypeStruct((B, 2, ML_H, ML_DH, ML_DH), F32),
                     jax.ShapeDtypeStruct((B, 2 * ML_H, ML_DH), F32),
                     jax.ShapeDtypeStruct((B, 2 * ML_H, LANES), F32)]
        out_specs = [y_spec,
                     pl.BlockSpec((1, 2, ML_H, ML_DH, ML_DH), lambda b: (b, 0, 0, 0, 0)),
                     pl.BlockSpec((1, 2 * ML_H, ML_DH), lambda b: (b, 0, 0)),
                     pl.BlockSpec((1, 2 * ML_H, LANES), lambda b: (b, 0, 0))]
    return pl.pallas_call(
        functools.partial(_mlstm_kernel, L=L, with_state=with_state),
        out_shape=out_shape,
        grid=(B,),
        in_specs=in_specs,
        out_specs=out_specs,
        compiler_params=pltpu.CompilerParams(
            dimension_semantics=("arbitrary",), vmem_limit_bytes=VMEM_LIMIT),
        name="mlstm",
    )(*args)


def _outmlp_kernel(x_ref, ya_ref, yb_ref, mod_ref, wo_ref, g2_ref, w1_ref, w2_ref, gf_ref, o_ref, *, final):
    mod = mod_ref[0]
    g1 = mod[:, 2 * D_MODEL:3 * D_MODEL]
    sh2 = mod[:, 3 * D_MODEL:4 * D_MODEL]
    sc2 = mod[:, 4 * D_MODEL:5 * D_MODEL]
    g2 = mod[:, 5 * D_MODEL:6 * D_MODEL]
    na = SC_W + HY_W
    attn = _dot(ya_ref[...].astype(BF16), wo_ref[0:na, :]) + _dot(yb_ref[...].astype(BF16), wo_ref[na:, :])
    x = x_ref[...] + g1 * attn
    u = ((_rms(x) * g2_ref[...]) * (1.0 + sc2) + sh2).astype(BF16)
    tf = 1024
    acc = jnp.zeros(x.shape, F32)
    for j in range(D_FF // tf):
        hcol = _dot(u, w1_ref[:, j * tf:(j + 1) * tf])
        hcol = jnp.square(jnp.maximum(hcol, 0.0)).astype(BF16)
        acc = acc + _dot(hcol, w2_ref[j * tf:(j + 1) * tf, :])
    x = x + g2 * acc
    if final:
        x = _rms(x) * gf_ref[...]
    o_ref[...] = x


def _outmlp(x2d, ya, yb, mod, w_out, g2, w1, w2, gf, *, L, final):
    T = x2d.shape[0]
    tm = ROW_BLOCK
    per_seq = L // tm
    const = _const_spec
    nmod = mod.shape[0]
    return pl.pallas_call(
        functools.partial(_outmlp_kernel, final=final),
        out_shape=jax.ShapeDtypeStruct((T, D_MODEL), F32),
        grid=(T // tm,),
        in_specs=[
            pl.BlockSpec((tm, D_MODEL), lambda i: (i, 0)),
            pl.BlockSpec((tm, SC_W + HY_W), lambda i: (i, 0)),
            pl.BlockSpec((tm, ML_W), lambda i: (i, 0)),
            pl.BlockSpec((1, 1, 6 * D_MODEL), lambda i: ((i // per_seq) % nmod, 0, 0)),
            const((D_MODEL, D_MODEL)), const((1, D_MODEL)),
            const((D_MODEL, D_FF)), const((D_FF, D_MODEL)), const((1, D_MODEL)),
        ],
        out_specs=pl.BlockSpec((tm, D_MODEL), lambda i: (i, 0)),
        compiler_params=pltpu.CompilerParams(
            dimension_semantics=("arbitrary",), vmem_limit_bytes=VMEM_LIMIT),
        name="outproj_mlp",
    )(x2d, ya, yb, mod, w_out, g2, w1, w2, gf)


def kernel(x_prompt, x_sample, state_C, state_n, state_m, c, c_ctx, norm1_g, ada_w, ada_b, w_in, sc_conv_w, hy_conv_w, hy_w1, hy_b1, hy_w2, hy_b2, hy_w3, hy_freq, hy_skip, ml_conv_w, ml_gate_b, ml_norm_g, w_out, norm2_g, mlp_w1, mlp_w2, norm_f_g):
    B, L = x_prompt.shape[0], x_prompt.shape[1]
    Bd, Ld = x_sample.shape[0], x_sample.shape[1]

    cond = jnp.concatenate([c_ctx[None, :], c, jnp.zeros((8 - 1 - Bd, D_MODEL), F32)], axis=0)
    mods = _ada(cond, ada_w, ada_b)

    w_main = w_in[:, :, :N_MAIN].astype(BF16)
    w_gate = jnp.pad(w_in[:, :, N_MAIN:], ((0, 0), (0, 0), (0, LANES - N_GATE))).astype(BF16)
    w_out_b = w_out.astype(BF16)
    w1_b = mlp_w1.astype(BF16)
    w2_b = mlp_w2.astype(BF16)
    gate_b = jnp.pad(ml_gate_b, ((0, 0), (0, LANES - N_GATE)))
    pad_h = LANES - HY_HID
    w1p = jnp.pad(hy_w1, ((0, 0), (0, LANES - HY_EMB), (0, pad_h)))
    b1p = jnp.pad(hy_b1, ((0, 0), (0, pad_h)))
    w2p = jnp.pad(hy_w2, ((0, 0), (0, pad_h), (0, pad_h)))
    b2p = jnp.pad(hy_b2, ((0, 0), (0, pad_h)))
    w3p = jnp.pad(hy_w3, ((0, 0), (0, pad_h), (0, 0)))
    fp = jnp.pad(hy_freq, ((0, 0), (0, pad_h)))
    pos = jnp.asarray(_pos_embed_np(Ld))
    m0_all = jnp.broadcast_to(state_m.reshape(Bd, DEPTH, 2 * ML_H, 1), (Bd, DEPTH, 2 * ML_H, LANES))
    n0_all = state_n.reshape(Bd, DEPTH, 2 * ML_H, ML_DH)
    gf = norm_f_g[None, :]

    xp = x_prompt.reshape(B * L, D_MODEL)
    xs = x_sample.reshape(Bd * Ld, D_MODEL)
    Cs, ns, ms = [], [], []
    for l in range(DEPTH):
        final = l == DEPTH - 1
        filt_args = (w1p[l], b1p[l][None], w2p[l], b2p[l][None], w3p[l], fp[l][None])
        g1 = norm1_g[l][None]
        g2 = norm2_g[l][None]

        mod_c = mods[l, 0:1][:, None, :]
        proj, gates = _inproj(xp, None, mod_c, g1, w_main[l], w_gate[l], L=L)
        hre, him = _filter_spectrum(L, *filt_args)
        ya = _schy(proj, sc_conv_w[l], hy_conv_w[l], hy_skip[l], hre, him, B=B, L=L)
        yb, C, n, m = _mlstm(proj, gates, gate_b[l][None], ml_conv_w[l], ml_norm_g[l][None], None, B=B, L=L)
        xp = _outmlp(xp, ya, yb, mod_c, w_out_b[l], g2, w1_b[l], w2_b[l], gf, L=L, final=final)
        Cs.append(C)
        ns.append(n.reshape(B, 2, ML_H, ML_DH))
        ms.append(m[:, :, 0].reshape(B, 2, ML_H))

        mod_s = mods[l, 1:1 + Bd][:, None, :]
        res = _inproj(xs, pos if l == 0 else None, mod_s, g1, w_main[l], w_gate[l], L=Ld)
        if l == 0:
            proj, gates, xs = res
        else:
            proj, gates = res
        hre, him = _filter_spectrum(Ld, *filt_args)
        ya = _schy(proj, sc_conv_w[l], hy_conv_w[l], hy_skip[l], hre, him, B=Bd, L=Ld)
        state = (state_C[:, l], n0_all[:, l], m0_all[:, l])
        yb = _mlstm(proj, gates, gate_b[l][None], ml_conv_w[l], ml_norm_g[l][None], state, B=Bd, L=Ld)
        xs = _outmlp(xs, ya, yb, mod_s, w_out_b[l], g2, w1_b[l], w2_b[l], gf, L=Ld, final=final)

    y_prompt = xp.reshape(B, L, D_MODEL)
    y_sample = xs.reshape(Bd, Ld, D_MODEL)
    return (y_prompt, y_sample, jnp.stack(Cs, axis=1), jnp.stack(ns, axis=1), jnp.stack(ms, axis=1))
```

```python
import functools
import math

import numpy as np
import jax
import jax.numpy as jnp
from jax import lax
from jax.experimental import pallas as pl
from jax.experimental.pallas import tpu as pltpu

F32 = jnp.float32
BF16 = jnp.bfloat16

D_MODEL = 1024
DEPTH = 2
GRID_W = 64
SC_W = 256
HY_W = 256
ML_W = 512
ML_H = 4
ML_DH = ML_W // ML_H
D_FF = 4 * D_MODEL
HY_ORDER = 2
HY_BANDS = 16
HY_EMB = 1 + 2 * HY_BANDS
HY_HID = 64
HY_DECAY_SHORT = 0.3
HY_DECAY_LONG = 1.5
HY_DECAY_TARGET = 1e-2
EPS = 1e-6
N_MAIN = 3 * SC_W + 3 * HY_W + 4 * ML_W
N_GATE = 4 * ML_H
LANES = 128
ROW_BLOCK = 256
ATT_BLOCK = 256
NEG_BIG = -1e30
LOG2E = math.log2(math.e)
VMEM_LIMIT = 60 * 1024 * 1024


def _dot(a, b):
    return jnp.dot(a, b, preferred_element_type=F32)


def _dot_nt(a, b):
    return lax.dot_general(a, b, (((1,), (1,)), ((), ())), preferred_element_type=F32)


def _split2(a):
    hi = a.astype(BF16)
    lo = (a - hi.astype(F32)).astype(BF16)
    return hi, lo


def _split3(a):
    hi = a.astype(BF16)
    r = a - hi.astype(F32)
    mid = r.astype(BF16)
    lo = (r - mid.astype(F32)).astype(BF16)
    return hi, mid, lo


def _dot3(a, b):
    ah, al = _split2(a)
    bh, bl = _split2(b)
    return _dot(ah, bh) + _dot(al, bh) + _dot(ah, bl)


def _dot3_tab(th, tl, b):
    bh, bl = _split2(b)
    return _dot(th, bh) + _dot(tl, bh) + _dot(th, bl)


def _sigmoid(x):
    return 1.0 / (1.0 + jnp.exp(-x))


def _silu(x):
    return x * _sigmoid(x)


def _log_sigmoid(x):
    return jnp.minimum(x, 0.0) - jnp.log(1.0 + jnp.exp(-jnp.abs(x)))


def _conv3(x, w):
    L = x.shape[0]
    row = lax.broadcasted_iota(jnp.int32, x.shape, 0)
    xm = jnp.where(row == 0, 0.0, pltpu.roll(x, 1, axis=0))
    xp = jnp.where(row == L - 1, 0.0, pltpu.roll(x, L - 1, axis=0))
    return xm * w[0:1, :] + x * w[1:2, :] + xp * w[2:3, :]


def _rms(x):
    return x * lax.rsqrt(jnp.mean(x * x, axis=-1, keepdims=True) + EPS)


def _scan_max(x, *, reverse):
    n = x.shape[1]
    lane = lax.broadcasted_iota(jnp.int32, x.shape, 1)
    d = 1
    while d < n:
        if reverse:
            shifted = jnp.where(lane < n - d, pltpu.roll(x, n - d, axis=1), NEG_BIG)
        else:
            shifted = jnp.where(lane >= d, pltpu.roll(x, d, axis=1), NEG_BIG)
        x = jnp.maximum(x, shifted)
        d *= 2
    return x


def _hi_lo(a64):
    a = a64.astype(np.float32)
    hi = a.astype(BF16)
    lo = (a - hi.astype(np.float32)).astype(BF16)
    return jnp.asarray(hi), jnp.asarray(lo)


def _const_spec(shape):
    return pl.BlockSpec(shape, lambda *_: tuple(0 for _ in shape), pipeline_mode=pl.Buffered(1))


@functools.lru_cache(maxsize=None)
def _dft_tables_np(L):
    k = np.arange(L, dtype=np.int64)
    m1 = ((2 * k[:, None] + 1) * k[None, :]) % (4 * L)
    a1 = np.pi * m1.astype(np.float64) / (2 * L)
    m2 = ((2 * k[:, None] + 1) * (2 * k[None, :] + 1)) % (8 * L)
    a2 = np.pi * m2.astype(np.float64) / (4 * L)
    return np.cos(a1), np.sin(a1), np.cos(a2), np.sin(a2)


@functools.lru_cache(maxsize=None)
def _filter_consts_np(L):
    t_idx = np.arange(L, dtype=np.float64)
    t = t_idx / (L - 1)
    bands = np.arange(1, HY_BANDS + 1, dtype=np.float64)
    ang = 2.0 * math.pi * t_idx[:, None] * bands[None, :] / L
    z = np.zeros((L, LANES), np.float64)
    z[:, 0] = t
    z[:, 1:1 + HY_BANDS] = np.cos(ang)
    z[:, 1 + HY_BANDS:1 + 2 * HY_BANDS] = -np.sin(ang)
    lin = np.linspace(math.log(HY_DECAY_TARGET) / HY_DECAY_LONG,
                      math.log(HY_DECAY_TARGET) / HY_DECAY_SHORT, HY_W).astype(np.float32)
    deltas = np.abs(lin).astype(np.float64)
    decay = np.exp(-t[:, None] * deltas[None, :])
    return z.astype(np.float32), decay.astype(np.float32)


@functools.lru_cache(maxsize=None)
def _pos_embed_np(L):
    rows = L // GRID_W
    r, cidx = np.meshgrid(np.arange(rows, dtype=np.float64), np.arange(GRID_W, dtype=np.float64), indexing="ij")
    r = r.reshape(-1)
    cidx = cidx.reshape(-1)
    quarter = D_MODEL // 4
    omega = 1.0 / (10000.0 ** (np.arange(quarter, dtype=np.float64) / quarter))
    ar = r[:, None] * omega[None, :]
    ac = cidx[:, None] * omega[None, :]
    return np.concatenate([np.sin(ar), np.cos(ar), np.sin(ac), np.cos(ac)], axis=-1).astype(np.float32)


def _ada_kernel(c_ref, w_ref, b_ref, o_ref):
    sc = _silu(c_ref[...])
    o_ref[0] = _dot3(sc, w_ref[0]) + b_ref[0]


def _ada(cond, ada_w, ada_b):
    tn = 1536
    nd = 6 * D_MODEL
    return pl.pallas_call(
        _ada_kernel,
        out_shape=jax.ShapeDtypeStruct((DEPTH, 8, nd), F32),
        grid=(DEPTH, nd // tn),
        in_specs=[
            pl.BlockSpec((8, D_MODEL), lambda l, j: (0, 0)),
            pl.BlockSpec((1, D_MODEL, tn), lambda l, j: (l, 0, j)),
            pl.BlockSpec((1, 1, tn), lambda l, j: (l, 0, j)),
        ],
        out_specs=pl.BlockSpec((1, 8, tn), lambda l, j: (l, 0, j)),
        compiler_params=pltpu.CompilerParams(
            dimension_semantics=("arbitrary", "arbitrary"), vmem_limit_bytes=VMEM_LIMIT),
        name="ada_mod",
    )(cond, ada_w, ada_b.reshape(DEPTH, 1, nd))


def _inproj_kernel(*refs, with_pos):
    if with_pos:
        x_ref, pos_ref, mod_ref, g_ref, w_ref, wg_ref, proj_ref, gate_ref, xo_ref = refs
        x = x_ref[...] + pos_ref[...]
        xo_ref[...] = x
    else:
        x_ref, mod_ref, g_ref, w_ref, wg_ref, proj_ref, gate_ref = refs
        x = x_ref[...]
    mod = mod_ref[0]
    sh1 = mod[:, 0:D_MODEL]
    sc1 = mod[:, D_MODEL:2 * D_MODEL]
    u = (_rms(x) * g_ref[...]) * (1.0 + sc1) + sh1
    ub = u.astype(BF16)
    tn = 512
    for j in range(N_MAIN // tn):
        proj_ref[:, j * tn:(j + 1) * tn] = _dot(ub, w_ref[:, j * tn:(j + 1) * tn])
    gate_ref[...] = _dot(ub, wg_ref[...])


def _inproj(x2d, pos, mod, g, w_main, w_gate, *, L):
    T = x2d.shape[0]
    tm = ROW_BLOCK
    nblk = T // tm
    per_seq = L // tm
    with_pos = pos is not None
    row_spec = pl.BlockSpec((tm, D_MODEL), lambda i: (i, 0))
    in_specs = [row_spec]
    args = [x2d]
    if with_pos:
        in_specs.append(pl.BlockSpec((tm, D_MODEL), lambda i: (i % per_seq, 0)))
        args.append(pos)
    nmod = mod.shape[0]
    in_specs += [
        pl.BlockSpec((1, 1, 6 * D_MODEL), lambda i: ((i // per_seq) % nmod, 0, 0)),
        _const_spec((1, D_MODEL)),
        _const_spec((D_MODEL, N_MAIN)),
        _const_spec((D_MODEL, LANES)),
    ]
    args += [mod, g, w_main, w_gate]
    out_shape = [jax.ShapeDtypeStruct((T, N_MAIN), F32), jax.ShapeDtypeStruct((T, LANES), F32)]
    out_specs = [pl.BlockSpec((tm, N_MAIN), lambda i: (i, 0)), pl.BlockSpec((tm, LANES), lambda i: (i, 0))]
    if with_pos:
        out_shape.append(jax.ShapeDtypeStruct((T, D_MODEL), F32))
        out_specs.append(row_spec)
    return pl.pallas_call(
        functools.partial(_inproj_kernel, with_pos=with_pos),
        out_shape=out_shape,
        grid=(nblk,),
        in_specs=in_specs,
        out_specs=out_specs,
        compiler_params=pltpu.CompilerParams(
            dimension_semantics=("arbitrary",), vmem_limit_bytes=VMEM_LIMIT),
        name="inproj",
    )(*args)


def _filter_kernel(z_ref, dec_ref, w1_ref, b1_ref, w2_ref, b2_ref, w3_ref, f_ref,
                   ch_ref, cl_ref, sh_ref, sl_ref, hre_ref, him_ref, *, L):
    f = f_ref[...]
    hdn = jnp.sin(f * (_dot3(z_ref[...], w1_ref[...]) + b1_ref[...]))
    hdn = jnp.sin(f * (_dot3(hdn, w2_ref[...]) + b2_ref[...]))
    h = _dot3(hdn, w3_ref[...])
    dec = dec_ref[...]
    row = lax.broadcasted_iota(jnp.int32, (L, HY_W), 0)
    hp, hm = [], []
    for o in range(HY_ORDER):
        hf = h[:, (2 * o) * HY_W:(2 * o + 1) * HY_W] * dec
        hb = jnp.where(row == 0, 0.0, h[:, (2 * o + 1) * HY_W:(2 * o + 2) * HY_W] * dec)
        nrm = jnp.sum(jnp.abs(hf), axis=0, keepdims=True) + jnp.sum(jnp.abs(hb), axis=0, keepdims=True)
        inv = 1.0 / nrm
        hp.append((hf + hb) * inv)
        hm.append((hf - hb) * inv)
    hp = jnp.concatenate(hp, axis=1)
    hm = jnp.concatenate(hm, axis=1)
    hre_ref[...] = _dot3_tab(ch_ref[...], cl_ref[...], hp)
    him_ref[...] = -_dot3_tab(sh_ref[...], sl_ref[...], hm)


def _filter_spectrum(L, w1p, b1p, w2p, b2p, w3p, fp):
    z_np, dec_np = _filter_consts_np(L)
    c1, s1, _, _ = _dft_tables_np(L)
    ch, cl = _hi_lo(c1)
    sh, sl = _hi_lo(s1)
    n = HY_ORDER * HY_W
    full = lambda shape: pl.BlockSpec(shape, lambda i: tuple(0 for _ in shape))
    return pl.pallas_call(
        functools.partial(_filter_kernel, L=L),
        out_shape=[jax.ShapeDtypeStruct((L, n), F32), jax.ShapeDtypeStruct((L, n), F32)],
        grid=(1,),
        in_specs=[full((L, LANES)), full((L, HY_W)), full((LANES, LANES)), full((1, LANES)),
                  full((LANES, LANES)), full((1, LANES)), full((LANES, 2 * n)), full((1, LANES)),
                  full((L, L)), full((L, L)), full((L, L)), full((L, L))],
        out_specs=[full((L, n)), full((L, n))],
        compiler_params=pltpu.CompilerParams(
            dimension_semantics=("arbitrary",), vmem_limit_bytes=VMEM_LIMIT),
        name="hyena_filter",
    )(jnp.asarray(z_np), jnp.asarray(dec_np), w1p, b1p, w2p, b2p, w3p, fp, ch, cl, sh, sl)


def _schy_kernel(sc_ref, hy_ref, scw_ref, hyw_ref, skip_ref, ch_ref, cl_ref, sh_ref, sl_ref,
                 hre_ref, him_ref, out_ref, *, L):
    sc = sc_ref[...]
    b_g = sc[:, 0:SC_W]
    c_g = sc[:, SC_W:2 * SC_W]
    xin = sc[:, 2 * SC_W:3 * SC_W]
    out_ref[:, 0:SC_W] = b_g * _conv3(c_g * xin, scw_ref[...])

    u3 = _conv3(hy_ref[...], hyw_ref[...])
    z = u3[:, 0:HY_W]
    gates = (u3[:, HY_W:2 * HY_W], u3[:, 2 * HY_W:3 * HY_W])
    ch, cl, sh, sl = ch_ref[...], cl_ref[...], sh_ref[...], sl_ref[...]
    inv_l = 1.0 / L
    for o in range(HY_ORDER):
        zh, zl = _split2(z)
        zc = _dot(ch, zh) + _dot(cl, zh) + _dot(ch, zl)
        zs = _dot(sh, zh) + _dot(sl, zh) + _dot(sh, zl)
        hre = hre_ref[:, o * HY_W:(o + 1) * HY_W]
        him = him_ref[:, o * HY_W:(o + 1) * HY_W]
        yre = zc * hre + zs * him
        yim = zc * him - zs * hre
        y = (_dot3_tab(ch, cl, yre) - _dot3_tab(sh, sl, yim)) * inv_l
        z = gates[o] * (y + skip_ref[o:o + 1, :] * z)
    out_ref[:, SC_W:SC_W + HY_W] = z


def _schy(proj, sc_w, hy_w, skip, hre, him, *, B, L):
    _, _, c2, s2 = _dft_tables_np(L)
    ch, cl = _hi_lo(c2)
    sh, sl = _hi_lo(s2)
    n = HY_ORDER * HY_W
    const = _const_spec
    return pl.pallas_call(
        functools.partial(_schy_kernel, L=L),
        out_shape=jax.ShapeDtypeStruct((B * L, SC_W + HY_W), F32),
        grid=(B,),
        in_specs=[
            pl.BlockSpec((L, 3 * SC_W), lambda b: (b, 0)),
            pl.BlockSpec((L, 3 * HY_W), lambda b: (b, 1)),
            const((3, SC_W)), const((3, 3 * HY_W)), const((HY_ORDER, HY_W)),
            const((L, L)), const((L, L)), const((L, L)), const((L, L)),
            const((L, n)), const((L, n)),
        ],
        out_specs=pl.BlockSpec((L, SC_W + HY_W), lambda b: (b, 0)),
        compiler_params=pltpu.CompilerParams(
            dimension_semantics=("arbitrary",), vmem_limit_bytes=VMEM_LIMIT),
        name="sconv_hyena",
    )(proj, proj, sc_w, hy_w, skip, ch, cl, sh, sl, hre, him)


def _mlstm_kernel(*refs, L, with_state):
    if with_state:
        (q_ref, k_ref, v_ref, o_ref, g_ref, gb_ref, cw_ref, ng_ref,
         c0_ref, n0_ref, m0_ref, y_ref) = refs
    else:
        (q_ref, k_ref, v_ref, o_ref, g_ref, gb_ref, cw_ref, ng_ref,
         y_ref, cout_ref, nout_ref, mout_ref) = refs

    tb = min(L // 2, ATT_BLOCK)
    cw = cw_ref[...]
    q_all = _silu(_conv3(q_ref[...], cw[:, 0:ML_W]))
    k_all = _silu(_conv3(k_ref[...], cw[:, ML_W:2 * ML_W])) * (ML_DH ** -0.5)

    g = g_ref[...] + gb_ref[...]
    lane = lax.broadcasted_iota(jnp.int32, g.shape, 1)
    is_f = ((lane >= ML_H) & (lane < 2 * ML_H)) | ((lane >= 3 * ML_H) & (lane < 4 * ML_H))
    pk = jnp.where(is_f, _log_sigmoid(g), g)
    x16 = pk.T[0:N_GATE, :]
    r_i = lax.broadcasted_iota(jnp.int32, (L, L), 0)
    c_i = lax.broadcasted_iota(jnp.int32, (L, L), 1)
    triu = jnp.where(r_i <= c_i, 1.0, 0.0).astype(BF16)
    cs = _dot(jnp.concatenate(_split3(x16), axis=0), triu)
    cum = cs[0:N_GATE] + cs[N_GATE:2 * N_GATE] + cs[2 * N_GATE:3 * N_GATE]
    suf = cum[:, L - 1:L] - cum + x16
    row16 = lax.broadcasted_iota(jnp.int32, (N_GATE, L), 0)
    fwd_rows = row16 < 2 * ML_H
    b16 = pltpu.roll(jnp.where(fwd_rows, cum, suf), N_GATE - ML_H, axis=0)
    r16 = x16 - b16
    cm = jnp.where(fwd_rows, _scan_max(r16, reverse=False), _scan_max(r16, reverse=True))
    if with_state:
        m16 = jnp.maximum(m0_ref[0][:, 0:1], cm)
    else:
        m16 = jnp.maximum(cm, 0.0)
    rl16 = r16 * LOG2E
    stack = jnp.concatenate([m16 * LOG2E, b16 + m16, rl16, jnp.zeros((LANES - 3 * N_GATE, L), F32)], axis=0)
    cols = stack.T

    t_loc = lax.broadcasted_iota(jnp.int32, (tb, tb), 0)
    s_loc = lax.broadcasted_iota(jnp.int32, (tb, tb), 1)
    mask_f = s_loc <= t_loc
    mask_b = s_loc >= t_loc
    ones_col = jnp.where(lax.broadcasted_iota(jnp.int32, (L, LANES), 1) == 0, 1.0, 0.0).astype(BF16)
    for h in range(ML_H):
        hs = slice(h * ML_DH, (h + 1) * ML_DH)
        q = q_all[:, hs]
        k = k_all[:, hs]
        v = v_ref[:, hs]
        qb = q.astype(BF16)
        kb = k.astype(BF16)
        vb = v.astype(BF16)
        vext = jnp.concatenate([vb, ones_col], axis=1)
        dirs = []
        for d in range(2):
            o8 = 2 * ML_H * d + h
            dirs.append((rl16[o8:o8 + 1, :], cols[:, o8:o8 + 1], cols[:, N_GATE + o8:N_GATE + o8 + 1]))
        if with_state:
            c0 = [c0_ref[0, d, h].astype(BF16) for d in range(2)]
            n0 = [n0_ref[0, d * ML_H + h:d * ML_H + h + 1, :] for d in range(2)]
            m0l = [m0_ref[0, 2 * ML_H * d + h:2 * ML_H * d + h + 1, 0:1] * LOG2E for d in range(2)]
        for r0 in range(0, L, tb):
            rs = slice(r0, r0 + tb)
            s = _dot_nt(qb[rs], kb)
            hout = None
            for d, (rl_row, ml_col, mt_col) in enumerate(dirs):
                ml = ml_col[rs]
                e_diag = jnp.exp2(jnp.where(mask_b if d else mask_f, rl_row[:, rs] - ml, NEG_BIG))
                acc = _dot((s[:, rs] * e_diag).astype(BF16), vext[rs])
                side = slice(r0 + tb, L) if d else slice(0, r0)
                if side.stop > side.start:
                    e_side = jnp.exp2(rl_row[:, side] - ml)
                    acc = acc + _dot((s[:, side] * e_side).astype(BF16), vext[side])
                num = acc[:, 0:ML_DH]
                den = acc[:, ML_DH:ML_DH + 1]
                if with_state:
                    wp = jnp.exp2(m0l[d] - ml)
                    den = den + wp * jnp.sum(q[rs] * n0[d], axis=-1, keepdims=True)
                    num = num + wp * _dot(qb[rs], c0[d])
                contrib = num / jnp.maximum(jnp.abs(den), jnp.exp(-mt_col[rs]))
                hout = contrib if hout is None else hout + contrib
            hn = hout * lax.rsqrt(jnp.mean(hout * hout, axis=-1, keepdims=True) + EPS)
            y_ref[rs, hs] = _sigmoid(o_ref[rs, hs]) * (hn * ng_ref[:, hs])

        if not with_state:
            for d in range(2):
                o8 = 2 * ML_H * d + h
                end = 0 if d else L - 1
                rl_col = cols[:, 2 * N_GATE + o8:2 * N_GATE + o8 + 1]
                w_s = jnp.exp2(rl_col - cols[end:end + 1, o8:o8 + 1])
                kw = k * w_s
                cout_ref[0, d, h] = _dot(kw.T.astype(BF16), vb)
                nout_ref[0, d * ML_H + h:d * ML_H + h + 1, :] = jnp.sum(kw, axis=0, keepdims=True)
                mout_ref[0, d * ML_H + h:d * ML_H + h + 1, :] = jnp.broadcast_to(
                    cols[end:end + 1, N_GATE + o8:N_GATE + o8 + 1], (1, LANES))


def _mlstm(proj, gates, gate_b, conv_w, norm_g, state, *, B, L):
    with_state = state is not None
    col0 = (3 * SC_W + 3 * HY_W) // ML_W
    const = _const_spec
    in_specs = [
        pl.BlockSpec((L, ML_W), lambda b: (b, col0)),
        pl.BlockSpec((L, ML_W), lambda b: (b, col0 + 1)),
        pl.BlockSpec((L, ML_W), lambda b: (b, col0 + 2)),
        pl.BlockSpec((L, ML_W), lambda b: (b, col0 + 3)),
        pl.BlockSpec((L, LANES), lambda b: (b, 0)),
        const((1, LANES)), const((3, 2 * ML_W)), const((1, ML_W)),
    ]
    args = [proj, proj, proj, proj, gates, gate_b, conv_w, norm_g]
    y_shape = jax.ShapeDtypeStruct((B * L, ML_W), F32)
    y_spec = pl.BlockSpec((L, ML_W), lambda b: (b, 0))
    if with_state:
        c0, n0, m0 = state
        in_specs += [
            pl.BlockSpec((1, 2, ML_H, ML_DH, ML_DH), lambda b: (b, 0, 0, 0, 0)),
            pl.BlockSpec((1, 2 * ML_H, ML_DH), lambda b: (b, 0, 0)),
            pl.BlockSpec((1, N_GATE, LANES), lambda b: (b, 0, 0)),
        ]
        args += [c0, n0, m0]
        out_shape, out_specs = y_shape, y_spec
    else:
        out_shape = [y_shape,
                     jax.ShapeDtypeStruct((B, 2, ML_H, ML_DH, ML_DH), F32),
                     jax.ShapeDtypeStruct((B, 2 * ML_H, ML_DH), F32),
                     jax.ShapeDtypeStruct((B, 2 * ML_H, LANES), F32)]
        out_specs = [y_spec,
                     pl.BlockSpec((1, 2, ML_H, ML_DH, ML_DH), lambda b: (b, 0, 0, 0, 0)),
                     pl.BlockSpec((1, 2 * ML_H, ML_DH), lambda b: (b, 0, 0)),
                     pl.BlockSpec((1, 2 * ML_H, LANES), lambda b: (b, 0, 0))]
    return pl.pallas_call(
        functools.partial(_mlstm_kernel, L=L, with_state=with_state),
        out_shape=out_shape,
        grid=(B,),
        in_specs=in_specs,
        out_specs=out_specs,
        compiler_params=pltpu.CompilerParams(
            dimension_semantics=("arbitrary",), vmem_limit_bytes=VMEM_LIMIT),
        name="mlstm",
    )(*args)


def _outmlp_kernel(x_ref, ya_ref, yb_ref, mod_ref, wo_ref, g2_ref, w1_ref, w2_ref, gf_ref, o_ref, *, final):
    mod = mod_ref[0]
    g1 = mod[:, 2 * D_MODEL:3 * D_MODEL]
    sh2 = mod[:, 3 * D_MODEL:4 * D_MODEL]
    sc2 = mod[:, 4 * D_MODEL:5 * D_MODEL]
    g2 = mod[:, 5 * D_MODEL:6 * D_MODEL]
    na = SC_W + HY_W
    attn = _dot(ya_ref[...].astype(BF16), wo_ref[0:na, :]) + _dot(yb_ref[...].astype(BF16), wo_ref[na:, :])
    x = x_ref[...] + g1 * attn
    u = ((_rms(x) * g2_ref[...]) * (1.0 + sc2) + sh2).astype(BF16)
    tf = 1024
    acc = jnp.zeros(x.shape, F32)
    for j in range(D_FF // tf):
        hcol = _dot(u, w1_ref[:, j * tf:(j + 1) * tf])
        hcol = jnp.square(jnp.maximum(hcol, 0.0)).astype(BF16)
        acc = acc + _dot(hcol, w2_ref[j * tf:(j + 1) * tf, :])
    x = x + g2 * acc
    if final:
        x = _rms(x) * gf_ref[...]
    o_ref[...] = x


def _outmlp(x2d, ya, yb, mod, w_out, g2, w1, w2, gf, *, L, final):
    T = x2d.shape[0]
    tm = ROW_BLOCK
    per_seq = L // tm
    const = _const_spec
    nmod = mod.shape[0]
    return pl.pallas_call(
        functools.partial(_outmlp_kernel, final=final),
        out_shape=jax.ShapeDtypeStruct((T, D_MODEL), F32),
        grid=(T // tm,),
        in_specs=[
            pl.BlockSpec((tm, D_MODEL), lambda i: (i, 0)),
            pl.BlockSpec((tm, SC_W + HY_W), lambda i: (i, 0)),
            pl.BlockSpec((tm, ML_W), lambda i: (i, 0)),
            pl.BlockSpec((1, 1, 6 * D_MODEL), lambda i: ((i // per_seq) % nmod, 0, 0)),
            const((D_MODEL, D_MODEL)), const((1, D_MODEL)),
            const((D_MODEL, D_FF)), const((D_FF, D_MODEL)), const((1, D_MODEL)),
        ],
        out_specs=pl.BlockSpec((tm, D_MODEL), lambda i: (i, 0)),
        compiler_params=pltpu.CompilerParams(
            dimension_semantics=("arbitrary",), vmem_limit_bytes=VMEM_LIMIT),
        name="outproj_mlp",
    )(x2d, ya, yb, mod, w_out, g2, w1, w2, gf)


def kernel(x_prompt, x_sample, state_C, state_n, state_m, c, c_ctx, norm1_g, ada_w, ada_b, w_in, sc_conv_w, hy_conv_w, hy_w1, hy_b1, hy_w2, hy_b2, hy_w3, hy_freq, hy_skip, ml_conv_w, ml_gate_b, ml_norm_g, w_out, norm2_g, mlp_w1, mlp_w2, norm_f_g):
    B, L = x_prompt.shape[0], x_prompt.shape[1]
    Bd, Ld = x_sample.shape[0], x_sample.shape[1]

    cond = jnp.concatenate([c_ctx[None, :], c, jnp.zeros((8 - 1 - Bd, D_MODEL), F32)], axis=0)
    mods = _ada(cond, ada_w, ada_b)

    w_main = w_in[:, :, :N_MAIN].astype(BF16)
    w_gate = jnp.pad(w_in[:, :, N_MAIN:], ((0, 0), (0, 0), (0, LANES - N_GATE))).astype(BF16)
    w_out_b = w_out.astype(BF16)
    w1_b = mlp_w1.astype(BF16)
    w2_b = mlp_w2.astype(BF16)
    gate_b = jnp.pad(ml_gate_b, ((0, 0), (0, LANES - N_GATE)))
    pad_h = LANES - HY_HID
    w1p = jnp.pad(hy_w1, ((0, 0), (0, LANES - HY_EMB), (0, pad_h)))
    b1p = jnp.pad(hy_b1, ((0, 0), (0, pad_h)))
    w2p = jnp.pad(hy_w2, ((0, 0), (0, pad_h), (0, pad_h)))
    b2p = jnp.pad(hy_b2, ((0, 0), (0, pad_h)))
    w3p = jnp.pad(hy_w3, ((0, 0), (0, pad_h), (0, 0)))
    fp = jnp.pad(hy_freq, ((0, 0), (0, pad_h)))
    pos = jnp.asarray(_pos_embed_np(Ld))
    m0_all = jnp.pad(state_m, ((0, 0), (0, 0), (0, 0), (0, ML_H))).reshape(Bd, DEPTH, N_GATE, 1)
    m0_all = jnp.broadcast_to(m0_all, (Bd, DEPTH, N_GATE, LANES))
    n0_all = state_n.reshape(Bd, DEPTH, 2 * ML_H, ML_DH)
    gf = norm_f_g[None, :]

    xp = x_prompt.reshape(B * L, D_MODEL)
    xs = x_sample.reshape(Bd * Ld, D_MODEL)
    Cs, ns, ms = [], [], []
    for l in range(DEPTH):
        final = l == DEPTH - 1
        filt_args = (w1p[l], b1p[l][None], w2p[l], b2p[l][None], w3p[l], fp[l][None])
        g1 = norm1_g[l][None]
        g2 = norm2_g[l][None]

        mod_c = mods[l, 0:1][:, None, :]
        proj, gates = _inproj(xp, None, mod_c, g1, w_main[l], w_gate[l], L=L)
        hre, him = _filter_spectrum(L, *filt_args)
        ya = _schy(proj, sc_conv_w[l], hy_conv_w[l], hy_skip[l], hre, him, B=B, L=L)
        yb, C, n, m = _mlstm(proj, gates, gate_b[l][None], ml_conv_w[l], ml_norm_g[l][None], None, B=B, L=L)
        xp = _outmlp(xp, ya, yb, mod_c, w_out_b[l], g2, w1_b[l], w2_b[l], gf, L=L, final=final)
        Cs.append(C)
        ns.append(n.reshape(B, 2, ML_H, ML_DH))
        ms.append(m[:, :, 0].reshape(B, 2, ML_H))

        mod_s = mods[l, 1:1 + Bd][:, None, :]
        res = _inproj(xs, pos if l == 0 else None, mod_s, g1, w_main[l], w_gate[l], L=Ld)
        if l == 0:
            proj, gates, xs = res
        else:
            proj, gates = res
        hre, him = _filter_spectrum(Ld, *filt_args)
        ya = _schy(proj, sc_conv_w[l], hy_conv_w[l], hy_skip[l], hre, him, B=Bd, L=Ld)
        state = (state_C[:, l], n0_all[:, l], m0_all[:, l])
        yb = _mlstm(proj, gates, gate_b[l][None], ml_conv_w[l], ml_norm_g[l][None], state, B=Bd, L=Ld)
        xs = _outmlp(xs, ya, yb, mod_s, w_out_b[l], g2, w1_b[l], w2_b[l], gf, L=Ld, final=final)

    y_prompt = xp.reshape(B, L, D_MODEL)
    y_sample = xs.reshape(Bd, Ld, D_MODEL)
    return (y_prompt, y_sample, jnp.stack(Cs, axis=1), jnp.stack(ns, axis=1), jnp.stack(ms, axis=1))
```

```python
import functools
import math

import numpy as np
import jax
import jax.numpy as jnp
from jax import lax
from jax.experimental import pallas as pl
from jax.experimental.pallas import tpu as pltpu

F32 = jnp.float32
BF16 = jnp.bfloat16

D_MODEL = 1024
DEPTH = 2
GRID_W = 64
SC_W = 256
HY_W = 256
ML_W = 512
ML_H = 4
ML_DH = ML_W // ML_H
D_FF = 4 * D_MODEL
HY_ORDER = 2
HY_BANDS = 16
HY_EMB = 1 + 2 * HY_BANDS
HY_HID = 64
HY_DECAY_SHORT = 0.3
HY_DECAY_LONG = 1.5
HY_DECAY_TARGET = 1e-2
EPS = 1e-6
N_MAIN = 3 * SC_W + 3 * HY_W + 4 * ML_W
N_GATE = 4 * ML_H
LANES = 128
ROW_BLOCK = 256
ATT_BLOCK = 256
NEG_BIG = -1e30
LOG2E = math.log2(math.e)
VMEM_LIMIT = 60 * 1024 * 1024


def _dot(a, b):
    return jnp.dot(a, b, preferred_element_type=F32)


def _dot_nt(a, b):
    return lax.dot_general(a, b, (((1,), (1,)), ((), ())), preferred_element_type=F32)


def _split2(a):
    hi = a.astype(BF16)
    lo = (a - hi.astype(F32)).astype(BF16)
    return hi, lo


def _split3(a):
    hi = a.astype(BF16)
    r = a - hi.astype(F32)
    mid = r.astype(BF16)
    lo = (r - mid.astype(F32)).astype(BF16)
    return hi, mid, lo


def _dot3(a, b):
    ah, al = _split2(a)
    bh, bl = _split2(b)
    return _dot(ah, bh) + _dot(al, bh) + _dot(ah, bl)


def _dot3_nt(a, b):
    ah, al = _split2(a)
    bh, bl = _split2(b)
    return _dot_nt(ah, bh) + _dot_nt(al, bh) + _dot_nt(ah, bl)


def _dot3_tab(th, tl, b):
    bh, bl = _split2(b)
    return _dot(th, bh) + _dot(tl, bh) + _dot(th, bl)


def _sigmoid(x):
    return 1.0 / (1.0 + jnp.exp(-x))


def _silu(x):
    return x * _sigmoid(x)


def _log_sigmoid(x):
    return jnp.minimum(x, 0.0) - jnp.log(1.0 + jnp.exp(-jnp.abs(x)))


def _conv3(x, w):
    L = x.shape[0]
    row = lax.broadcasted_iota(jnp.int32, x.shape, 0)
    xm = jnp.where(row == 0, 0.0, pltpu.roll(x, 1, axis=0))
    xp = jnp.where(row == L - 1, 0.0, pltpu.roll(x, L - 1, axis=0))
    return xm * w[0:1, :] + x * w[1:2, :] + xp * w[2:3, :]


def _rms(x):
    return x * lax.rsqrt(jnp.mean(x * x, axis=-1, keepdims=True) + EPS)


def _scan_max(x, *, reverse):
    n = x.shape[1]
    lane = lax.broadcasted_iota(jnp.int32, x.shape, 1)
    d = 1
    while d < n:
        if reverse:
            shifted = jnp.where(lane < n - d, pltpu.roll(x, n - d, axis=1), NEG_BIG)
        else:
            shifted = jnp.where(lane >= d, pltpu.roll(x, d, axis=1), NEG_BIG)
        x = jnp.maximum(x, shifted)
        d *= 2
    return x


def _hi_lo(a64):
    a = a64.astype(np.float32)
    hi = a.astype(BF16)
    lo = (a - hi.astype(np.float32)).astype(BF16)
    return jnp.asarray(hi), jnp.asarray(lo)


def _const_spec(shape):
    return pl.BlockSpec(shape, lambda *_: tuple(0 for _ in shape), pipeline_mode=pl.Buffered(1))


@functools.lru_cache(maxsize=None)
def _dft_tables_np(L):
    k = np.arange(L, dtype=np.int64)
    m1 = ((2 * k[:, None] + 1) * k[None, :]) % (4 * L)
    a1 = np.pi * m1.astype(np.float64) / (2 * L)
    m2 = ((2 * k[:, None] + 1) * (2 * k[None, :] + 1)) % (8 * L)
    a2 = np.pi * m2.astype(np.float64) / (4 * L)
    return np.cos(a1), np.sin(a1), np.cos(a2), np.sin(a2)


@functools.lru_cache(maxsize=None)
def _filter_consts_np(L):
    t_idx = np.arange(L, dtype=np.float64)
    t = t_idx / (L - 1)
    bands = np.arange(1, HY_BANDS + 1, dtype=np.float64)
    ang = 2.0 * math.pi * t_idx[:, None] * bands[None, :] / L
    z = np.zeros((L, LANES), np.float64)
    z[:, 0] = t
    z[:, 1:1 + HY_BANDS] = np.cos(ang)
    z[:, 1 + HY_BANDS:1 + 2 * HY_BANDS] = -np.sin(ang)
    lin = np.linspace(math.log(HY_DECAY_TARGET) / HY_DECAY_LONG,
                      math.log(HY_DECAY_TARGET) / HY_DECAY_SHORT, HY_W).astype(np.float32)
    deltas = np.abs(lin).astype(np.float64)
    decay = np.exp(-t[:, None] * deltas[None, :])
    return z.astype(np.float32), decay.astype(np.float32)


@functools.lru_cache(maxsize=None)
def _pos_embed_np(L):
    rows = L // GRID_W
    r, cidx = np.meshgrid(np.arange(rows, dtype=np.float64), np.arange(GRID_W, dtype=np.float64), indexing="ij")
    r = r.reshape(-1)
    cidx = cidx.reshape(-1)
    quarter = D_MODEL // 4
    omega = 1.0 / (10000.0 ** (np.arange(quarter, dtype=np.float64) / quarter))
    ar = r[:, None] * omega[None, :]
    ac = cidx[:, None] * omega[None, :]
    return np.concatenate([np.sin(ar), np.cos(ar), np.sin(ac), np.cos(ac)], axis=-1).astype(np.float32)


def _ada_kernel(c_ref, w_ref, b_ref, o_ref):
    sc = _silu(c_ref[...])
    o_ref[0] = _dot3(sc, w_ref[0]) + b_ref[0]


def _ada(cond, ada_w, ada_b):
    tn = 1536
    nd = 6 * D_MODEL
    return pl.pallas_call(
        _ada_kernel,
        out_shape=jax.ShapeDtypeStruct((DEPTH, 8, nd), F32),
        grid=(DEPTH, nd // tn),
        in_specs=[
            pl.BlockSpec((8, D_MODEL), lambda l, j: (0, 0)),
            pl.BlockSpec((1, D_MODEL, tn), lambda l, j: (l, 0, j)),
            pl.BlockSpec((1, 1, tn), lambda l, j: (l, 0, j)),
        ],
        out_specs=pl.BlockSpec((1, 8, tn), lambda l, j: (l, 0, j)),
        compiler_params=pltpu.CompilerParams(
            dimension_semantics=("arbitrary", "arbitrary"), vmem_limit_bytes=VMEM_LIMIT),
        name="ada_mod",
    )(cond, ada_w, ada_b.reshape(DEPTH, 1, nd))


def _inproj_kernel(*refs, with_pos):
    if with_pos:
        x_ref, pos_ref, mod_ref, g_ref, w_ref, wg_ref, proj_ref, gate_ref, xo_ref = refs
        x = x_ref[...] + pos_ref[...]
        xo_ref[...] = x
    else:
        x_ref, mod_ref, g_ref, w_ref, wg_ref, proj_ref, gate_ref = refs
        x = x_ref[...]
    mod = mod_ref[0]
    sh1 = mod[:, 0:D_MODEL]
    sc1 = mod[:, D_MODEL:2 * D_MODEL]
    u = (_rms(x) * g_ref[...]) * (1.0 + sc1) + sh1
    ub = u.astype(BF16)
    tn = 512
    for j in range(N_MAIN // tn):
        proj_ref[:, j * tn:(j + 1) * tn] = _dot(ub, w_ref[:, j * tn:(j + 1) * tn])
    gate_ref[...] = _dot_nt(wg_ref[...], ub)


def _inproj(x2d, pos, mod, g, w_main, w_gate, *, L):
    T = x2d.shape[0]
    tm = ROW_BLOCK
    nblk = T // tm
    per_seq = L // tm
    with_pos = pos is not None
    row_spec = pl.BlockSpec((tm, D_MODEL), lambda i: (i, 0))
    in_specs = [row_spec]
    args = [x2d]
    if with_pos:
        in_specs.append(pl.BlockSpec((tm, D_MODEL), lambda i: (i % per_seq, 0)))
        args.append(pos)
    nmod = mod.shape[0]
    in_specs += [
        pl.BlockSpec((1, 1, 6 * D_MODEL), lambda i: ((i // per_seq) % nmod, 0, 0)),
        _const_spec((1, D_MODEL)),
        _const_spec((D_MODEL, N_MAIN)),
        _const_spec((N_GATE, D_MODEL)),
    ]
    args += [mod, g, w_main, w_gate]
    out_shape = [jax.ShapeDtypeStruct((T, N_MAIN), F32), jax.ShapeDtypeStruct((N_GATE, T), F32)]
    out_specs = [pl.BlockSpec((tm, N_MAIN), lambda i: (i, 0)), pl.BlockSpec((N_GATE, tm), lambda i: (0, i))]
    if with_pos:
        out_shape.append(jax.ShapeDtypeStruct((T, D_MODEL), F32))
        out_specs.append(row_spec)
    return pl.pallas_call(
        functools.partial(_inproj_kernel, with_pos=with_pos),
        out_shape=out_shape,
        grid=(nblk,),
        in_specs=in_specs,
        out_specs=out_specs,
        compiler_params=pltpu.CompilerParams(
            dimension_semantics=("arbitrary",), vmem_limit_bytes=VMEM_LIMIT),
        name="inproj",
    )(*args)


def _filter_kernel(z_ref, dec_ref, w1_ref, b1_ref, w2_ref, b2_ref, w3_ref, f_ref,
                   ch_ref, cl_ref, sh_ref, sl_ref, hre_ref, him_ref, *, L):
    f = f_ref[...]
    hdn = jnp.sin(f * (_dot3(z_ref[...], w1_ref[...]) + b1_ref[...]))
    hdn = jnp.sin(f * (_dot3(hdn, w2_ref[...]) + b2_ref[...]))
    h = _dot3(hdn, w3_ref[...])
    dec = dec_ref[...]
    row = lax.broadcasted_iota(jnp.int32, (L, HY_W), 0)
    hp, hm = [], []
    for o in range(HY_ORDER):
        hf = h[:, (2 * o) * HY_W:(2 * o + 1) * HY_W] * dec
        hb = jnp.where(row == 0, 0.0, h[:, (2 * o + 1) * HY_W:(2 * o + 2) * HY_W] * dec)
        nrm = jnp.sum(jnp.abs(hf), axis=0, keepdims=True) + jnp.sum(jnp.abs(hb), axis=0, keepdims=True)
        inv = 1.0 / nrm
        hp.append((hf + hb) * inv)
        hm.append((hf - hb) * inv)
    hp = jnp.concatenate(hp, axis=1)
    hm = jnp.concatenate(hm, axis=1)
    hre_ref[...] = _dot3_tab(ch_ref[...], cl_ref[...], hp)
    him_ref[...] = -_dot3_tab(sh_ref[...], sl_ref[...], hm)


def _filter_spectrum(L, w1p, b1p, w2p, b2p, w3p, fp):
    z_np, dec_np = _filter_consts_np(L)
    c1, s1, _, _ = _dft_tables_np(L)
    ch, cl = _hi_lo(c1)
    sh, sl = _hi_lo(s1)
    n = HY_ORDER * HY_W
    full = lambda shape: pl.BlockSpec(shape, lambda i: tuple(0 for _ in shape))
    return pl.pallas_call(
        functools.partial(_filter_kernel, L=L),
        out_shape=[jax.ShapeDtypeStruct((L, n), F32), jax.ShapeDtypeStruct((L, n), F32)],
        grid=(1,),
        in_specs=[full((L, LANES)), full((L, HY_W)), full((LANES, LANES)), full((1, LANES)),
                  full((LANES, LANES)), full((1, LANES)), full((LANES, 2 * n)), full((1, LANES)),
                  full((L, L)), full((L, L)), full((L, L)), full((L, L))],
        out_specs=[full((L, n)), full((L, n))],
        compiler_params=pltpu.CompilerParams(
            dimension_semantics=("arbitrary",), vmem_limit_bytes=VMEM_LIMIT),
        name="hyena_filter",
    )(jnp.asarray(z_np), jnp.asarray(dec_np), w1p, b1p, w2p, b2p, w3p, fp, ch, cl, sh, sl)


def _schy_kernel(sc_ref, hy_ref, scw_ref, hyw_ref, skip_ref, ch_ref, cl_ref, sh_ref, sl_ref,
                 hre_ref, him_ref, out_ref, *, L):
    sc = sc_ref[...]
    b_g = sc[:, 0:SC_W]
    c_g = sc[:, SC_W:2 * SC_W]
    xin = sc[:, 2 * SC_W:3 * SC_W]
    out_ref[:, 0:SC_W] = b_g * _conv3(c_g * xin, scw_ref[...])

    u3 = _conv3(hy_ref[...], hyw_ref[...])
    z = u3[:, 0:HY_W]
    gates = (u3[:, HY_W:2 * HY_W], u3[:, 2 * HY_W:3 * HY_W])
    ch, cl, sh, sl = ch_ref[...], cl_ref[...], sh_ref[...], sl_ref[...]
    inv_l = 1.0 / L
    for o in range(HY_ORDER):
        zh, zl = _split2(z)
        zc = _dot(ch, zh) + _dot(cl, zh) + _dot(ch, zl)
        zs = _dot(sh, zh) + _dot(sl, zh) + _dot(sh, zl)
        hre = hre_ref[:, o * HY_W:(o + 1) * HY_W]
        him = him_ref[:, o * HY_W:(o + 1) * HY_W]
        yre = zc * hre + zs * him
        yim = zc * him - zs * hre
        y = (_dot3_tab(ch, cl, yre) - _dot3_tab(sh, sl, yim)) * inv_l
        z = gates[o] * (y + skip_ref[o:o + 1, :] * z)
    out_ref[:, SC_W:SC_W + HY_W] = z


def _schy(proj, sc_w, hy_w, skip, hre, him, *, B, L):
    _, _, c2, s2 = _dft_tables_np(L)
    ch, cl = _hi_lo(c2)
    sh, sl = _hi_lo(s2)
    n = HY_ORDER * HY_W
    const = _const_spec
    return pl.pallas_call(
        functools.partial(_schy_kernel, L=L),
        out_shape=jax.ShapeDtypeStruct((B * L, SC_W + HY_W), F32),
        grid=(B,),
        in_specs=[
            pl.BlockSpec((L, 3 * SC_W), lambda b: (b, 0)),
            pl.BlockSpec((L, 3 * HY_W), lambda b: (b, 1)),
            const((3, SC_W)), const((3, 3 * HY_W)), const((HY_ORDER, HY_W)),
            const((L, L)), const((L, L)), const((L, L)), const((L, L)),
            const((L, n)), const((L, n)),
        ],
        out_specs=pl.BlockSpec((L, SC_W + HY_W), lambda b: (b, 0)),
        compiler_params=pltpu.CompilerParams(
            dimension_semantics=("arbitrary",), vmem_limit_bytes=VMEM_LIMIT),
        name="sconv_hyena",
    )(proj, proj, sc_w, hy_w, skip, ch, cl, sh, sl, hre, him)


def _gate_prep_kernel(*refs, B, L, with_state):
    if with_state:
        g_ref, gb_ref, m0_ref, rows_ref, cols_ref = refs
    else:
        g_ref, gb_ref, rows_ref, cols_ref = refs
    R = B * N_GATE
    x = jnp.concatenate([g_ref[:, b * L:(b + 1) * L] for b in range(B)], axis=0) + gb_ref[:, 0:1]
    rr = lax.broadcasted_iota(jnp.int32, (R, L), 0) & (N_GATE - 1)
    is_f = ((rr >= ML_H) & (rr < 2 * ML_H)) | (rr >= 3 * ML_H)
    fwd_rows = rr < 2 * ML_H
    pk = jnp.where(is_f, _log_sigmoid(x), x)
    r_i = lax.broadcasted_iota(jnp.int32, (L, L), 0)
    c_i = lax.broadcasted_iota(jnp.int32, (L, L), 1)
    triu = jnp.where(r_i <= c_i, 1.0, 0.0).astype(BF16)
    cs = _dot(jnp.concatenate(_split3(pk), axis=0), triu)
    cum = cs[0:R] + cs[R:2 * R] + cs[2 * R:3 * R]
    suf = cum[:, L - 1:L] - cum + pk
    bsum = pltpu.roll(jnp.where(fwd_rows, cum, suf), R - ML_H, axis=0)
    r = pk - bsum
    cm = jnp.where(fwd_rows, _scan_max(r, reverse=False), _scan_max(r, reverse=True))
    m = jnp.maximum(m0_ref[:, 0:1], cm) if with_state else jnp.maximum(cm, 0.0)
    rl = r * LOG2E
    ml = m * LOG2E
    mt = bsum + m
    pieces = [rl, ml, jnp.exp(-mt), mt]
    if not with_state:
        pieces.append(jnp.exp2(rl - jnp.where(fwd_rows, ml[:, L - 1:L], ml[:, 0:1])))
    pad = jnp.zeros((LANES - N_GATE, L), F32)
    for b in range(B):
        bs = slice(b * N_GATE, (b + 1) * N_GATE)
        for j, piece in enumerate(pieces):
            rows_ref[b, j * N_GATE:(j + 1) * N_GATE, :] = piece[bs]
        cols_ref[b * L:(b + 1) * L, :] = jnp.concatenate([rl[bs], pad], axis=0).T


def _gate_prep(g_t, gate_b, m0, *, B, L):
    with_state = m0 is not None
    n_rows = (4 if with_state else 5) * N_GATE
    full = lambda shape: pl.BlockSpec(shape, lambda i: tuple(0 for _ in shape))
    args = [g_t, jnp.tile(gate_b, (B, 1))]
    in_specs = [full((N_GATE, B * L)), full((B * N_GATE, LANES))]
    if with_state:
        args.append(m0.reshape(B * N_GATE, LANES))
        in_specs.append(full((B * N_GATE, LANES)))
    return pl.pallas_call(
        functools.partial(_gate_prep_kernel, B=B, L=L, with_state=with_state),
        out_shape=[jax.ShapeDtypeStruct((B, n_rows, L), F32), jax.ShapeDtypeStruct((B * L, LANES), F32)],
        grid=(1,),
        in_specs=in_specs,
        out_specs=[full((B, n_rows, L)), full((B * L, LANES))],
        compiler_params=pltpu.CompilerParams(
            dimension_semantics=("arbitrary",), vmem_limit_bytes=VMEM_LIMIT),
        name="gate_prep",
    )(*args)


def _mlstm_kernel(*refs, L, with_state):
    if with_state:
        (q_ref, k_ref, v_ref, o_ref, rows_ref, cols_ref, cw_ref, ng_ref,
         c0_ref, n0_ref, m0_ref, y_ref) = refs
    else:
        (q_ref, k_ref, v_ref, o_ref, rows_ref, cols_ref, cw_ref, ng_ref,
         y_ref, cout_ref, nout_ref, mout_ref) = refs

    tb = min(L // 2, ATT_BLOCK)
    cw = cw_ref[...]
    q_all = _silu(_conv3(q_ref[...], cw[:, 0:ML_W]))
    k_all = _silu(_conv3(k_ref[...], cw[:, ML_W:2 * ML_W])) * (ML_DH ** -0.5)

    rl16, ml16, em16, mt16 = (rows_ref[0, j * N_GATE:(j + 1) * N_GATE, :] for j in range(4))
    rl_cols = cols_ref[...]
    if not with_state:
        w16 = rows_ref[0, 4 * N_GATE:5 * N_GATE, :]

    s_loc = lax.broadcasted_iota(jnp.int32, (tb, tb), 0)
    t_loc = lax.broadcasted_iota(jnp.int32, (tb, tb), 1)
    masks = (s_loc <= t_loc, s_loc >= t_loc)
    ones_rows = jnp.ones((2 * 8, L), BF16)
    ng = ng_ref[...]
    for h in range(ML_H):
        hs = slice(h * ML_DH, (h + 1) * ML_DH)
        q = q_all[:, hs]
        k = k_all[:, hs]
        qb = q.astype(BF16)
        kb = k.astype(BF16)
        vb = v_ref[:, hs].astype(BF16)
        vext_t = jnp.concatenate([v_ref[:, hs].T.astype(BF16), ones_rows], axis=0)
        rlb = [jnp.broadcast_to(rl_cols[:, 2 * ML_H * d + h:2 * ML_H * d + h + 1], (L, tb)) for d in range(2)]
        if with_state:
            c0_t = [c0_ref[0, d, h].T.astype(BF16) for d in range(2)]
        for r0 in range(0, L, tb):
            rs = slice(r0, r0 + tb)
            s_t = _dot_nt(kb, qb[rs])
            if with_state:
                qn_rows = _dot3_nt(n0_ref[0], q[rs])
            hout_t = None
            for d in range(2):
                o8 = 2 * ML_H * d + h
                ml_row = ml16[o8:o8 + 1, rs]
                e_diag = jnp.exp2(jnp.where(masks[d], rlb[d][rs] - ml_row, NEG_BIG))
                acc = _dot(vext_t[:, rs], (s_t[rs] * e_diag).astype(BF16))
                side = slice(r0 + tb, L) if d else slice(0, r0)
                if side.stop > side.start:
                    e_side = jnp.exp2(rlb[d][side] - ml_row)
                    acc = acc + _dot(vext_t[:, side], (s_t[side] * e_side).astype(BF16))
                num_t = acc[0:ML_DH]
                den = acc[ML_DH:ML_DH + 1]
                if with_state:
                    wp = jnp.exp2(m0_ref[0, o8:o8 + 1, 0:1] * LOG2E - ml_row)
                    den = den + wp * qn_rows[d * ML_H + h:d * ML_H + h + 1]
                    num_t = num_t + wp * _dot_nt(c0_t[d], qb[rs])
                contrib = num_t * (1.0 / jnp.maximum(jnp.abs(den), em16[o8:o8 + 1, rs]))
                hout_t = contrib if hout_t is None else hout_t + contrib
            hn_t = hout_t * lax.rsqrt(jnp.mean(hout_t * hout_t, axis=0, keepdims=True) + EPS)
            y_ref[rs, hs] = _sigmoid(o_ref[rs, hs]) * (hn_t.T * ng[:, hs])

        if not with_state:
            k_t = k.T
            n_rows = _dot3(w16, k)
            for d in range(2):
                o8 = 2 * ML_H * d + h
                end = 0 if d else L - 1
                cout_ref[0, d, h] = _dot((k_t * w16[o8:o8 + 1, :]).astype(BF16), vb)
                nout_ref[0, d * ML_H + h:d * ML_H + h + 1, :] = n_rows[o8:o8 + 1, :]
                mout_ref[0, d * ML_H + h:d * ML_H + h + 1, :] = jnp.broadcast_to(
                    mt16[o8:o8 + 1, end:end + 1], (1, LANES))


def _mlstm(proj, g_t, gate_b, conv_w, norm_g, state, *, B, L):
    with_state = state is not None
    rows, cols = _gate_prep(g_t, gate_b, state[2] if with_state else None, B=B, L=L)
    col0 = (3 * SC_W + 3 * HY_W) // ML_W
    const = _const_spec
    in_specs = [
        pl.BlockSpec((L, ML_W), lambda b: (b, col0)),
        pl.BlockSpec((L, ML_W), lambda b: (b, col0 + 1)),
        pl.BlockSpec((L, ML_W), lambda b: (b, col0 + 2)),
        pl.BlockSpec((L, ML_W), lambda b: (b, col0 + 3)),
        pl.BlockSpec((1, rows.shape[1], L), lambda b: (b, 0, 0)),
        pl.BlockSpec((L, LANES), lambda b: (b, 0)),
        const((3, 2 * ML_W)), const((1, ML_W)),
    ]
    args = [proj, proj, proj, proj, rows, cols, conv_w, norm_g]
    y_shape = jax.ShapeDtypeStruct((B * L, ML_W), F32)
    y_spec = pl.BlockSpec((L, ML_W), lambda b: (b, 0))
    if with_state:
        c0, n0, m0 = state
        in_specs += [
            pl.BlockSpec((1, 2, ML_H, ML_DH, ML_DH), lambda b: (b, 0, 0, 0, 0)),
            pl.BlockSpec((1, 2 * ML_H, ML_DH), lambda b: (b, 0, 0)),
            pl.BlockSpec((1, N_GATE, LANES), lambda b: (b, 0, 0)),
        ]
        args += [c0, n0, m0]
        out_shape, out_specs = y_shape, y_spec
    else:
        out_shape = [y_shape,
                     jax.ShapeDtypeStruct((B, 2, ML_H, ML_DH, ML_DH), F32),
                     jax.ShapeDtypeStruct((B, 2 * ML_H, ML_DH), F32),
                     jax.ShapeDtypeStruct((B, 2 * ML_H, LANES), F32)]
        out_specs = [y_spec,
                     pl.BlockSpec((1, 2, ML_H, ML_DH, ML_DH), lambda b: (b, 0, 0, 0, 0)),
                     pl.BlockSpec((1, 2 * ML_H, ML_DH), lambda b: (b, 0, 0)),
                     pl.BlockSpec((1, 2 * ML_H, LANES), lambda b: (b, 0, 0))]
    return pl.pallas_call(
        functools.partial(_mlstm_kernel, L=L, with_state=with_state),
        out_shape=out_shape,
        grid=(B,),
        in_specs=in_specs,
        out_specs=out_specs,
        compiler_params=pltpu.CompilerParams(
            dimension_semantics=("arbitrary",), vmem_limit_bytes=VMEM_LIMIT),
        name="mlstm",
    )(*args)


def _outmlp_kernel(x_ref, ya_ref, yb_ref, mod_ref, wo_ref, g2_ref, w1_ref, w2_ref, gf_ref, o_ref, *, final):
    mod = mod_ref[0]
    g1 = mod[:, 2 * D_MODEL:3 * D_MODEL]
    sh2 = mod[:, 3 * D_MODEL:4 * D_MODEL]
    sc2 = mod[:, 4 * D_MODEL:5 * D_MODEL]
    g2 = mod[:, 5 * D_MODEL:6 * D_MODEL]
    na = SC_W + HY_W
    attn = _dot(ya_ref[...].astype(BF16), wo_ref[0:na, :]) + _dot(yb_ref[...].astype(BF16), wo_ref[na:, :])
    x = x_ref[...] + g1 * attn
    u = ((_rms(x) * g2_ref[...]) * (1.0 + sc2) + sh2).astype(BF16)
    tf = 1024
    acc = jnp.zeros(x.shape, F32)
    for j in range(D_FF // tf):
        hcol = _dot(u, w1_ref[:, j * tf:(j + 1) * tf])
        hcol = jnp.square(jnp.maximum(hcol, 0.0)).astype(BF16)
        acc = acc + _dot(hcol, w2_ref[j * tf:(j + 1) * tf, :])
    x = x + g2 * acc
    if final:
        x = _rms(x) * gf_ref[...]
    o_ref[...] = x


def _outmlp(x2d, ya, yb, mod, w_out, g2, w1, w2, gf, *, L, final):
    T = x2d.shape[0]
    tm = ROW_BLOCK
    per_seq = L // tm
    const = _const_spec
    nmod = mod.shape[0]
    return pl.pallas_call(
        functools.partial(_outmlp_kernel, final=final),
        out_shape=jax.ShapeDtypeStruct((T, D_MODEL), F32),
        grid=(T // tm,),
        in_specs=[
            pl.BlockSpec((tm, D_MODEL), lambda i: (i, 0)),
            pl.BlockSpec((tm, SC_W + HY_W), lambda i: (i, 0)),
            pl.BlockSpec((tm, ML_W), lambda i: (i, 0)),
            pl.BlockSpec((1, 1, 6 * D_MODEL), lambda i: ((i // per_seq) % nmod, 0, 0)),
            const((D_MODEL, D_MODEL)), const((1, D_MODEL)),
            const((D_MODEL, D_FF)), const((D_FF, D_MODEL)), const((1, D_MODEL)),
        ],
        out_specs=pl.BlockSpec((tm, D_MODEL), lambda i: (i, 0)),
        compiler_params=pltpu.CompilerParams(
            dimension_semantics=("arbitrary",), vmem_limit_bytes=VMEM_LIMIT),
        name="outproj_mlp",
    )(x2d, ya, yb, mod, w_out, g2, w1, w2, gf)


def kernel(x_prompt, x_sample, state_C, state_n, state_m, c, c_ctx, norm1_g, ada_w, ada_b, w_in, sc_conv_w, hy_conv_w, hy_w1, hy_b1, hy_w2, hy_b2, hy_w3, hy_freq, hy_skip, ml_conv_w, ml_gate_b, ml_norm_g, w_out, norm2_g, mlp_w1, mlp_w2, norm_f_g):
    B, L = x_prompt.shape[0], x_prompt.shape[1]
    Bd, Ld = x_sample.shape[0], x_sample.shape[1]

    cond = jnp.concatenate([c_ctx[None, :], c, jnp.zeros((8 - 1 - Bd, D_MODEL), F32)], axis=0)
    mods = _ada(cond, ada_w, ada_b)

    w_main = w_in[:, :, :N_MAIN].astype(BF16)
    w_gate = jnp.swapaxes(w_in[:, :, N_MAIN:], 1, 2).astype(BF16)
    w_out_b = w_out.astype(BF16)
    w1_b = mlp_w1.astype(BF16)
    w2_b = mlp_w2.astype(BF16)
    gate_b = jnp.broadcast_to(ml_gate_b[:, :, None], (DEPTH, N_GATE, LANES))
    pad_h = LANES - HY_HID
    w1p = jnp.pad(hy_w1, ((0, 0), (0, LANES - HY_EMB), (0, pad_h)))
    b1p = jnp.pad(hy_b1, ((0, 0), (0, pad_h)))
    w2p = jnp.pad(hy_w2, ((0, 0), (0, pad_h), (0, pad_h)))
    b2p = jnp.pad(hy_b2, ((0, 0), (0, pad_h)))
    w3p = jnp.pad(hy_w3, ((0, 0), (0, pad_h), (0, 0)))
    fp = jnp.pad(hy_freq, ((0, 0), (0, pad_h)))
    pos = jnp.asarray(_pos_embed_np(Ld))
    m0_all = jnp.pad(state_m, ((0, 0), (0, 0), (0, 0), (0, ML_H))).reshape(Bd, DEPTH, N_GATE, 1)
    m0_all = jnp.broadcast_to(m0_all, (Bd, DEPTH, N_GATE, LANES))
    n0_all = state_n.reshape(Bd, DEPTH, 2 * ML_H, ML_DH)
    gf = norm_f_g[None, :]

    xp = x_prompt.reshape(B * L, D_MODEL)
    xs = x_sample.reshape(Bd * Ld, D_MODEL)
    Cs, ns, ms = [], [], []
    for l in range(DEPTH):
        final = l == DEPTH - 1
        filt_args = (w1p[l], b1p[l][None], w2p[l], b2p[l][None], w3p[l], fp[l][None])
        g1 = norm1_g[l][None]
        g2 = norm2_g[l][None]

        mod_c = mods[l, 0:1][:, None, :]
        proj, gates = _inproj(xp, None, mod_c, g1, w_main[l], w_gate[l], L=L)
        hre, him = _filter_spectrum(L, *filt_args)
        ya = _schy(proj, sc_conv_w[l], hy_conv_w[l], hy_skip[l], hre, him, B=B, L=L)
        yb, C, n, m = _mlstm(proj, gates, gate_b[l], ml_conv_w[l], ml_norm_g[l][None], None, B=B, L=L)
        xp = _outmlp(xp, ya, yb, mod_c, w_out_b[l], g2, w1_b[l], w2_b[l], gf, L=L, final=final)
        Cs.append(C)
        ns.append(n.reshape(B, 2, ML_H, ML_DH))
        ms.append(m[:, :, 0].reshape(B, 2, ML_H))

        mod_s = mods[l, 1:1 + Bd][:, None, :]
        res = _inproj(xs, pos if l == 0 else None, mod_s, g1, w_main[l], w_gate[l], L=Ld)
        if l == 0:
            proj, gates, xs = res
        else:
            proj, gates = res
        hre, him = _filter_spectrum(Ld, *filt_args)
        ya = _schy(proj, sc_conv_w[l], hy_conv_w[l], hy_skip[l], hre, him, B=Bd, L=Ld)
        state = (state_C[:, l], n0_all[:, l], m0_all[:, l])
        yb = _mlstm(proj, gates, gate_b[l], ml_conv_w[l], ml_norm_g[l][None], state, B=Bd, L=Ld)
        xs = _outmlp(xs, ya, yb, mod_s, w_out_b[l], g2, w1_b[l], w2_b[l], gf, L=Ld, final=final)

    y_prompt = xp.reshape(B, L, D_MODEL)
    y_sample = xs.reshape(Bd, Ld, D_MODEL)
    return (y_prompt, y_sample, jnp.stack(Cs, axis=1), jnp.stack(ns, axis=1), jnp.stack(ms, axis=1))
```

```python
import functools
import math

import numpy as np
import jax
import jax.numpy as jnp
from jax import lax
from jax.experimental import pallas as pl
from jax.experimental.pallas import tpu as pltpu

F32 = jnp.float32
BF16 = jnp.bfloat16

D_MODEL = 1024
DEPTH = 2
GRID_W = 64
SC_W = 256
HY_W = 256
ML_W = 512
ML_H = 4
ML_DH = ML_W // ML_H
D_FF = 4 * D_MODEL
HY_ORDER = 2
HY_BANDS = 16
HY_EMB = 1 + 2 * HY_BANDS
HY_HID = 64
HY_DECAY_SHORT = 0.3
HY_DECAY_LONG = 1.5
HY_DECAY_TARGET = 1e-2
EPS = 1e-6
N_MAIN = 3 * SC_W + 3 * HY_W + 4 * ML_W
N_GATE = 4 * ML_H
LANES = 128
ROW_BLOCK = 512
ATT_BLOCK = 256
NEG_BIG = -1e30
LOG2E = math.log2(math.e)
VMEM_LIMIT = 60 * 1024 * 1024


def _dot(a, b):
    return jnp.dot(a, b, preferred_element_type=F32)


def _dot_nt(a, b):
    return lax.dot_general(a, b, (((1,), (1,)), ((), ())), preferred_element_type=F32)


def _split2(a):
    hi = a.astype(BF16)
    lo = (a - hi.astype(F32)).astype(BF16)
    return hi, lo


def _split3(a):
    hi = a.astype(BF16)
    r = a - hi.astype(F32)
    mid = r.astype(BF16)
    lo = (r - mid.astype(F32)).astype(BF16)
    return hi, mid, lo


def _dot3(a, b):
    ah, al = _split2(a)
    bh, bl = _split2(b)
    return _dot(ah, bh) + _dot(al, bh) + _dot(ah, bl)


def _dot3_nt(a, b):
    ah, al = _split2(a)
    bh, bl = _split2(b)
    return _dot_nt(ah, bh) + _dot_nt(al, bh) + _dot_nt(ah, bl)


def _dot3_tab(th, tl, b):
    bh, bl = _split2(b)
    return _dot(th, bh) + _dot(tl, bh) + _dot(th, bl)


def _sigmoid(x):
    return 1.0 / (1.0 + jnp.exp(-x))


def _silu(x):
    return x * _sigmoid(x)


def _log_sigmoid(x):
    return jnp.minimum(x, 0.0) - jnp.log(1.0 + jnp.exp(-jnp.abs(x)))


def _conv3(x, w):
    L = x.shape[0]
    row = lax.broadcasted_iota(jnp.int32, x.shape, 0)
    xm = jnp.where(row == 0, 0.0, pltpu.roll(x, 1, axis=0))
    xp = jnp.where(row == L - 1, 0.0, pltpu.roll(x, L - 1, axis=0))
    return xm * w[0:1, :] + x * w[1:2, :] + xp * w[2:3, :]


def _rms(x):
    return x * lax.rsqrt(jnp.mean(x * x, axis=-1, keepdims=True) + EPS)


def _scan_max(x, *, reverse):
    n = x.shape[1]
    lane = lax.broadcasted_iota(jnp.int32, x.shape, 1)
    d = 1
    while d < n:
        if reverse:
            shifted = jnp.where(lane < n - d, pltpu.roll(x, n - d, axis=1), NEG_BIG)
        else:
            shifted = jnp.where(lane >= d, pltpu.roll(x, d, axis=1), NEG_BIG)
        x = jnp.maximum(x, shifted)
        d *= 2
    return x


def _hi_lo(a64):
    a = a64.astype(np.float32)
    hi = a.astype(BF16)
    lo = (a - hi.astype(np.float32)).astype(BF16)
    return jnp.asarray(hi), jnp.asarray(lo)


def _const_spec(shape):
    return pl.BlockSpec(shape, lambda *_: tuple(0 for _ in shape), pipeline_mode=pl.Buffered(1))


def _layer_spec(shape, l):
    return pl.BlockSpec((None,) + tuple(shape), lambda *_: (l,) + tuple(0 for _ in shape),
                        pipeline_mode=pl.Buffered(1))


@functools.lru_cache(maxsize=None)
def _dft_tables_np(L):
    k = np.arange(L, dtype=np.int64)
    m1 = ((2 * k[:, None] + 1) * k[None, :]) % (4 * L)
    a1 = np.pi * m1.astype(np.float64) / (2 * L)
    m2 = ((2 * k[:, None] + 1) * (2 * k[None, :] + 1)) % (8 * L)
    a2 = np.pi * m2.astype(np.float64) / (4 * L)
    return np.cos(a1), np.sin(a1), np.cos(a2), np.sin(a2)


@functools.lru_cache(maxsize=None)
def _filter_consts_np(L):
    t_idx = np.arange(L, dtype=np.float64)
    t = t_idx / (L - 1)
    bands = np.arange(1, HY_BANDS + 1, dtype=np.float64)
    ang = 2.0 * math.pi * t_idx[:, None] * bands[None, :] / L
    z = np.zeros((L, LANES), np.float64)
    z[:, 0] = t
    z[:, 1:1 + HY_BANDS] = np.cos(ang)
    z[:, 1 + HY_BANDS:1 + 2 * HY_BANDS] = -np.sin(ang)
    lin = np.linspace(math.log(HY_DECAY_TARGET) / HY_DECAY_LONG,
                      math.log(HY_DECAY_TARGET) / HY_DECAY_SHORT, HY_W).astype(np.float32)
    deltas = np.abs(lin).astype(np.float64)
    decay = np.exp(-t[:, None] * deltas[None, :])
    return z.astype(np.float32), decay.astype(np.float32)


@functools.lru_cache(maxsize=None)
def _pos_embed_np(L):
    rows = L // GRID_W
    r, cidx = np.meshgrid(np.arange(rows, dtype=np.float64), np.arange(GRID_W, dtype=np.float64), indexing="ij")
    r = r.reshape(-1)
    cidx = cidx.reshape(-1)
    quarter = D_MODEL // 4
    omega = 1.0 / (10000.0 ** (np.arange(quarter, dtype=np.float64) / quarter))
    ar = r[:, None] * omega[None, :]
    ac = cidx[:, None] * omega[None, :]
    return np.concatenate([np.sin(ar), np.cos(ar), np.sin(ac), np.cos(ac)], axis=-1).astype(np.float32)


def _ada_kernel(c_ref, w_ref, b_ref, o_ref):
    sc = _silu(c_ref[...])
    o_ref[0] = _dot3(sc, w_ref[0]) + b_ref[0]


def _ada(cond, ada_w, ada_b):
    tn = 1536
    nd = 6 * D_MODEL
    return pl.pallas_call(
        _ada_kernel,
        out_shape=jax.ShapeDtypeStruct((DEPTH, 8, nd), F32),
        grid=(DEPTH, nd // tn),
        in_specs=[
            pl.BlockSpec((8, D_MODEL), lambda l, j: (0, 0)),
            pl.BlockSpec((1, D_MODEL, tn), lambda l, j: (l, 0, j)),
            pl.BlockSpec((1, 1, tn), lambda l, j: (l, 0, j)),
        ],
        out_specs=pl.BlockSpec((1, 8, tn), lambda l, j: (l, 0, j)),
        compiler_params=pltpu.CompilerParams(
            dimension_semantics=("arbitrary", "arbitrary"), vmem_limit_bytes=VMEM_LIMIT),
        name="ada_mod",
    )(cond, ada_w, ada_b.reshape(DEPTH, 1, nd))


def _inproj_kernel(*refs, with_pos):
    if with_pos:
        x_ref, pos_ref, mod_ref, g_ref, w_ref, wg_ref, proj_ref, gate_ref, xo_ref = refs
        x = x_ref[...] + pos_ref[...]
        xo_ref[...] = x
    else:
        x_ref, mod_ref, g_ref, w_ref, wg_ref, proj_ref, gate_ref = refs
        x = x_ref[...]
    mod = mod_ref[0]
    sh1 = mod[:, 0:D_MODEL]
    sc1 = mod[:, D_MODEL:2 * D_MODEL]
    u = (_rms(x) * g_ref[...]) * (1.0 + sc1) + sh1
    ub = u.astype(BF16)
    tn = 512
    for j in range(N_MAIN // tn):
        proj_ref[:, j * tn:(j + 1) * tn] = _dot(ub, w_ref[:, j * tn:(j + 1) * tn])
    gate_ref[...] = _dot_nt(wg_ref[...], ub)


def _mod_spec(l, L, tm, per_request):
    if per_request:
        return pl.BlockSpec((1, 1, 6 * D_MODEL), lambda i: (l * 8 + 1 + (i * tm) // L, 0, 0))
    return pl.BlockSpec((1, 1, 6 * D_MODEL), lambda i: (l * 8, 0, 0))


def _inproj(x2d, pos, mods, g, w_in_b, w_gate, *, l, L, per_request):
    T = x2d.shape[0]
    tm = ROW_BLOCK
    nblk = T // tm
    per_seq = L // tm
    with_pos = pos is not None
    row_spec = pl.BlockSpec((tm, D_MODEL), lambda i: (i, 0))
    in_specs = [row_spec]
    args = [x2d]
    if with_pos:
        in_specs.append(pl.BlockSpec((tm, D_MODEL), lambda i: (i % per_seq, 0)))
        args.append(pos)
    in_specs += [
        _mod_spec(l, L, tm, per_request),
        _layer_spec((1, D_MODEL), l),
        _layer_spec((D_MODEL, N_MAIN), l),
        _layer_spec((N_GATE, D_MODEL), l),
    ]
    args += [mods, g, w_in_b, w_gate]
    out_shape = [jax.ShapeDtypeStruct((T, N_MAIN), F32), jax.ShapeDtypeStruct((N_GATE, T), F32)]
    out_specs = [pl.BlockSpec((tm, N_MAIN), lambda i: (i, 0)), pl.BlockSpec((N_GATE, tm), lambda i: (0, i))]
    if with_pos:
        out_shape.append(jax.ShapeDtypeStruct((T, D_MODEL), F32))
        out_specs.append(row_spec)
    return pl.pallas_call(
        functools.partial(_inproj_kernel, with_pos=with_pos),
        out_shape=out_shape,
        grid=(nblk,),
        in_specs=in_specs,
        out_specs=out_specs,
        compiler_params=pltpu.CompilerParams(
            dimension_semantics=("arbitrary",), vmem_limit_bytes=VMEM_LIMIT),
        name="inproj",
    )(*args)


def _filter_kernel(z_ref, dec_ref, w1_ref, b1_ref, w2_ref, b2_ref, w3_ref, f_ref,
                   ch_ref, cl_ref, sh_ref, sl_ref, hre_ref, him_ref, *, L):
    f = f_ref[...]
    hdn = jnp.sin(f * (_dot3(z_ref[...], w1_ref[...]) + b1_ref[...]))
    hdn = jnp.sin(f * (_dot3(hdn, w2_ref[...]) + b2_ref[...]))
    h = _dot3(hdn, w3_ref[...])
    dec = dec_ref[...]
    row = lax.broadcasted_iota(jnp.int32, (L, HY_W), 0)
    hp, hm = [], []
    for o in range(HY_ORDER):
        hf = h[:, (2 * o) * HY_W:(2 * o + 1) * HY_W] * dec
        hb = jnp.where(row == 0, 0.0, h[:, (2 * o + 1) * HY_W:(2 * o + 2) * HY_W] * dec)
        nrm = jnp.sum(jnp.abs(hf), axis=0, keepdims=True) + jnp.sum(jnp.abs(hb), axis=0, keepdims=True)
        inv = 1.0 / nrm
        hp.append((hf + hb) * inv)
        hm.append((hf - hb) * inv)
    hp = jnp.concatenate(hp, axis=1)
    hm = jnp.concatenate(hm, axis=1)
    hre_ref[...] = _dot3_tab(ch_ref[...], cl_ref[...], hp)
    him_ref[...] = -_dot3_tab(sh_ref[...], sl_ref[...], hm)


def _filter_spectrum(L, w1p, b1p, w2p, b2p, w3p, fp):
    z_np, dec_np = _filter_consts_np(L)
    c1, s1, _, _ = _dft_tables_np(L)
    ch, cl = _hi_lo(c1)
    sh, sl = _hi_lo(s1)
    n = HY_ORDER * HY_W
    const = _const_spec
    per_layer = lambda shape: pl.BlockSpec((None,) + shape, lambda l: (l,) + tuple(0 for _ in shape))
    return pl.pallas_call(
        functools.partial(_filter_kernel, L=L),
        out_shape=[jax.ShapeDtypeStruct((DEPTH, L, n), F32), jax.ShapeDtypeStruct((DEPTH, L, n), F32)],
        grid=(DEPTH,),
        in_specs=[const((L, LANES)), const((L, HY_W)), per_layer((LANES, LANES)), per_layer((1, LANES)),
                  per_layer((LANES, LANES)), per_layer((1, LANES)), per_layer((LANES, 2 * n)), per_layer((1, LANES)),
                  const((L, L)), const((L, L)), const((L, L)), const((L, L))],
        out_specs=[per_layer((L, n)), per_layer((L, n))],
        compiler_params=pltpu.CompilerParams(
            dimension_semantics=("arbitrary",), vmem_limit_bytes=VMEM_LIMIT),
        name="hyena_filter",
    )(jnp.asarray(z_np), jnp.asarray(dec_np), w1p, b1p, w2p, b2p, w3p, fp, ch, cl, sh, sl)


def _schy_kernel(sc_ref, hy_ref, scw_ref, hyw_ref, skip_ref, ch_ref, cl_ref, sh_ref, sl_ref,
                 hre_ref, him_ref, out_ref, *, L):
    sc = sc_ref[...]
    b_g = sc[:, 0:SC_W]
    c_g = sc[:, SC_W:2 * SC_W]
    xin = sc[:, 2 * SC_W:3 * SC_W]
    out_ref[:, 0:SC_W] = b_g * _conv3(c_g * xin, scw_ref[...])

    u3 = _conv3(hy_ref[...], hyw_ref[...])
    z = u3[:, 0:HY_W]
    gates = (u3[:, HY_W:2 * HY_W], u3[:, 2 * HY_W:3 * HY_W])
    ch, cl, sh, sl = ch_ref[...], cl_ref[...], sh_ref[...], sl_ref[...]
    inv_l = 1.0 / L
    for o in range(HY_ORDER):
        zh, zl = _split2(z)
        zc = _dot(ch, zh) + _dot(cl, zh) + _dot(ch, zl)
        zs = _dot(sh, zh) + _dot(sl, zh) + _dot(sh, zl)
        hre = hre_ref[:, o * HY_W:(o + 1) * HY_W]
        him = him_ref[:, o * HY_W:(o + 1) * HY_W]
        yre = zc * hre + zs * him
        yim = zc * him - zs * hre
        y = (_dot3_tab(ch, cl, yre) - _dot3_tab(sh, sl, yim)) * inv_l
        z = gates[o] * (y + skip_ref[o:o + 1, :] * z)
    out_ref[:, SC_W:SC_W + HY_W] = z


def _schy(proj, sc_w, hy_w, skip, hre, him, *, l, B, L):
    _, _, c2, s2 = _dft_tables_np(L)
    ch, cl = _hi_lo(c2)
    sh, sl = _hi_lo(s2)
    n = HY_ORDER * HY_W
    const = _const_spec
    return pl.pallas_call(
        functools.partial(_schy_kernel, L=L),
        out_shape=jax.ShapeDtypeStruct((B * L, SC_W + HY_W), F32),
        grid=(B,),
        in_specs=[
            pl.BlockSpec((L, 3 * SC_W), lambda b: (b, 0)),
            pl.BlockSpec((L, 3 * HY_W), lambda b: (b, 1)),
            _layer_spec((3, SC_W), l), _layer_spec((3, 3 * HY_W), l), _layer_spec((HY_ORDER, HY_W), l),
            const((L, L)), const((L, L)), const((L, L)), const((L, L)),
            _layer_spec((L, n), l), _layer_spec((L, n), l),
        ],
        out_specs=pl.BlockSpec((L, SC_W + HY_W), lambda b: (b, 0)),
        compiler_params=pltpu.CompilerParams(
            dimension_semantics=("arbitrary",), vmem_limit_bytes=VMEM_LIMIT),
        name="sconv_hyena",
    )(proj, proj, sc_w, hy_w, skip, ch, cl, sh, sl, hre, him)


def _gate_prep_kernel(*refs, B, L, with_state):
    if with_state:
        g_ref, gb_ref, m0_ref, rows_ref, cols_ref = refs
    else:
        g_ref, gb_ref, rows_ref, cols_ref = refs
    R = B * N_GATE
    gb = gb_ref[:, 0:1]
    x = jnp.concatenate([g_ref[:, b * L:(b + 1) * L] + gb for b in range(B)], axis=0)
    rr = lax.broadcasted_iota(jnp.int32, (R, L), 0) & (N_GATE - 1)
    is_f = ((rr >= ML_H) & (rr < 2 * ML_H)) | (rr >= 3 * ML_H)
    fwd_rows = rr < 2 * ML_H
    pk = jnp.where(is_f, _log_sigmoid(x), x)
    r_i = lax.broadcasted_iota(jnp.int32, (L, L), 0)
    c_i = lax.broadcasted_iota(jnp.int32, (L, L), 1)
    triu = jnp.where(r_i <= c_i, 1.0, 0.0).astype(BF16)
    cs = _dot(jnp.concatenate(_split3(pk), axis=0), triu)
    cum = cs[0:R] + cs[R:2 * R] + cs[2 * R:3 * R]
    suf = cum[:, L - 1:L] - cum + pk
    bsum = pltpu.roll(jnp.where(fwd_rows, cum, suf), R - ML_H, axis=0)
    r = pk - bsum
    cm = jnp.where(fwd_rows, _scan_max(r, reverse=False), _scan_max(r, reverse=True))
    if with_state:
        m0 = jnp.concatenate([m0_ref[b][:, 0:1] for b in range(B)], axis=0)
        m = jnp.maximum(m0, cm)
    else:
        m = jnp.maximum(cm, 0.0)
    rl = r * LOG2E
    ml = m * LOG2E
    mt = bsum + m
    pieces = [rl, ml, jnp.exp(-mt), mt]
    if not with_state:
        pieces.append(jnp.exp2(rl - jnp.where(fwd_rows, ml[:, L - 1:L], ml[:, 0:1])))
    pad = jnp.zeros((LANES - N_GATE, L), F32)
    for b in range(B):
        bs = slice(b * N_GATE, (b + 1) * N_GATE)
        for j, piece in enumerate(pieces):
            rows_ref[b, j * N_GATE:(j + 1) * N_GATE, :] = piece[bs]
        cols_ref[b * L:(b + 1) * L, :] = jnp.concatenate([rl[bs], pad], axis=0).T


def _gate_prep(g_t, gate_b, m0, *, l, B, L):
    with_state = m0 is not None
    n_rows = (4 if with_state else 5) * N_GATE
    full = lambda shape: pl.BlockSpec(shape, lambda i: tuple(0 for _ in shape))
    args = [g_t, gate_b]
    in_specs = [full((N_GATE, B * L)), pl.BlockSpec((None, N_GATE, LANES), lambda i: (l, 0, 0))]
    if with_state:
        args.append(m0)
        in_specs.append(pl.BlockSpec((B, None, N_GATE, LANES), lambda i: (0, l, 0, 0)))
    return pl.pallas_call(
        functools.partial(_gate_prep_kernel, B=B, L=L, with_state=with_state),
        out_shape=[jax.ShapeDtypeStruct((B, n_rows, L), F32), jax.ShapeDtypeStruct((B * L, LANES), F32)],
        grid=(1,),
        in_specs=in_specs,
        out_specs=[full((B, n_rows, L)), full((B * L, LANES))],
        compiler_params=pltpu.CompilerParams(
            dimension_semantics=("arbitrary",), vmem_limit_bytes=VMEM_LIMIT),
        name="gate_prep",
    )(*args)


def _mlstm_kernel(*refs, L, with_state, n_prev):
    if with_state:
        (q_ref, k_ref, v_ref, o_ref, rows_ref, cols_ref, cw_ref, ng_ref,
         c0_ref, n0_ref, m0_ref, y_ref) = refs
    elif n_prev:
        (q_ref, k_ref, v_ref, o_ref, rows_ref, cols_ref, cw_ref, ng_ref,
         cprev_ref, nprev_ref, mprev_ref, y_ref, cout_ref, nout_ref, mout_ref) = refs
        cout_ref[0, 0:n_prev] = cprev_ref[0]
        nout_ref[0, 0:n_prev] = nprev_ref[0]
        mout_ref[0, 0:n_prev] = mprev_ref[0]
        cout_ref, nout_ref, mout_ref = (r.at[:, n_prev] for r in (cout_ref, nout_ref, mout_ref))
    else:
        (q_ref, k_ref, v_ref, o_ref, rows_ref, cols_ref, cw_ref, ng_ref,
         y_ref, cout_ref, nout_ref, mout_ref) = refs

    tb = min(L // 2, ATT_BLOCK)
    cw = cw_ref[...]
    q_all = _silu(_conv3(q_ref[...], cw[:, 0:ML_W]))
    k_all = _silu(_conv3(k_ref[...], cw[:, ML_W:2 * ML_W])) * (ML_DH ** -0.5)

    rl16, ml16, em16, mt16 = (rows_ref[0, j * N_GATE:(j + 1) * N_GATE, :] for j in range(4))
    rl_cols = cols_ref[...]
    if not with_state:
        w16 = rows_ref[0, 4 * N_GATE:5 * N_GATE, :]

    s_loc = lax.broadcasted_iota(jnp.int32, (tb, tb), 0)
    t_loc = lax.broadcasted_iota(jnp.int32, (tb, tb), 1)
    masks = (s_loc <= t_loc, s_loc >= t_loc)
    ones_rows = jnp.ones((2 * 8, L), BF16)
    ng = ng_ref[...]
    for h in range(ML_H):
        hs = slice(h * ML_DH, (h + 1) * ML_DH)
        q = q_all[:, hs]
        k = k_all[:, hs]
        qb = q.astype(BF16)
        kb = k.astype(BF16)
        vb = v_ref[:, hs].astype(BF16)
        vext_t = jnp.concatenate([v_ref[:, hs].T.astype(BF16), ones_rows], axis=0)
        rlb = [jnp.broadcast_to(rl_cols[:, 2 * ML_H * d + h:2 * ML_H * d + h + 1], (L, tb)) for d in range(2)]
        if with_state:
            c0_t = [c0_ref[0, d, h].T.astype(BF16) for d in range(2)]
        for r0 in range(0, L, tb):
            rs = slice(r0, r0 + tb)
            s_t = _dot_nt(kb, qb[rs])
            if with_state:
                qn_rows = _dot3_nt(n0_ref[0], q[rs])
            hout_t = None
            for d in range(2):
                o8 = 2 * ML_H * d + h
                ml_row = ml16[o8:o8 + 1, rs]
                e_diag = jnp.exp2(jnp.where(masks[d], rlb[d][rs] - ml_row, NEG_BIG))
                acc = _dot(vext_t[:, rs], (s_t[rs] * e_diag).astype(BF16))
                side = slice(r0 + tb, L) if d else slice(0, r0)
                if side.stop > side.start:
                    e_side = jnp.exp2(rlb[d][side] - ml_row)
                    acc = acc + _dot(vext_t[:, side], (s_t[side] * e_side).astype(BF16))
                num_t = acc[0:ML_DH]
                den = acc[ML_DH:ML_DH + 1]
                if with_state:
                    wp = jnp.exp2(m0_ref[0, o8:o8 + 1, 0:1] * LOG2E - ml_row)
                    den = den + wp * qn_rows[d * ML_H + h:d * ML_H + h + 1]
                    num_t = num_t + wp * _dot_nt(c0_t[d], qb[rs])
                contrib = num_t * (1.0 / jnp.maximum(jnp.abs(den), em16[o8:o8 + 1, rs]))
                hout_t = contrib if hout_t is None else hout_t + contrib
            hn_t = hout_t * lax.rsqrt(jnp.mean(hout_t * hout_t, axis=0, keepdims=True) + EPS)
            y_ref[rs, hs] = _sigmoid(o_ref[rs, hs]) * (hn_t.T * ng[:, hs])

        if not with_state:
            k_t = k.T
            n_rows = _dot3(w16, k)
            for d in range(2):
                o8 = 2 * ML_H * d + h
                end = 0 if d else L - 1
                cout_ref[0, d, h] = _dot((k_t * w16[o8:o8 + 1, :]).astype(BF16), vb)
                nout_ref[0, d * ML_H + h:d * ML_H + h + 1, :] = n_rows[o8:o8 + 1, :]
                mout_ref[0, d * ML_H + h:d * ML_H + h + 1, :] = jnp.broadcast_to(
                    mt16[o8:o8 + 1, end:end + 1], (1, LANES))


def _mlstm(proj, g_t, gate_b, conv_w, norm_g, state, prev, *, l, B, L):
    with_state = state is not None
    n_prev = 0 if prev is None else l
    rows, cols = _gate_prep(g_t, gate_b, state[2] if with_state else None, l=l, B=B, L=L)
    col0 = (3 * SC_W + 3 * HY_W) // ML_W
    in_specs = [
        pl.BlockSpec((L, ML_W), lambda b: (b, col0)),
        pl.BlockSpec((L, ML_W), lambda b: (b, col0 + 1)),
        pl.BlockSpec((L, ML_W), lambda b: (b, col0 + 2)),
        pl.BlockSpec((L, ML_W), lambda b: (b, col0 + 3)),
        pl.BlockSpec((1, rows.shape[1], L), lambda b: (b, 0, 0)),
        pl.BlockSpec((L, LANES), lambda b: (b, 0)),
        _layer_spec((3, 2 * ML_W), l), _layer_spec((1, ML_W), l),
    ]
    args = [proj, proj, proj, proj, rows, cols, conv_w, norm_g]
    y_shape = jax.ShapeDtypeStruct((B * L, ML_W), F32)
    y_spec = pl.BlockSpec((L, ML_W), lambda b: (b, 0))
    c_blk, n_blk, m_blk = (2, ML_H, ML_DH, ML_DH), (2 * ML_H, ML_DH), (2 * ML_H, LANES)
    if with_state:
        c0, n0, m0 = state
        in_specs += [
            pl.BlockSpec((1, None) + c_blk, lambda b: (b, l, 0, 0, 0, 0)),
            pl.BlockSpec((1, None) + n_blk, lambda b: (b, l, 0, 0)),
            pl.BlockSpec((1, None, N_GATE, LANES), lambda b: (b, l, 0, 0)),
        ]
        args += [c0, n0, m0]
        out_shape, out_specs = y_shape, y_spec
    else:
        lead = (l + 1,) if n_prev else ()
        if n_prev:
            in_specs += [pl.BlockSpec((1, n_prev) + blk, lambda b, nz=len(blk) + 1: (b,) + (0,) * nz)
                         for blk in (c_blk, n_blk, m_blk)]
            args += list(prev)
        out_shape = [y_shape] + [jax.ShapeDtypeStruct((B,) + lead + blk, F32) for blk in (c_blk, n_blk, m_blk)]
        out_specs = [y_spec] + [pl.BlockSpec((1,) + lead + blk, lambda b, nz=len(lead) + len(blk): (b,) + (0,) * nz)
                                for blk in (c_blk, n_blk, m_blk)]
    return pl.pallas_call(
        functools.partial(_mlstm_kernel, L=L, with_state=with_state, n_prev=n_prev),
        out_shape=out_shape,
        grid=(B,),
        in_specs=in_specs,
        out_specs=out_specs,
        compiler_params=pltpu.CompilerParams(
            dimension_semantics=("arbitrary",), vmem_limit_bytes=VMEM_LIMIT),
        name="mlstm",
    )(*args)


def _outmlp_kernel(x_ref, ya_ref, yb_ref, mod_ref, wo_ref, g2_ref, w1_ref, w2_ref, gf_ref, o_ref, *, final):
    mod = mod_ref[0]
    g1 = mod[:, 2 * D_MODEL:3 * D_MODEL]
    sh2 = mod[:, 3 * D_MODEL:4 * D_MODEL]
    sc2 = mod[:, 4 * D_MODEL:5 * D_MODEL]
    g2 = mod[:, 5 * D_MODEL:6 * D_MODEL]
    na = SC_W + HY_W
    attn = _dot(ya_ref[...].astype(BF16), wo_ref[0:na, :]) + _dot(yb_ref[...].astype(BF16), wo_ref[na:, :])
    x = x_ref[...] + g1 * attn
    u = ((_rms(x) * g2_ref[...]) * (1.0 + sc2) + sh2).astype(BF16)
    tf = 1024
    acc = jnp.zeros(x.shape, F32)
    for j in range(D_FF // tf):
        hcol = _dot(u, w1_ref[:, j * tf:(j + 1) * tf])
        hcol = jnp.square(jnp.maximum(hcol, 0.0)).astype(BF16)
        acc = acc + _dot(hcol, w2_ref[j * tf:(j + 1) * tf, :])
    x = x + g2 * acc
    if final:
        x = _rms(x) * gf_ref[...]
    o_ref[...] = x


def _outmlp(x2d, ya, yb, mods, w_out, g2, w1, w2, gf, *, l, L, per_request, final):
    T = x2d.shape[0]
    tm = ROW_BLOCK
    return pl.pallas_call(
        functools.partial(_outmlp_kernel, final=final),
        out_shape=jax.ShapeDtypeStruct((T, D_MODEL), F32),
        grid=(T // tm,),
        in_specs=[
            pl.BlockSpec((tm, D_MODEL), lambda i: (i, 0)),
            pl.BlockSpec((tm, SC_W + HY_W), lambda i: (i, 0)),
            pl.BlockSpec((tm, ML_W), lambda i: (i, 0)),
            _mod_spec(l, L, tm, per_request),
            _layer_spec((D_MODEL, D_MODEL), l), _layer_spec((1, D_MODEL), l),
            _layer_spec((D_MODEL, D_FF), l), _layer_spec((D_FF, D_MODEL), l), _const_spec((1, D_MODEL)),
        ],
        out_specs=pl.BlockSpec((tm, D_MODEL), lambda i: (i, 0)),
        compiler_params=pltpu.CompilerParams(
            dimension_semantics=("arbitrary",), vmem_limit_bytes=VMEM_LIMIT),
        name="outproj_mlp",
    )(x2d, ya, yb, mods, w_out, g2, w1, w2, gf)


def kernel(x_prompt, x_sample, state_C, state_n, state_m, c, c_ctx, norm1_g, ada_w, ada_b, w_in, sc_conv_w, hy_conv_w, hy_w1, hy_b1, hy_w2, hy_b2, hy_w3, hy_freq, hy_skip, ml_conv_w, ml_gate_b, ml_norm_g, w_out, norm2_g, mlp_w1, mlp_w2, norm_f_g):
    B, L = x_prompt.shape[0], x_prompt.shape[1]
    Bd, Ld = x_sample.shape[0], x_sample.shape[1]

    cond = jnp.concatenate([c_ctx[None, :], c, jnp.zeros((8 - 1 - Bd, D_MODEL), F32)], axis=0)
    mods = _ada(cond, ada_w, ada_b).reshape(DEPTH * 8, 1, 6 * D_MODEL)

    w_in_b = w_in.astype(BF16)
    w_gate = jnp.swapaxes(w_in[:, :, N_MAIN:], 1, 2).astype(BF16)
    w_out_b = w_out.astype(BF16)
    w1_b = mlp_w1.astype(BF16)
    w2_b = mlp_w2.astype(BF16)
    gate_b = jnp.broadcast_to(ml_gate_b[:, :, None], (DEPTH, N_GATE, LANES))
    pad_h = LANES - HY_HID
    w1p = jnp.pad(hy_w1, ((0, 0), (0, LANES - HY_EMB), (0, pad_h)))
    b1p = jnp.pad(hy_b1, ((0, 0), (0, pad_h)))
    w2p = jnp.pad(hy_w2, ((0, 0), (0, pad_h), (0, pad_h)))
    b2p = jnp.pad(hy_b2, ((0, 0), (0, pad_h)))
    w3p = jnp.pad(hy_w3, ((0, 0), (0, pad_h), (0, 0)))
    fp = jnp.pad(hy_freq, ((0, 0), (0, pad_h)))
    pos = jnp.asarray(_pos_embed_np(Ld))
    m0_all = jnp.pad(state_m, ((0, 0), (0, 0), (0, 0), (0, ML_H))).reshape(Bd, DEPTH, N_GATE, 1)
    m0_all = jnp.broadcast_to(m0_all, (Bd, DEPTH, N_GATE, LANES))
    n0_all = state_n.reshape(Bd, DEPTH, 2 * ML_H, ML_DH)
    gf = norm_f_g[None, :]
    g1 = norm1_g[:, None, :]
    g2 = norm2_g[:, None, :]
    ml_ng = ml_norm_g[:, None, :]
    filt_args = (w1p, b1p[:, None, :], w2p, b2p[:, None, :], w3p, fp[:, None, :])
    hre_c, him_c = _filter_spectrum(L, *filt_args)
    hre_s, him_s = _filter_spectrum(Ld, *filt_args)
    state = (state_C, n0_all, m0_all)

    xp = x_prompt.reshape(B * L, D_MODEL)
    xs = x_sample.reshape(Bd * Ld, D_MODEL)
    prev = None
    for l in range(DEPTH):
        final = l == DEPTH - 1

        proj, g_t = _inproj(xp, None, mods, g1, w_in_b, w_gate, l=l, L=L, per_request=False)
        ya = _schy(proj, sc_conv_w, hy_conv_w, hy_skip, hre_c, him_c, l=l, B=B, L=L)
        yb, *states = _mlstm(proj, g_t, gate_b, ml_conv_w, ml_ng, None, prev, l=l, B=B, L=L)
        xp = _outmlp(xp, ya, yb, mods, w_out_b, g2, w1_b, w2_b, gf, l=l, L=L, per_request=False, final=final)
        if not final:
            prev = [s.reshape((B, 1) + s.shape[1:]) for s in states] if l == 0 else states

        res = _inproj(xs, pos if l == 0 else None, mods, g1, w_in_b, w_gate, l=l, L=Ld, per_request=True)
        if l == 0:
            proj, g_t, xs = res
        else:
            proj, g_t = res
        ya = _schy(proj, sc_conv_w, hy_conv_w, hy_skip, hre_s, him_s, l=l, B=Bd, L=Ld)
        yb = _mlstm(proj, g_t, gate_b, ml_conv_w, ml_ng, state, None, l=l, B=Bd, L=Ld)
        xs = _outmlp(xs, ya, yb, mods, w_out_b, g2, w1_b, w2_b, gf, l=l, L=Ld, per_request=True, final=final)

    y_prompt = xp.reshape(B, L, D_MODEL)
    y_sample = xs.reshape(Bd, Ld, D_MODEL)
    new_c, new_n, new_m = states
    return (y_prompt, y_sample, new_c, new_n.reshape(B, DEPTH, 2, ML_H, ML_DH),
            new_m[..., 0].reshape(B, DEPTH, 2, ML_H))
```

```python
import functools
import math

import numpy as np
import jax
import jax.numpy as jnp
from jax import lax
from jax.experimental import pallas as pl
from jax.experimental.pallas import tpu as pltpu

F32 = jnp.float32
BF16 = jnp.bfloat16

D_MODEL = 1024
DEPTH = 2
GRID_W = 64
SC_W = 256
HY_W = 256
ML_W = 512
ML_H = 4
ML_DH = ML_W // ML_H
D_FF = 4 * D_MODEL
HY_ORDER = 2
HY_BANDS = 16
HY_EMB = 1 + 2 * HY_BANDS
HY_HID = 64
HY_DECAY_SHORT = 0.3
HY_DECAY_LONG = 1.5
HY_DECAY_TARGET = 1e-2
EPS = 1e-6
N_MAIN = 3 * SC_W + 3 * HY_W + 4 * ML_W
N_GATE = 4 * ML_H
LANES = 128
ROW_BLOCK = 512
ATT_BLOCK = 256
HY_BLOCK = 256
HY_STEP_ROWS = 1024
NEG_BIG = -1e30
LOG2E = math.log2(math.e)
VMEM_LIMIT = 60 * 1024 * 1024


def _dot(a, b):
    return jnp.dot(a, b, preferred_element_type=F32)


def _dot_nt(a, b):
    return lax.dot_general(a, b, (((1,), (1,)), ((), ())), preferred_element_type=F32)


def _split2(a):
    hi = a.astype(BF16)
    lo = (a - hi.astype(F32)).astype(BF16)
    return hi, lo


def _split3(a):
    hi = a.astype(BF16)
    r = a - hi.astype(F32)
    mid = r.astype(BF16)
    lo = (r - mid.astype(F32)).astype(BF16)
    return hi, mid, lo


def _dot3(a, b):
    ah, al = _split2(a)
    bh, bl = _split2(b)
    return _dot(ah, bh) + _dot(al, bh) + _dot(ah, bl)


def _dot3_nt(a, b):
    ah, al = _split2(a)
    bh, bl = _split2(b)
    return _dot_nt(ah, bh) + _dot_nt(al, bh) + _dot_nt(ah, bl)


def _dot3_tab(th, tl, b):
    bh, bl = _split2(b)
    return _dot(th, bh) + _dot(tl, bh) + _dot(th, bl)


def _sigmoid(x):
    return 1.0 / (1.0 + jnp.exp(-x))


def _silu(x):
    return x * _sigmoid(x)


def _log_sigmoid(x):
    return jnp.minimum(x, 0.0) - jnp.log(1.0 + jnp.exp(-jnp.abs(x)))


def _conv3(x, w, seq_len=None):
    R = x.shape[0]
    seq_len = R if seq_len is None else seq_len
    pos = lax.broadcasted_iota(jnp.int32, x.shape, 0) & (seq_len - 1)
    xm = jnp.where(pos == 0, 0.0, pltpu.roll(x, 1, axis=0))
    xp = jnp.where(pos == seq_len - 1, 0.0, pltpu.roll(x, R - 1, axis=0))
    return xm * w[0:1, :] + x * w[1:2, :] + xp * w[2:3, :]


def _rms(x):
    return x * lax.rsqrt(jnp.mean(x * x, axis=-1, keepdims=True) + EPS)


def _scan_max(x, *, reverse):
    n = x.shape[1]
    lane = lax.broadcasted_iota(jnp.int32, x.shape, 1)
    d = 1
    while d < n:
        if reverse:
            shifted = jnp.where(lane < n - d, pltpu.roll(x, n - d, axis=1), NEG_BIG)
        else:
            shifted = jnp.where(lane >= d, pltpu.roll(x, d, axis=1), NEG_BIG)
        x = jnp.maximum(x, shifted)
        d *= 2
    return x


def _hi_lo(a64):
    a = a64.astype(np.float32)
    hi = a.astype(BF16)
    lo = (a - hi.astype(np.float32)).astype(BF16)
    return jnp.asarray(hi), jnp.asarray(lo)


def _const_spec(shape):
    return pl.BlockSpec(shape, lambda *_: tuple(0 for _ in shape), pipeline_mode=pl.Buffered(1))


def _layer_spec(shape, l):
    return pl.BlockSpec((None,) + tuple(shape), lambda *_: (l,) + tuple(0 for _ in shape),
                        pipeline_mode=pl.Buffered(1))


@functools.lru_cache(maxsize=None)
def _dft_tables_np(L):
    k = np.arange(L, dtype=np.int64)
    m1 = ((2 * k[:, None] + 1) * k[None, :]) % (4 * L)
    a1 = np.pi * m1.astype(np.float64) / (2 * L)
    m2 = ((2 * k[:, None] + 1) * (2 * k[None, :] + 1)) % (8 * L)
    a2 = np.pi * m2.astype(np.float64) / (4 * L)
    return np.cos(a1), np.sin(a1), np.cos(a2), np.sin(a2)


@functools.lru_cache(maxsize=None)
def _filter_consts_np(L):
    t_idx = np.arange(L, dtype=np.float64)
    t = t_idx / (L - 1)
    bands = np.arange(1, HY_BANDS + 1, dtype=np.float64)
    ang = 2.0 * math.pi * t_idx[:, None] * bands[None, :] / L
    z = np.zeros((L, LANES), np.float64)
    z[:, 0] = t
    z[:, 1:1 + HY_BANDS] = np.cos(ang)
    z[:, 1 + HY_BANDS:1 + 2 * HY_BANDS] = -np.sin(ang)
    lin = np.linspace(math.log(HY_DECAY_TARGET) / HY_DECAY_LONG,
                      math.log(HY_DECAY_TARGET) / HY_DECAY_SHORT, HY_W).astype(np.float32)
    deltas = np.abs(lin).astype(np.float64)
    decay = np.exp(-t[:, None] * deltas[None, :])
    return z.astype(np.float32), decay.astype(np.float32)


@functools.lru_cache(maxsize=None)
def _pos_embed_np(L):
    rows = L // GRID_W
    r, cidx = np.meshgrid(np.arange(rows, dtype=np.float64), np.arange(GRID_W, dtype=np.float64), indexing="ij")
    r = r.reshape(-1)
    cidx = cidx.reshape(-1)
    quarter = D_MODEL // 4
    omega = 1.0 / (10000.0 ** (np.arange(quarter, dtype=np.float64) / quarter))
    ar = r[:, None] * omega[None, :]
    ac = cidx[:, None] * omega[None, :]
    return np.concatenate([np.sin(ar), np.cos(ar), np.sin(ac), np.cos(ac)], axis=-1).astype(np.float32)


def _ada_kernel(c_ref, w_ref, b_ref, o_ref):
    sc = _silu(c_ref[...])
    o_ref[0] = _dot3(sc, w_ref[0]) + b_ref[0]


def _ada(cond, ada_w, ada_b):
    tn = 1536
    nd = 6 * D_MODEL
    return pl.pallas_call(
        _ada_kernel,
        out_shape=jax.ShapeDtypeStruct((DEPTH, 8, nd), F32),
        grid=(DEPTH, nd // tn),
        in_specs=[
            pl.BlockSpec((8, D_MODEL), lambda l, j: (0, 0)),
            pl.BlockSpec((1, D_MODEL, tn), lambda l, j: (l, 0, j)),
            pl.BlockSpec((1, 1, tn), lambda l, j: (l, 0, j)),
        ],
        out_specs=pl.BlockSpec((1, 8, tn), lambda l, j: (l, 0, j)),
        compiler_params=pltpu.CompilerParams(
            dimension_semantics=("arbitrary", "arbitrary"), vmem_limit_bytes=VMEM_LIMIT),
        name="ada_mod",
    )(cond, ada_w, ada_b.reshape(DEPTH, 1, nd))


def _inproj_kernel(*refs, with_pos):
    if with_pos:
        x_ref, pos_ref, mod_ref, g_ref, w_ref, wg_ref, proj_ref, gate_ref, xo_ref = refs
        x = x_ref[...] + pos_ref[...]
        xo_ref[...] = x
    else:
        x_ref, mod_ref, g_ref, w_ref, wg_ref, proj_ref, gate_ref = refs
        x = x_ref[...]
    mod = mod_ref[0]
    sh1 = mod[:, 0:D_MODEL]
    sc1 = mod[:, D_MODEL:2 * D_MODEL]
    u = (_rms(x) * g_ref[...]) * (1.0 + sc1) + sh1
    ub = u.astype(BF16)
    tn = 512
    for j in range(N_MAIN // tn):
        proj_ref[:, j * tn:(j + 1) * tn] = _dot(ub, w_ref[:, j * tn:(j + 1) * tn])
    gate_ref[...] = _dot_nt(wg_ref[...], ub)


def _mod_spec(l, L, tm, per_request):
    if per_request:
        return pl.BlockSpec((1, 1, 6 * D_MODEL), lambda i: (l * 8 + 1 + (i * tm) // L, 0, 0))
    return pl.BlockSpec((1, 1, 6 * D_MODEL), lambda i: (l * 8, 0, 0))


def _inproj(x2d, pos, mods, g, w_in_b, w_gate, *, l, L, per_request):
    T = x2d.shape[0]
    tm = ROW_BLOCK
    nblk = T // tm
    per_seq = L // tm
    with_pos = pos is not None
    row_spec = pl.BlockSpec((tm, D_MODEL), lambda i: (i, 0))
    in_specs = [row_spec]
    args = [x2d]
    if with_pos:
        in_specs.append(pl.BlockSpec((tm, D_MODEL), lambda i: (i % per_seq, 0)))
        args.append(pos)
    in_specs += [
        _mod_spec(l, L, tm, per_request),
        _layer_spec((1, D_MODEL), l),
        _layer_spec((D_MODEL, N_MAIN), l),
        _layer_spec((N_GATE, D_MODEL), l),
    ]
    args += [mods, g, w_in_b, w_gate]
    out_shape = [jax.ShapeDtypeStruct((T, N_MAIN), F32), jax.ShapeDtypeStruct((N_GATE, T), F32)]
    out_specs = [pl.BlockSpec((tm, N_MAIN), lambda i: (i, 0)), pl.BlockSpec((N_GATE, tm), lambda i: (0, i))]
    if with_pos:
        out_shape.append(jax.ShapeDtypeStruct((T, D_MODEL), F32))
        out_specs.append(row_spec)
    return pl.pallas_call(
        functools.partial(_inproj_kernel, with_pos=with_pos),
        out_shape=out_shape,
        grid=(nblk,),
        in_specs=in_specs,
        out_specs=out_specs,
        compiler_params=pltpu.CompilerParams(
            dimension_semantics=("arbitrary",), vmem_limit_bytes=VMEM_LIMIT),
        name="inproj",
    )(*args)


def _filter_kernel(z_ref, dec_ref, w1_ref, b1_ref, w2_ref, b2_ref, w3_ref, f_ref,
                   ch_ref, cl_ref, sh_ref, sl_ref, hre_ref, him_ref, *, L):
    T = HY_BLOCK
    P = L // T
    f = f_ref[...]
    hdn = jnp.sin(f * (_dot3(z_ref[...], w1_ref[...]) + b1_ref[...]))
    hdn = jnp.sin(f * (_dot3(hdn, w2_ref[...]) + b2_ref[...]))
    h = _dot3(hdn, w3_ref[...])
    dec = dec_ref[...]
    row = lax.broadcasted_iota(jnp.int32, (L, HY_W), 0)
    halves = []
    for o in range(HY_ORDER):
        hf = h[:, (2 * o) * HY_W:(2 * o + 1) * HY_W] * dec
        hb = jnp.where(row == 0, 0.0, h[:, (2 * o + 1) * HY_W:(2 * o + 2) * HY_W] * dec)
        nrm = jnp.sum(jnp.abs(hf), axis=0, keepdims=True) + jnp.sum(jnp.abs(hb), axis=0, keepdims=True)
        inv = 1.0 / nrm
        halves.append((hf * inv, hb * inv))
    blocks = [half[e * T:(e + 1) * T] for pair in halves for half in pair for e in range(P)]
    x = jnp.concatenate(blocks, axis=1)
    xre = _dot3_tab(ch_ref[...], cl_ref[...], x)
    xim = -_dot3_tab(sh_ref[...], sl_ref[...], x)
    sgn = jnp.where((lax.broadcasted_iota(jnp.int32, (T, HY_W), 0) & 1) == 0, 1.0, -1.0)
    for o in range(HY_ORDER):
        def half_spec(which, e):
            j = (o * 2 + which) * P + e
            cs = slice(j * HY_W, (j + 1) * HY_W)
            return xre[:, cs], xim[:, cs], halves[o][which][e * T:e * T + 1]
        for d in range(-(P - 1), P):
            if d == 0:
                (fre, fim, _), (gre, gim, _) = half_spec(0, 0), half_spec(1, 0)
                re, im = fre + gre, fim - gim
            elif d > 0:
                (fre, fim, _), (pre, pim, p0) = half_spec(0, d), half_spec(0, d - 1)
                re, im = fre - sgn * pim, fim + sgn * (pre - p0)
            else:
                (gre, gim, _), (pre, pim, p0) = half_spec(1, -d), half_spec(1, -d - 1)
                re, im = gre - sgn * pim, -gim - sgn * (pre - p0)
            hre_ref[d + P - 1, :, o * HY_W:(o + 1) * HY_W] = re
            him_ref[d + P - 1, :, o * HY_W:(o + 1) * HY_W] = im


def _filter_spectrum(L, w1p, b1p, w2p, b2p, w3p, fp):
    T = HY_BLOCK
    z_np, dec_np = _filter_consts_np(L)
    c1, s1, _, _ = _dft_tables_np(T)
    ch, cl = _hi_lo(c1)
    sh, sl = _hi_lo(s1)
    n = HY_ORDER * HY_W
    nd = 2 * (L // T) - 1
    const = _const_spec
    per_layer = lambda shape: pl.BlockSpec((None,) + shape, lambda l: (l,) + tuple(0 for _ in shape))
    return pl.pallas_call(
        functools.partial(_filter_kernel, L=L),
        out_shape=[jax.ShapeDtypeStruct((DEPTH, nd, T, n), F32), jax.ShapeDtypeStruct((DEPTH, nd, T, n), F32)],
        grid=(DEPTH,),
        in_specs=[const((L, LANES)), const((L, HY_W)), per_layer((LANES, LANES)), per_layer((1, LANES)),
                  per_layer((LANES, LANES)), per_layer((1, LANES)), per_layer((LANES, 2 * n)), per_layer((1, LANES)),
                  const((T, T)), const((T, T)), const((T, T)), const((T, T))],
        out_specs=[per_layer((nd, T, n)), per_layer((nd, T, n))],
        compiler_params=pltpu.CompilerParams(
            dimension_semantics=("arbitrary",), vmem_limit_bytes=VMEM_LIMIT),
        name="hyena_filter",
    )(jnp.asarray(z_np), jnp.asarray(dec_np), w1p, b1p, w2p, b2p, w3p, fp, ch, cl, sh, sl)


def _schy_kernel(sc_ref, hy_ref, scw_ref, hyw_ref, skip_ref, fwd_ref, inv_ref, hre_ref, him_ref, out_ref, *, L):
    T = HY_BLOCK
    P = L // T
    NB = sc_ref.shape[0] // T
    sc = sc_ref[...]
    b_g = sc[:, 0:SC_W]
    c_g = sc[:, SC_W:2 * SC_W]
    xin = sc[:, 2 * SC_W:3 * SC_W]
    out_ref[:, 0:SC_W] = b_g * _conv3(c_g * xin, scw_ref[...], L)

    u3 = _conv3(hy_ref[...], hyw_ref[...], L)
    z = u3[:, 0:HY_W]
    gates = (u3[:, HY_W:2 * HY_W], u3[:, 2 * HY_W:3 * HY_W])
    fwd, inv = fwd_ref[...], inv_ref[...]
    lanes = lambda j: slice(j * HY_W, (j + 1) * HY_W)
    for o in range(HY_ORDER):
        zcat = jnp.concatenate([z[j * T:(j + 1) * T] for j in range(NB)], axis=1)
        spec = _dot(fwd, zcat.astype(BF16))
        yre, yim = [], []
        for bi in range(NB):
            s0, i = bi - bi % P, bi % P
            re = im = None
            for j in range(P):
                d = i - j + P - 1
                hre = hre_ref[d, :, lanes(o)]
                him = him_ref[d, :, lanes(o)]
                zc = spec[0:T, lanes(s0 + j)]
                zs = spec[T:2 * T, lanes(s0 + j)]
                pre = zc * hre + zs * him
                pim = zc * him - zs * hre
                re = pre if re is None else re + pre
                im = pim if im is None else im + pim
            yre.append(re)
            yim.append(im)
        prod = jnp.concatenate([jnp.concatenate(yre, axis=1), jnp.concatenate(yim, axis=1)], axis=0)
        ycat = _dot(inv, prod.astype(BF16)) * (1.0 / T)
        y = jnp.concatenate([ycat[:, lanes(bi)] for bi in range(NB)], axis=0)
        z = gates[o] * (y + skip_ref[o:o + 1, :] * z)
    out_ref[:, SC_W:SC_W + HY_W] = z


def _schy(proj, sc_w, hy_w, skip, hre, him, *, l, B, L):
    T = HY_BLOCK
    _, _, c2, s2 = _dft_tables_np(T)
    fwd = jnp.asarray(np.concatenate([c2, s2], axis=0).astype(np.float32).astype(BF16))
    inv = jnp.asarray(np.concatenate([c2, -s2], axis=1).astype(np.float32).astype(BF16))
    n = HY_ORDER * HY_W
    nd = 2 * (L // T) - 1
    rows = max(L, HY_STEP_ROWS)
    return pl.pallas_call(
        functools.partial(_schy_kernel, L=L),
        out_shape=jax.ShapeDtypeStruct((B * L, SC_W + HY_W), F32),
        grid=(B * L // rows,),
        in_specs=[
            pl.BlockSpec((rows, 3 * SC_W), lambda b: (b, 0)),
            pl.BlockSpec((rows, 3 * HY_W), lambda b: (b, 1)),
            _layer_spec((3, SC_W), l), _layer_spec((3, 3 * HY_W), l), _layer_spec((HY_ORDER, HY_W), l),
            _const_spec((2 * T, T)), _const_spec((T, 2 * T)),
            _layer_spec((nd, T, n), l), _layer_spec((nd, T, n), l),
        ],
        out_specs=pl.BlockSpec((rows, SC_W + HY_W), lambda b: (b, 0)),
        compiler_params=pltpu.CompilerParams(
            dimension_semantics=("arbitrary",), vmem_limit_bytes=VMEM_LIMIT),
        name="sconv_hyena",
    )(proj, proj, sc_w, hy_w, skip, fwd, inv, hre, him)


def _gate_prep_kernel(*refs, B, L, with_state):
    if with_state:
        g_ref, gb_ref, m0_ref, rows_ref, cols_ref = refs
    else:
        g_ref, gb_ref, rows_ref, cols_ref = refs
    R = B * N_GATE
    gb = gb_ref[:, 0:1]
    x = jnp.concatenate([g_ref[:, b * L:(b + 1) * L] + gb for b in range(B)], axis=0)
    rr = lax.broadcasted_iota(jnp.int32, (R, L), 0) & (N_GATE - 1)
    is_f = ((rr >= ML_H) & (rr < 2 * ML_H)) | (rr >= 3 * ML_H)
    fwd_rows = rr < 2 * ML_H
    pk = jnp.where(is_f, _log_sigmoid(x), x)
    r_i = lax.broadcasted_iota(jnp.int32, (L, L), 0)
    c_i = lax.broadcasted_iota(jnp.int32, (L, L), 1)
    triu = jnp.where(r_i <= c_i, 1.0, 0.0).astype(BF16)
    cs = _dot(jnp.concatenate(_split3(pk), axis=0), triu)
    cum = cs[0:R] + cs[R:2 * R] + cs[2 * R:3 * R]
    suf = cum[:, L - 1:L] - cum + pk
    bsum = pltpu.roll(jnp.where(fwd_rows, cum, suf), R - ML_H, axis=0)
    r = pk - bsum
    cm = jnp.where(fwd_rows, _scan_max(r, reverse=False), _scan_max(r, reverse=True))
    if with_state:
        m0 = jnp.concatenate([m0_ref[b][:, 0:1] for b in range(B)], axis=0)
        m = jnp.maximum(m0, cm)
    else:
        m = jnp.maximum(cm, 0.0)
    rl = r * LOG2E
    ml = m * LOG2E
    mt = bsum + m
    pieces = [rl, ml, jnp.exp(-mt), mt]
    if not with_state:
        pieces.append(jnp.exp2(rl - jnp.where(fwd_rows, ml[:, L - 1:L], ml[:, 0:1])))
    pad = jnp.zeros((LANES - N_GATE, L), F32)
    for b in range(B):
        bs = slice(b * N_GATE, (b + 1) * N_GATE)
        for j, piece in enumerate(pieces):
            rows_ref[b, j * N_GATE:(j + 1) * N_GATE, :] = piece[bs]
        cols_ref[b * L:(b + 1) * L, :] = jnp.concatenate([rl[bs], pad], axis=0).T


def _gate_prep(g_t, gate_b, m0, *, l, B, L):
    with_state = m0 is not None
    n_rows = (4 if with_state else 5) * N_GATE
    full = lambda shape: pl.BlockSpec(shape, lambda i: tuple(0 for _ in shape))
    args = [g_t, gate_b]
    in_specs = [full((N_GATE, B * L)), pl.BlockSpec((None, N_GATE, LANES), lambda i: (l, 0, 0))]
    if with_state:
        args.append(m0)
        in_specs.append(pl.BlockSpec((B, None, N_GATE, LANES), lambda i: (0, l, 0, 0)))
    return pl.pallas_call(
        functools.partial(_gate_prep_kernel, B=B, L=L, with_state=with_state),
        out_shape=[jax.ShapeDtypeStruct((B, n_rows, L), F32), jax.ShapeDtypeStruct((B * L, LANES), F32)],
        grid=(1,),
        in_specs=in_specs,
        out_specs=[full((B, n_rows, L)), full((B * L, LANES))],
        compiler_params=pltpu.CompilerParams(
            dimension_semantics=("arbitrary",), vmem_limit_bytes=VMEM_LIMIT),
        name="gate_prep",
    )(*args)


def _mlstm_kernel(*refs, L, with_state, n_prev):
    if with_state:
        (q_ref, k_ref, v_ref, o_ref, rows_ref, cols_ref, cw_ref, ng_ref,
         c0_ref, n0_ref, m0_ref, y_ref) = refs
    elif n_prev:
        (q_ref, k_ref, v_ref, o_ref, rows_ref, cols_ref, cw_ref, ng_ref,
         cprev_ref, nprev_ref, mprev_ref, y_ref, cout_ref, nout_ref, mout_ref) = refs
        cout_ref[0, 0:n_prev] = cprev_ref[0]
        nout_ref[0, 0:n_prev] = nprev_ref[0]
        mout_ref[0, 0:n_prev] = mprev_ref[0]
        cout_ref, nout_ref, mout_ref = (r.at[:, n_prev] for r in (cout_ref, nout_ref, mout_ref))
    else:
        (q_ref, k_ref, v_ref, o_ref, rows_ref, cols_ref, cw_ref, ng_ref,
         y_ref, cout_ref, nout_ref, mout_ref) = refs

    tb = min(L // 2, ATT_BLOCK)
    cw = cw_ref[...]
    q_all = _silu(_conv3(q_ref[...], cw[:, 0:ML_W]))
    k_all = _silu(_conv3(k_ref[...], cw[:, ML_W:2 * ML_W])) * (ML_DH ** -0.5)

    rl16, ml16, em16, mt16 = (rows_ref[0, j * N_GATE:(j + 1) * N_GATE, :] for j in range(4))
    rl_cols = cols_ref[...]
    if not with_state:
        w16 = rows_ref[0, 4 * N_GATE:5 * N_GATE, :]

    s_loc = lax.broadcasted_iota(jnp.int32, (tb, tb), 0)
    t_loc = lax.broadcasted_iota(jnp.int32, (tb, tb), 1)
    masks = (s_loc <= t_loc, s_loc >= t_loc)
    ones_rows = jnp.ones((2 * 8, L), BF16)
    ng = ng_ref[...]
    for h in range(ML_H):
        hs = slice(h * ML_DH, (h + 1) * ML_DH)
        q = q_all[:, hs]
        k = k_all[:, hs]
        qb = q.astype(BF16)
        kb = k.astype(BF16)
        vb = v_ref[:, hs].astype(BF16)
        vext_t = jnp.concatenate([v_ref[:, hs].T.astype(BF16), ones_rows], axis=0)
        rlb = [jnp.broadcast_to(rl_cols[:, 2 * ML_H * d + h:2 * ML_H * d + h + 1], (L, tb)) for d in range(2)]
        if with_state:
            c0_t = [c0_ref[0, d, h].T.astype(BF16) for d in range(2)]
        for r0 in range(0, L, tb):
            rs = slice(r0, r0 + tb)
            s_t = _dot_nt(kb, qb[rs])
            if with_state:
                qn_rows = _dot3_nt(n0_ref[0], q[rs])
            hout_t = None
            for d in range(2):
                o8 = 2 * ML_H * d + h
                ml_row = ml16[o8:o8 + 1, rs]
                e_diag = jnp.exp2(jnp.where(masks[d], rlb[d][rs] - ml_row, NEG_BIG))
                acc = _dot(vext_t[:, rs], (s_t[rs] * e_diag).astype(BF16))
                side = slice(r0 + tb, L) if d else slice(0, r0)
                if side.stop > side.start:
                    e_side = jnp.exp2(rlb[d][side] - ml_row)
                    acc = acc + _dot(vext_t[:, side], (s_t[side] * e_side).astype(BF16))
                num_t = acc[0:ML_DH]
                den = acc[ML_DH:ML_DH + 1]
                if with_state:
                    wp = jnp.exp2(m0_ref[0, o8:o8 + 1, 0:1] * LOG2E - ml_row)
                    den = den + wp * qn_rows[d * ML_H + h:d * ML_H + h + 1]
                    num_t = num_t + wp * _dot_nt(c0_t[d], qb[rs])
                contrib = num_t * (1.0 / jnp.maximum(jnp.abs(den), em16[o8:o8 + 1, rs]))
                hout_t = contrib if hout_t is None else hout_t + contrib
            hn_t = hout_t * lax.rsqrt(jnp.mean(hout_t * hout_t, axis=0, keepdims=True) + EPS)
            y_ref[rs, hs] = _sigmoid(o_ref[rs, hs]) * (hn_t.T * ng[:, hs])

        if not with_state:
            k_t = k.T
            n_rows = _dot3(w16, k)
            for d in range(2):
                o8 = 2 * ML_H * d + h
                end = 0 if d else L - 1
                cout_ref[0, d, h] = _dot((k_t * w16[o8:o8 + 1, :]).astype(BF16), vb)
                nout_ref[0, d * ML_H + h:d * ML_H + h + 1, :] = n_rows[o8:o8 + 1, :]
                mout_ref[0, d * ML_H + h:d * ML_H + h + 1, :] = jnp.broadcast_to(
                    mt16[o8:o8 + 1, end:end + 1], (1, LANES))


def _mlstm(proj, g_t, gate_b, conv_w, norm_g, state, prev, *, l, B, L):
    with_state = state is not None
    n_prev = 0 if prev is None else l
    rows, cols = _gate_prep(g_t, gate_b, state[2] if with_state else None, l=l, B=B, L=L)
    col0 = (3 * SC_W + 3 * HY_W) // ML_W
    in_specs = [
        pl.BlockSpec((L, ML_W), lambda b: (b, col0)),
        pl.BlockSpec((L, ML_W), lambda b: (b, col0 + 1)),
        pl.BlockSpec((L, ML_W), lambda b: (b, col0 + 2)),
        pl.BlockSpec((L, ML_W), lambda b: (b, col0 + 3)),
        pl.BlockSpec((1, rows.shape[1], L), lambda b: (b, 0, 0)),
        pl.BlockSpec((L, LANES), lambda b: (b, 0)),
        _layer_spec((3, 2 * ML_W), l), _layer_spec((1, ML_W), l),
    ]
    args = [proj, proj, proj, proj, rows, cols, conv_w, norm_g]
    y_shape = jax.ShapeDtypeStruct((B * L, ML_W), F32)
    y_spec = pl.BlockSpec((L, ML_W), lambda b: (b, 0))
    c_blk, n_blk, m_blk = (2, ML_H, ML_DH, ML_DH), (2 * ML_H, ML_DH), (2 * ML_H, LANES)
    if with_state:
        c0, n0, m0 = state
        in_specs += [
            pl.BlockSpec((1, None) + c_blk, lambda b: (b, l, 0, 0, 0, 0)),
            pl.BlockSpec((1, None) + n_blk, lambda b: (b, l, 0, 0)),
            pl.BlockSpec((1, None, N_GATE, LANES), lambda b: (b, l, 0, 0)),
        ]
        args += [c0, n0, m0]
        out_shape, out_specs = y_shape, y_spec
    else:
        lead = (l + 1,) if n_prev else ()
        if n_prev:
            in_specs += [pl.BlockSpec((1, n_prev) + blk, lambda b, nz=len(blk) + 1: (b,) + (0,) * nz)
                         for blk in (c_blk, n_blk, m_blk)]
            args += list(prev)
        out_shape = [y_shape] + [jax.ShapeDtypeStruct((B,) + lead + blk, F32) for blk in (c_blk, n_blk, m_blk)]
        out_specs = [y_spec] + [pl.BlockSpec((1,) + lead + blk, lambda b, nz=len(lead) + len(blk): (b,) + (0,) * nz)
                                for blk in (c_blk, n_blk, m_blk)]
    return pl.pallas_call(
        functools.partial(_mlstm_kernel, L=L, with_state=with_state, n_prev=n_prev),
        out_shape=out_shape,
        grid=(B,),
        in_specs=in_specs,
        out_specs=out_specs,
        compiler_params=pltpu.CompilerParams(
            dimension_semantics=("arbitrary",), vmem_limit_bytes=VMEM_LIMIT),
        name="mlstm",
    )(*args)


def _outmlp_kernel(x_ref, ya_ref, yb_ref, mod_ref, wo_ref, g2_ref, w1_ref, w2_ref, gf_ref, o_ref, *, final):
    mod = mod_ref[0]
    g1 = mod[:, 2 * D_MODEL:3 * D_MODEL]
    sh2 = mod[:, 3 * D_MODEL:4 * D_MODEL]
    sc2 = mod[:, 4 * D_MODEL:5 * D_MODEL]
    g2 = mod[:, 5 * D_MODEL:6 * D_MODEL]
    na = SC_W + HY_W
    attn = _dot(ya_ref[...].astype(BF16), wo_ref[0:na, :]) + _dot(yb_ref[...].astype(BF16), wo_ref[na:, :])
    x = x_ref[...] + g1 * attn
    u = ((_rms(x) * g2_ref[...]) * (1.0 + sc2) + sh2).astype(BF16)
    tf = 1024
    acc = jnp.zeros(x.shape, F32)
    for j in range(D_FF // tf):
        hcol = _dot(u, w1_ref[:, j * tf:(j + 1) * tf])
        hcol = jnp.square(jnp.maximum(hcol, 0.0)).astype(BF16)
        acc = acc + _dot(hcol, w2_ref[j * tf:(j + 1) * tf, :])
    x = x + g2 * acc
    if final:
        x = _rms(x) * gf_ref[...]
    o_ref[...] = x


def _outmlp(x2d, ya, yb, mods, w_out, g2, w1, w2, gf, *, l, L, per_request, final):
    T = x2d.shape[0]
    tm = ROW_BLOCK
    return pl.pallas_call(
        functools.partial(_outmlp_kernel, final=final),
        out_shape=jax.ShapeDtypeStruct((T, D_MODEL), F32),
        grid=(T // tm,),
        in_specs=[
            pl.BlockSpec((tm, D_MODEL), lambda i: (i, 0)),
            pl.BlockSpec((tm, SC_W + HY_W), lambda i: (i, 0)),
            pl.BlockSpec((tm, ML_W), lambda i: (i, 0)),
            _mod_spec(l, L, tm, per_request),
            _layer_spec((D_MODEL, D_MODEL), l), _layer_spec((1, D_MODEL), l),
            _layer_spec((D_MODEL, D_FF), l), _layer_spec((D_FF, D_MODEL), l), _const_spec((1, D_MODEL)),
        ],
        out_specs=pl.BlockSpec((tm, D_MODEL), lambda i: (i, 0)),
        compiler_params=pltpu.CompilerParams(
            dimension_semantics=("arbitrary",), vmem_limit_bytes=VMEM_LIMIT),
        name="outproj_mlp",
    )(x2d, ya, yb, mods, w_out, g2, w1, w2, gf)


def kernel(x_prompt, x_sample, state_C, state_n, state_m, c, c_ctx, norm1_g, ada_w, ada_b, w_in, sc_conv_w, hy_conv_w, hy_w1, hy_b1, hy_w2, hy_b2, hy_w3, hy_freq, hy_skip, ml_conv_w, ml_gate_b, ml_norm_g, w_out, norm2_g, mlp_w1, mlp_w2, norm_f_g):
    B, L = x_prompt.shape[0], x_prompt.shape[1]
    Bd, Ld = x_sample.shape[0], x_sample.shape[1]

    cond = jnp.concatenate([c_ctx[None, :], c, jnp.zeros((8 - 1 - Bd, D_MODEL), F32)], axis=0)
    mods = _ada(cond, ada_w, ada_b).reshape(DEPTH * 8, 1, 6 * D_MODEL)

    w_in_b = w_in.astype(BF16)
    w_gate = jnp.swapaxes(w_in[:, :, N_MAIN:], 1, 2).astype(BF16)
    w_out_b = w_out.astype(BF16)
    w1_b = mlp_w1.astype(BF16)
    w2_b = mlp_w2.astype(BF16)
    gate_b = jnp.broadcast_to(ml_gate_b[:, :, None], (DEPTH, N_GATE, LANES))
    pad_h = LANES - HY_HID
    w1p = jnp.pad(hy_w1, ((0, 0), (0, LANES - HY_EMB), (0, pad_h)))
    b1p = jnp.pad(hy_b1, ((0, 0), (0, pad_h)))
    w2p = jnp.pad(hy_w2, ((0, 0), (0, pad_h), (0, pad_h)))
    b2p = jnp.pad(hy_b2, ((0, 0), (0, pad_h)))
    w3p = jnp.pad(hy_w3, ((0, 0), (0, pad_h), (0, 0)))
    fp = jnp.pad(hy_freq, ((0, 0), (0, pad_h)))
    pos = jnp.asarray(_pos_embed_np(Ld))
    m0_all = jnp.pad(state_m, ((0, 0), (0, 0), (0, 0), (0, ML_H))).reshape(Bd, DEPTH, N_GATE, 1)
    m0_all = jnp.broadcast_to(m0_all, (Bd, DEPTH, N_GATE, LANES))
    n0_all = state_n.reshape(Bd, DEPTH, 2 * ML_H, ML_DH)
    gf = norm_f_g[None, :]
    g1 = norm1_g[:, None, :]
    g2 = norm2_g[:, None, :]
    ml_ng = ml_norm_g[:, None, :]
    filt_args = (w1p, b1p[:, None, :], w2p, b2p[:, None, :], w3p, fp[:, None, :])
    hre_c, him_c = _filter_spectrum(L, *filt_args)
    hre_s, him_s = _filter_spectrum(Ld, *filt_args)
    state = (state_C, n0_all, m0_all)

    xp = x_prompt.reshape(B * L, D_MODEL)
    xs = x_sample.reshape(Bd * Ld, D_MODEL)
    prev = None
    for l in range(DEPTH):
        final = l == DEPTH - 1

        proj, g_t = _inproj(xp, None, mods, g1, w_in_b, w_gate, l=l, L=L, per_request=False)
        ya = _schy(proj, sc_conv_w, hy_conv_w, hy_skip, hre_c, him_c, l=l, B=B, L=L)
        yb, *states = _mlstm(proj, g_t, gate_b, ml_conv_w, ml_ng, None, prev, l=l, B=B, L=L)
        xp = _outmlp(xp, ya, yb, mods, w_out_b, g2, w1_b, w2_b, gf, l=l, L=L, per_request=False, final=final)
        if not final:
            prev = [s.reshape((B, 1) + s.shape[1:]) for s in states] if l == 0 else states

        res = _inproj(xs, pos if l == 0 else None, mods, g1, w_in_b, w_gate, l=l, L=Ld, per_request=True)
        if l == 0:
            proj, g_t, xs = res
        else:
            proj, g_t = res
        ya = _schy(proj, sc_conv_w, hy_conv_w, hy_skip, hre_s, him_s, l=l, B=Bd, L=Ld)
        yb = _mlstm(proj, g_t, gate_b, ml_conv_w, ml_ng, state, None, l=l, B=Bd, L=Ld)
        xs = _outmlp(xs, ya, yb, mods, w_out_b, g2, w1_b, w2_b, gf, l=l, L=Ld, per_request=True, final=final)

    y_prompt = xp.reshape(B, L, D_MODEL)
    y_sample = xs.reshape(Bd, Ld, D_MODEL)
    new_c, new_n, new_m = states
    return (y_prompt, y_sample, new_c, new_n.reshape(B, DEPTH, 2, ML_H, ML_DH),
            new_m[..., 0].reshape(B, DEPTH, 2, ML_H))
```

```python
import functools
import math

import numpy as np
import jax
import jax.numpy as jnp
from jax import lax
from jax.experimental import pallas as pl
from jax.experimental.pallas import tpu as pltpu

F32 = jnp.float32
BF16 = jnp.bfloat16

D_MODEL = 1024
DEPTH = 2
GRID_W = 64
SC_W = 256
HY_W = 256
ML_W = 512
ML_H = 4
ML_DH = ML_W // ML_H
D_FF = 4 * D_MODEL
HY_ORDER = 2
HY_BANDS = 16
HY_EMB = 1 + 2 * HY_BANDS
HY_HID = 64
HY_DECAY_SHORT = 0.3
HY_DECAY_LONG = 1.5
HY_DECAY_TARGET = 1e-2
EPS = 1e-6
N_MAIN = 3 * SC_W + 3 * HY_W + 4 * ML_W
N_GATE = 4 * ML_H
LANES = 128
ROW_BLOCK = 512
ATT_BLOCK = 256
INPROJ_CHUNK = 256
HY_BLOCK = 256
HY_STEP_ROWS = 1024
NEG_BIG = -1e30
LOG2E = math.log2(math.e)
VMEM_LIMIT = 60 * 1024 * 1024


def _dot(a, b):
    return jnp.dot(a, b, preferred_element_type=F32)


def _dot_nt(a, b):
    return lax.dot_general(a, b, (((1,), (1,)), ((), ())), preferred_element_type=F32)


def _split2(a):
    hi = a.astype(BF16)
    lo = (a - hi.astype(F32)).astype(BF16)
    return hi, lo


def _split3(a):
    hi = a.astype(BF16)
    r = a - hi.astype(F32)
    mid = r.astype(BF16)
    lo = (r - mid.astype(F32)).astype(BF16)
    return hi, mid, lo


def _dot3(a, b):
    ah, al = _split2(a)
    bh, bl = _split2(b)
    return _dot(ah, bh) + _dot(al, bh) + _dot(ah, bl)


def _dot3_nt(a, b):
    ah, al = _split2(a)
    bh, bl = _split2(b)
    return _dot_nt(ah, bh) + _dot_nt(al, bh) + _dot_nt(ah, bl)


def _dot3_tab(th, tl, b):
    bh, bl = _split2(b)
    return _dot(th, bh) + _dot(tl, bh) + _dot(th, bl)


def _sigmoid(x):
    return 1.0 / (1.0 + jnp.exp(-x))


def _silu(x):
    return x * _sigmoid(x)


def _log_sigmoid(x):
    return jnp.minimum(x, 0.0) - jnp.log(1.0 + jnp.exp(-jnp.abs(x)))


def _conv3(x, w, seq_len=None):
    R = x.shape[0]
    seq_len = R if seq_len is None else seq_len
    pos = lax.broadcasted_iota(jnp.int32, x.shape, 0) & (seq_len - 1)
    xm = jnp.where(pos == 0, 0.0, pltpu.roll(x, 1, axis=0))
    xp = jnp.where(pos == seq_len - 1, 0.0, pltpu.roll(x, R - 1, axis=0))
    return xm * w[0:1, :] + x * w[1:2, :] + xp * w[2:3, :]


def _rms(x):
    return x * lax.rsqrt(jnp.mean(x * x, axis=-1, keepdims=True) + EPS)


def _scan_max(x, *, reverse):
    n = x.shape[1]
    lane = lax.broadcasted_iota(jnp.int32, x.shape, 1)
    d = 1
    while d < n:
        if reverse:
            shifted = jnp.where(lane < n - d, pltpu.roll(x, n - d, axis=1), NEG_BIG)
        else:
            shifted = jnp.where(lane >= d, pltpu.roll(x, d, axis=1), NEG_BIG)
        x = jnp.maximum(x, shifted)
        d *= 2
    return x


def _hi_lo(a64):
    a = a64.astype(np.float32)
    hi = a.astype(BF16)
    lo = (a - hi.astype(np.float32)).astype(BF16)
    return jnp.asarray(hi), jnp.asarray(lo)


def _const_spec(shape):
    return pl.BlockSpec(shape, lambda *_: tuple(0 for _ in shape), pipeline_mode=pl.Buffered(1))


def _layer_spec(shape, l):
    return pl.BlockSpec((None,) + tuple(shape), lambda *_: (l,) + tuple(0 for _ in shape),
                        pipeline_mode=pl.Buffered(1))


@functools.lru_cache(maxsize=None)
def _dft_tables_np(L):
    k = np.arange(L, dtype=np.int64)
    m1 = ((2 * k[:, None] + 1) * k[None, :]) % (4 * L)
    a1 = np.pi * m1.astype(np.float64) / (2 * L)
    m2 = ((2 * k[:, None] + 1) * (2 * k[None, :] + 1)) % (8 * L)
    a2 = np.pi * m2.astype(np.float64) / (4 * L)
    return np.cos(a1), np.sin(a1), np.cos(a2), np.sin(a2)


@functools.lru_cache(maxsize=None)
def _filter_consts_np(L):
    t_idx = np.arange(L, dtype=np.float64)
    t = t_idx / (L - 1)
    bands = np.arange(1, HY_BANDS + 1, dtype=np.float64)
    ang = 2.0 * math.pi * t_idx[:, None] * bands[None, :] / L
    z = np.zeros((L, LANES), np.float64)
    z[:, 0] = t
    z[:, 1:1 + HY_BANDS] = np.cos(ang)
    z[:, 1 + HY_BANDS:1 + 2 * HY_BANDS] = -np.sin(ang)
    lin = np.linspace(math.log(HY_DECAY_TARGET) / HY_DECAY_LONG,
                      math.log(HY_DECAY_TARGET) / HY_DECAY_SHORT, HY_W).astype(np.float32)
    deltas = np.abs(lin).astype(np.float64)
    decay = np.exp(-t[:, None] * deltas[None, :])
    return z.astype(np.float32), decay.astype(np.float32)


@functools.lru_cache(maxsize=None)
def _pos_embed_np(L):
    rows = L // GRID_W
    r, cidx = np.meshgrid(np.arange(rows, dtype=np.float64), np.arange(GRID_W, dtype=np.float64), indexing="ij")
    r = r.reshape(-1)
    cidx = cidx.reshape(-1)
    quarter = D_MODEL // 4
    omega = 1.0 / (10000.0 ** (np.arange(quarter, dtype=np.float64) / quarter))
    ar = r[:, None] * omega[None, :]
    ac = cidx[:, None] * omega[None, :]
    return np.concatenate([np.sin(ar), np.cos(ar), np.sin(ac), np.cos(ac)], axis=-1).astype(np.float32)


def _ada_kernel(c_ref, w_ref, b_ref, o_ref):
    sc = _silu(c_ref[...])
    o_ref[0] = _dot3(sc, w_ref[0]) + b_ref[0]


def _ada(cond, ada_w, ada_b):
    tn = 1536
    nd = 6 * D_MODEL
    return pl.pallas_call(
        _ada_kernel,
        out_shape=jax.ShapeDtypeStruct((DEPTH, 8, nd), F32),
        grid=(DEPTH, nd // tn),
        in_specs=[
            pl.BlockSpec((8, D_MODEL), lambda l, j: (0, 0)),
            pl.BlockSpec((1, D_MODEL, tn), lambda l, j: (l, 0, j)),
            pl.BlockSpec((1, 1, tn), lambda l, j: (l, 0, j)),
        ],
        out_specs=pl.BlockSpec((1, 8, tn), lambda l, j: (l, 0, j)),
        compiler_params=pltpu.CompilerParams(
            dimension_semantics=("arbitrary", "arbitrary"), vmem_limit_bytes=VMEM_LIMIT),
        name="ada_mod",
    )(cond, ada_w, ada_b.reshape(DEPTH, 1, nd))


def _inproj_kernel(*refs, with_pos, L):
    if with_pos:
        (x_ref, pos_ref, mod_ref, g_ref, w_ref, wvt_ref, wg_ref, scw_ref, hyw_ref, mlw_ref,
         ysc_ref, hy_ref, qk_ref, vt_ref, og_ref, gate_ref, xo_ref) = refs
        x = x_ref[...] + pos_ref[...]
        xo_ref[...] = x
    else:
        (x_ref, mod_ref, g_ref, w_ref, wvt_ref, wg_ref, scw_ref, hyw_ref, mlw_ref,
         ysc_ref, hy_ref, qk_ref, vt_ref, og_ref, gate_ref) = refs
        x = x_ref[...]
    mod = mod_ref[0]
    sh1 = mod[:, 0:D_MODEL]
    sc1 = mod[:, D_MODEL:2 * D_MODEL]
    u = (_rms(x) * g_ref[...]) * (1.0 + sc1) + sh1
    ub = u.astype(BF16)
    o_hy = 3 * SC_W
    o_q = o_hy + 3 * HY_W
    o_k, o_o = o_q + ML_W, o_q + 3 * ML_W

    cw = INPROJ_CHUNK
    mlw = mlw_ref[...]
    proj = lambda c0, c1: (lambda: _dot(ub, w_ref[:, c0:c1]))
    work = []

    def sc_epilogue(sc):
        ysc_ref[...] = sc[:, 0:SC_W] * _conv3(sc[:, SC_W:2 * SC_W] * sc[:, 2 * SC_W:3 * SC_W], scw_ref[...], L)
    work.append((proj(0, o_hy), sc_epilogue))
    for c in range(0, 3 * HY_W, cw):
        def hy_epilogue(r, c=c):
            hy_ref[:, c:c + cw] = _conv3(r, hyw_ref[:, c:c + cw], L)
        work.append((proj(o_hy + c, o_hy + c + cw), hy_epilogue))
    for c in range(0, 2 * ML_W, cw):
        def qk_epilogue(r, c=c):
            a = _silu(_conv3(r, mlw[:, c:c + cw], L))
            qk_ref[:, c:c + cw] = (a if c < ML_W else a * (ML_DH ** -0.5)).astype(BF16)
        work.append((proj(o_q + c, o_q + c + cw), qk_epilogue))
    for c in range(0, ML_W, cw):
        def vt_epilogue(r, c=c):
            vt_ref[c:c + cw, :] = r.astype(BF16)
        work.append((lambda c=c: _dot_nt(wvt_ref[c:c + cw, :], ub), vt_epilogue))

        def og_epilogue(r, c=c):
            og_ref[:, c:c + cw] = _sigmoid(r)
        work.append((proj(o_o + c, o_o + c + cw), og_epilogue))

    def gate_epilogue(r):
        gate_ref[...] = r
    work.append((lambda: _dot_nt(wg_ref[...], ub), gate_epilogue))

    pending = None
    for matmul, epilogue in work:
        result = matmul()
        if pending is not None:
            pending[1](pending[0])
        pending = (result, epilogue)
    pending[1](pending[0])


def _mod_spec(l, L, tm, per_request):
    if per_request:
        return pl.BlockSpec((1, 1, 6 * D_MODEL), lambda i: (l * 8 + 1 + (i * tm) // L, 0, 0))
    return pl.BlockSpec((1, 1, 6 * D_MODEL), lambda i: (l * 8, 0, 0))


def _inproj(x2d, pos, mods, g, w_in_b, w_vt, w_gate, sc_w, hy_w, ml_w, *, l, L, per_request):
    T = x2d.shape[0]
    tm = max(ROW_BLOCK, L)
    nblk = T // tm
    with_pos = pos is not None
    row = lambda width: pl.BlockSpec((tm, width), lambda i: (i, 0))
    in_specs = [row(D_MODEL)]
    args = [x2d]
    if with_pos:
        assert tm == L
        in_specs.append(_const_spec((tm, D_MODEL)))
        args.append(pos)
    in_specs += [
        _mod_spec(l, L, tm, per_request),
        _layer_spec((1, D_MODEL), l),
        _layer_spec((D_MODEL, N_MAIN), l),
        _layer_spec((ML_W, D_MODEL), l),
        _layer_spec((N_GATE, D_MODEL), l),
        _layer_spec((3, SC_W), l), _layer_spec((3, 3 * HY_W), l), _layer_spec((3, 2 * ML_W), l),
    ]
    args += [mods, g, w_in_b, w_vt, w_gate, sc_w, hy_w, ml_w]
    out_shape = [jax.ShapeDtypeStruct((T, SC_W), F32), jax.ShapeDtypeStruct((T, 3 * HY_W), F32),
                 jax.ShapeDtypeStruct((T, 2 * ML_W), BF16), jax.ShapeDtypeStruct((ML_W, T), BF16),
                 jax.ShapeDtypeStruct((T, ML_W), F32), jax.ShapeDtypeStruct((N_GATE, T), F32)]
    out_specs = [row(SC_W), row(3 * HY_W), row(2 * ML_W), pl.BlockSpec((ML_W, tm), lambda i: (0, i)),
                 row(ML_W), pl.BlockSpec((N_GATE, tm), lambda i: (0, i))]
    if with_pos:
        out_shape.append(jax.ShapeDtypeStruct((T, D_MODEL), F32))
        out_specs.append(row(D_MODEL))
    return pl.pallas_call(
        functools.partial(_inproj_kernel, with_pos=with_pos, L=L),
        out_shape=out_shape,
        grid=(nblk,),
        in_specs=in_specs,
        out_specs=out_specs,
        compiler_params=pltpu.CompilerParams(
            dimension_semantics=("arbitrary",), vmem_limit_bytes=VMEM_LIMIT),
        name="inproj",
    )(*args)


def _filter_kernel(z_ref, dec_ref, w1_ref, b1_ref, w2_ref, b2_ref, w3_ref, f_ref,
                   ch_ref, cl_ref, sh_ref, sl_ref, hre_ref, him_ref, *, L):
    T = HY_BLOCK
    P = L // T
    f = f_ref[...]
    hdn = jnp.sin(f * (_dot3(z_ref[...], w1_ref[...]) + b1_ref[...]))
    hdn = jnp.sin(f * (_dot3(hdn, w2_ref[...]) + b2_ref[...]))
    h = _dot3(hdn, w3_ref[...])
    dec = dec_ref[...]
    row = lax.broadcasted_iota(jnp.int32, (L, HY_W), 0)
    halves = []
    for o in range(HY_ORDER):
        hf = h[:, (2 * o) * HY_W:(2 * o + 1) * HY_W] * dec
        hb = jnp.where(row == 0, 0.0, h[:, (2 * o + 1) * HY_W:(2 * o + 2) * HY_W] * dec)
        nrm = jnp.sum(jnp.abs(hf), axis=0, keepdims=True) + jnp.sum(jnp.abs(hb), axis=0, keepdims=True)
        inv = 1.0 / nrm
        halves.append((hf * inv, hb * inv))
    blocks = [half[e * T:(e + 1) * T] for pair in halves for half in pair for e in range(P)]
    x = jnp.concatenate(blocks, axis=1)
    xre = _dot3_tab(ch_ref[...], cl_ref[...], x)
    xim = -_dot3_tab(sh_ref[...], sl_ref[...], x)
    sgn = jnp.where((lax.broadcasted_iota(jnp.int32, (T, HY_W), 0) & 1) == 0, 1.0, -1.0)
    for o in range(HY_ORDER):
        def half_spec(which, e):
            j = (o * 2 + which) * P + e
            cs = slice(j * HY_W, (j + 1) * HY_W)
            return xre[:, cs], xim[:, cs], halves[o][which][e * T:e * T + 1]
        for d in range(-(P - 1), P):
            if d == 0:
                (fre, fim, _), (gre, gim, _) = half_spec(0, 0), half_spec(1, 0)
                re, im = fre + gre, fim - gim
            elif d > 0:
                (fre, fim, _), (pre, pim, p0) = half_spec(0, d), half_spec(0, d - 1)
                re, im = fre - sgn * pim, fim + sgn * (pre - p0)
            else:
                (gre, gim, _), (pre, pim, p0) = half_spec(1, -d), half_spec(1, -d - 1)
                re, im = gre - sgn * pim, -gim - sgn * (pre - p0)
            hre_ref[d + P - 1, :, o * HY_W:(o + 1) * HY_W] = re
            him_ref[d + P - 1, :, o * HY_W:(o + 1) * HY_W] = im


def _filter_spectrum(L, w1p, b1p, w2p, b2p, w3p, fp):
    T = HY_BLOCK
    z_np, dec_np = _filter_consts_np(L)
    c1, s1, _, _ = _dft_tables_np(T)
    ch, cl = _hi_lo(c1)
    sh, sl = _hi_lo(s1)
    n = HY_ORDER * HY_W
    nd = 2 * (L // T) - 1
    const = _const_spec
    per_layer = lambda shape: pl.BlockSpec((None,) + shape, lambda l: (l,) + tuple(0 for _ in shape))
    return pl.pallas_call(
        functools.partial(_filter_kernel, L=L),
        out_shape=[jax.ShapeDtypeStruct((DEPTH, nd, T, n), F32), jax.ShapeDtypeStruct((DEPTH, nd, T, n), F32)],
        grid=(DEPTH,),
        in_specs=[const((L, LANES)), const((L, HY_W)), per_layer((LANES, LANES)), per_layer((1, LANES)),
                  per_layer((LANES, LANES)), per_layer((1, LANES)), per_layer((LANES, 2 * n)), per_layer((1, LANES)),
                  const((T, T)), const((T, T)), const((T, T)), const((T, T))],
        out_specs=[per_layer((nd, T, n)), per_layer((nd, T, n))],
        compiler_params=pltpu.CompilerParams(
            dimension_semantics=("arbitrary",), vmem_limit_bytes=VMEM_LIMIT),
        name="hyena_filter",
    )(jnp.asarray(z_np), jnp.asarray(dec_np), w1p, b1p, w2p, b2p, w3p, fp, ch, cl, sh, sl)


def _hyena_kernel(hy_ref, skip_ref, fwd_ref, inv_ref, hre_ref, him_ref, out_ref, *, L):
    T = HY_BLOCK
    P = L // T
    NB = hy_ref.shape[0] // T
    z = hy_ref[:, 0:HY_W]
    gates = (hy_ref[:, HY_W:2 * HY_W], hy_ref[:, 2 * HY_W:3 * HY_W])
    fwd, inv = fwd_ref[...], inv_ref[...]
    lanes = lambda j: slice(j * HY_W, (j + 1) * HY_W)
    for o in range(HY_ORDER):
        zcat = jnp.concatenate([z[j * T:(j + 1) * T] for j in range(NB)], axis=1)
        spec = _dot(fwd, zcat.astype(BF16))
        yre, yim = [], []
        for bi in range(NB):
            s0, i = bi - bi % P, bi % P
            re = im = None
            for j in range(P):
                d = i - j + P - 1
                hre = hre_ref[d, :, lanes(o)]
                him = him_ref[d, :, lanes(o)]
                zc = spec[0:T, lanes(s0 + j)]
                zs = spec[T:2 * T, lanes(s0 + j)]
                pre = zc * hre + zs * him
                pim = zc * him - zs * hre
                re = pre if re is None else re + pre
                im = pim if im is None else im + pim
            yre.append(re)
            yim.append(im)
        prod = jnp.concatenate([jnp.concatenate(yre, axis=1), jnp.concatenate(yim, axis=1)], axis=0)
        ycat = _dot(inv, prod.astype(BF16)) * (1.0 / T)
        y = jnp.concatenate([ycat[:, lanes(bi)] for bi in range(NB)], axis=0)
        z = gates[o] * (y + skip_ref[o:o + 1, :] * z)
    out_ref[...] = z


def _hyena(hy, skip, hre, him, *, l, B, L):
    T = HY_BLOCK
    _, _, c2, s2 = _dft_tables_np(T)
    fwd = jnp.asarray(np.concatenate([c2, s2], axis=0).astype(np.float32).astype(BF16))
    inv = jnp.asarray(np.concatenate([c2, -s2], axis=1).astype(np.float32).astype(BF16))
    n = HY_ORDER * HY_W
    nd = 2 * (L // T) - 1
    rows = max(L, HY_STEP_ROWS)
    return pl.pallas_call(
        functools.partial(_hyena_kernel, L=L),
        out_shape=jax.ShapeDtypeStruct((B * L, HY_W), F32),
        grid=(B * L // rows,),
        in_specs=[
            pl.BlockSpec((rows, 3 * HY_W), lambda b: (b, 0)),
            _layer_spec((HY_ORDER, HY_W), l),
            _const_spec((2 * T, T)), _const_spec((T, 2 * T)),
            _layer_spec((nd, T, n), l), _layer_spec((nd, T, n), l),
        ],
        out_specs=pl.BlockSpec((rows, HY_W), lambda b: (b, 0)),
        compiler_params=pltpu.CompilerParams(
            dimension_semantics=("arbitrary",), vmem_limit_bytes=VMEM_LIMIT),
        name="hyena",
    )(hy, skip, fwd, inv, hre, him)


def _gate_prep_kernel(*refs, B, L, with_state):
    if with_state:
        g_ref, gb_ref, m0_ref, rows_ref, cols_ref = refs
    else:
        g_ref, gb_ref, rows_ref, cols_ref = refs
    R = B * N_GATE
    gb = gb_ref[:, 0:1]
    x = jnp.concatenate([g_ref[:, b * L:(b + 1) * L] + gb for b in range(B)], axis=0)
    rr = lax.broadcasted_iota(jnp.int32, (R, L), 0) & (N_GATE - 1)
    is_f = ((rr >= ML_H) & (rr < 2 * ML_H)) | (rr >= 3 * ML_H)
    fwd_rows = rr < 2 * ML_H
    pk = jnp.where(is_f, _log_sigmoid(x), x)
    r_i = lax.broadcasted_iota(jnp.int32, (L, L), 0)
    c_i = lax.broadcasted_iota(jnp.int32, (L, L), 1)
    triu = jnp.where(r_i <= c_i, 1.0, 0.0).astype(BF16)
    cs = _dot(jnp.concatenate(_split3(pk), axis=0), triu)
    cum = cs[0:R] + cs[R:2 * R] + cs[2 * R:3 * R]
    suf = cum[:, L - 1:L] - cum + pk
    bsum = pltpu.roll(jnp.where(fwd_rows, cum, suf), R - ML_H, axis=0)
    r = pk - bsum
    cm = jnp.where(fwd_rows, _scan_max(r, reverse=False), _scan_max(r, reverse=True))
    if with_state:
        m0 = jnp.concatenate([m0_ref[b][:, 0:1] for b in range(B)], axis=0)
        m = jnp.maximum(m0, cm)
    else:
        m = jnp.maximum(cm, 0.0)
    rl = r * LOG2E
    ml = m * LOG2E
    mt = bsum + m
    pieces = [rl, ml, jnp.exp(-mt), mt]
    if not with_state:
        pieces.append(jnp.exp2(rl - jnp.where(fwd_rows, ml[:, L - 1:L], ml[:, 0:1])))
    pad = jnp.zeros((LANES - N_GATE, L), F32)
    for b in range(B):
        bs = slice(b * N_GATE, (b + 1) * N_GATE)
        for j, piece in enumerate(pieces):
            rows_ref[b, j * N_GATE:(j + 1) * N_GATE, :] = piece[bs]
        cols_ref[b * L:(b + 1) * L, :] = jnp.concatenate([rl[bs], pad], axis=0).T


def _gate_prep(g_t, gate_b, m0, *, l, B, L):
    with_state = m0 is not None
    n_rows = (4 if with_state else 5) * N_GATE
    full = lambda shape: pl.BlockSpec(shape, lambda i: tuple(0 for _ in shape))
    args = [g_t, gate_b]
    in_specs = [full((N_GATE, B * L)), pl.BlockSpec((None, N_GATE, LANES), lambda i: (l, 0, 0))]
    if with_state:
        args.append(m0)
        in_specs.append(pl.BlockSpec((B, None, N_GATE, LANES), lambda i: (0, l, 0, 0)))
    return pl.pallas_call(
        functools.partial(_gate_prep_kernel, B=B, L=L, with_state=with_state),
        out_shape=[jax.ShapeDtypeStruct((B, n_rows, L), F32), jax.ShapeDtypeStruct((B * L, LANES), F32)],
        grid=(1,),
        in_specs=in_specs,
        out_specs=[full((B, n_rows, L)), full((B * L, LANES))],
        compiler_params=pltpu.CompilerParams(
            dimension_semantics=("arbitrary",), vmem_limit_bytes=VMEM_LIMIT),
        name="gate_prep",
    )(*args)


def _mlstm_kernel(*refs, L, with_state, n_prev):
    if with_state:
        (qk_ref, vt_ref, og_ref, rows_ref, cols_ref, ng_ref,
         c0_ref, n0_ref, m0_ref, y_ref) = refs
    elif n_prev:
        (qk_ref, vt_ref, og_ref, rows_ref, cols_ref, ng_ref,
         cprev_ref, nprev_ref, mprev_ref, y_ref, cout_ref, nout_ref, mout_ref) = refs
        cout_ref[0, 0:n_prev] = cprev_ref[0]
        nout_ref[0, 0:n_prev] = nprev_ref[0]
        mout_ref[0, 0:n_prev] = mprev_ref[0]
        cout_ref, nout_ref, mout_ref = (r.at[:, n_prev] for r in (cout_ref, nout_ref, mout_ref))
    else:
        (qk_ref, vt_ref, og_ref, rows_ref, cols_ref, ng_ref,
         y_ref, cout_ref, nout_ref, mout_ref) = refs

    tb = min(L // 2, ATT_BLOCK)
    rl16, ml16, em16, mt16 = (rows_ref[0, j * N_GATE:(j + 1) * N_GATE, :] for j in range(4))
    rl_cols = cols_ref[...]
    if not with_state:
        w16 = rows_ref[0, 4 * N_GATE:5 * N_GATE, :]

    s_loc = lax.broadcasted_iota(jnp.int32, (tb, tb), 0)
    t_loc = lax.broadcasted_iota(jnp.int32, (tb, tb), 1)
    masks = (s_loc <= t_loc, s_loc >= t_loc)
    ones_rows = jnp.ones((2 * 8, L), BF16)
    ng = ng_ref[...]
    if with_state:
        n0_hi, n0_lo = _split2(n0_ref[0])
    else:
        w_hi, w_lo = _split2(w16)
    for h in range(ML_H):
        hs = slice(h * ML_DH, (h + 1) * ML_DH)
        qb = qk_ref[:, h * ML_DH:(h + 1) * ML_DH]
        kb = qk_ref[:, ML_W + h * ML_DH:ML_W + (h + 1) * ML_DH]
        vt = vt_ref[hs, :]
        vext_t = jnp.concatenate([vt, ones_rows], axis=0)
        rlb = [jnp.broadcast_to(rl_cols[:, 2 * ML_H * d + h:2 * ML_H * d + h + 1], (L, tb)) for d in range(2)]
        if with_state:
            c0_t = [c0_ref[0, d, h].T.astype(BF16) for d in range(2)]
        for r0 in range(0, L, tb):
            rs = slice(r0, r0 + tb)
            s_t = _dot_nt(kb, qb[rs])
            if with_state:
                qn_rows = _dot_nt(n0_hi, qb[rs]) + _dot_nt(n0_lo, qb[rs])
            hout_t = None
            for d in range(2):
                o8 = 2 * ML_H * d + h
                ml_row = ml16[o8:o8 + 1, rs]
                e_diag = jnp.exp2(jnp.where(masks[d], rlb[d][rs] - ml_row, NEG_BIG))
                acc = _dot(vext_t[:, rs], (s_t[rs] * e_diag).astype(BF16))
                side = slice(r0 + tb, L) if d else slice(0, r0)
                if side.stop > side.start:
                    e_side = jnp.exp2(rlb[d][side] - ml_row)
                    acc = acc + _dot(vext_t[:, side], (s_t[side] * e_side).astype(BF16))
                num_t = acc[0:ML_DH]
                den = acc[ML_DH:ML_DH + 1]
                if with_state:
                    wp = jnp.exp2(m0_ref[0, o8:o8 + 1, 0:1] * LOG2E - ml_row)
                    den = den + wp * qn_rows[d * ML_H + h:d * ML_H + h + 1]
                    num_t = num_t + wp * _dot_nt(c0_t[d], qb[rs])
                contrib = num_t * (1.0 / jnp.maximum(jnp.abs(den), em16[o8:o8 + 1, rs]))
                hout_t = contrib if hout_t is None else hout_t + contrib
            hn_t = hout_t * lax.rsqrt(jnp.mean(hout_t * hout_t, axis=0, keepdims=True) + EPS)
            y_ref[rs, hs] = og_ref[rs, hs] * (hn_t.T * ng[:, hs])

        if not with_state:
            n_rows = _dot(w_hi, kb) + _dot(w_lo, kb)
            vt_f = vt.astype(F32)
            for d in range(2):
                o8 = 2 * ML_H * d + h
                end = 0 if d else L - 1
                c_t = _dot((vt_f * w16[o8:o8 + 1, :]).astype(BF16), kb)
                cout_ref[0, d, h] = c_t.T
                nout_ref[0, d * ML_H + h:d * ML_H + h + 1, :] = n_rows[o8:o8 + 1, :]
                mout_ref[0, d * ML_H + h:d * ML_H + h + 1, :] = jnp.broadcast_to(
                    mt16[o8:o8 + 1, end:end + 1], (1, LANES))


def _mlstm(qk, v_t, o_gate, g_t, gate_b, norm_g, state, prev, *, l, B, L):
    with_state = state is not None
    n_prev = 0 if prev is None else l
    rows, cols = _gate_prep(g_t, gate_b, state[2] if with_state else None, l=l, B=B, L=L)
    in_specs = [
        pl.BlockSpec((L, 2 * ML_W), lambda b: (b, 0)),
        pl.BlockSpec((ML_W, L), lambda b: (0, b)),
        pl.BlockSpec((L, ML_W), lambda b: (b, 0)),
        pl.BlockSpec((1, rows.shape[1], L), lambda b: (b, 0, 0)),
        pl.BlockSpec((L, LANES), lambda b: (b, 0)),
        _layer_spec((1, ML_W), l),
    ]
    args = [qk, v_t, o_gate, rows, cols, norm_g]
    y_shape = jax.ShapeDtypeStruct((B * L, ML_W), F32)
    y_spec = pl.BlockSpec((L, ML_W), lambda b: (b, 0))
    c_blk, n_blk, m_blk = (2, ML_H, ML_DH, ML_DH), (2 * ML_H, ML_DH), (2 * ML_H, LANES)
    if with_state:
        c0, n0, m0 = state
        in_specs += [
            pl.BlockSpec((1, None) + c_blk, lambda b: (b, l, 0, 0, 0, 0)),
            pl.BlockSpec((1, None) + n_blk, lambda b: (b, l, 0, 0)),
            pl.BlockSpec((1, None, N_GATE, LANES), lambda b: (b, l, 0, 0)),
        ]
        args += [c0, n0, m0]
        out_shape, out_specs = y_shape, y_spec
    else:
        lead = (l + 1,) if n_prev else ()
        if n_prev:
            in_specs += [pl.BlockSpec((1, n_prev) + blk, lambda b, nz=len(blk) + 1: (b,) + (0,) * nz)
                         for blk in (c_blk, n_blk, m_blk)]
            args += list(prev)
        out_shape = [y_shape] + [jax.ShapeDtypeStruct((B,) + lead + blk, F32) for blk in (c_blk, n_blk, m_blk)]
        out_specs = [y_spec] + [pl.BlockSpec((1,) + lead + blk, lambda b, nz=len(lead) + len(blk): (b,) + (0,) * nz)
                                for blk in (c_blk, n_blk, m_blk)]
    return pl.pallas_call(
        functools.partial(_mlstm_kernel, L=L, with_state=with_state, n_prev=n_prev),
        out_shape=out_shape,
        grid=(B,),
        in_specs=in_specs,
        out_specs=out_specs,
        compiler_params=pltpu.CompilerParams(
            dimension_semantics=("arbitrary",), vmem_limit_bytes=VMEM_LIMIT),
        name="mlstm",
    )(*args)


def _outmlp_kernel(x_ref, ysc_ref, yhy_ref, yml_ref, mod_ref, wo_ref, g2_ref, w1_ref, w2_ref, gf_ref, o_ref, *,
                   final):
    mod = mod_ref[0]
    g1 = mod[:, 2 * D_MODEL:3 * D_MODEL]
    sh2 = mod[:, 3 * D_MODEL:4 * D_MODEL]
    sc2 = mod[:, 4 * D_MODEL:5 * D_MODEL]
    g2 = mod[:, 5 * D_MODEL:6 * D_MODEL]
    o_hy, o_ml = SC_W, SC_W + HY_W
    attn = (_dot(ysc_ref[...].astype(BF16), wo_ref[0:o_hy, :]) + _dot(yhy_ref[...].astype(BF16), wo_ref[o_hy:o_ml, :])
            + _dot(yml_ref[...].astype(BF16), wo_ref[o_ml:, :]))
    x = x_ref[...] + g1 * attn
    u = ((_rms(x) * g2_ref[...]) * (1.0 + sc2) + sh2).astype(BF16)
    tf = 1024
    acc = jnp.zeros(x.shape, F32)
    for j in range(D_FF // tf):
        hcol = _dot(u, w1_ref[:, j * tf:(j + 1) * tf])
        hcol = jnp.square(jnp.maximum(hcol, 0.0)).astype(BF16)
        acc = acc + _dot(hcol, w2_ref[j * tf:(j + 1) * tf, :])
    x = x + g2 * acc
    if final:
        x = _rms(x) * gf_ref[...]
    o_ref[...] = x


def _outmlp(x2d, ysc, yhy, yml, mods, w_out, g2, w1, w2, gf, *, l, L, per_request, final):
    T = x2d.shape[0]
    tm = ROW_BLOCK
    return pl.pallas_call(
        functools.partial(_outmlp_kernel, final=final),
        out_shape=jax.ShapeDtypeStruct((T, D_MODEL), F32),
        grid=(T // tm,),
        in_specs=[
            pl.BlockSpec((tm, D_MODEL), lambda i: (i, 0)),
            pl.BlockSpec((tm, SC_W), lambda i: (i, 0)),
            pl.BlockSpec((tm, HY_W), lambda i: (i, 0)),
            pl.BlockSpec((tm, ML_W), lambda i: (i, 0)),
            _mod_spec(l, L, tm, per_request),
            _layer_spec((D_MODEL, D_MODEL), l), _layer_spec((1, D_MODEL), l),
            _layer_spec((D_MODEL, D_FF), l), _layer_spec((D_FF, D_MODEL), l), _const_spec((1, D_MODEL)),
        ],
        out_specs=pl.BlockSpec((tm, D_MODEL), lambda i: (i, 0)),
        compiler_params=pltpu.CompilerParams(
            dimension_semantics=("arbitrary",), vmem_limit_bytes=VMEM_LIMIT),
        name="outproj_mlp",
    )(x2d, ysc, yhy, yml, mods, w_out, g2, w1, w2, gf)


def kernel(x_prompt, x_sample, state_C, state_n, state_m, c, c_ctx, norm1_g, ada_w, ada_b, w_in, sc_conv_w, hy_conv_w, hy_w1, hy_b1, hy_w2, hy_b2, hy_w3, hy_freq, hy_skip, ml_conv_w, ml_gate_b, ml_norm_g, w_out, norm2_g, mlp_w1, mlp_w2, norm_f_g):
    B, L = x_prompt.shape[0], x_prompt.shape[1]
    Bd, Ld = x_sample.shape[0], x_sample.shape[1]

    cond = jnp.concatenate([c_ctx[None, :], c, jnp.zeros((8 - 1 - Bd, D_MODEL), F32)], axis=0)
    mods = _ada(cond, ada_w, ada_b).reshape(DEPTH * 8, 1, 6 * D_MODEL)

    w_in_b = w_in.astype(BF16)
    w_gate = jnp.swapaxes(w_in[:, :, N_MAIN:], 1, 2).astype(BF16)
    w_out_b = w_out.astype(BF16)
    w1_b = mlp_w1.astype(BF16)
    w2_b = mlp_w2.astype(BF16)
    gate_b = jnp.broadcast_to(ml_gate_b[:, :, None], (DEPTH, N_GATE, LANES))
    pad_h = LANES - HY_HID
    w1p = jnp.pad(hy_w1, ((0, 0), (0, LANES - HY_EMB), (0, pad_h)))
    b1p = jnp.pad(hy_b1, ((0, 0), (0, pad_h)))
    w2p = jnp.pad(hy_w2, ((0, 0), (0, pad_h), (0, pad_h)))
    b2p = jnp.pad(hy_b2, ((0, 0), (0, pad_h)))
    w3p = jnp.pad(hy_w3, ((0, 0), (0, pad_h), (0, 0)))
    fp = jnp.pad(hy_freq, ((0, 0), (0, pad_h)))
    pos = jnp.asarray(_pos_embed_np(Ld))
    m0_all = jnp.pad(state_m, ((0, 0), (0, 0), (0, 0), (0, ML_H))).reshape(Bd, DEPTH, N_GATE, 1)
    m0_all = jnp.broadcast_to(m0_all, (Bd, DEPTH, N_GATE, LANES))
    n0_all = state_n.reshape(Bd, DEPTH, 2 * ML_H, ML_DH)
    gf = norm_f_g[None, :]
    g1 = norm1_g[:, None, :]
    g2 = norm2_g[:, None, :]
    ml_ng = ml_norm_g[:, None, :]
    filt_args = (w1p, b1p[:, None, :], w2p, b2p[:, None, :], w3p, fp[:, None, :])
    hre_c, him_c = _filter_spectrum(L, *filt_args)
    hre_s, him_s = _filter_spectrum(Ld, *filt_args)
    state = (state_C, n0_all, m0_all)
    w_vt = jnp.swapaxes(w_in[:, :, N_MAIN - 2 * ML_W:N_MAIN - ML_W], 1, 2).astype(BF16)
    proj_args = (mods, g1, w_in_b, w_vt, w_gate, sc_conv_w, hy_conv_w, ml_conv_w)
    mlp_args = (mods, w_out_b, g2, w1_b, w2_b, gf)

    xp = x_prompt.reshape(B * L, D_MODEL)
    xs = x_sample.reshape(Bd * Ld, D_MODEL)
    prev = None
    for l in range(DEPTH):
        final = l == DEPTH - 1

        ysc, hy, qk, v_t, o_gate, g_t = _inproj(xp, None, *proj_args, l=l, L=L, per_request=False)
        yhy = _hyena(hy, hy_skip, hre_c, him_c, l=l, B=B, L=L)
        yml, *states = _mlstm(qk, v_t, o_gate, g_t, gate_b, ml_ng, None, prev, l=l, B=B, L=L)
        xp = _outmlp(xp, ysc, yhy, yml, *mlp_args, l=l, L=L, per_request=False, final=final)
        if not final:
            prev = [s.reshape((B, 1) + s.shape[1:]) for s in states] if l == 0 else states

        res = _inproj(xs, pos if l == 0 else None, *proj_args, l=l, L=Ld, per_request=True)
        if l == 0:
            *res, xs = res
        ysc, hy, qk, v_t, o_gate, g_t = res
        yhy = _hyena(hy, hy_skip, hre_s, him_s, l=l, B=Bd, L=Ld)
        yml = _mlstm(qk, v_t, o_gate, g_t, gate_b, ml_ng, state, None, l=l, B=Bd, L=Ld)
        xs = _outmlp(xs, ysc, yhy, yml, *mlp_args, l=l, L=Ld, per_request=True, final=final)

    y_prompt = xp.reshape(B, L, D_MODEL)
    y_sample = xs.reshape(Bd, Ld, D_MODEL)
    new_c, new_n, new_m = states
    return (y_prompt, y_sample, new_c, new_n.reshape(B, DEPTH, 2, ML_H, ML_DH),
            new_m[..., 0].reshape(B, DEPTH, 2, ML_H))
```

```python
import functools
import math

import numpy as np
import jax
import jax.numpy as jnp
from jax import lax
from jax.experimental import pallas as pl
from jax.experimental.pallas import tpu as pltpu

F32 = jnp.float32
BF16 = jnp.bfloat16

D_MODEL = 1024
DEPTH = 2
GRID_W = 64
SC_W = 256
HY_W = 256
ML_W = 512
ML_H = 4
ML_DH = ML_W // ML_H
D_FF = 4 * D_MODEL
HY_ORDER = 2
HY_BANDS = 16
HY_EMB = 1 + 2 * HY_BANDS
HY_HID = 64
HY_DECAY_SHORT = 0.3
HY_DECAY_LONG = 1.5
HY_DECAY_TARGET = 1e-2
EPS = 1e-6
N_MAIN = 3 * SC_W + 3 * HY_W + 4 * ML_W
N_GATE = 4 * ML_H
LANES = 128
ROW_BLOCK = 512
ATT_BLOCK = 256
HY_BLOCK = 256
HY_STEP_ROWS = 1024
NEG_BIG = -1e30
LOG2E = math.log2(math.e)
VMEM_LIMIT = 60 * 1024 * 1024


def _dot(a, b):
    return jnp.dot(a, b, preferred_element_type=F32)


def _dot_nt(a, b):
    return lax.dot_general(a, b, (((1,), (1,)), ((), ())), preferred_element_type=F32)


def _split2(a):
    hi = a.astype(BF16)
    lo = (a - hi.astype(F32)).astype(BF16)
    return hi, lo


def _split3(a):
    hi = a.astype(BF16)
    r = a - hi.astype(F32)
    mid = r.astype(BF16)
    lo = (r - mid.astype(F32)).astype(BF16)
    return hi, mid, lo


def _dot3(a, b):
    ah, al = _split2(a)
    bh, bl = _split2(b)
    return _dot(ah, bh) + _dot(al, bh) + _dot(ah, bl)


def _dot3_nt(a, b):
    ah, al = _split2(a)
    bh, bl = _split2(b)
    return _dot_nt(ah, bh) + _dot_nt(al, bh) + _dot_nt(ah, bl)


def _dot3_tab(th, tl, b):
    bh, bl = _split2(b)
    return _dot(th, bh) + _dot(tl, bh) + _dot(th, bl)


def _sigmoid(x):
    return 1.0 / (1.0 + jnp.exp(-x))


def _silu(x):
    return x * _sigmoid(x)


def _log_sigmoid(x):
    return jnp.minimum(x, 0.0) - jnp.log(1.0 + jnp.exp(-jnp.abs(x)))


def _conv3(x, w, seq_len=None):
    R = x.shape[0]
    seq_len = R if seq_len is None else seq_len
    pos = lax.broadcasted_iota(jnp.int32, x.shape, 0) & (seq_len - 1)
    xm = jnp.where(pos == 0, 0.0, pltpu.roll(x, 1, axis=0))
    xp = jnp.where(pos == seq_len - 1, 0.0, pltpu.roll(x, R - 1, axis=0))
    return xm * w[0:1, :] + x * w[1:2, :] + xp * w[2:3, :]


def _rms(x):
    return x * lax.rsqrt(jnp.mean(x * x, axis=-1, keepdims=True) + EPS)


def _scan_max(x, *, reverse):
    n = x.shape[1]
    lane = lax.broadcasted_iota(jnp.int32, x.shape, 1)
    d = 1
    while d < n:
        if reverse:
            shifted = jnp.where(lane < n - d, pltpu.roll(x, n - d, axis=1), NEG_BIG)
        else:
            shifted = jnp.where(lane >= d, pltpu.roll(x, d, axis=1), NEG_BIG)
        x = jnp.maximum(x, shifted)
        d *= 2
    return x


def _hi_lo(a64):
    a = a64.astype(np.float32)
    hi = a.astype(BF16)
    lo = (a - hi.astype(np.float32)).astype(BF16)
    return jnp.asarray(hi), jnp.asarray(lo)


def _const_spec(shape):
    return pl.BlockSpec(shape, lambda *_: tuple(0 for _ in shape), pipeline_mode=pl.Buffered(1))


def _layer_spec(shape, l):
    return pl.BlockSpec((None,) + tuple(shape), lambda *_: (l,) + tuple(0 for _ in shape),
                        pipeline_mode=pl.Buffered(1))


@functools.lru_cache(maxsize=None)
def _dft_tables_np(L):
    k = np.arange(L, dtype=np.int64)
    m1 = ((2 * k[:, None] + 1) * k[None, :]) % (4 * L)
    a1 = np.pi * m1.astype(np.float64) / (2 * L)
    m2 = ((2 * k[:, None] + 1) * (2 * k[None, :] + 1)) % (8 * L)
    a2 = np.pi * m2.astype(np.float64) / (4 * L)
    return np.cos(a1), np.sin(a1), np.cos(a2), np.sin(a2)


@functools.lru_cache(maxsize=None)
def _filter_consts_np(L):
    t_idx = np.arange(L, dtype=np.float64)
    t = t_idx / (L - 1)
    bands = np.arange(1, HY_BANDS + 1, dtype=np.float64)
    ang = 2.0 * math.pi * t_idx[:, None] * bands[None, :] / L
    z = np.zeros((L, LANES), np.float64)
    z[:, 0] = t
    z[:, 1:1 + HY_BANDS] = np.cos(ang)
    z[:, 1 + HY_BANDS:1 + 2 * HY_BANDS] = -np.sin(ang)
    lin = np.linspace(math.log(HY_DECAY_TARGET) / HY_DECAY_LONG,
                      math.log(HY_DECAY_TARGET) / HY_DECAY_SHORT, HY_W).astype(np.float32)
    deltas = np.abs(lin).astype(np.float64)
    decay = np.exp(-t[:, None] * deltas[None, :])
    return z.astype(np.float32), decay.astype(np.float32)


@functools.lru_cache(maxsize=None)
def _pos_embed_np(L):
    rows = L // GRID_W
    r, cidx = np.meshgrid(np.arange(rows, dtype=np.float64), np.arange(GRID_W, dtype=np.float64), indexing="ij")
    r = r.reshape(-1)
    cidx = cidx.reshape(-1)
    quarter = D_MODEL // 4
    omega = 1.0 / (10000.0 ** (np.arange(quarter, dtype=np.float64) / quarter))
    ar = r[:, None] * omega[None, :]
    ac = cidx[:, None] * omega[None, :]
    return np.concatenate([np.sin(ar), np.cos(ar), np.sin(ac), np.cos(ac)], axis=-1).astype(np.float32)


def _ada_kernel(c_ref, w_ref, b_ref, o_ref):
    sc = _silu(c_ref[...])
    o_ref[0] = _dot3(sc, w_ref[0]) + b_ref[0]


def _ada(cond, ada_w, ada_b):
    tn = 1536
    nd = 6 * D_MODEL
    return pl.pallas_call(
        _ada_kernel,
        out_shape=jax.ShapeDtypeStruct((DEPTH, 8, nd), F32),
        grid=(DEPTH, nd // tn),
        in_specs=[
            pl.BlockSpec((8, D_MODEL), lambda l, j: (0, 0)),
            pl.BlockSpec((1, D_MODEL, tn), lambda l, j: (l, 0, j)),
            pl.BlockSpec((1, 1, tn), lambda l, j: (l, 0, j)),
        ],
        out_specs=pl.BlockSpec((1, 8, tn), lambda l, j: (l, 0, j)),
        compiler_params=pltpu.CompilerParams(
            dimension_semantics=("arbitrary", "arbitrary"), vmem_limit_bytes=VMEM_LIMIT),
        name="ada_mod",
    )(cond, ada_w, ada_b.reshape(DEPTH, 1, nd))


def _inproj_kernel(*refs, with_pos, n_cast):
    if n_cast:
        n_in = 6 if with_pos else 5
        cast_in, cast_out = refs[n_in:n_in + n_cast], refs[-n_cast:]
        refs = refs[:n_in] + refs[n_in + n_cast:-n_cast]
        for src, dst in zip(cast_in, cast_out):
            dst[...] = src[...].astype(BF16)
    if with_pos:
        x_ref, pos_ref, mod_ref, g_ref, w_ref, wg_ref, proj_ref, gate_ref, xo_ref = refs
        x = x_ref[...] + pos_ref[...]
        xo_ref[...] = x
    else:
        x_ref, mod_ref, g_ref, w_ref, wg_ref, proj_ref, gate_ref = refs
        x = x_ref[...]
    mod = mod_ref[0]
    sh1 = mod[:, 0:D_MODEL]
    sc1 = mod[:, D_MODEL:2 * D_MODEL]
    u = (_rms(x) * g_ref[...]) * (1.0 + sc1) + sh1
    ub = u.astype(BF16)
    tn = 512
    for j in range(N_MAIN // tn):
        proj_ref[:, j * tn:(j + 1) * tn] = _dot_nt(ub, w_ref[j * tn:(j + 1) * tn, :])
    gate_ref[...] = _dot_nt(wg_ref[...], ub)


def _mod_spec(l, L, tm, per_request):
    if per_request:
        return pl.BlockSpec((1, 1, 6 * D_MODEL), lambda i: (l * 8 + 1 + (i * tm) // L, 0, 0))
    return pl.BlockSpec((1, 1, 6 * D_MODEL), lambda i: (l * 8, 0, 0))


def _inproj(x2d, pos, mods, g, w_in_t, w_gate, cast=(), *, l, L, per_request):
    T = x2d.shape[0]
    tm = ROW_BLOCK
    nblk = T // tm
    per_seq = L // tm
    with_pos = pos is not None
    row_spec = pl.BlockSpec((tm, D_MODEL), lambda i: (i, 0))
    in_specs = [row_spec]
    args = [x2d]
    if with_pos:
        in_specs.append(pl.BlockSpec((tm, D_MODEL), lambda i: (i % per_seq, 0)))
        args.append(pos)
    in_specs += [
        _mod_spec(l, L, tm, per_request),
        _layer_spec((1, D_MODEL), l),
        _layer_spec((N_MAIN, D_MODEL), l),
        _layer_spec((N_GATE, D_MODEL), l),
    ]
    args += [mods, g, w_in_t, w_gate]
    out_shape = [jax.ShapeDtypeStruct((T, N_MAIN), F32), jax.ShapeDtypeStruct((N_GATE, T), F32)]
    out_specs = [pl.BlockSpec((tm, N_MAIN), lambda i: (i, 0)), pl.BlockSpec((N_GATE, tm), lambda i: (0, i))]
    if with_pos:
        out_shape.append(jax.ShapeDtypeStruct((T, D_MODEL), F32))
        out_specs.append(row_spec)
    for w in cast:
        _, rows, cols = w.shape
        in_specs.append(pl.BlockSpec((None, rows // nblk, cols), lambda i: (l, i, 0)))
        args.append(w)
        out_shape.append(jax.ShapeDtypeStruct((rows, cols), BF16))
        out_specs.append(pl.BlockSpec((rows // nblk, cols), lambda i: (i, 0)))
    return pl.pallas_call(
        functools.partial(_inproj_kernel, with_pos=with_pos, n_cast=len(cast)),
        out_shape=out_shape,
        grid=(nblk,),
        in_specs=in_specs,
        out_specs=out_specs,
        compiler_params=pltpu.CompilerParams(
            dimension_semantics=("arbitrary",), vmem_limit_bytes=VMEM_LIMIT),
        name="inproj",
    )(*args)


def _filter_kernel(z_ref, dec_ref, w1_ref, b1_ref, w2_ref, b2_ref, w3_ref, f_ref,
                   ch_ref, cl_ref, sh_ref, sl_ref, hre_ref, him_ref, *, L):
    T = HY_BLOCK
    P = L // T
    f = f_ref[...]
    hdn = jnp.sin(f * (_dot3(z_ref[...], w1_ref[...]) + b1_ref[...]))
    hdn = jnp.sin(f * (_dot3(hdn, w2_ref[...]) + b2_ref[...]))
    h = _dot3(hdn, w3_ref[...])
    dec = dec_ref[...]
    row = lax.broadcasted_iota(jnp.int32, (L, HY_W), 0)
    halves = []
    for o in range(HY_ORDER):
        hf = h[:, (2 * o) * HY_W:(2 * o + 1) * HY_W] * dec
        hb = jnp.where(row == 0, 0.0, h[:, (2 * o + 1) * HY_W:(2 * o + 2) * HY_W] * dec)
        nrm = jnp.sum(jnp.abs(hf), axis=0, keepdims=True) + jnp.sum(jnp.abs(hb), axis=0, keepdims=True)
        inv = 1.0 / nrm
        halves.append((hf * inv, hb * inv))
    blocks = [half[e * T:(e + 1) * T] for pair in halves for half in pair for e in range(P)]
    x = jnp.concatenate(blocks, axis=1)
    xre = _dot3_tab(ch_ref[...], cl_ref[...], x)
    xim = -_dot3_tab(sh_ref[...], sl_ref[...], x)
    sgn = jnp.where((lax.broadcasted_iota(jnp.int32, (T, HY_W), 0) & 1) == 0, 1.0, -1.0)
    for o in range(HY_ORDER):
        def half_spec(which, e):
            j = (o * 2 + which) * P + e
            cs = slice(j * HY_W, (j + 1) * HY_W)
            return xre[:, cs], xim[:, cs], halves[o][which][e * T:e * T + 1]
        for d in range(-(P - 1), P):
            if d == 0:
                (fre, fim, _), (gre, gim, _) = half_spec(0, 0), half_spec(1, 0)
                re, im = fre + gre, fim - gim
            elif d > 0:
                (fre, fim, _), (pre, pim, p0) = half_spec(0, d), half_spec(0, d - 1)
                re, im = fre - sgn * pim, fim + sgn * (pre - p0)
            else:
                (gre, gim, _), (pre, pim, p0) = half_spec(1, -d), half_spec(1, -d - 1)
                re, im = gre - sgn * pim, -gim - sgn * (pre - p0)
            hre_ref[d + P - 1, :, o * HY_W:(o + 1) * HY_W] = re
            him_ref[d + P - 1, :, o * HY_W:(o + 1) * HY_W] = im


def _filter_spectrum(L, w1p, b1p, w2p, b2p, w3p, fp):
    T = HY_BLOCK
    z_np, dec_np = _filter_consts_np(L)
    c1, s1, _, _ = _dft_tables_np(T)
    ch, cl = _hi_lo(c1)
    sh, sl = _hi_lo(s1)
    n = HY_ORDER * HY_W
    nd = 2 * (L // T) - 1
    const = _const_spec
    per_layer = lambda shape: pl.BlockSpec((None,) + shape, lambda l: (l,) + tuple(0 for _ in shape))
    return pl.pallas_call(
        functools.partial(_filter_kernel, L=L),
        out_shape=[jax.ShapeDtypeStruct((DEPTH, nd, T, n), F32), jax.ShapeDtypeStruct((DEPTH, nd, T, n), F32)],
        grid=(DEPTH,),
        in_specs=[const((L, LANES)), const((L, HY_W)), per_layer((LANES, LANES)), per_layer((1, LANES)),
                  per_layer((LANES, LANES)), per_layer((1, LANES)), per_layer((LANES, 2 * n)), per_layer((1, LANES)),
                  const((T, T)), const((T, T)), const((T, T)), const((T, T))],
        out_specs=[per_layer((nd, T, n)), per_layer((nd, T, n))],
        compiler_params=pltpu.CompilerParams(
            dimension_semantics=("arbitrary",), vmem_limit_bytes=VMEM_LIMIT),
        name="hyena_filter",
    )(jnp.asarray(z_np), jnp.asarray(dec_np), w1p, b1p, w2p, b2p, w3p, fp, ch, cl, sh, sl)


def _schy_kernel(sc_ref, hy_ref, scw_ref, hyw_ref, skip_ref, fwd_ref, inv_ref, hre_ref, him_ref, out_ref, *, L):
    T = HY_BLOCK
    P = L // T
    NB = sc_ref.shape[0] // T
    sc = sc_ref[...]
    b_g = sc[:, 0:SC_W]
    c_g = sc[:, SC_W:2 * SC_W]
    xin = sc[:, 2 * SC_W:3 * SC_W]
    out_ref[:, 0:SC_W] = b_g * _conv3(c_g * xin, scw_ref[...], L)

    u3 = _conv3(hy_ref[...], hyw_ref[...], L)
    z = u3[:, 0:HY_W]
    gates = (u3[:, HY_W:2 * HY_W], u3[:, 2 * HY_W:3 * HY_W])
    fwd, inv = fwd_ref[...], inv_ref[...]
    lanes = lambda j: slice(j * HY_W, (j + 1) * HY_W)
    for o in range(HY_ORDER):
        zcat = jnp.concatenate([z[j * T:(j + 1) * T] for j in range(NB)], axis=1)
        spec = _dot(fwd, zcat.astype(BF16))
        yre, yim = [], []
        for bi in range(NB):
            s0, i = bi - bi % P, bi % P
            re = im = None
            for j in range(P):
                d = i - j + P - 1
                hre = hre_ref[d, :, lanes(o)]
                him = him_ref[d, :, lanes(o)]
                zc = spec[0:T, lanes(s0 + j)]
                zs = spec[T:2 * T, lanes(s0 + j)]
                pre = zc * hre + zs * him
                pim = zc * him - zs * hre
                re = pre if re is None else re + pre
                im = pim if im is None else im + pim
            yre.append(re)
            yim.append(im)
        prod = jnp.concatenate([jnp.concatenate(yre, axis=1), jnp.concatenate(yim, axis=1)], axis=0)
        ycat = _dot(inv, prod.astype(BF16)) * (1.0 / T)
        y = jnp.concatenate([ycat[:, lanes(bi)] for bi in range(NB)], axis=0)
        z = gates[o] * (y + skip_ref[o:o + 1, :] * z)
    out_ref[:, SC_W:SC_W + HY_W] = z


def _schy(proj, sc_w, hy_w, skip, hre, him, *, l, B, L):
    T = HY_BLOCK
    _, _, c2, s2 = _dft_tables_np(T)
    fwd = jnp.asarray(np.concatenate([c2, s2], axis=0).astype(np.float32).astype(BF16))
    inv = jnp.asarray(np.concatenate([c2, -s2], axis=1).astype(np.float32).astype(BF16))
    n = HY_ORDER * HY_W
    nd = 2 * (L // T) - 1
    rows = max(L, HY_STEP_ROWS)
    return pl.pallas_call(
        functools.partial(_schy_kernel, L=L),
        out_shape=jax.ShapeDtypeStruct((B * L, SC_W + HY_W), F32),
        grid=(B * L // rows,),
        in_specs=[
            pl.BlockSpec((rows, 3 * SC_W), lambda b: (b, 0)),
            pl.BlockSpec((rows, 3 * HY_W), lambda b: (b, 1)),
            _layer_spec((3, SC_W), l), _layer_spec((3, 3 * HY_W), l), _layer_spec((HY_ORDER, HY_W), l),
            _const_spec((2 * T, T)), _const_spec((T, 2 * T)),
            _layer_spec((nd, T, n), l), _layer_spec((nd, T, n), l),
        ],
        out_specs=pl.BlockSpec((rows, SC_W + HY_W), lambda b: (b, 0)),
        compiler_params=pltpu.CompilerParams(
            dimension_semantics=("arbitrary",), vmem_limit_bytes=VMEM_LIMIT),
        name="sconv_hyena",
    )(proj, proj, sc_w, hy_w, skip, fwd, inv, hre, him)


def _gate_prep_kernel(*refs, B, L, with_state):
    if with_state:
        g_ref, gb_ref, m0_ref, rows_ref, cols_ref = refs
    else:
        g_ref, gb_ref, rows_ref, cols_ref = refs
    R = B * N_GATE
    gb = gb_ref[:, 0:1]
    x = jnp.concatenate([g_ref[:, b * L:(b + 1) * L] + gb for b in range(B)], axis=0)
    rr = lax.broadcasted_iota(jnp.int32, (R, L), 0) & (N_GATE - 1)
    is_f = ((rr >= ML_H) & (rr < 2 * ML_H)) | (rr >= 3 * ML_H)
    fwd_rows = rr < 2 * ML_H
    pk = jnp.where(is_f, _log_sigmoid(x), x)
    r_i = lax.broadcasted_iota(jnp.int32, (L, L), 0)
    c_i = lax.broadcasted_iota(jnp.int32, (L, L), 1)
    triu = jnp.where(r_i <= c_i, 1.0, 0.0).astype(BF16)
    cs = _dot(jnp.concatenate(_split3(pk), axis=0), triu)
    cum = cs[0:R] + cs[R:2 * R] + cs[2 * R:3 * R]
    suf = cum[:, L - 1:L] - cum + pk
    bsum = pltpu.roll(jnp.where(fwd_rows, cum, suf), R - ML_H, axis=0)
    r = pk - bsum
    cm = jnp.where(fwd_rows, _scan_max(r, reverse=False), _scan_max(r, reverse=True))
    if with_state:
        m0 = jnp.concatenate([m0_ref[b][:, 0:1] for b in range(B)], axis=0)
        m = jnp.maximum(m0, cm)
    else:
        m = jnp.maximum(cm, 0.0)
    rl = r * LOG2E
    ml = m * LOG2E
    mt = bsum + m
    pieces = [rl, ml, jnp.exp(-mt), mt]
    if not with_state:
        pieces.append(jnp.exp2(rl - jnp.where(fwd_rows, ml[:, L - 1:L], ml[:, 0:1])))
    pad = jnp.zeros((LANES - N_GATE, L), F32)
    for b in range(B):
        bs = slice(b * N_GATE, (b + 1) * N_GATE)
        for j, piece in enumerate(pieces):
            rows_ref[b, j * N_GATE:(j + 1) * N_GATE, :] = piece[bs]
        cols_ref[b * L:(b + 1) * L, :] = jnp.concatenate([rl[bs], pad], axis=0).T


def _gate_prep(g_t, gate_b, m0, *, l, B, L):
    with_state = m0 is not None
    n_rows = (4 if with_state else 5) * N_GATE
    full = lambda shape: pl.BlockSpec(shape, lambda i: tuple(0 for _ in shape))
    args = [g_t, gate_b]
    in_specs = [full((N_GATE, B * L)), pl.BlockSpec((None, N_GATE, LANES), lambda i: (l, 0, 0))]
    if with_state:
        args.append(m0)
        in_specs.append(pl.BlockSpec((B, None, N_GATE, LANES), lambda i: (0, l, 0, 0)))
    return pl.pallas_call(
        functools.partial(_gate_prep_kernel, B=B, L=L, with_state=with_state),
        out_shape=[jax.ShapeDtypeStruct((B, n_rows, L), F32), jax.ShapeDtypeStruct((B * L, LANES), F32)],
        grid=(1,),
        in_specs=in_specs,
        out_specs=[full((B, n_rows, L)), full((B * L, LANES))],
        compiler_params=pltpu.CompilerParams(
            dimension_semantics=("arbitrary",), vmem_limit_bytes=VMEM_LIMIT),
        name="gate_prep",
    )(*args)


def _mlstm_kernel(*refs, L, with_state, n_prev):
    if with_state:
        (q_ref, k_ref, v_ref, o_ref, rows_ref, cols_ref, cw_ref, ng_ref,
         c0_ref, n0_ref, m0_ref, y_ref) = refs
    elif n_prev:
        (q_ref, k_ref, v_ref, o_ref, rows_ref, cols_ref, cw_ref, ng_ref,
         cprev_ref, nprev_ref, mprev_ref, y_ref, cout_ref, nout_ref, mout_ref) = refs
        cout_ref[0, 0:n_prev] = cprev_ref[0]
        nout_ref[0, 0:n_prev] = nprev_ref[0]
        mout_ref[0, 0:n_prev] = mprev_ref[0]
        cout_ref, nout_ref, mout_ref = (r.at[:, n_prev] for r in (cout_ref, nout_ref, mout_ref))
    else:
        (q_ref, k_ref, v_ref, o_ref, rows_ref, cols_ref, cw_ref, ng_ref,
         y_ref, cout_ref, nout_ref, mout_ref) = refs

    tb = min(L // 2, ATT_BLOCK)
    cw = cw_ref[...]
    q_all = _silu(_conv3(q_ref[...], cw[:, 0:ML_W]))
    k_all = _silu(_conv3(k_ref[...], cw[:, ML_W:2 * ML_W])) * (ML_DH ** -0.5)

    rl16, ml16, em16, mt16 = (rows_ref[0, j * N_GATE:(j + 1) * N_GATE, :] for j in range(4))
    rl_cols = cols_ref[...]
    if not with_state:
        w16 = rows_ref[0, 4 * N_GATE:5 * N_GATE, :]

    s_loc = lax.broadcasted_iota(jnp.int32, (tb, tb), 0)
    t_loc = lax.broadcasted_iota(jnp.int32, (tb, tb), 1)
    masks = (s_loc <= t_loc, s_loc >= t_loc)
    ones_rows = jnp.ones((2 * 8, L), BF16)
    ng = ng_ref[...]
    for h in range(ML_H):
        hs = slice(h * ML_DH, (h + 1) * ML_DH)
        q = q_all[:, hs]
        k = k_all[:, hs]
        qb = q.astype(BF16)
        kb = k.astype(BF16)
        vb = v_ref[:, hs].astype(BF16)
        vext_t = jnp.concatenate([v_ref[:, hs].T.astype(BF16), ones_rows], axis=0)
        rlb = [jnp.broadcast_to(rl_cols[:, 2 * ML_H * d + h:2 * ML_H * d + h + 1], (L, tb)) for d in range(2)]
        if with_state:
            c0_t = [c0_ref[0, d, h].T.astype(BF16) for d in range(2)]
        for r0 in range(0, L, tb):
            rs = slice(r0, r0 + tb)
            s_t = _dot_nt(kb, qb[rs])
            if with_state:
                qn_rows = _dot3_nt(n0_ref[0], q[rs])
            hout_t = None
            for d in range(2):
                o8 = 2 * ML_H * d + h
                ml_row = ml16[o8:o8 + 1, rs]
                e_diag = jnp.exp2(jnp.where(masks[d], rlb[d][rs] - ml_row, NEG_BIG))
                acc = _dot(vext_t[:, rs], (s_t[rs] * e_diag).astype(BF16))
                side = slice(r0 + tb, L) if d else slice(0, r0)
                if side.stop > side.start:
                    e_side = jnp.exp2(rlb[d][side] - ml_row)
                    acc = acc + _dot(vext_t[:, side], (s_t[side] * e_side).astype(BF16))
                num_t = acc[0:ML_DH]
                den = acc[ML_DH:ML_DH + 1]
                if with_state:
                    wp = jnp.exp2(m0_ref[0, o8:o8 + 1, 0:1] * LOG2E - ml_row)
                    den = den + wp * qn_rows[d * ML_H + h:d * ML_H + h + 1]
                    num_t = num_t + wp * _dot_nt(c0_t[d], qb[rs])
                contrib = num_t * (1.0 / jnp.maximum(jnp.abs(den), em16[o8:o8 + 1, rs]))
                hout_t = contrib if hout_t is None else hout_t + contrib
            hn_t = hout_t * lax.rsqrt(jnp.mean(hout_t * hout_t, axis=0, keepdims=True) + EPS)
            y_ref[rs, hs] = _sigmoid(o_ref[rs, hs]) * (hn_t.T * ng[:, hs])

        if not with_state:
            k_t = k.T
            n_rows = _dot3(w16, k)
            for d in range(2):
                o8 = 2 * ML_H * d + h
                end = 0 if d else L - 1
                cout_ref[0, d, h] = _dot((k_t * w16[o8:o8 + 1, :]).astype(BF16), vb)
                nout_ref[0, d * ML_H + h:d * ML_H + h + 1, :] = n_rows[o8:o8 + 1, :]
                mout_ref[0, d * ML_H + h:d * ML_H + h + 1, :] = jnp.broadcast_to(
                    mt16[o8:o8 + 1, end:end + 1], (1, LANES))


def _mlstm(proj, g_t, gate_b, conv_w, norm_g, state, prev, *, l, B, L):
    with_state = state is not None
    n_prev = 0 if prev is None else l
    rows, cols = _gate_prep(g_t, gate_b, state[2] if with_state else None, l=l, B=B, L=L)
    col0 = (3 * SC_W + 3 * HY_W) // ML_W
    in_specs = [
        pl.BlockSpec((L, ML_W), lambda b: (b, col0)),
        pl.BlockSpec((L, ML_W), lambda b: (b, col0 + 1)),
        pl.BlockSpec((L, ML_W), lambda b: (b, col0 + 2)),
        pl.BlockSpec((L, ML_W), lambda b: (b, col0 + 3)),
        pl.BlockSpec((1, rows.shape[1], L), lambda b: (b, 0, 0)),
        pl.BlockSpec((L, LANES), lambda b: (b, 0)),
        _layer_spec((3, 2 * ML_W), l), _layer_spec((1, ML_W), l),
    ]
    args = [proj, proj, proj, proj, rows, cols, conv_w, norm_g]
    y_shape = jax.ShapeDtypeStruct((B * L, ML_W), F32)
    y_spec = pl.BlockSpec((L, ML_W), lambda b: (b, 0))
    c_blk, n_blk, m_blk = (2, ML_H, ML_DH, ML_DH), (2 * ML_H, ML_DH), (2 * ML_H, LANES)
    if with_state:
        c0, n0, m0 = state
        in_specs += [
            pl.BlockSpec((1, None) + c_blk, lambda b: (b, l, 0, 0, 0, 0)),
            pl.BlockSpec((1, None) + n_blk, lambda b: (b, l, 0, 0)),
            pl.BlockSpec((1, None, N_GATE, LANES), lambda b: (b, l, 0, 0)),
        ]
        args += [c0, n0, m0]
        out_shape, out_specs = y_shape, y_spec
    else:
        lead = (l + 1,) if n_prev else ()
        if n_prev:
            in_specs += [pl.BlockSpec((1, n_prev) + blk, lambda b, nz=len(blk) + 1: (b,) + (0,) * nz)
                         for blk in (c_blk, n_blk, m_blk)]
            args += list(prev)
        out_shape = [y_shape] + [jax.ShapeDtypeStruct((B,) + lead + blk, F32) for blk in (c_blk, n_blk, m_blk)]
        out_specs = [y_spec] + [pl.BlockSpec((1,) + lead + blk, lambda b, nz=len(lead) + len(blk): (b,) + (0,) * nz)
                                for blk in (c_blk, n_blk, m_blk)]
    return pl.pallas_call(
        functools.partial(_mlstm_kernel, L=L, with_state=with_state, n_prev=n_prev),
        out_shape=out_shape,
        grid=(B,),
        in_specs=in_specs,
        out_specs=out_specs,
        compiler_params=pltpu.CompilerParams(
            dimension_semantics=("arbitrary",), vmem_limit_bytes=VMEM_LIMIT),
        name="mlstm",
    )(*args)


def _outmlp_kernel(x_ref, ya_ref, yb_ref, mod_ref, wo_ref, g2_ref, w1_ref, w2_ref, gf_ref, o_ref, *, final):
    mod = mod_ref[0]
    g1 = mod[:, 2 * D_MODEL:3 * D_MODEL]
    sh2 = mod[:, 3 * D_MODEL:4 * D_MODEL]
    sc2 = mod[:, 4 * D_MODEL:5 * D_MODEL]
    g2 = mod[:, 5 * D_MODEL:6 * D_MODEL]
    na = SC_W + HY_W
    attn = _dot(ya_ref[...].astype(BF16), wo_ref[0:na, :]) + _dot(yb_ref[...].astype(BF16), wo_ref[na:, :])
    x = x_ref[...] + g1 * attn
    u = ((_rms(x) * g2_ref[...]) * (1.0 + sc2) + sh2).astype(BF16)
    tf = 1024
    acc = jnp.zeros(x.shape, F32)
    for j in range(D_FF // tf):
        hcol = _dot(u, w1_ref[:, j * tf:(j + 1) * tf])
        hcol = jnp.square(jnp.maximum(hcol, 0.0)).astype(BF16)
        acc = acc + _dot(hcol, w2_ref[j * tf:(j + 1) * tf, :])
    x = x + g2 * acc
    if final:
        x = _rms(x) * gf_ref[...]
    o_ref[...] = x


def _outmlp(x2d, ya, yb, mods, w_out, g2, w1, w2, gf, *, l, L, per_request, final):
    T = x2d.shape[0]
    tm = ROW_BLOCK
    return pl.pallas_call(
        functools.partial(_outmlp_kernel, final=final),
        out_shape=jax.ShapeDtypeStruct((T, D_MODEL), F32),
        grid=(T // tm,),
        in_specs=[
            pl.BlockSpec((tm, D_MODEL), lambda i: (i, 0)),
            pl.BlockSpec((tm, SC_W + HY_W), lambda i: (i, 0)),
            pl.BlockSpec((tm, ML_W), lambda i: (i, 0)),
            _mod_spec(l, L, tm, per_request),
            _const_spec((D_MODEL, D_MODEL)), _layer_spec((1, D_MODEL), l),
            _const_spec((D_MODEL, D_FF)), _const_spec((D_FF, D_MODEL)), _const_spec((1, D_MODEL)),
        ],
        out_specs=pl.BlockSpec((tm, D_MODEL), lambda i: (i, 0)),
        compiler_params=pltpu.CompilerParams(
            dimension_semantics=("arbitrary",), vmem_limit_bytes=VMEM_LIMIT),
        name="outproj_mlp",
    )(x2d, ya, yb, mods, w_out, g2, w1, w2, gf)


def kernel(x_prompt, x_sample, state_C, state_n, state_m, c, c_ctx, norm1_g, ada_w, ada_b, w_in, sc_conv_w, hy_conv_w, hy_w1, hy_b1, hy_w2, hy_b2, hy_w3, hy_freq, hy_skip, ml_conv_w, ml_gate_b, ml_norm_g, w_out, norm2_g, mlp_w1, mlp_w2, norm_f_g):
    B, L = x_prompt.shape[0], x_prompt.shape[1]
    Bd, Ld = x_sample.shape[0], x_sample.shape[1]

    cond = jnp.concatenate([c_ctx[None, :], c, jnp.zeros((8 - 1 - Bd, D_MODEL), F32)], axis=0)
    mods = _ada(cond, ada_w, ada_b).reshape(DEPTH * 8, 1, 6 * D_MODEL)

    w_in_t = jnp.swapaxes(w_in, 1, 2).astype(BF16)
    w_gate = w_in_t[:, N_MAIN:, :]
    gate_b = jnp.broadcast_to(ml_gate_b[:, :, None], (DEPTH, N_GATE, LANES))
    pad_h = LANES - HY_HID
    w1p = jnp.pad(hy_w1, ((0, 0), (0, LANES - HY_EMB), (0, pad_h)))
    b1p = jnp.pad(hy_b1, ((0, 0), (0, pad_h)))
    w2p = jnp.pad(hy_w2, ((0, 0), (0, pad_h), (0, pad_h)))
    b2p = jnp.pad(hy_b2, ((0, 0), (0, pad_h)))
    w3p = jnp.pad(hy_w3, ((0, 0), (0, pad_h), (0, 0)))
    fp = jnp.pad(hy_freq, ((0, 0), (0, pad_h)))
    pos = jnp.asarray(_pos_embed_np(Ld))
    m0_all = jnp.pad(state_m, ((0, 0), (0, 0), (0, 0), (0, ML_H))).reshape(Bd, DEPTH, N_GATE, 1)
    m0_all = jnp.broadcast_to(m0_all, (Bd, DEPTH, N_GATE, LANES))
    n0_all = state_n.reshape(Bd, DEPTH, 2 * ML_H, ML_DH)
    gf = norm_f_g[None, :]
    g1 = norm1_g[:, None, :]
    g2 = norm2_g[:, None, :]
    ml_ng = ml_norm_g[:, None, :]
    filt_args = (w1p, b1p[:, None, :], w2p, b2p[:, None, :], w3p, fp[:, None, :])
    hre_c, him_c = _filter_spectrum(L, *filt_args)
    hre_s, him_s = _filter_spectrum(Ld, *filt_args)
    state = (state_C, n0_all, m0_all)

    xp = x_prompt.reshape(B * L, D_MODEL)
    xs = x_sample.reshape(Bd * Ld, D_MODEL)
    prev = None
    for l in range(DEPTH):
        final = l == DEPTH - 1

        proj, g_t, w_out_b, w1_b, w2_b = _inproj(xp, None, mods, g1, w_in_t, w_gate, (w_out, mlp_w1, mlp_w2),
                                                 l=l, L=L, per_request=False)
        ya = _schy(proj, sc_conv_w, hy_conv_w, hy_skip, hre_c, him_c, l=l, B=B, L=L)
        yb, *states = _mlstm(proj, g_t, gate_b, ml_conv_w, ml_ng, None, prev, l=l, B=B, L=L)
        xp = _outmlp(xp, ya, yb, mods, w_out_b, g2, w1_b, w2_b, gf, l=l, L=L, per_request=False, final=final)
        if not final:
            prev = [s.reshape((B, 1) + s.shape[1:]) for s in states] if l == 0 else states

        res = _inproj(xs, pos if l == 0 else None, mods, g1, w_in_t, w_gate, l=l, L=Ld, per_request=True)
        if l == 0:
            proj, g_t, xs = res
        else:
            proj, g_t = res
        ya = _schy(proj, sc_conv_w, hy_conv_w, hy_skip, hre_s, him_s, l=l, B=Bd, L=Ld)
        yb = _mlstm(proj, g_t, gate_b, ml_conv_w, ml_ng, state, None, l=l, B=Bd, L=Ld)
        xs = _outmlp(xs, ya, yb, mods, w_out_b, g2, w1_b, w2_b, gf, l=l, L=Ld, per_request=True, final=final)

    y_prompt = xp.reshape(B, L, D_MODEL)
    y_sample = xs.reshape(Bd, Ld, D_MODEL)
    new_c, new_n, new_m = states
    return (y_prompt, y_sample, new_c, new_n.reshape(B, DEPTH, 2, ML_H, ML_DH),
            new_m[..., 0].reshape(B, DEPTH, 2, ML_H))
```

```python
import functools
import math

import numpy as np
import jax
import jax.numpy as jnp
from jax import lax
from jax.experimental import pallas as pl
from jax.experimental.pallas import tpu as pltpu

F32 = jnp.float32
BF16 = jnp.bfloat16

D_MODEL = 1024
DEPTH = 2
GRID_W = 64
SC_W = 256
HY_W = 256
ML_W = 512
ML_H = 4
ML_DH = ML_W // ML_H
D_FF = 4 * D_MODEL
HY_ORDER = 2
HY_BANDS = 16
HY_EMB = 1 + 2 * HY_BANDS
HY_HID = 64
HY_DECAY_SHORT = 0.3
HY_DECAY_LONG = 1.5
HY_DECAY_TARGET = 1e-2
EPS = 1e-6
N_MAIN = 3 * SC_W + 3 * HY_W + 4 * ML_W
N_GATE = 4 * ML_H
LANES = 128
ROW_BLOCK = 512
ATT_BLOCK = 256
HY_BLOCK = 256
HY_STEP_ROWS = 1024
ML_STEP_ROWS = 512
NEG_BIG = -1e30
LOG2E = math.log2(math.e)
VMEM_LIMIT = 60 * 1024 * 1024


def _dot(a, b):
    return jnp.dot(a, b, preferred_element_type=F32)


def _dot_nt(a, b):
    return lax.dot_general(a, b, (((1,), (1,)), ((), ())), preferred_element_type=F32)


def _split2(a):
    hi = a.astype(BF16)
    lo = (a - hi.astype(F32)).astype(BF16)
    return hi, lo


def _split3(a):
    hi = a.astype(BF16)
    r = a - hi.astype(F32)
    mid = r.astype(BF16)
    lo = (r - mid.astype(F32)).astype(BF16)
    return hi, mid, lo


def _dot3(a, b):
    ah, al = _split2(a)
    bh, bl = _split2(b)
    return _dot(ah, bh) + _dot(al, bh) + _dot(ah, bl)


def _dot3_nt(a, b):
    ah, al = _split2(a)
    bh, bl = _split2(b)
    return _dot_nt(ah, bh) + _dot_nt(al, bh) + _dot_nt(ah, bl)


def _dot3_tab(th, tl, b):
    bh, bl = _split2(b)
    return _dot(th, bh) + _dot(tl, bh) + _dot(th, bl)


def _sigmoid(x):
    return 1.0 / (1.0 + jnp.exp(-x))


def _silu(x):
    return x * _sigmoid(x)


def _log_sigmoid(x):
    return jnp.minimum(x, 0.0) - jnp.log(1.0 + jnp.exp(-jnp.abs(x)))


def _conv3(x, w, seq_len=None):
    R = x.shape[0]
    seq_len = R if seq_len is None else seq_len
    pos = lax.broadcasted_iota(jnp.int32, x.shape, 0) & (seq_len - 1)
    xm = jnp.where(pos == 0, 0.0, pltpu.roll(x, 1, axis=0))
    xp = jnp.where(pos == seq_len - 1, 0.0, pltpu.roll(x, R - 1, axis=0))
    return xm * w[0:1, :] + x * w[1:2, :] + xp * w[2:3, :]


def _rms(x):
    return x * lax.rsqrt(jnp.mean(x * x, axis=-1, keepdims=True) + EPS)


def _scan_max(x, *, reverse):
    n = x.shape[1]
    lane = lax.broadcasted_iota(jnp.int32, x.shape, 1)
    d = 1
    while d < n:
        if reverse:
            shifted = jnp.where(lane < n - d, pltpu.roll(x, n - d, axis=1), NEG_BIG)
        else:
            shifted = jnp.where(lane >= d, pltpu.roll(x, d, axis=1), NEG_BIG)
        x = jnp.maximum(x, shifted)
        d *= 2
    return x


def _hi_lo(a64):
    a = a64.astype(np.float32)
    hi = a.astype(BF16)
    lo = (a - hi.astype(np.float32)).astype(BF16)
    return jnp.asarray(hi), jnp.asarray(lo)


def _const_spec(shape):
    return pl.BlockSpec(shape, lambda *_: tuple(0 for _ in shape), pipeline_mode=pl.Buffered(1))


def _layer_spec(shape, l):
    return pl.BlockSpec((None,) + tuple(shape), lambda *_: (l,) + tuple(0 for _ in shape),
                        pipeline_mode=pl.Buffered(1))


@functools.lru_cache(maxsize=None)
def _dft_tables_np(L):
    k = np.arange(L, dtype=np.int64)
    m1 = ((2 * k[:, None] + 1) * k[None, :]) % (4 * L)
    a1 = np.pi * m1.astype(np.float64) / (2 * L)
    m2 = ((2 * k[:, None] + 1) * (2 * k[None, :] + 1)) % (8 * L)
    a2 = np.pi * m2.astype(np.float64) / (4 * L)
    return np.cos(a1), np.sin(a1), np.cos(a2), np.sin(a2)


@functools.lru_cache(maxsize=None)
def _filter_consts_np(L):
    t_idx = np.arange(L, dtype=np.float64)
    t = t_idx / (L - 1)
    bands = np.arange(1, HY_BANDS + 1, dtype=np.float64)
    ang = 2.0 * math.pi * t_idx[:, None] * bands[None, :] / L
    z = np.zeros((L, LANES), np.float64)
    z[:, 0] = t
    z[:, 1:1 + HY_BANDS] = np.cos(ang)
    z[:, 1 + HY_BANDS:1 + 2 * HY_BANDS] = -np.sin(ang)
    lin = np.linspace(math.log(HY_DECAY_TARGET) / HY_DECAY_LONG,
                      math.log(HY_DECAY_TARGET) / HY_DECAY_SHORT, HY_W).astype(np.float32)
    deltas = np.abs(lin).astype(np.float64)
    decay = np.exp(-t[:, None] * deltas[None, :])
    return z.astype(np.float32), decay.astype(np.float32)


@functools.lru_cache(maxsize=None)
def _pos_embed_np(L):
    rows = L // GRID_W
    r, cidx = np.meshgrid(np.arange(rows, dtype=np.float64), np.arange(GRID_W, dtype=np.float64), indexing="ij")
    r = r.reshape(-1)
    cidx = cidx.reshape(-1)
    quarter = D_MODEL // 4
    omega = 1.0 / (10000.0 ** (np.arange(quarter, dtype=np.float64) / quarter))
    ar = r[:, None] * omega[None, :]
    ac = cidx[:, None] * omega[None, :]
    return np.concatenate([np.sin(ar), np.cos(ar), np.sin(ac), np.cos(ac)], axis=-1).astype(np.float32)


def _ada_kernel(c_ref, w_ref, b_ref, o_ref):
    sc = _silu(c_ref[...])
    o_ref[0] = _dot3(sc, w_ref[0]) + b_ref[0]


def _ada(cond, ada_w, ada_b):
    tn = 1536
    nd = 6 * D_MODEL
    return pl.pallas_call(
        _ada_kernel,
        out_shape=jax.ShapeDtypeStruct((DEPTH, 8, nd), F32),
        grid=(DEPTH, nd // tn),
        in_specs=[
            pl.BlockSpec((8, D_MODEL), lambda l, j: (0, 0)),
            pl.BlockSpec((1, D_MODEL, tn), lambda l, j: (l, 0, j)),
            pl.BlockSpec((1, 1, tn), lambda l, j: (l, 0, j)),
        ],
        out_specs=pl.BlockSpec((1, 8, tn), lambda l, j: (l, 0, j)),
        compiler_params=pltpu.CompilerParams(
            dimension_semantics=("arbitrary", "arbitrary"), vmem_limit_bytes=VMEM_LIMIT),
        name="ada_mod",
    )(cond, ada_w, ada_b.reshape(DEPTH, 1, nd))


def _inproj_kernel(*refs, with_pos, n_cast):
    if n_cast:
        n_in = 6 if with_pos else 5
        cast_in, cast_out = refs[n_in:n_in + n_cast], refs[-n_cast:]
        refs = refs[:n_in] + refs[n_in + n_cast:-n_cast]
        for src, dst in zip(cast_in, cast_out):
            dst[...] = src[...].astype(BF16)
    if with_pos:
        x_ref, pos_ref, mod_ref, g_ref, w_ref, wg_ref, proj_ref, gate_ref, xo_ref = refs
        x = x_ref[...] + pos_ref[...]
        xo_ref[...] = x
    else:
        x_ref, mod_ref, g_ref, w_ref, wg_ref, proj_ref, gate_ref = refs
        x = x_ref[...]
    mod = mod_ref[0]
    sh1 = mod[:, 0:D_MODEL]
    sc1 = mod[:, D_MODEL:2 * D_MODEL]
    u = (_rms(x) * g_ref[...]) * (1.0 + sc1) + sh1
    ub = u.astype(BF16)
    tn = 512
    for j in range(N_MAIN // tn):
        proj_ref[:, j * tn:(j + 1) * tn] = _dot_nt(ub, w_ref[j * tn:(j + 1) * tn, :])
    gate_ref[...] = _dot_nt(wg_ref[...], ub)


def _mod_spec(l, L, tm, per_request):
    if per_request:
        return pl.BlockSpec((1, 1, 6 * D_MODEL), lambda i: (l * 8 + 1 + (i * tm) // L, 0, 0))
    return pl.BlockSpec((1, 1, 6 * D_MODEL), lambda i: (l * 8, 0, 0))


def _inproj(x2d, pos, mods, g, w_in_t, w_gate, cast=(), *, l, L, per_request):
    T = x2d.shape[0]
    tm = ROW_BLOCK
    nblk = T // tm
    per_seq = L // tm
    with_pos = pos is not None
    row_spec = pl.BlockSpec((tm, D_MODEL), lambda i: (i, 0))
    in_specs = [row_spec]
    args = [x2d]
    if with_pos:
        in_specs.append(pl.BlockSpec((tm, D_MODEL), lambda i: (i % per_seq, 0)))
        args.append(pos)
    in_specs += [
        _mod_spec(l, L, tm, per_request),
        _layer_spec((1, D_MODEL), l),
        _layer_spec((N_MAIN, D_MODEL), l),
        _layer_spec((N_GATE, D_MODEL), l),
    ]
    args += [mods, g, w_in_t, w_gate]
    out_shape = [jax.ShapeDtypeStruct((T, N_MAIN), F32), jax.ShapeDtypeStruct((N_GATE, T), F32)]
    out_specs = [pl.BlockSpec((tm, N_MAIN), lambda i: (i, 0)), pl.BlockSpec((N_GATE, tm), lambda i: (0, i))]
    if with_pos:
        out_shape.append(jax.ShapeDtypeStruct((T, D_MODEL), F32))
        out_specs.append(row_spec)
    for w in cast:
        _, rows, cols = w.shape
        in_specs.append(pl.BlockSpec((None, rows // nblk, cols), lambda i: (l, i, 0)))
        args.append(w)
        out_shape.append(jax.ShapeDtypeStruct((rows, cols), BF16))
        out_specs.append(pl.BlockSpec((rows // nblk, cols), lambda i: (i, 0)))
    return pl.pallas_call(
        functools.partial(_inproj_kernel, with_pos=with_pos, n_cast=len(cast)),
        out_shape=out_shape,
        grid=(nblk,),
        in_specs=in_specs,
        out_specs=out_specs,
        compiler_params=pltpu.CompilerParams(
            dimension_semantics=("arbitrary",), vmem_limit_bytes=VMEM_LIMIT),
        name="inproj",
    )(*args)


def _filter_kernel(z_ref, dec_ref, w1_ref, b1_ref, w2_ref, b2_ref, w3_ref, f_ref,
                   ch_ref, cl_ref, sh_ref, sl_ref, hre_ref, him_ref, *, L):
    T = HY_BLOCK
    P = L // T
    f = f_ref[...]
    hdn = jnp.sin(f * (_dot3(z_ref[...], w1_ref[...]) + b1_ref[...]))
    hdn = jnp.sin(f * (_dot3(hdn, w2_ref[...]) + b2_ref[...]))
    h = _dot3(hdn, w3_ref[...])
    dec = dec_ref[...]
    row = lax.broadcasted_iota(jnp.int32, (L, HY_W), 0)
    halves = []
    for o in range(HY_ORDER):
        hf = h[:, (2 * o) * HY_W:(2 * o + 1) * HY_W] * dec
        hb = jnp.where(row == 0, 0.0, h[:, (2 * o + 1) * HY_W:(2 * o + 2) * HY_W] * dec)
        nrm = jnp.sum(jnp.abs(hf), axis=0, keepdims=True) + jnp.sum(jnp.abs(hb), axis=0, keepdims=True)
        inv = 1.0 / nrm
        halves.append((hf * inv, hb * inv))
    blocks = [half[e * T:(e + 1) * T] for pair in halves for half in pair for e in range(P)]
    x = jnp.concatenate(blocks, axis=1)
    xre = _dot3_tab(ch_ref[...], cl_ref[...], x)
    xim = -_dot3_tab(sh_ref[...], sl_ref[...], x)
    sgn = jnp.where((lax.broadcasted_iota(jnp.int32, (T, HY_W), 0) & 1) == 0, 1.0, -1.0)
    for o in range(HY_ORDER):
        def half_spec(which, e):
            j = (o * 2 + which) * P + e
            cs = slice(j * HY_W, (j + 1) * HY_W)
            return xre[:, cs], xim[:, cs], halves[o][which][e * T:e * T + 1]
        for d in range(-(P - 1), P):
            if d == 0:
                (fre, fim, _), (gre, gim, _) = half_spec(0, 0), half_spec(1, 0)
                re, im = fre + gre, fim - gim
            elif d > 0:
                (fre, fim, _), (pre, pim, p0) = half_spec(0, d), half_spec(0, d - 1)
                re, im = fre - sgn * pim, fim + sgn * (pre - p0)
            else:
                (gre, gim, _), (pre, pim, p0) = half_spec(1, -d), half_spec(1, -d - 1)
                re, im = gre - sgn * pim, -gim - sgn * (pre - p0)
            hre_ref[d + P - 1, :, o * HY_W:(o + 1) * HY_W] = re
            him_ref[d + P - 1, :, o * HY_W:(o + 1) * HY_W] = im


def _filter_spectrum(L, w1p, b1p, w2p, b2p, w3p, fp):
    T = HY_BLOCK
    z_np, dec_np = _filter_consts_np(L)
    c1, s1, _, _ = _dft_tables_np(T)
    ch, cl = _hi_lo(c1)
    sh, sl = _hi_lo(s1)
    n = HY_ORDER * HY_W
    nd = 2 * (L // T) - 1
    const = _const_spec
    per_layer = lambda shape: pl.BlockSpec((None,) + shape, lambda l: (l,) + tuple(0 for _ in shape))
    return pl.pallas_call(
        functools.partial(_filter_kernel, L=L),
        out_shape=[jax.ShapeDtypeStruct((DEPTH, nd, T, n), F32), jax.ShapeDtypeStruct((DEPTH, nd, T, n), F32)],
        grid=(DEPTH,),
        in_specs=[const((L, LANES)), const((L, HY_W)), per_layer((LANES, LANES)), per_layer((1, LANES)),
                  per_layer((LANES, LANES)), per_layer((1, LANES)), per_layer((LANES, 2 * n)), per_layer((1, LANES)),
                  const((T, T)), const((T, T)), const((T, T)), const((T, T))],
        out_specs=[per_layer((nd, T, n)), per_layer((nd, T, n))],
        compiler_params=pltpu.CompilerParams(
            dimension_semantics=("arbitrary",), vmem_limit_bytes=VMEM_LIMIT),
        name="hyena_filter",
    )(jnp.asarray(z_np), jnp.asarray(dec_np), w1p, b1p, w2p, b2p, w3p, fp, ch, cl, sh, sl)


def _schy_kernel(sc_ref, hy_ref, scw_ref, hyw_ref, skip_ref, fwd_ref, inv_ref, hre_ref, him_ref, out_ref, *, L):
    T = HY_BLOCK
    P = L // T
    NB = sc_ref.shape[0] // T
    sc = sc_ref[...]
    b_g = sc[:, 0:SC_W]
    c_g = sc[:, SC_W:2 * SC_W]
    xin = sc[:, 2 * SC_W:3 * SC_W]
    out_ref[:, 0:SC_W] = b_g * _conv3(c_g * xin, scw_ref[...], L)

    u3 = _conv3(hy_ref[...], hyw_ref[...], L)
    z = u3[:, 0:HY_W]
    gates = (u3[:, HY_W:2 * HY_W], u3[:, 2 * HY_W:3 * HY_W])
    fwd, inv = fwd_ref[...], inv_ref[...]
    lanes = lambda j: slice(j * HY_W, (j + 1) * HY_W)
    for o in range(HY_ORDER):
        zcat = jnp.concatenate([z[j * T:(j + 1) * T] for j in range(NB)], axis=1)
        spec = _dot(fwd, zcat.astype(BF16))
        yre, yim = [], []
        for bi in range(NB):
            s0, i = bi - bi % P, bi % P
            re = im = None
            for j in range(P):
                d = i - j + P - 1
                hre = hre_ref[d, :, lanes(o)]
                him = him_ref[d, :, lanes(o)]
                zc = spec[0:T, lanes(s0 + j)]
                zs = spec[T:2 * T, lanes(s0 + j)]
                pre = zc * hre + zs * him
                pim = zc * him - zs * hre
                re = pre if re is None else re + pre
                im = pim if im is None else im + pim
            yre.append(re)
            yim.append(im)
        prod = jnp.concatenate([jnp.concatenate(yre, axis=1), jnp.concatenate(yim, axis=1)], axis=0)
        ycat = _dot(inv, prod.astype(BF16)) * (1.0 / T)
        y = jnp.concatenate([ycat[:, lanes(bi)] for bi in range(NB)], axis=0)
        z = gates[o] * (y + skip_ref[o:o + 1, :] * z)
    out_ref[:, SC_W:SC_W + HY_W] = z


def _schy(proj, sc_w, hy_w, skip, hre, him, *, l, B, L):
    T = HY_BLOCK
    _, _, c2, s2 = _dft_tables_np(T)
    fwd = jnp.asarray(np.concatenate([c2, s2], axis=0).astype(np.float32).astype(BF16))
    inv = jnp.asarray(np.concatenate([c2, -s2], axis=1).astype(np.float32).astype(BF16))
    n = HY_ORDER * HY_W
    nd = 2 * (L // T) - 1
    rows = max(L, HY_STEP_ROWS)
    return pl.pallas_call(
        functools.partial(_schy_kernel, L=L),
        out_shape=jax.ShapeDtypeStruct((B * L, SC_W + HY_W), F32),
        grid=(B * L // rows,),
        in_specs=[
            pl.BlockSpec((rows, 3 * SC_W), lambda b: (b, 0)),
            pl.BlockSpec((rows, 3 * HY_W), lambda b: (b, 1)),
            _layer_spec((3, SC_W), l), _layer_spec((3, 3 * HY_W), l), _layer_spec((HY_ORDER, HY_W), l),
            _const_spec((2 * T, T)), _const_spec((T, 2 * T)),
            _layer_spec((nd, T, n), l), _layer_spec((nd, T, n), l),
        ],
        out_specs=pl.BlockSpec((rows, SC_W + HY_W), lambda b: (b, 0)),
        compiler_params=pltpu.CompilerParams(
            dimension_semantics=("arbitrary",), vmem_limit_bytes=VMEM_LIMIT),
        name="sconv_hyena",
    )(proj, proj, sc_w, hy_w, skip, fwd, inv, hre, him)


def _gate_prep_kernel(*refs, B, L, with_state):
    if with_state:
        g_ref, gb_ref, m0_ref, rows_ref, cols_ref = refs
    else:
        g_ref, gb_ref, rows_ref, cols_ref = refs
    R = B * N_GATE
    gb = gb_ref[:, 0:1]
    x = jnp.concatenate([g_ref[:, b * L:(b + 1) * L] + gb for b in range(B)], axis=0)
    rr = lax.broadcasted_iota(jnp.int32, (R, L), 0) & (N_GATE - 1)
    is_f = ((rr >= ML_H) & (rr < 2 * ML_H)) | (rr >= 3 * ML_H)
    fwd_rows = rr < 2 * ML_H
    pk = jnp.where(is_f, _log_sigmoid(x), x)
    r_i = lax.broadcasted_iota(jnp.int32, (L, L), 0)
    c_i = lax.broadcasted_iota(jnp.int32, (L, L), 1)
    triu = jnp.where(r_i <= c_i, 1.0, 0.0).astype(BF16)
    cs = _dot(jnp.concatenate(_split3(pk), axis=0), triu)
    cum = cs[0:R] + cs[R:2 * R] + cs[2 * R:3 * R]
    suf = cum[:, L - 1:L] - cum + pk
    bsum = pltpu.roll(jnp.where(fwd_rows, cum, suf), R - ML_H, axis=0)
    r = pk - bsum
    cm = jnp.where(fwd_rows, _scan_max(r, reverse=False), _scan_max(r, reverse=True))
    if with_state:
        m0 = jnp.concatenate([m0_ref[b][:, 0:1] for b in range(B)], axis=0)
        m = jnp.maximum(m0, cm)
    else:
        m = jnp.maximum(cm, 0.0)
    rl = r * LOG2E
    ml = m * LOG2E
    mt = bsum + m
    pieces = [rl, ml, jnp.exp(-mt), mt]
    if not with_state:
        pieces.append(jnp.exp2(rl - jnp.where(fwd_rows, ml[:, L - 1:L], ml[:, 0:1])))
    pad = jnp.zeros((LANES - N_GATE, L), F32)
    for b in range(B):
        bs = slice(b * N_GATE, (b + 1) * N_GATE)
        for j, piece in enumerate(pieces):
            rows_ref[b, j * N_GATE:(j + 1) * N_GATE, :] = piece[bs]
        cols_ref[b * L:(b + 1) * L, :] = jnp.concatenate([rl[bs], pad], axis=0).T


def _gate_prep(g_t, gate_b, m0, *, l, B, L):
    with_state = m0 is not None
    n_rows = (4 if with_state else 5) * N_GATE
    full = lambda shape: pl.BlockSpec(shape, lambda i: tuple(0 for _ in shape))
    args = [g_t, gate_b]
    in_specs = [full((N_GATE, B * L)), pl.BlockSpec((None, N_GATE, LANES), lambda i: (l, 0, 0))]
    if with_state:
        args.append(m0)
        in_specs.append(pl.BlockSpec((B, None, N_GATE, LANES), lambda i: (0, l, 0, 0)))
    return pl.pallas_call(
        functools.partial(_gate_prep_kernel, B=B, L=L, with_state=with_state),
        out_shape=[jax.ShapeDtypeStruct((B, n_rows, L), F32), jax.ShapeDtypeStruct((B * L, LANES), F32)],
        grid=(1,),
        in_specs=in_specs,
        out_specs=[full((B, n_rows, L)), full((B * L, LANES))],
        compiler_params=pltpu.CompilerParams(
            dimension_semantics=("arbitrary",), vmem_limit_bytes=VMEM_LIMIT),
        name="gate_prep",
    )(*args)


def _mlstm_kernel(*refs, L, with_state, n_prev):
    if with_state:
        (q_ref, k_ref, v_ref, o_ref, rows_ref, cols_ref, cw_ref, ng_ref,
         c0_ref, n0_ref, m0_ref, y_ref) = refs
    elif n_prev:
        (q_ref, k_ref, v_ref, o_ref, rows_ref, cols_ref, cw_ref, ng_ref,
         cprev_ref, nprev_ref, mprev_ref, y_ref, cout_ref, nout_ref, mout_ref) = refs
        cout_ref[:, 0:n_prev] = cprev_ref[...]
        nout_ref[:, 0:n_prev] = nprev_ref[...]
        mout_ref[:, 0:n_prev] = mprev_ref[...]
        cout_ref, nout_ref, mout_ref = (r.at[:, n_prev] for r in (cout_ref, nout_ref, mout_ref))
    else:
        (q_ref, k_ref, v_ref, o_ref, rows_ref, cols_ref, cw_ref, ng_ref,
         y_ref, cout_ref, nout_ref, mout_ref) = refs

    S = rows_ref.shape[0]
    tb = min(L // 2, ATT_BLOCK)
    cw = cw_ref[...]
    q_all = _silu(_conv3(q_ref[...], cw[:, 0:ML_W], L))
    k_all = _silu(_conv3(k_ref[...], cw[:, ML_W:2 * ML_W], L)) * (ML_DH ** -0.5)

    s_loc = lax.broadcasted_iota(jnp.int32, (tb, tb), 0)
    t_loc = lax.broadcasted_iota(jnp.int32, (tb, tb), 1)
    masks = (s_loc <= t_loc, s_loc >= t_loc)
    ones_rows = jnp.ones((2 * 8, L), BF16)
    ng = ng_ref[...]
    for h in range(ML_H):
        hs = slice(h * ML_DH, (h + 1) * ML_DH)
        seqs = []
        for s in range(S):
            sl = slice(s * L, (s + 1) * L)
            rl16, ml16, em16, mt16 = (rows_ref[s, j * N_GATE:(j + 1) * N_GATE, :] for j in range(4))
            q = q_all[sl, hs]
            k = k_all[sl, hs]
            v = v_ref[sl, hs]
            vext_t = jnp.concatenate([v.T.astype(BF16), ones_rows], axis=0)
            rlb = [jnp.broadcast_to(cols_ref[sl, 2 * ML_H * d + h:2 * ML_H * d + h + 1], (L, tb)) for d in range(2)]
            seqs.append(dict(q=q, k=k, qb=q.astype(BF16), kb=k.astype(BF16), vb=v.astype(BF16), vext_t=vext_t,
                             rlb=rlb, ml16=ml16, em16=em16, mt16=mt16))
            if with_state:
                seqs[s]["c0_t"] = [c0_ref[s, d, h].T.astype(BF16) for d in range(2)]
        for r0 in range(0, L, tb):
            rs = slice(r0, r0 + tb)
            for s, sq in enumerate(seqs):
                s_t = _dot_nt(sq["kb"], sq["qb"][rs])
                if with_state:
                    qn_rows = _dot3_nt(n0_ref[s], sq["q"][rs])
                hout_t = None
                for d in range(2):
                    o8 = 2 * ML_H * d + h
                    ml_row = sq["ml16"][o8:o8 + 1, rs]
                    e_diag = jnp.exp2(jnp.where(masks[d], sq["rlb"][d][rs] - ml_row, NEG_BIG))
                    acc = _dot(sq["vext_t"][:, rs], (s_t[rs] * e_diag).astype(BF16))
                    side = slice(r0 + tb, L) if d else slice(0, r0)
                    if side.stop > side.start:
                        e_side = jnp.exp2(sq["rlb"][d][side] - ml_row)
                        acc = acc + _dot(sq["vext_t"][:, side], (s_t[side] * e_side).astype(BF16))
                    num_t = acc[0:ML_DH]
                    den = acc[ML_DH:ML_DH + 1]
                    if with_state:
                        wp = jnp.exp2(m0_ref[s, o8:o8 + 1, 0:1] * LOG2E - ml_row)
                        den = den + wp * qn_rows[d * ML_H + h:d * ML_H + h + 1]
                        num_t = num_t + wp * _dot_nt(sq["c0_t"][d], sq["qb"][rs])
                    contrib = num_t * (1.0 / jnp.maximum(jnp.abs(den), sq["em16"][o8:o8 + 1, rs]))
                    hout_t = contrib if hout_t is None else hout_t + contrib
                hn_t = hout_t * lax.rsqrt(jnp.mean(hout_t * hout_t, axis=0, keepdims=True) + EPS)
                rows_out = slice(s * L + r0, s * L + r0 + tb)
                y_ref[rows_out, hs] = _sigmoid(o_ref[rows_out, hs]) * (hn_t.T * ng[:, hs])

        if not with_state:
            for s, sq in enumerate(seqs):
                w16 = rows_ref[s, 4 * N_GATE:5 * N_GATE, :]
                k_t = sq["k"].T
                n_rows = _dot3(w16, sq["k"])
                for d in range(2):
                    o8 = 2 * ML_H * d + h
                    end = 0 if d else L - 1
                    cout_ref[s, d, h] = _dot((k_t * w16[o8:o8 + 1, :]).astype(BF16), sq["vb"])
                    nout_ref[s, d * ML_H + h:d * ML_H + h + 1, :] = n_rows[o8:o8 + 1, :]
                    mout_ref[s, d * ML_H + h:d * ML_H + h + 1, :] = jnp.broadcast_to(
                        sq["mt16"][o8:o8 + 1, end:end + 1], (1, LANES))


def _mlstm(proj, g_t, gate_b, conv_w, norm_g, state, prev, *, l, B, L):
    with_state = state is not None
    n_prev = 0 if prev is None else l
    rows, cols = _gate_prep(g_t, gate_b, state[2] if with_state else None, l=l, B=B, L=L)
    S = max(1, ML_STEP_ROWS // L)
    SL = S * L
    col0 = (3 * SC_W + 3 * HY_W) // ML_W
    in_specs = [
        pl.BlockSpec((SL, ML_W), lambda b: (b, col0)),
        pl.BlockSpec((SL, ML_W), lambda b: (b, col0 + 1)),
        pl.BlockSpec((SL, ML_W), lambda b: (b, col0 + 2)),
        pl.BlockSpec((SL, ML_W), lambda b: (b, col0 + 3)),
        pl.BlockSpec((S, rows.shape[1], L), lambda b: (b, 0, 0)),
        pl.BlockSpec((SL, LANES), lambda b: (b, 0)),
        _layer_spec((3, 2 * ML_W), l), _layer_spec((1, ML_W), l),
    ]
    args = [proj, proj, proj, proj, rows, cols, conv_w, norm_g]
    y_shape = jax.ShapeDtypeStruct((B * L, ML_W), F32)
    y_spec = pl.BlockSpec((SL, ML_W), lambda b: (b, 0))
    c_blk, n_blk, m_blk = (2, ML_H, ML_DH, ML_DH), (2 * ML_H, ML_DH), (2 * ML_H, LANES)
    if with_state:
        c0, n0, m0 = state
        in_specs += [
            pl.BlockSpec((S, None) + c_blk, lambda b: (b, l, 0, 0, 0, 0)),
            pl.BlockSpec((S, None) + n_blk, lambda b: (b, l, 0, 0)),
            pl.BlockSpec((S, None, N_GATE, LANES), lambda b: (b, l, 0, 0)),
        ]
        args += [c0, n0, m0]
        out_shape, out_specs = y_shape, y_spec
    else:
        lead = (l + 1,) if n_prev else ()
        if n_prev:
            in_specs += [pl.BlockSpec((S, n_prev) + blk, lambda b, nz=len(blk) + 1: (b,) + (0,) * nz)
                         for blk in (c_blk, n_blk, m_blk)]
            args += list(prev)
        out_shape = [y_shape] + [jax.ShapeDtypeStruct((B,) + lead + blk, F32) for blk in (c_blk, n_blk, m_blk)]
        out_specs = [y_spec] + [pl.BlockSpec((S,) + lead + blk, lambda b, nz=len(lead) + len(blk): (b,) + (0,) * nz)
                                for blk in (c_blk, n_blk, m_blk)]
    return pl.pallas_call(
        functools.partial(_mlstm_kernel, L=L, with_state=with_state, n_prev=n_prev),
        out_shape=out_shape,
        grid=(B // S,),
        in_specs=in_specs,
        out_specs=out_specs,
        compiler_params=pltpu.CompilerParams(
            dimension_semantics=("arbitrary",), vmem_limit_bytes=VMEM_LIMIT),
        name="mlstm",
    )(*args)


def _outmlp_kernel(x_ref, ya_ref, yb_ref, mod_ref, wo_ref, g2_ref, w1_ref, w2_ref, gf_ref, o_ref, *, final):
    mod = mod_ref[0]
    g1 = mod[:, 2 * D_MODEL:3 * D_MODEL]
    sh2 = mod[:, 3 * D_MODEL:4 * D_MODEL]
    sc2 = mod[:, 4 * D_MODEL:5 * D_MODEL]
    g2 = mod[:, 5 * D_MODEL:6 * D_MODEL]
    na = SC_W + HY_W
    attn = _dot(ya_ref[...].astype(BF16), wo_ref[0:na, :]) + _dot(yb_ref[...].astype(BF16), wo_ref[na:, :])
    x = x_ref[...] + g1 * attn
    u = ((_rms(x) * g2_ref[...]) * (1.0 + sc2) + sh2).astype(BF16)
    tf = 1024
    acc = jnp.zeros(x.shape, F32)
    for j in range(D_FF // tf):
        hcol = _dot(u, w1_ref[:, j * tf:(j + 1) * tf])
        hcol = jnp.square(jnp.maximum(hcol, 0.0)).astype(BF16)
        acc = acc + _dot(hcol, w2_ref[j * tf:(j + 1) * tf, :])
    x = x + g2 * acc
    if final:
        x = _rms(x) * gf_ref[...]
    o_ref[...] = x


def _outmlp(x2d, ya, yb, mods, w_out, g2, w1, w2, gf, *, l, L, per_request, final):
    T = x2d.shape[0]
    tm = ROW_BLOCK
    return pl.pallas_call(
        functools.partial(_outmlp_kernel, final=final),
        out_shape=jax.ShapeDtypeStruct((T, D_MODEL), F32),
        grid=(T // tm,),
        in_specs=[
            pl.BlockSpec((tm, D_MODEL), lambda i: (i, 0)),
            pl.BlockSpec((tm, SC_W + HY_W), lambda i: (i, 0)),
            pl.BlockSpec((tm, ML_W), lambda i: (i, 0)),
            _mod_spec(l, L, tm, per_request),
            _const_spec((D_MODEL, D_MODEL)), _layer_spec((1, D_MODEL), l),
            _const_spec((D_MODEL, D_FF)), _const_spec((D_FF, D_MODEL)), _const_spec((1, D_MODEL)),
        ],
        out_specs=pl.BlockSpec((tm, D_MODEL), lambda i: (i, 0)),
        compiler_params=pltpu.CompilerParams(
            dimension_semantics=("arbitrary",), vmem_limit_bytes=VMEM_LIMIT),
        name="outproj_mlp",
    )(x2d, ya, yb, mods, w_out, g2, w1, w2, gf)


def kernel(x_prompt, x_sample, state_C, state_n, state_m, c, c_ctx, norm1_g, ada_w, ada_b, w_in, sc_conv_w, hy_conv_w, hy_w1, hy_b1, hy_w2, hy_b2, hy_w3, hy_freq, hy_skip, ml_conv_w, ml_gate_b, ml_norm_g, w_out, norm2_g, mlp_w1, mlp_w2, norm_f_g):
    B, L = x_prompt.shape[0], x_prompt.shape[1]
    Bd, Ld = x_sample.shape[0], x_sample.shape[1]

    cond = jnp.concatenate([c_ctx[None, :], c, jnp.zeros((8 - 1 - Bd, D_MODEL), F32)], axis=0)
    mods = _ada(cond, ada_w, ada_b).reshape(DEPTH * 8, 1, 6 * D_MODEL)

    w_in_t = jnp.swapaxes(w_in, 1, 2).astype(BF16)
    w_gate = w_in_t[:, N_MAIN:, :]
    gate_b = jnp.broadcast_to(ml_gate_b[:, :, None], (DEPTH, N_GATE, LANES))
    pad_h = LANES - HY_HID
    w1p = jnp.pad(hy_w1, ((0, 0), (0, LANES - HY_EMB), (0, pad_h)))
    b1p = jnp.pad(hy_b1, ((0, 0), (0, pad_h)))
    w2p = jnp.pad(hy_w2, ((0, 0), (0, pad_h), (0, pad_h)))
    b2p = jnp.pad(hy_b2, ((0, 0), (0, pad_h)))
    w3p = jnp.pad(hy_w3, ((0, 0), (0, pad_h), (0, 0)))
    fp = jnp.pad(hy_freq, ((0, 0), (0, pad_h)))
    pos = jnp.asarray(_pos_embed_np(Ld))
    m0_all = jnp.pad(state_m, ((0, 0), (0, 0), (0, 0), (0, ML_H))).reshape(Bd, DEPTH, N_GATE, 1)
    m0_all = jnp.broadcast_to(m0_all, (Bd, DEPTH, N_GATE, LANES))
    n0_all = state_n.reshape(Bd, DEPTH, 2 * ML_H, ML_DH)
    gf = norm_f_g[None, :]
    g1 = norm1_g[:, None, :]
    g2 = norm2_g[:, None, :]
    ml_ng = ml_norm_g[:, None, :]
    filt_args = (w1p, b1p[:, None, :], w2p, b2p[:, None, :], w3p, fp[:, None, :])
    hre_c, him_c = _filter_spectrum(L, *filt_args)
    hre_s, him_s = _filter_spectrum(Ld, *filt_args)
    state = (state_C, n0_all, m0_all)

    xp = x_prompt.reshape(B * L, D_MODEL)
    xs = x_sample.reshape(Bd * Ld, D_MODEL)
    prev = None
    for l in range(DEPTH):
        final = l == DEPTH - 1

        proj, g_t, w_out_b, w1_b, w2_b = _inproj(xp, None, mods, g1, w_in_t, w_gate, (w_out, mlp_w1, mlp_w2),
                                                 l=l, L=L, per_request=False)
        ya = _schy(proj, sc_conv_w, hy_conv_w, hy_skip, hre_c, him_c, l=l, B=B, L=L)
        yb, *states = _mlstm(proj, g_t, gate_b, ml_conv_w, ml_ng, None, prev, l=l, B=B, L=L)
        xp = _outmlp(xp, ya, yb, mods, w_out_b, g2, w1_b, w2_b, gf, l=l, L=L, per_request=False, final=final)
        if not final:
            prev = [s.reshape((B, 1) + s.shape[1:]) for s in states] if l == 0 else states

        res = _inproj(xs, pos if l == 0 else None, mods, g1, w_in_t, w_gate, l=l, L=Ld, per_request=True)
        if l == 0:
            proj, g_t, xs = res
        else:
            proj, g_t = res
        ya = _schy(proj, sc_conv_w, hy_conv_w, hy_skip, hre_s, him_s, l=l, B=Bd, L=Ld)
        yb = _mlstm(proj, g_t, gate_b, ml_conv_w, ml_ng, state, None, l=l, B=Bd, L=Ld)
        xs = _outmlp(xs, ya, yb, mods, w_out_b, g2, w1_b, w2_b, gf, l=l, L=Ld, per_request=True, final=final)

    y_prompt = xp.reshape(B, L, D_MODEL)
    y_sample = xs.reshape(Bd, Ld, D_MODEL)
    new_c, new_n, new_m = states
    return (y_prompt, y_sample, new_c, new_n.reshape(B, DEPTH, 2, ML_H, ML_DH),
            new_m[..., 0].reshape(B, DEPTH, 2, ML_H))
```

```python
import functools
import math

import numpy as np
import jax
import jax.numpy as jnp
from jax import lax
from jax.experimental import pallas as pl
from jax.experimental.pallas import tpu as pltpu

F32 = jnp.float32
BF16 = jnp.bfloat16

D_MODEL = 1024
DEPTH = 2
GRID_W = 64
SC_W = 256
HY_W = 256
ML_W = 512
ML_H = 4
ML_DH = ML_W // ML_H
D_FF = 4 * D_MODEL
HY_ORDER = 2
HY_BANDS = 16
HY_EMB = 1 + 2 * HY_BANDS
HY_HID = 64
HY_DECAY_SHORT = 0.3
HY_DECAY_LONG = 1.5
HY_DECAY_TARGET = 1e-2
EPS = 1e-6
N_MAIN = 3 * SC_W + 3 * HY_W + 4 * ML_W
N_GATE = 4 * ML_H
LANES = 128
ROW_BLOCK = 512
ATT_BLOCK = 256
HY_BLOCK = 256
HY_STEP_ROWS = 1024
ML_STEP_ROWS = 512
PROJ_DTYPE = BF16
NEG_BIG = -1e30
LOG2E = math.log2(math.e)
VMEM_LIMIT = 60 * 1024 * 1024


def _dot(a, b):
    return jnp.dot(a, b, preferred_element_type=F32)


def _dot_nt(a, b):
    return lax.dot_general(a, b, (((1,), (1,)), ((), ())), preferred_element_type=F32)


def _split2(a):
    hi = a.astype(BF16)
    lo = (a - hi.astype(F32)).astype(BF16)
    return hi, lo


def _split3(a):
    hi = a.astype(BF16)
    r = a - hi.astype(F32)
    mid = r.astype(BF16)
    lo = (r - mid.astype(F32)).astype(BF16)
    return hi, mid, lo


def _dot3(a, b):
    ah, al = _split2(a)
    bh, bl = _split2(b)
    return _dot(ah, bh) + _dot(al, bh) + _dot(ah, bl)


def _dot3_nt(a, b):
    ah, al = _split2(a)
    bh, bl = _split2(b)
    return _dot_nt(ah, bh) + _dot_nt(al, bh) + _dot_nt(ah, bl)


def _dot3_tab(th, tl, b):
    bh, bl = _split2(b)
    return _dot(th, bh) + _dot(tl, bh) + _dot(th, bl)


def _sigmoid(x):
    return 1.0 / (1.0 + jnp.exp(-x))


def _silu(x):
    return x * _sigmoid(x)


def _log_sigmoid(x):
    return jnp.minimum(x, 0.0) - jnp.log(1.0 + jnp.exp(-jnp.abs(x)))


def _conv3(x, w, seq_len=None):
    R = x.shape[0]
    seq_len = R if seq_len is None else seq_len
    pos = lax.broadcasted_iota(jnp.int32, x.shape, 0) & (seq_len - 1)
    xm = jnp.where(pos == 0, 0.0, pltpu.roll(x, 1, axis=0))
    xp = jnp.where(pos == seq_len - 1, 0.0, pltpu.roll(x, R - 1, axis=0))
    return xm * w[0:1, :] + x * w[1:2, :] + xp * w[2:3, :]


def _rms(x):
    return x * lax.rsqrt(jnp.mean(x * x, axis=-1, keepdims=True) + EPS)


def _scan_max(x, *, reverse):
    n = x.shape[1]
    lane = lax.broadcasted_iota(jnp.int32, x.shape, 1)
    d = 1
    while d < n:
        if reverse:
            shifted = jnp.where(lane < n - d, pltpu.roll(x, n - d, axis=1), NEG_BIG)
        else:
            shifted = jnp.where(lane >= d, pltpu.roll(x, d, axis=1), NEG_BIG)
        x = jnp.maximum(x, shifted)
        d *= 2
    return x


def _hi_lo(a64):
    a = a64.astype(np.float32)
    hi = a.astype(BF16)
    lo = (a - hi.astype(np.float32)).astype(BF16)
    return jnp.asarray(hi), jnp.asarray(lo)


def _const_spec(shape):
    return pl.BlockSpec(shape, lambda *_: tuple(0 for _ in shape), pipeline_mode=pl.Buffered(1))


def _layer_spec(shape, l):
    return pl.BlockSpec((None,) + tuple(shape), lambda *_: (l,) + tuple(0 for _ in shape),
                        pipeline_mode=pl.Buffered(1))


@functools.lru_cache(maxsize=None)
def _dft_tables_np(L):
    k = np.arange(L, dtype=np.int64)
    m1 = ((2 * k[:, None] + 1) * k[None, :]) % (4 * L)
    a1 = np.pi * m1.astype(np.float64) / (2 * L)
    m2 = ((2 * k[:, None] + 1) * (2 * k[None, :] + 1)) % (8 * L)
    a2 = np.pi * m2.astype(np.float64) / (4 * L)
    return np.cos(a1), np.sin(a1), np.cos(a2), np.sin(a2)


@functools.lru_cache(maxsize=None)
def _filter_consts_np(L):
    t_idx = np.arange(L, dtype=np.float64)
    t = t_idx / (L - 1)
    bands = np.arange(1, HY_BANDS + 1, dtype=np.float64)
    ang = 2.0 * math.pi * t_idx[:, None] * bands[None, :] / L
    z = np.zeros((L, LANES), np.float64)
    z[:, 0] = t
    z[:, 1:1 + HY_BANDS] = np.cos(ang)
    z[:, 1 + HY_BANDS:1 + 2 * HY_BANDS] = -np.sin(ang)
    lin = np.linspace(math.log(HY_DECAY_TARGET) / HY_DECAY_LONG,
                      math.log(HY_DECAY_TARGET) / HY_DECAY_SHORT, HY_W).astype(np.float32)
    deltas = np.abs(lin).astype(np.float64)
    decay = np.exp(-t[:, None] * deltas[None, :])
    return z.astype(np.float32), decay.astype(np.float32)


@functools.lru_cache(maxsize=None)
def _pos_embed_np(L):
    rows = L // GRID_W
    r, cidx = np.meshgrid(np.arange(rows, dtype=np.float64), np.arange(GRID_W, dtype=np.float64), indexing="ij")
    r = r.reshape(-1)
    cidx = cidx.reshape(-1)
    quarter = D_MODEL // 4
    omega = 1.0 / (10000.0 ** (np.arange(quarter, dtype=np.float64) / quarter))
    ar = r[:, None] * omega[None, :]
    ac = cidx[:, None] * omega[None, :]
    return np.concatenate([np.sin(ar), np.cos(ar), np.sin(ac), np.cos(ac)], axis=-1).astype(np.float32)


def _ada_kernel(c_ref, w_ref, b_ref, o_ref):
    sc = _silu(c_ref[...])
    o_ref[0] = _dot3(sc, w_ref[0]) + b_ref[0]


def _ada(cond, ada_w, ada_b):
    tn = 1536
    nd = 6 * D_MODEL
    return pl.pallas_call(
        _ada_kernel,
        out_shape=jax.ShapeDtypeStruct((DEPTH, 8, nd), F32),
        grid=(DEPTH, nd // tn),
        in_specs=[
            pl.BlockSpec((8, D_MODEL), lambda l, j: (0, 0)),
            pl.BlockSpec((1, D_MODEL, tn), lambda l, j: (l, 0, j)),
            pl.BlockSpec((1, 1, tn), lambda l, j: (l, 0, j)),
        ],
        out_specs=pl.BlockSpec((1, 8, tn), lambda l, j: (l, 0, j)),
        compiler_params=pltpu.CompilerParams(
            dimension_semantics=("arbitrary", "arbitrary"), vmem_limit_bytes=VMEM_LIMIT),
        name="ada_mod",
    )(cond, ada_w, ada_b.reshape(DEPTH, 1, nd))


def _inproj_kernel(*refs, with_pos, n_cast):
    if n_cast:
        n_in = 6 if with_pos else 5
        cast_in, cast_out = refs[n_in:n_in + n_cast], refs[-n_cast:]
        refs = refs[:n_in] + refs[n_in + n_cast:-n_cast]
        for src, dst in zip(cast_in, cast_out):
            dst[...] = src[...].astype(BF16)
    if with_pos:
        x_ref, pos_ref, mod_ref, g_ref, w_ref, wg_ref, proj_ref, gate_ref, xo_ref = refs
        x = x_ref[...] + pos_ref[...]
        xo_ref[...] = x
    else:
        x_ref, mod_ref, g_ref, w_ref, wg_ref, proj_ref, gate_ref = refs
        x = x_ref[...]
    mod = mod_ref[0]
    sh1 = mod[:, 0:D_MODEL]
    sc1 = mod[:, D_MODEL:2 * D_MODEL]
    u = (_rms(x) * g_ref[...]) * (1.0 + sc1) + sh1
    ub = u.astype(BF16)
    tn = 512
    for j in range(N_MAIN // tn):
        proj_ref[:, j * tn:(j + 1) * tn] = _dot_nt(ub, w_ref[j * tn:(j + 1) * tn, :]).astype(PROJ_DTYPE)
    gate_ref[...] = _dot_nt(wg_ref[...], ub)


def _mod_spec(l, L, tm, per_request):
    if per_request:
        return pl.BlockSpec((1, 1, 6 * D_MODEL), lambda i: (l * 8 + 1 + (i * tm) // L, 0, 0))
    return pl.BlockSpec((1, 1, 6 * D_MODEL), lambda i: (l * 8, 0, 0))


def _inproj(x2d, pos, mods, g, w_in_t, w_gate, cast=(), *, l, L, per_request):
    T = x2d.shape[0]
    tm = ROW_BLOCK
    nblk = T // tm
    per_seq = L // tm
    with_pos = pos is not None
    row_spec = pl.BlockSpec((tm, D_MODEL), lambda i: (i, 0))
    in_specs = [row_spec]
    args = [x2d]
    if with_pos:
        in_specs.append(pl.BlockSpec((tm, D_MODEL), lambda i: (i % per_seq, 0)))
        args.append(pos)
    in_specs += [
        _mod_spec(l, L, tm, per_request),
        _layer_spec((1, D_MODEL), l),
        _layer_spec((N_MAIN, D_MODEL), l),
        _layer_spec((N_GATE, D_MODEL), l),
    ]
    args += [mods, g, w_in_t, w_gate]
    out_shape = [jax.ShapeDtypeStruct((T, N_MAIN), PROJ_DTYPE), jax.ShapeDtypeStruct((N_GATE, T), F32)]
    out_specs = [pl.BlockSpec((tm, N_MAIN), lambda i: (i, 0)), pl.BlockSpec((N_GATE, tm), lambda i: (0, i))]
    if with_pos:
        out_shape.append(jax.ShapeDtypeStruct((T, D_MODEL), F32))
        out_specs.append(row_spec)
    for w in cast:
        _, rows, cols = w.shape
        in_specs.append(pl.BlockSpec((None, rows // nblk, cols), lambda i: (l, i, 0)))
        args.append(w)
        out_shape.append(jax.ShapeDtypeStruct((rows, cols), BF16))
        out_specs.append(pl.BlockSpec((rows // nblk, cols), lambda i: (i, 0)))
    return pl.pallas_call(
        functools.partial(_inproj_kernel, with_pos=with_pos, n_cast=len(cast)),
        out_shape=out_shape,
        grid=(nblk,),
        in_specs=in_specs,
        out_specs=out_specs,
        compiler_params=pltpu.CompilerParams(
            dimension_semantics=("arbitrary",), vmem_limit_bytes=VMEM_LIMIT),
        name="inproj",
    )(*args)


def _filter_kernel(z_ref, dec_ref, w1_ref, b1_ref, w2_ref, b2_ref, w3_ref, f_ref,
                   ch_ref, cl_ref, sh_ref, sl_ref, hre_ref, him_ref, *, L):
    T = HY_BLOCK
    P = L // T
    f = f_ref[...]
    hdn = jnp.sin(f * (_dot3(z_ref[...], w1_ref[...]) + b1_ref[...]))
    hdn = jnp.sin(f * (_dot3(hdn, w2_ref[...]) + b2_ref[...]))
    h = _dot3(hdn, w3_ref[...])
    dec = dec_ref[...]
    row = lax.broadcasted_iota(jnp.int32, (L, HY_W), 0)
    halves = []
    for o in range(HY_ORDER):
        hf = h[:, (2 * o) * HY_W:(2 * o + 1) * HY_W] * dec
        hb = jnp.where(row == 0, 0.0, h[:, (2 * o + 1) * HY_W:(2 * o + 2) * HY_W] * dec)
        nrm = jnp.sum(jnp.abs(hf), axis=0, keepdims=True) + jnp.sum(jnp.abs(hb), axis=0, keepdims=True)
        inv = 1.0 / nrm
        halves.append((hf * inv, hb * inv))
    blocks = [half[e * T:(e + 1) * T] for pair in halves for half in pair for e in range(P)]
    x = jnp.concatenate(blocks, axis=1)
    xre = _dot3_tab(ch_ref[...], cl_ref[...], x)
    xim = -_dot3_tab(sh_ref[...], sl_ref[...], x)
    sgn = jnp.where((lax.broadcasted_iota(jnp.int32, (T, HY_W), 0) & 1) == 0, 1.0, -1.0)
    for o in range(HY_ORDER):
        def half_spec(which, e):
            j = (o * 2 + which) * P + e
            cs = slice(j * HY_W, (j + 1) * HY_W)
            return xre[:, cs], xim[:, cs], halves[o][which][e * T:e * T + 1]
        for d in range(-(P - 1), P):
            if d == 0:
                (fre, fim, _), (gre, gim, _) = half_spec(0, 0), half_spec(1, 0)
                re, im = fre + gre, fim - gim
            elif d > 0:
                (fre, fim, _), (pre, pim, p0) = half_spec(0, d), half_spec(0, d - 1)
                re, im = fre - sgn * pim, fim + sgn * (pre - p0)
            else:
                (gre, gim, _), (pre, pim, p0) = half_spec(1, -d), half_spec(1, -d - 1)
                re, im = gre - sgn * pim, -gim - sgn * (pre - p0)
            hre_ref[d + P - 1, :, o * HY_W:(o + 1) * HY_W] = re
            him_ref[d + P - 1, :, o * HY_W:(o + 1) * HY_W] = im


def _filter_spectrum(L, w1p, b1p, w2p, b2p, w3p, fp):
    T = HY_BLOCK
    z_np, dec_np = _filter_consts_np(L)
    c1, s1, _, _ = _dft_tables_np(T)
    ch, cl = _hi_lo(c1)
    sh, sl = _hi_lo(s1)
    n = HY_ORDER * HY_W
    nd = 2 * (L // T) - 1
    const = _const_spec
    per_layer = lambda shape: pl.BlockSpec((None,) + shape, lambda l: (l,) + tuple(0 for _ in shape))
    return pl.pallas_call(
        functools.partial(_filter_kernel, L=L),
        out_shape=[jax.ShapeDtypeStruct((DEPTH, nd, T, n), F32), jax.ShapeDtypeStruct((DEPTH, nd, T, n), F32)],
        grid=(DEPTH,),
        in_specs=[const((L, LANES)), const((L, HY_W)), per_layer((LANES, LANES)), per_layer((1, LANES)),
                  per_layer((LANES, LANES)), per_layer((1, LANES)), per_layer((LANES, 2 * n)), per_layer((1, LANES)),
                  const((T, T)), const((T, T)), const((T, T)), const((T, T))],
        out_specs=[per_layer((nd, T, n)), per_layer((nd, T, n))],
        compiler_params=pltpu.CompilerParams(
            dimension_semantics=("arbitrary",), vmem_limit_bytes=VMEM_LIMIT),
        name="hyena_filter",
    )(jnp.asarray(z_np), jnp.asarray(dec_np), w1p, b1p, w2p, b2p, w3p, fp, ch, cl, sh, sl)


def _schy_kernel(sc_ref, hy_ref, scw_ref, hyw_ref, skip_ref, fwd_ref, inv_ref, hre_ref, him_ref, out_ref, *, L):
    T = HY_BLOCK
    P = L // T
    NB = sc_ref.shape[0] // T
    sc = sc_ref[...].astype(F32)
    b_g = sc[:, 0:SC_W]
    c_g = sc[:, SC_W:2 * SC_W]
    xin = sc[:, 2 * SC_W:3 * SC_W]
    out_ref[:, 0:SC_W] = b_g * _conv3(c_g * xin, scw_ref[...], L)

    u3 = _conv3(hy_ref[...].astype(F32), hyw_ref[...], L)
    z = u3[:, 0:HY_W]
    gates = (u3[:, HY_W:2 * HY_W], u3[:, 2 * HY_W:3 * HY_W])
    fwd, inv = fwd_ref[...], inv_ref[...]
    lanes = lambda j: slice(j * HY_W, (j + 1) * HY_W)
    for o in range(HY_ORDER):
        zcat = jnp.concatenate([z[j * T:(j + 1) * T] for j in range(NB)], axis=1)
        spec = _dot(fwd, zcat.astype(BF16))
        yre, yim = [], []
        for bi in range(NB):
            s0, i = bi - bi % P, bi % P
            re = im = None
            for j in range(P):
                d = i - j + P - 1
                hre = hre_ref[d, :, lanes(o)]
                him = him_ref[d, :, lanes(o)]
                zc = spec[0:T, lanes(s0 + j)]
                zs = spec[T:2 * T, lanes(s0 + j)]
                pre = zc * hre + zs * him
                pim = zc * him - zs * hre
                re = pre if re is None else re + pre
                im = pim if im is None else im + pim
            yre.append(re)
            yim.append(im)
        prod = jnp.concatenate([jnp.concatenate(yre, axis=1), jnp.concatenate(yim, axis=1)], axis=0)
        ycat = _dot(inv, prod.astype(BF16)) * (1.0 / T)
        y = jnp.concatenate([ycat[:, lanes(bi)] for bi in range(NB)], axis=0)
        z = gates[o] * (y + skip_ref[o:o + 1, :] * z)
    out_ref[:, SC_W:SC_W + HY_W] = z


def _schy(proj, sc_w, hy_w, skip, hre, him, *, l, B, L):
    T = HY_BLOCK
    _, _, c2, s2 = _dft_tables_np(T)
    fwd = jnp.asarray(np.concatenate([c2, s2], axis=0).astype(np.float32).astype(BF16))
    inv = jnp.asarray(np.concatenate([c2, -s2], axis=1).astype(np.float32).astype(BF16))
    n = HY_ORDER * HY_W
    nd = 2 * (L // T) - 1
    rows = max(L, HY_STEP_ROWS)
    return pl.pallas_call(
        functools.partial(_schy_kernel, L=L),
        out_shape=jax.ShapeDtypeStruct((B * L, SC_W + HY_W), F32),
        grid=(B * L // rows,),
        in_specs=[
            pl.BlockSpec((rows, 3 * SC_W), lambda b: (b, 0)),
            pl.BlockSpec((rows, 3 * HY_W), lambda b: (b, 1)),
            _layer_spec((3, SC_W), l), _layer_spec((3, 3 * HY_W), l), _layer_spec((HY_ORDER, HY_W), l),
            _const_spec((2 * T, T)), _const_spec((T, 2 * T)),
            _layer_spec((nd, T, n), l), _layer_spec((nd, T, n), l),
        ],
        out_specs=pl.BlockSpec((rows, SC_W + HY_W), lambda b: (b, 0)),
        compiler_params=pltpu.CompilerParams(
            dimension_semantics=("arbitrary",), vmem_limit_bytes=VMEM_LIMIT),
        name="sconv_hyena",
    )(proj, proj, sc_w, hy_w, skip, fwd, inv, hre, him)


def _gate_prep_kernel(*refs, B, L, with_state):
    if with_state:
        g_ref, gb_ref, m0_ref, rows_ref, cols_ref = refs
    else:
        g_ref, gb_ref, rows_ref, cols_ref = refs
    R = B * N_GATE
    gb = gb_ref[:, 0:1]
    x = jnp.concatenate([g_ref[:, b * L:(b + 1) * L] + gb for b in range(B)], axis=0)
    rr = lax.broadcasted_iota(jnp.int32, (R, L), 0) & (N_GATE - 1)
    is_f = ((rr >= ML_H) & (rr < 2 * ML_H)) | (rr >= 3 * ML_H)
    fwd_rows = rr < 2 * ML_H
    pk = jnp.where(is_f, _log_sigmoid(x), x)
    r_i = lax.broadcasted_iota(jnp.int32, (L, L), 0)
    c_i = lax.broadcasted_iota(jnp.int32, (L, L), 1)
    triu = jnp.where(r_i <= c_i, 1.0, 0.0).astype(BF16)
    cs = _dot(jnp.concatenate(_split3(pk), axis=0), triu)
    cum = cs[0:R] + cs[R:2 * R] + cs[2 * R:3 * R]
    suf = cum[:, L - 1:L] - cum + pk
    bsum = pltpu.roll(jnp.where(fwd_rows, cum, suf), R - ML_H, axis=0)
    r = pk - bsum
    cm = jnp.where(fwd_rows, _scan_max(r, reverse=False), _scan_max(r, reverse=True))
    if with_state:
        m0 = jnp.concatenate([m0_ref[b][:, 0:1] for b in range(B)], axis=0)
        m = jnp.maximum(m0, cm)
    else:
        m = jnp.maximum(cm, 0.0)
    rl = r * LOG2E
    ml = m * LOG2E
    mt = bsum + m
    pieces = [rl, ml, jnp.exp(-mt), mt]
    if not with_state:
        pieces.append(jnp.exp2(rl - jnp.where(fwd_rows, ml[:, L - 1:L], ml[:, 0:1])))
    pad = jnp.zeros((LANES - N_GATE, L), F32)
    for b in range(B):
        bs = slice(b * N_GATE, (b + 1) * N_GATE)
        for j, piece in enumerate(pieces):
            rows_ref[b, j * N_GATE:(j + 1) * N_GATE, :] = piece[bs]
        cols_ref[b * L:(b + 1) * L, :] = jnp.concatenate([rl[bs], pad], axis=0).T


def _gate_prep(g_t, gate_b, m0, *, l, B, L):
    with_state = m0 is not None
    n_rows = (4 if with_state else 5) * N_GATE
    full = lambda shape: pl.BlockSpec(shape, lambda i: tuple(0 for _ in shape))
    args = [g_t, gate_b]
    in_specs = [full((N_GATE, B * L)), pl.BlockSpec((None, N_GATE, LANES), lambda i: (l, 0, 0))]
    if with_state:
        args.append(m0)
        in_specs.append(pl.BlockSpec((B, None, N_GATE, LANES), lambda i: (0, l, 0, 0)))
    return pl.pallas_call(
        functools.partial(_gate_prep_kernel, B=B, L=L, with_state=with_state),
        out_shape=[jax.ShapeDtypeStruct((B, n_rows, L), F32), jax.ShapeDtypeStruct((B * L, LANES), F32)],
        grid=(1,),
        in_specs=in_specs,
        out_specs=[full((B, n_rows, L)), full((B * L, LANES))],
        compiler_params=pltpu.CompilerParams(
            dimension_semantics=("arbitrary",), vmem_limit_bytes=VMEM_LIMIT),
        name="gate_prep",
    )(*args)


def _mlstm_kernel(*refs, L, with_state, n_prev):
    if with_state:
        (q_ref, k_ref, v_ref, o_ref, rows_ref, cols_ref, cw_ref, ng_ref,
         c0_ref, n0_ref, m0_ref, y_ref) = refs
    elif n_prev:
        (q_ref, k_ref, v_ref, o_ref, rows_ref, cols_ref, cw_ref, ng_ref,
         cprev_ref, nprev_ref, mprev_ref, y_ref, cout_ref, nout_ref, mout_ref) = refs
        cout_ref[:, 0:n_prev] = cprev_ref[...]
        nout_ref[:, 0:n_prev] = nprev_ref[...]
        mout_ref[:, 0:n_prev] = mprev_ref[...]
        cout_ref, nout_ref, mout_ref = (r.at[:, n_prev] for r in (cout_ref, nout_ref, mout_ref))
    else:
        (q_ref, k_ref, v_ref, o_ref, rows_ref, cols_ref, cw_ref, ng_ref,
         y_ref, cout_ref, nout_ref, mout_ref) = refs

    S = rows_ref.shape[0]
    tb = min(L // 2, ATT_BLOCK)
    cw = cw_ref[...]
    q_all = _silu(_conv3(q_ref[...].astype(F32), cw[:, 0:ML_W], L))
    k_all = _silu(_conv3(k_ref[...].astype(F32), cw[:, ML_W:2 * ML_W], L)) * (ML_DH ** -0.5)

    s_loc = lax.broadcasted_iota(jnp.int32, (tb, tb), 0)
    t_loc = lax.broadcasted_iota(jnp.int32, (tb, tb), 1)
    masks = (s_loc <= t_loc, s_loc >= t_loc)
    ones_rows = jnp.ones((2 * 8, L), BF16)
    ng = ng_ref[...]
    for h in range(ML_H):
        hs = slice(h * ML_DH, (h + 1) * ML_DH)
        seqs = []
        for s in range(S):
            sl = slice(s * L, (s + 1) * L)
            rl16, ml16, em16, mt16 = (rows_ref[s, j * N_GATE:(j + 1) * N_GATE, :] for j in range(4))
            q = q_all[sl, hs]
            k = k_all[sl, hs]
            v = v_ref[sl, hs].astype(F32)
            vext_t = jnp.concatenate([v.T.astype(BF16), ones_rows], axis=0)
            rlb = [jnp.broadcast_to(cols_ref[sl, 2 * ML_H * d + h:2 * ML_H * d + h + 1], (L, tb)) for d in range(2)]
            seqs.append(dict(q=q, k=k, qb=q.astype(BF16), kb=k.astype(BF16), vb=v.astype(BF16), vext_t=vext_t,
                             rlb=rlb, ml16=ml16, em16=em16, mt16=mt16))
            if with_state:
                seqs[s]["c0_t"] = [c0_ref[s, d, h].T.astype(BF16) for d in range(2)]
        for r0 in range(0, L, tb):
            rs = slice(r0, r0 + tb)
            for s, sq in enumerate(seqs):
                s_t = _dot_nt(sq["kb"], sq["qb"][rs])
                if with_state:
                    qn_rows = _dot3_nt(n0_ref[s], sq["q"][rs])
                hout_t = None
                for d in range(2):
                    o8 = 2 * ML_H * d + h
                    ml_row = sq["ml16"][o8:o8 + 1, rs]
                    e_diag = jnp.exp2(jnp.where(masks[d], sq["rlb"][d][rs] - ml_row, NEG_BIG))
                    acc = _dot(sq["vext_t"][:, rs], (s_t[rs] * e_diag).astype(BF16))
                    side = slice(r0 + tb, L) if d else slice(0, r0)
                    if side.stop > side.start:
                        e_side = jnp.exp2(sq["rlb"][d][side] - ml_row)
                        acc = acc + _dot(sq["vext_t"][:, side], (s_t[side] * e_side).astype(BF16))
                    num_t = acc[0:ML_DH]
                    den = acc[ML_DH:ML_DH + 1]
                    if with_state:
                        wp = jnp.exp2(m0_ref[s, o8:o8 + 1, 0:1] * LOG2E - ml_row)
                        den = den + wp * qn_rows[d * ML_H + h:d * ML_H + h + 1]
                        num_t = num_t + wp * _dot_nt(sq["c0_t"][d], sq["qb"][rs])
                    contrib = num_t * (1.0 / jnp.maximum(jnp.abs(den), sq["em16"][o8:o8 + 1, rs]))
                    hout_t = contrib if hout_t is None else hout_t + contrib
                hn_t = hout_t * lax.rsqrt(jnp.mean(hout_t * hout_t, axis=0, keepdims=True) + EPS)
                rows_out = slice(s * L + r0, s * L + r0 + tb)
                y_ref[rows_out, hs] = _sigmoid(o_ref[rows_out, hs].astype(F32)) * (hn_t.T * ng[:, hs])

        if not with_state:
            for s, sq in enumerate(seqs):
                w16 = rows_ref[s, 4 * N_GATE:5 * N_GATE, :]
                k_t = sq["k"].T
                n_rows = _dot3(w16, sq["k"])
                for d in range(2):
                    o8 = 2 * ML_H * d + h
                    end = 0 if d else L - 1
                    cout_ref[s, d, h] = _dot((k_t * w16[o8:o8 + 1, :]).astype(BF16), sq["vb"])
                    nout_ref[s, d * ML_H + h:d * ML_H + h + 1, :] = n_rows[o8:o8 + 1, :]
                    mout_ref[s, d * ML_H + h:d * ML_H + h + 1, :] = jnp.broadcast_to(
                        sq["mt16"][o8:o8 + 1, end:end + 1], (1, LANES))


def _mlstm(proj, g_t, gate_b, conv_w, norm_g, state, prev, *, l, B, L):
    with_state = state is not None
    n_prev = 0 if prev is None else l
    rows, cols = _gate_prep(g_t, gate_b, state[2] if with_state else None, l=l, B=B, L=L)
    S = max(1, ML_STEP_ROWS // L)
    SL = S * L
    col0 = (3 * SC_W + 3 * HY_W) // ML_W
    in_specs = [
        pl.BlockSpec((SL, ML_W), lambda b: (b, col0)),
        pl.BlockSpec((SL, ML_W), lambda b: (b, col0 + 1)),
        pl.BlockSpec((SL, ML_W), lambda b: (b, col0 + 2)),
        pl.BlockSpec((SL, ML_W), lambda b: (b, col0 + 3)),
        pl.BlockSpec((S, rows.shape[1], L), lambda b: (b, 0, 0)),
        pl.BlockSpec((SL, LANES), lambda b: (b, 0)),
        _layer_spec((3, 2 * ML_W), l), _layer_spec((1, ML_W), l),
    ]
    args = [proj, proj, proj, proj, rows, cols, conv_w, norm_g]
    y_shape = jax.ShapeDtypeStruct((B * L, ML_W), F32)
    y_spec = pl.BlockSpec((SL, ML_W), lambda b: (b, 0))
    c_blk, n_blk, m_blk = (2, ML_H, ML_DH, ML_DH), (2 * ML_H, ML_DH), (2 * ML_H, LANES)
    if with_state:
        c0, n0, m0 = state
        in_specs += [
            pl.BlockSpec((S, None) + c_blk, lambda b: (b, l, 0, 0, 0, 0)),
            pl.BlockSpec((S, None) + n_blk, lambda b: (b, l, 0, 0)),
            pl.BlockSpec((S, None, N_GATE, LANES), lambda b: (b, l, 0, 0)),
        ]
        args += [c0, n0, m0]
        out_shape, out_specs = y_shape, y_spec
    else:
        lead = (l + 1,) if n_prev else ()
        if n_prev:
            in_specs += [pl.BlockSpec((S, n_prev) + blk, lambda b, nz=len(blk) + 1: (b,) + (0,) * nz)
                         for blk in (c_blk, n_blk, m_blk)]
            args += list(prev)
        out_shape = [y_shape] + [jax.ShapeDtypeStruct((B,) + lead + blk, F32) for blk in (c_blk, n_blk, m_blk)]
        out_specs = [y_spec] + [pl.BlockSpec((S,) + lead + blk, lambda b, nz=len(lead) + len(blk): (b,) + (0,) * nz)
                                for blk in (c_blk, n_blk, m_blk)]
    return pl.pallas_call(
        functools.partial(_mlstm_kernel, L=L, with_state=with_state, n_prev=n_prev),
        out_shape=out_shape,
        grid=(B // S,),
        in_specs=in_specs,
        out_specs=out_specs,
        compiler_params=pltpu.CompilerParams(
            dimension_semantics=("arbitrary",), vmem_limit_bytes=VMEM_LIMIT),
        name="mlstm",
    )(*args)


def _outmlp_kernel(x_ref, ya_ref, yb_ref, mod_ref, wo_ref, g2_ref, w1_ref, w2_ref, gf_ref, o_ref, *, final):
    mod = mod_ref[0]
    g1 = mod[:, 2 * D_MODEL:3 * D_MODEL]
    sh2 = mod[:, 3 * D_MODEL:4 * D_MODEL]
    sc2 = mod[:, 4 * D_MODEL:5 * D_MODEL]
    g2 = mod[:, 5 * D_MODEL:6 * D_MODEL]
    na = SC_W + HY_W
    attn = _dot(ya_ref[...].astype(BF16), wo_ref[0:na, :]) + _dot(yb_ref[...].astype(BF16), wo_ref[na:, :])
    x = x_ref[...] + g1 * attn
    u = ((_rms(x) * g2_ref[...]) * (1.0 + sc2) + sh2).astype(BF16)
    tf = 1024
    acc = jnp.zeros(x.shape, F32)
    for j in range(D_FF // tf):
        hcol = _dot(u, w1_ref[:, j * tf:(j + 1) * tf])
        hcol = jnp.square(jnp.maximum(hcol, 0.0)).astype(BF16)
        acc = acc + _dot(hcol, w2_ref[j * tf:(j + 1) * tf, :])
    x = x + g2 * acc
    if final:
        x = _rms(x) * gf_ref[...]
    o_ref[...] = x


def _outmlp(x2d, ya, yb, mods, w_out, g2, w1, w2, gf, *, l, L, per_request, final):
    T = x2d.shape[0]
    tm = ROW_BLOCK
    return pl.pallas_call(
        functools.partial(_outmlp_kernel, final=final),
        out_shape=jax.ShapeDtypeStruct((T, D_MODEL), F32),
        grid=(T // tm,),
        in_specs=[
            pl.BlockSpec((tm, D_MODEL), lambda i: (i, 0)),
            pl.BlockSpec((tm, SC_W + HY_W), lambda i: (i, 0)),
            pl.BlockSpec((tm, ML_W), lambda i: (i, 0)),
            _mod_spec(l, L, tm, per_request),
            _const_spec((D_MODEL, D_MODEL)), _layer_spec((1, D_MODEL), l),
            _const_spec((D_MODEL, D_FF)), _const_spec((D_FF, D_MODEL)), _const_spec((1, D_MODEL)),
        ],
        out_specs=pl.BlockSpec((tm, D_MODEL), lambda i: (i, 0)),
        compiler_params=pltpu.CompilerParams(
            dimension_semantics=("arbitrary",), vmem_limit_bytes=VMEM_LIMIT),
        name="outproj_mlp",
    )(x2d, ya, yb, mods, w_out, g2, w1, w2, gf)


def kernel(x_prompt, x_sample, state_C, state_n, state_m, c, c_ctx, norm1_g, ada_w, ada_b, w_in, sc_conv_w, hy_conv_w, hy_w1, hy_b1, hy_w2, hy_b2, hy_w3, hy_freq, hy_skip, ml_conv_w, ml_gate_b, ml_norm_g, w_out, norm2_g, mlp_w1, mlp_w2, norm_f_g):
    B, L = x_prompt.shape[0], x_prompt.shape[1]
    Bd, Ld = x_sample.shape[0], x_sample.shape[1]

    cond = jnp.concatenate([c_ctx[None, :], c, jnp.zeros((8 - 1 - Bd, D_MODEL), F32)], axis=0)
    mods = _ada(cond, ada_w, ada_b).reshape(DEPTH * 8, 1, 6 * D_MODEL)

    w_in_t = jnp.swapaxes(w_in, 1, 2).astype(BF16)
    w_gate = w_in_t[:, N_MAIN:, :]
    gate_b = jnp.broadcast_to(ml_gate_b[:, :, None], (DEPTH, N_GATE, LANES))
    pad_h = LANES - HY_HID
    w1p = jnp.pad(hy_w1, ((0, 0), (0, LANES - HY_EMB), (0, pad_h)))
    b1p = jnp.pad(hy_b1, ((0, 0), (0, pad_h)))
    w2p = jnp.pad(hy_w2, ((0, 0), (0, pad_h), (0, pad_h)))
    b2p = jnp.pad(hy_b2, ((0, 0), (0, pad_h)))
    w3p = jnp.pad(hy_w3, ((0, 0), (0, pad_h), (0, 0)))
    fp = jnp.pad(hy_freq, ((0, 0), (0, pad_h)))
    pos = jnp.asarray(_pos_embed_np(Ld))
    m0_all = jnp.pad(state_m, ((0, 0), (0, 0), (0, 0), (0, ML_H))).reshape(Bd, DEPTH, N_GATE, 1)
    m0_all = jnp.broadcast_to(m0_all, (Bd, DEPTH, N_GATE, LANES))
    n0_all = state_n.reshape(Bd, DEPTH, 2 * ML_H, ML_DH)
    gf = norm_f_g[None, :]
    g1 = norm1_g[:, None, :]
    g2 = norm2_g[:, None, :]
    ml_ng = ml_norm_g[:, None, :]
    filt_args = (w1p, b1p[:, None, :], w2p, b2p[:, None, :], w3p, fp[:, None, :])
    hre_c, him_c = _filter_spectrum(L, *filt_args)
    hre_s, him_s = _filter_spectrum(Ld, *filt_args)
    state = (state_C, n0_all, m0_all)

    xp = x_prompt.reshape(B * L, D_MODEL)
    xs = x_sample.reshape(Bd * Ld, D_MODEL)
    prev = None
    for l in range(DEPTH):
        final = l == DEPTH - 1

        proj, g_t, w_out_b, w1_b, w2_b = _inproj(xp, None, mods, g1, w_in_t, w_gate, (w_out, mlp_w1, mlp_w2),
                                                 l=l, L=L, per_request=False)
        ya = _schy(proj, sc_conv_w, hy_conv_w, hy_skip, hre_c, him_c, l=l, B=B, L=L)
        yb, *states = _mlstm(proj, g_t, gate_b, ml_conv_w, ml_ng, None, prev, l=l, B=B, L=L)
        xp = _outmlp(xp, ya, yb, mods, w_out_b, g2, w1_b, w2_b, gf, l=l, L=L, per_request=False, final=final)
        if not final:
            prev = [s.reshape((B, 1) + s.shape[1:]) for s in states] if l == 0 else states

        res = _inproj(xs, pos if l == 0 else None, mods, g1, w_in_t, w_gate, l=l, L=Ld, per_request=True)
        if l == 0:
            proj, g_t, xs = res
        else:
            proj, g_t = res
        ya = _schy(proj, sc_conv_w, hy_conv_w, hy_skip, hre_s, him_s, l=l, B=Bd, L=Ld)
        yb = _mlstm(proj, g_t, gate_b, ml_conv_w, ml_ng, state, None, l=l, B=Bd, L=Ld)
        xs = _outmlp(xs, ya, yb, mods, w_out_b, g2, w1_b, w2_b, gf, l=l, L=Ld, per_request=True, final=final)

    y_prompt = xp.reshape(B, L, D_MODEL)
    y_sample = xs.reshape(Bd, Ld, D_MODEL)
    new_c, new_n, new_m = states
    return (y_prompt, y_sample, new_c, new_n.reshape(B, DEPTH, 2, ML_H, ML_DH),
            new_m[..., 0].reshape(B, DEPTH, 2, ML_H))
```

```python
import functools
import math

import numpy as np
import jax
import jax.numpy as jnp
from jax import lax
from jax.experimental import pallas as pl
from jax.experimental.pallas import tpu as pltpu

F32 = jnp.float32
BF16 = jnp.bfloat16

D_MODEL = 1024
DEPTH = 2
GRID_W = 64
SC_W = 256
HY_W = 256
ML_W = 512
ML_H = 4
ML_DH = ML_W // ML_H
D_FF = 4 * D_MODEL
HY_ORDER = 2
HY_BANDS = 16
HY_EMB = 1 + 2 * HY_BANDS
HY_HID = 64
HY_DECAY_SHORT = 0.3
HY_DECAY_LONG = 1.5
HY_DECAY_TARGET = 1e-2
EPS = 1e-6
N_MAIN = 3 * SC_W + 3 * HY_W + 4 * ML_W
N_GATE = 4 * ML_H
LANES = 128
ROW_BLOCK = 512
ATT_BLOCK = 256
HY_BLOCK = 256
HY_STEP_ROWS = 1024
ML_STEP_ROWS = 512
PROJ_DTYPE = BF16
NEG_BIG = -1e30
LOG2E = math.log2(math.e)
VMEM_LIMIT = 60 * 1024 * 1024


def _dot(a, b):
    return jnp.dot(a, b, preferred_element_type=F32)


def _dot_nt(a, b):
    return lax.dot_general(a, b, (((1,), (1,)), ((), ())), preferred_element_type=F32)


def _split2(a):
    hi = a.astype(BF16)
    lo = (a - hi.astype(F32)).astype(BF16)
    return hi, lo


def _split3(a):
    hi = a.astype(BF16)
    r = a - hi.astype(F32)
    mid = r.astype(BF16)
    lo = (r - mid.astype(F32)).astype(BF16)
    return hi, mid, lo


def _dot3(a, b):
    ah, al = _split2(a)
    bh, bl = _split2(b)
    return _dot(ah, bh) + _dot(al, bh) + _dot(ah, bl)


def _dot3_nt(a, b):
    ah, al = _split2(a)
    bh, bl = _split2(b)
    return _dot_nt(ah, bh) + _dot_nt(al, bh) + _dot_nt(ah, bl)


def _dot3_tab(th, tl, b):
    bh, bl = _split2(b)
    return _dot(th, bh) + _dot(tl, bh) + _dot(th, bl)


def _sigmoid(x):
    return 1.0 / (1.0 + jnp.exp(-x))


def _silu(x):
    return x * _sigmoid(x)


def _log_sigmoid(x):
    return jnp.minimum(x, 0.0) - jnp.log(1.0 + jnp.exp(-jnp.abs(x)))


def _conv3(x, w, seq_len=None):
    R = x.shape[0]
    seq_len = R if seq_len is None else seq_len
    pos = lax.broadcasted_iota(jnp.int32, x.shape, 0) & (seq_len - 1)
    xm = jnp.where(pos == 0, 0.0, pltpu.roll(x, 1, axis=0))
    xp = jnp.where(pos == seq_len - 1, 0.0, pltpu.roll(x, R - 1, axis=0))
    return xm * w[0:1, :] + x * w[1:2, :] + xp * w[2:3, :]


def _rms(x):
    return x * lax.rsqrt(jnp.mean(x * x, axis=-1, keepdims=True) + EPS)


def _scan_max(x, *, reverse):
    n = x.shape[1]
    lane = lax.broadcasted_iota(jnp.int32, x.shape, 1)
    d = 1
    while d < n:
        if reverse:
            shifted = jnp.where(lane < n - d, pltpu.roll(x, n - d, axis=1), NEG_BIG)
        else:
            shifted = jnp.where(lane >= d, pltpu.roll(x, d, axis=1), NEG_BIG)
        x = jnp.maximum(x, shifted)
        d *= 2
    return x


def _hi_lo(a64):
    a = a64.astype(np.float32)
    hi = a.astype(BF16)
    lo = (a - hi.astype(np.float32)).astype(BF16)
    return jnp.asarray(hi), jnp.asarray(lo)


def _const_spec(shape):
    return pl.BlockSpec(shape, lambda *_: tuple(0 for _ in shape), pipeline_mode=pl.Buffered(1))


def _layer_spec(shape, l):
    return pl.BlockSpec((None,) + tuple(shape), lambda *_: (l,) + tuple(0 for _ in shape),
                        pipeline_mode=pl.Buffered(1))


@functools.lru_cache(maxsize=None)
def _dft_tables_np(L):
    k = np.arange(L, dtype=np.int64)
    m1 = ((2 * k[:, None] + 1) * k[None, :]) % (4 * L)
    a1 = np.pi * m1.astype(np.float64) / (2 * L)
    m2 = ((2 * k[:, None] + 1) * (2 * k[None, :] + 1)) % (8 * L)
    a2 = np.pi * m2.astype(np.float64) / (4 * L)
    return np.cos(a1), np.sin(a1), np.cos(a2), np.sin(a2)


@functools.lru_cache(maxsize=None)
def _filter_consts_np(L):
    t_idx = np.arange(L, dtype=np.float64)
    t = t_idx / (L - 1)
    bands = np.arange(1, HY_BANDS + 1, dtype=np.float64)
    ang = 2.0 * math.pi * t_idx[:, None] * bands[None, :] / L
    z = np.zeros((L, LANES), np.float64)
    z[:, 0] = t
    z[:, 1:1 + HY_BANDS] = np.cos(ang)
    z[:, 1 + HY_BANDS:1 + 2 * HY_BANDS] = -np.sin(ang)
    lin = np.linspace(math.log(HY_DECAY_TARGET) / HY_DECAY_LONG,
                      math.log(HY_DECAY_TARGET) / HY_DECAY_SHORT, HY_W).astype(np.float32)
    deltas = np.abs(lin).astype(np.float64)
    decay = np.exp(-t[:, None] * deltas[None, :])
    return z.astype(np.float32), decay.astype(np.float32)


@functools.lru_cache(maxsize=None)
def _pos_embed_np(L):
    rows = L // GRID_W
    r, cidx = np.meshgrid(np.arange(rows, dtype=np.float64), np.arange(GRID_W, dtype=np.float64), indexing="ij")
    r = r.reshape(-1)
    cidx = cidx.reshape(-1)
    quarter = D_MODEL // 4
    omega = 1.0 / (10000.0 ** (np.arange(quarter, dtype=np.float64) / quarter))
    ar = r[:, None] * omega[None, :]
    ac = cidx[:, None] * omega[None, :]
    return np.concatenate([np.sin(ar), np.cos(ar), np.sin(ac), np.cos(ac)], axis=-1).astype(np.float32)


def _ada_kernel(c_ref, w_ref, b_ref, o_ref):
    sc = _silu(c_ref[...])
    o_ref[0] = _dot3(sc, w_ref[0]) + b_ref[0]


def _ada(cond, ada_w, ada_b):
    tn = 1536
    nd = 6 * D_MODEL
    return pl.pallas_call(
        _ada_kernel,
        out_shape=jax.ShapeDtypeStruct((DEPTH, 8, nd), F32),
        grid=(DEPTH, nd // tn),
        in_specs=[
            pl.BlockSpec((8, D_MODEL), lambda l, j: (0, 0)),
            pl.BlockSpec((1, D_MODEL, tn), lambda l, j: (l, 0, j)),
            pl.BlockSpec((1, 1, tn), lambda l, j: (l, 0, j)),
        ],
        out_specs=pl.BlockSpec((1, 8, tn), lambda l, j: (l, 0, j)),
        compiler_params=pltpu.CompilerParams(
            dimension_semantics=("arbitrary", "arbitrary"), vmem_limit_bytes=VMEM_LIMIT),
        name="ada_mod",
    )(cond, ada_w, ada_b.reshape(DEPTH, 1, nd))


def _inproj_body(x, mod_ref, g_ref, w_ref, wg_ref, proj_ref, gate_ref):
    mod = mod_ref[0]
    sh1 = mod[:, 0:D_MODEL]
    sc1 = mod[:, D_MODEL:2 * D_MODEL]
    u = (_rms(x) * g_ref[...]) * (1.0 + sc1) + sh1
    ub = u.astype(BF16)
    tn = 512
    for j in range(N_MAIN // tn):
        proj_ref[:, j * tn:(j + 1) * tn] = _dot_nt(ub, w_ref[j * tn:(j + 1) * tn, :]).astype(PROJ_DTYPE)
    gate_ref[...] = _dot_nt(wg_ref[...], ub)


def _inproj_kernel(*refs, with_pos, n_cast, n_ctx):
    it = iter(refs)
    xc_ref, xl_ref = next(it), next(it)
    pos_ref = next(it) if with_pos else None
    mod_ref, g_ref, w_ref, wg_ref = (next(it) for _ in range(4))
    cast_in = [next(it) for _ in range(n_cast)]
    projc_ref, gatec_ref, projl_ref, gatel_ref = (next(it) for _ in range(4))
    xo_ref = next(it) if with_pos else None
    cast_out = [next(it) for _ in range(n_cast)]
    i = pl.program_id(0)

    @pl.when(i < n_ctx)
    def _():
        for src, dst in zip(cast_in, cast_out):
            dst[...] = src[...].astype(BF16)
        _inproj_body(xc_ref[...], mod_ref, g_ref, w_ref, wg_ref, projc_ref, gatec_ref)

    @pl.when(i >= n_ctx)
    def _():
        x = xl_ref[...]
        if with_pos:
            x = x + pos_ref[...]
            xo_ref[...] = x
        _inproj_body(x, mod_ref, g_ref, w_ref, wg_ref, projl_ref, gatel_ref)


def _two_pass_maps(n_ctx):
    ctx = lambda i: jnp.minimum(i, n_ctx - 1)
    lat = lambda i: jnp.maximum(i - n_ctx, 0)
    return ctx, lat


def _mod_spec(l, Ld, tm, n_ctx):
    return pl.BlockSpec((1, 1, 6 * D_MODEL),
                        lambda i: (l * 8 + jnp.where(i < n_ctx, 0, 1 + ((i - n_ctx) * tm) // Ld), 0, 0))


def _inproj(xc, xl, pos, mods, g, w_in_t, w_gate, cast=(), *, l, Ld):
    tm = ROW_BLOCK
    Tc, Tl = xc.shape[0], xl.shape[0]
    n_ctx, n_lat = Tc // tm, Tl // tm
    per_seq = Ld // tm
    ctx, lat = _two_pass_maps(n_ctx)
    with_pos = pos is not None
    in_specs = [pl.BlockSpec((tm, D_MODEL), lambda i: (ctx(i), 0)), pl.BlockSpec((tm, D_MODEL), lambda i: (lat(i), 0))]
    args = [xc, xl]
    if with_pos:
        in_specs.append(pl.BlockSpec((tm, D_MODEL), lambda i: (lat(i) % per_seq, 0)))
        args.append(pos)
    in_specs += [
        _mod_spec(l, Ld, tm, n_ctx),
        _layer_spec((1, D_MODEL), l),
        _layer_spec((N_MAIN, D_MODEL), l),
        _layer_spec((N_GATE, D_MODEL), l),
    ]
    args += [mods, g, w_in_t, w_gate]
    out_shape, out_specs = [], []
    for T, idx in ((Tc, ctx), (Tl, lat)):
        out_shape += [jax.ShapeDtypeStruct((T, N_MAIN), PROJ_DTYPE), jax.ShapeDtypeStruct((N_GATE, T), F32)]
        out_specs += [pl.BlockSpec((tm, N_MAIN), lambda i, idx=idx: (idx(i), 0)),
                      pl.BlockSpec((N_GATE, tm), lambda i, idx=idx: (0, idx(i)))]
    if with_pos:
        out_shape.append(jax.ShapeDtypeStruct((Tl, D_MODEL), F32))
        out_specs.append(pl.BlockSpec((tm, D_MODEL), lambda i: (lat(i), 0)))
    for w in cast:
        _, rows, cols = w.shape
        in_specs.append(pl.BlockSpec((None, rows // n_ctx, cols), lambda i: (l, ctx(i), 0)))
        args.append(w)
        out_shape.append(jax.ShapeDtypeStruct((rows, cols), BF16))
        out_specs.append(pl.BlockSpec((rows // n_ctx, cols), lambda i: (ctx(i), 0)))
    return pl.pallas_call(
        functools.partial(_inproj_kernel, with_pos=with_pos, n_cast=len(cast), n_ctx=n_ctx),
        out_shape=out_shape,
        grid=(n_ctx + n_lat,),
        in_specs=in_specs,
        out_specs=out_specs,
        compiler_params=pltpu.CompilerParams(
            dimension_semantics=("arbitrary",), vmem_limit_bytes=VMEM_LIMIT),
        name="inproj",
    )(*args)


def _filter_kernel(z_ref, dec_ref, w1_ref, b1_ref, w2_ref, b2_ref, w3_ref, f_ref,
                   ch_ref, cl_ref, sh_ref, sl_ref, hre_ref, him_ref, *, L):
    T = HY_BLOCK
    P = L // T
    f = f_ref[...]
    hdn = jnp.sin(f * (_dot3(z_ref[...], w1_ref[...]) + b1_ref[...]))
    hdn = jnp.sin(f * (_dot3(hdn, w2_ref[...]) + b2_ref[...]))
    h = _dot3(hdn, w3_ref[...])
    dec = dec_ref[...]
    row = lax.broadcasted_iota(jnp.int32, (L, HY_W), 0)
    halves = []
    for o in range(HY_ORDER):
        hf = h[:, (2 * o) * HY_W:(2 * o + 1) * HY_W] * dec
        hb = jnp.where(row == 0, 0.0, h[:, (2 * o + 1) * HY_W:(2 * o + 2) * HY_W] * dec)
        nrm = jnp.sum(jnp.abs(hf), axis=0, keepdims=True) + jnp.sum(jnp.abs(hb), axis=0, keepdims=True)
        inv = 1.0 / nrm
        halves.append((hf * inv, hb * inv))
    blocks = [half[e * T:(e + 1) * T] for pair in halves for half in pair for e in range(P)]
    x = jnp.concatenate(blocks, axis=1)
    xre = _dot3_tab(ch_ref[...], cl_ref[...], x)
    xim = -_dot3_tab(sh_ref[...], sl_ref[...], x)
    sgn = jnp.where((lax.broadcasted_iota(jnp.int32, (T, HY_W), 0) & 1) == 0, 1.0, -1.0)
    for o in range(HY_ORDER):
        def half_spec(which, e):
            j = (o * 2 + which) * P + e
            cs = slice(j * HY_W, (j + 1) * HY_W)
            return xre[:, cs], xim[:, cs], halves[o][which][e * T:e * T + 1]
        for d in range(-(P - 1), P):
            if d == 0:
                (fre, fim, _), (gre, gim, _) = half_spec(0, 0), half_spec(1, 0)
                re, im = fre + gre, fim - gim
            elif d > 0:
                (fre, fim, _), (pre, pim, p0) = half_spec(0, d), half_spec(0, d - 1)
                re, im = fre - sgn * pim, fim + sgn * (pre - p0)
            else:
                (gre, gim, _), (pre, pim, p0) = half_spec(1, -d), half_spec(1, -d - 1)
                re, im = gre - sgn * pim, -gim - sgn * (pre - p0)
            hre_ref[d + P - 1, :, o * HY_W:(o + 1) * HY_W] = re
            him_ref[d + P - 1, :, o * HY_W:(o + 1) * HY_W] = im


def _filter_spectrum(L, w1p, b1p, w2p, b2p, w3p, fp):
    T = HY_BLOCK
    z_np, dec_np = _filter_consts_np(L)
    c1, s1, _, _ = _dft_tables_np(T)
    ch, cl = _hi_lo(c1)
    sh, sl = _hi_lo(s1)
    n = HY_ORDER * HY_W
    nd = 2 * (L // T) - 1
    const = _const_spec
    per_layer = lambda shape: pl.BlockSpec((None,) + shape, lambda l: (l,) + tuple(0 for _ in shape))
    return pl.pallas_call(
        functools.partial(_filter_kernel, L=L),
        out_shape=[jax.ShapeDtypeStruct((DEPTH, nd, T, n), F32), jax.ShapeDtypeStruct((DEPTH, nd, T, n), F32)],
        grid=(DEPTH,),
        in_specs=[const((L, LANES)), const((L, HY_W)), per_layer((LANES, LANES)), per_layer((1, LANES)),
                  per_layer((LANES, LANES)), per_layer((1, LANES)), per_layer((LANES, 2 * n)), per_layer((1, LANES)),
                  const((T, T)), const((T, T)), const((T, T)), const((T, T))],
        out_specs=[per_layer((nd, T, n)), per_layer((nd, T, n))],
        compiler_params=pltpu.CompilerParams(
            dimension_semantics=("arbitrary",), vmem_limit_bytes=VMEM_LIMIT),
        name="hyena_filter",
    )(jnp.asarray(z_np), jnp.asarray(dec_np), w1p, b1p, w2p, b2p, w3p, fp, ch, cl, sh, sl)


def _schy_kernel(sc_ref, hy_ref, scw_ref, hyw_ref, skip_ref, fwd_ref, inv_ref, hre_ref, him_ref, out_ref, *, L):
    T = HY_BLOCK
    P = L // T
    NB = sc_ref.shape[0] // T
    sc = sc_ref[...].astype(F32)
    b_g = sc[:, 0:SC_W]
    c_g = sc[:, SC_W:2 * SC_W]
    xin = sc[:, 2 * SC_W:3 * SC_W]
    out_ref[:, 0:SC_W] = b_g * _conv3(c_g * xin, scw_ref[...], L)

    u3 = _conv3(hy_ref[...].astype(F32), hyw_ref[...], L)
    z = u3[:, 0:HY_W]
    gates = (u3[:, HY_W:2 * HY_W], u3[:, 2 * HY_W:3 * HY_W])
    fwd, inv = fwd_ref[...], inv_ref[...]
    lanes = lambda j: slice(j * HY_W, (j + 1) * HY_W)
    for o in range(HY_ORDER):
        zcat = jnp.concatenate([z[j * T:(j + 1) * T] for j in range(NB)], axis=1)
        spec = _dot(fwd, zcat.astype(BF16))
        yre, yim = [], []
        for bi in range(NB):
            s0, i = bi - bi % P, bi % P
            re = im = None
            for j in range(P):
                d = i - j + P - 1
                hre = hre_ref[d, :, lanes(o)]
                him = him_ref[d, :, lanes(o)]
                zc = spec[0:T, lanes(s0 + j)]
                zs = spec[T:2 * T, lanes(s0 + j)]
                pre = zc * hre + zs * him
                pim = zc * him - zs * hre
                re = pre if re is None else re + pre
                im = pim if im is None else im + pim
            yre.append(re)
            yim.append(im)
        prod = jnp.concatenate([jnp.concatenate(yre, axis=1), jnp.concatenate(yim, axis=1)], axis=0)
        ycat = _dot(inv, prod.astype(BF16)) * (1.0 / T)
        y = jnp.concatenate([ycat[:, lanes(bi)] for bi in range(NB)], axis=0)
        z = gates[o] * (y + skip_ref[o:o + 1, :] * z)
    out_ref[:, SC_W:SC_W + HY_W] = z


def _schy(proj, sc_w, hy_w, skip, hre, him, *, l, B, L):
    T = HY_BLOCK
    _, _, c2, s2 = _dft_tables_np(T)
    fwd = jnp.asarray(np.concatenate([c2, s2], axis=0).astype(np.float32).astype(BF16))
    inv = jnp.asarray(np.concatenate([c2, -s2], axis=1).astype(np.float32).astype(BF16))
    n = HY_ORDER * HY_W
    nd = 2 * (L // T) - 1
    rows = max(L, HY_STEP_ROWS)
    return pl.pallas_call(
        functools.partial(_schy_kernel, L=L),
        out_shape=jax.ShapeDtypeStruct((B * L, SC_W + HY_W), F32),
        grid=(B * L // rows,),
        in_specs=[
            pl.BlockSpec((rows, 3 * SC_W), lambda b: (b, 0)),
            pl.BlockSpec((rows, 3 * HY_W), lambda b: (b, 1)),
            _layer_spec((3, SC_W), l), _layer_spec((3, 3 * HY_W), l), _layer_spec((HY_ORDER, HY_W), l),
            _const_spec((2 * T, T)), _const_spec((T, 2 * T)),
            _layer_spec((nd, T, n), l), _layer_spec((nd, T, n), l),
        ],
        out_specs=pl.BlockSpec((rows, SC_W + HY_W), lambda b: (b, 0)),
        compiler_params=pltpu.CompilerParams(
            dimension_semantics=("arbitrary",), vmem_limit_bytes=VMEM_LIMIT),
        name="sconv_hyena",
    )(proj, proj, sc_w, hy_w, skip, fwd, inv, hre, him)


def _gate_prep_kernel(*refs, B, L, with_state):
    if with_state:
        g_ref, gb_ref, m0_ref, rows_ref, cols_ref = refs
    else:
        g_ref, gb_ref, rows_ref, cols_ref = refs
    R = B * N_GATE
    gb = gb_ref[:, 0:1]
    x = jnp.concatenate([g_ref[:, b * L:(b + 1) * L] + gb for b in range(B)], axis=0)
    rr = lax.broadcasted_iota(jnp.int32, (R, L), 0) & (N_GATE - 1)
    is_f = ((rr >= ML_H) & (rr < 2 * ML_H)) | (rr >= 3 * ML_H)
    fwd_rows = rr < 2 * ML_H
    pk = jnp.where(is_f, _log_sigmoid(x), x)
    r_i = lax.broadcasted_iota(jnp.int32, (L, L), 0)
    c_i = lax.broadcasted_iota(jnp.int32, (L, L), 1)
    triu = jnp.where(r_i <= c_i, 1.0, 0.0).astype(BF16)
    cs = _dot(jnp.concatenate(_split3(pk), axis=0), triu)
    cum = cs[0:R] + cs[R:2 * R] + cs[2 * R:3 * R]
    suf = cum[:, L - 1:L] - cum + pk
    bsum = pltpu.roll(jnp.where(fwd_rows, cum, suf), R - ML_H, axis=0)
    r = pk - bsum
    cm = jnp.where(fwd_rows, _scan_max(r, reverse=False), _scan_max(r, reverse=True))
    if with_state:
        m0 = jnp.concatenate([m0_ref[b][:, 0:1] for b in range(B)], axis=0)
        m = jnp.maximum(m0, cm)
    else:
        m = jnp.maximum(cm, 0.0)
    rl = r * LOG2E
    ml = m * LOG2E
    mt = bsum + m
    pieces = [rl, ml, jnp.exp(-mt), mt]
    if not with_state:
        pieces.append(jnp.exp2(rl - jnp.where(fwd_rows, ml[:, L - 1:L], ml[:, 0:1])))
    pad = jnp.zeros((LANES - N_GATE, L), F32)
    for b in range(B):
        bs = slice(b * N_GATE, (b + 1) * N_GATE)
        for j, piece in enumerate(pieces):
            rows_ref[b, j * N_GATE:(j + 1) * N_GATE, :] = piece[bs]
        cols_ref[b * L:(b + 1) * L, :] = jnp.concatenate([rl[bs], pad], axis=0).T


def _gate_prep(g_t, gate_b, m0, *, l, B, L):
    with_state = m0 is not None
    n_rows = (4 if with_state else 5) * N_GATE
    full = lambda shape: pl.BlockSpec(shape, lambda i: tuple(0 for _ in shape))
    args = [g_t, gate_b]
    in_specs = [full((N_GATE, B * L)), pl.BlockSpec((None, N_GATE, LANES), lambda i: (l, 0, 0))]
    if with_state:
        args.append(m0)
        in_specs.append(pl.BlockSpec((B, None, N_GATE, LANES), lambda i: (0, l, 0, 0)))
    return pl.pallas_call(
        functools.partial(_gate_prep_kernel, B=B, L=L, with_state=with_state),
        out_shape=[jax.ShapeDtypeStruct((B, n_rows, L), F32), jax.ShapeDtypeStruct((B * L, LANES), F32)],
        grid=(1,),
        in_specs=in_specs,
        out_specs=[full((B, n_rows, L)), full((B * L, LANES))],
        compiler_params=pltpu.CompilerParams(
            dimension_semantics=("arbitrary",), vmem_limit_bytes=VMEM_LIMIT),
        name="gate_prep",
    )(*args)


def _mlstm_kernel(*refs, L, with_state, n_prev):
    if with_state:
        (q_ref, k_ref, v_ref, o_ref, rows_ref, cols_ref, cw_ref, ng_ref,
         c0_ref, n0_ref, m0_ref, y_ref) = refs
    elif n_prev:
        (q_ref, k_ref, v_ref, o_ref, rows_ref, cols_ref, cw_ref, ng_ref,
         cprev_ref, nprev_ref, mprev_ref, y_ref, cout_ref, nout_ref, mout_ref) = refs
        cout_ref[:, 0:n_prev] = cprev_ref[...]
        nout_ref[:, 0:n_prev] = nprev_ref[...]
        mout_ref[:, 0:n_prev] = mprev_ref[...]
        cout_ref, nout_ref, mout_ref = (r.at[:, n_prev] for r in (cout_ref, nout_ref, mout_ref))
    else:
        (q_ref, k_ref, v_ref, o_ref, rows_ref, cols_ref, cw_ref, ng_ref,
         y_ref, cout_ref, nout_ref, mout_ref) = refs

    S = rows_ref.shape[0]
    tb = min(L // 2, ATT_BLOCK)
    cw = cw_ref[...]
    q_all = _silu(_conv3(q_ref[...].astype(F32), cw[:, 0:ML_W], L))
    k_all = _silu(_conv3(k_ref[...].astype(F32), cw[:, ML_W:2 * ML_W], L)) * (ML_DH ** -0.5)

    s_loc = lax.broadcasted_iota(jnp.int32, (tb, tb), 0)
    t_loc = lax.broadcasted_iota(jnp.int32, (tb, tb), 1)
    masks = (s_loc <= t_loc, s_loc >= t_loc)
    ones_rows = jnp.ones((2 * 8, L), BF16)
    ng = ng_ref[...]
    for h in range(ML_H):
        hs = slice(h * ML_DH, (h + 1) * ML_DH)
        seqs = []
        for s in range(S):
            sl = slice(s * L, (s + 1) * L)
            rl16, ml16, em16, mt16 = (rows_ref[s, j * N_GATE:(j + 1) * N_GATE, :] for j in range(4))
            q = q_all[sl, hs]
            k = k_all[sl, hs]
            v = v_ref[sl, hs].astype(F32)
            vext_t = jnp.concatenate([v.T.astype(BF16), ones_rows], axis=0)
            rlb = [jnp.broadcast_to(cols_ref[sl, 2 * ML_H * d + h:2 * ML_H * d + h + 1], (L, tb)) for d in range(2)]
            seqs.append(dict(q=q, k=k, qb=q.astype(BF16), kb=k.astype(BF16), vb=v.astype(BF16), vext_t=vext_t,
                             rlb=rlb, ml16=ml16, em16=em16, mt16=mt16))
            if with_state:
                seqs[s]["c0_t"] = [c0_ref[s, d, h].T.astype(BF16) for d in range(2)]
        for r0 in range(0, L, tb):
            rs = slice(r0, r0 + tb)
            for s, sq in enumerate(seqs):
                s_t = _dot_nt(sq["kb"], sq["qb"][rs])
                if with_state:
                    qn_rows = _dot3_nt(n0_ref[s], sq["q"][rs])
                hout_t = None
                for d in range(2):
                    o8 = 2 * ML_H * d + h
                    ml_row = sq["ml16"][o8:o8 + 1, rs]
                    e_diag = jnp.exp2(jnp.where(masks[d], sq["rlb"][d][rs] - ml_row, NEG_BIG))
                    acc = _dot(sq["vext_t"][:, rs], (s_t[rs] * e_diag).astype(BF16))
                    side = slice(r0 + tb, L) if d else slice(0, r0)
                    if side.stop > side.start:
                        e_side = jnp.exp2(sq["rlb"][d][side] - ml_row)
                        acc = acc + _dot(sq["vext_t"][:, side], (s_t[side] * e_side).astype(BF16))
                    num_t = acc[0:ML_DH]
                    den = acc[ML_DH:ML_DH + 1]
                    if with_state:
                        wp = jnp.exp2(m0_ref[s, o8:o8 + 1, 0:1] * LOG2E - ml_row)
                        den = den + wp * qn_rows[d * ML_H + h:d * ML_H + h + 1]
                        num_t = num_t + wp * _dot_nt(sq["c0_t"][d], sq["qb"][rs])
                    contrib = num_t * (1.0 / jnp.maximum(jnp.abs(den), sq["em16"][o8:o8 + 1, rs]))
                    hout_t = contrib if hout_t is None else hout_t + contrib
                hn_t = hout_t * lax.rsqrt(jnp.mean(hout_t * hout_t, axis=0, keepdims=True) + EPS)
                rows_out = slice(s * L + r0, s * L + r0 + tb)
                y_ref[rows_out, hs] = _sigmoid(o_ref[rows_out, hs].astype(F32)) * (hn_t.T * ng[:, hs])

        if not with_state:
            for s, sq in enumerate(seqs):
                w16 = rows_ref[s, 4 * N_GATE:5 * N_GATE, :]
                k_t = sq["k"].T
                n_rows = _dot3(w16, sq["k"])
                for d in range(2):
                    o8 = 2 * ML_H * d + h
                    end = 0 if d else L - 1
                    cout_ref[s, d, h] = _dot((k_t * w16[o8:o8 + 1, :]).astype(BF16), sq["vb"])
                    nout_ref[s, d * ML_H + h:d * ML_H + h + 1, :] = n_rows[o8:o8 + 1, :]
                    mout_ref[s, d * ML_H + h:d * ML_H + h + 1, :] = jnp.broadcast_to(
                        sq["mt16"][o8:o8 + 1, end:end + 1], (1, LANES))


def _mlstm(proj, g_t, gate_b, conv_w, norm_g, state, prev, *, l, B, L):
    with_state = state is not None
    n_prev = 0 if prev is None else l
    rows, cols = _gate_prep(g_t, gate_b, state[2] if with_state else None, l=l, B=B, L=L)
    S = max(1, ML_STEP_ROWS // L)
    SL = S * L
    col0 = (3 * SC_W + 3 * HY_W) // ML_W
    in_specs = [
        pl.BlockSpec((SL, ML_W), lambda b: (b, col0)),
        pl.BlockSpec((SL, ML_W), lambda b: (b, col0 + 1)),
        pl.BlockSpec((SL, ML_W), lambda b: (b, col0 + 2)),
        pl.BlockSpec((SL, ML_W), lambda b: (b, col0 + 3)),
        pl.BlockSpec((S, rows.shape[1], L), lambda b: (b, 0, 0)),
        pl.BlockSpec((SL, LANES), lambda b: (b, 0)),
        _layer_spec((3, 2 * ML_W), l), _layer_spec((1, ML_W), l),
    ]
    args = [proj, proj, proj, proj, rows, cols, conv_w, norm_g]
    y_shape = jax.ShapeDtypeStruct((B * L, ML_W), F32)
    y_spec = pl.BlockSpec((SL, ML_W), lambda b: (b, 0))
    c_blk, n_blk, m_blk = (2, ML_H, ML_DH, ML_DH), (2 * ML_H, ML_DH), (2 * ML_H, LANES)
    if with_state:
        c0, n0, m0 = state
        in_specs += [
            pl.BlockSpec((S, None) + c_blk, lambda b: (b, l, 0, 0, 0, 0)),
            pl.BlockSpec((S, None) + n_blk, lambda b: (b, l, 0, 0)),
            pl.BlockSpec((S, None, N_GATE, LANES), lambda b: (b, l, 0, 0)),
        ]
        args += [c0, n0, m0]
        out_shape, out_specs = y_shape, y_spec
    else:
        lead = (l + 1,) if n_prev else ()
        if n_prev:
            in_specs += [pl.BlockSpec((S, n_prev) + blk, lambda b, nz=len(blk) + 1: (b,) + (0,) * nz)
                         for blk in (c_blk, n_blk, m_blk)]
            args += list(prev)
        out_shape = [y_shape] + [jax.ShapeDtypeStruct((B,) + lead + blk, F32) for blk in (c_blk, n_blk, m_blk)]
        out_specs = [y_spec] + [pl.BlockSpec((S,) + lead + blk, lambda b, nz=len(lead) + len(blk): (b,) + (0,) * nz)
                                for blk in (c_blk, n_blk, m_blk)]
    return pl.pallas_call(
        functools.partial(_mlstm_kernel, L=L, with_state=with_state, n_prev=n_prev),
        out_shape=out_shape,
        grid=(B // S,),
        in_specs=in_specs,
        out_specs=out_specs,
        compiler_params=pltpu.CompilerParams(
            dimension_semantics=("arbitrary",), vmem_limit_bytes=VMEM_LIMIT),
        name="mlstm",
    )(*args)


def _outmlp_kernel(xc_ref, yac_ref, ybc_ref, xl_ref, yal_ref, ybl_ref, mod_ref, wo_ref, g2_ref, w1_ref, w2_ref,
                   gf_ref, oc_ref, ol_ref, *, final, n_ctx):
    i = pl.program_id(0)
    shared = (mod_ref, wo_ref, g2_ref, w1_ref, w2_ref, gf_ref)

    @pl.when(i < n_ctx)
    def _():
        _outmlp_body(xc_ref, yac_ref, ybc_ref, *shared, oc_ref, final=final)

    @pl.when(i >= n_ctx)
    def _():
        _outmlp_body(xl_ref, yal_ref, ybl_ref, *shared, ol_ref, final=final)


def _outmlp_body(x_ref, ya_ref, yb_ref, mod_ref, wo_ref, g2_ref, w1_ref, w2_ref, gf_ref, o_ref, *, final):
    mod = mod_ref[0]
    g1 = mod[:, 2 * D_MODEL:3 * D_MODEL]
    sh2 = mod[:, 3 * D_MODEL:4 * D_MODEL]
    sc2 = mod[:, 4 * D_MODEL:5 * D_MODEL]
    g2 = mod[:, 5 * D_MODEL:6 * D_MODEL]
    na = SC_W + HY_W
    attn = _dot(ya_ref[...].astype(BF16), wo_ref[0:na, :]) + _dot(yb_ref[...].astype(BF16), wo_ref[na:, :])
    x = x_ref[...] + g1 * attn
    u = ((_rms(x) * g2_ref[...]) * (1.0 + sc2) + sh2).astype(BF16)
    tf = 1024
    acc = jnp.zeros(x.shape, F32)
    for j in range(D_FF // tf):
        hcol = _dot(u, w1_ref[:, j * tf:(j + 1) * tf])
        hcol = jnp.square(jnp.maximum(hcol, 0.0)).astype(BF16)
        acc = acc + _dot(hcol, w2_ref[j * tf:(j + 1) * tf, :])
    x = x + g2 * acc
    if final:
        x = _rms(x) * gf_ref[...]
    o_ref[...] = x


def _outmlp(ctx_ops, lat_ops, mods, w_out, g2, w1, w2, gf, *, l, Ld, final):
    tm = ROW_BLOCK
    Tc, Tl = ctx_ops[0].shape[0], lat_ops[0].shape[0]
    n_ctx, n_lat = Tc // tm, Tl // tm
    ctx, lat = _two_pass_maps(n_ctx)
    token_specs = lambda idx: [pl.BlockSpec((tm, width), lambda i: (idx(i), 0))
                               for width in (D_MODEL, SC_W + HY_W, ML_W)]
    return pl.pallas_call(
        functools.partial(_outmlp_kernel, final=final, n_ctx=n_ctx),
        out_shape=[jax.ShapeDtypeStruct((Tc, D_MODEL), F32), jax.ShapeDtypeStruct((Tl, D_MODEL), F32)],
        grid=(n_ctx + n_lat,),
        in_specs=token_specs(ctx) + token_specs(lat) + [
            _mod_spec(l, Ld, tm, n_ctx),
            _const_spec((D_MODEL, D_MODEL)), _layer_spec((1, D_MODEL), l),
            _const_spec((D_MODEL, D_FF)), _const_spec((D_FF, D_MODEL)), _const_spec((1, D_MODEL)),
        ],
        out_specs=[pl.BlockSpec((tm, D_MODEL), lambda i: (ctx(i), 0)), pl.BlockSpec((tm, D_MODEL), lambda i: (lat(i), 0))],
        compiler_params=pltpu.CompilerParams(
            dimension_semantics=("arbitrary",), vmem_limit_bytes=VMEM_LIMIT),
        name="outproj_mlp",
    )(*ctx_ops, *lat_ops, mods, w_out, g2, w1, w2, gf)


def kernel(x_prompt, x_sample, state_C, state_n, state_m, c, c_ctx, norm1_g, ada_w, ada_b, w_in, sc_conv_w, hy_conv_w, hy_w1, hy_b1, hy_w2, hy_b2, hy_w3, hy_freq, hy_skip, ml_conv_w, ml_gate_b, ml_norm_g, w_out, norm2_g, mlp_w1, mlp_w2, norm_f_g):
    B, L = x_prompt.shape[0], x_prompt.shape[1]
    Bd, Ld = x_sample.shape[0], x_sample.shape[1]

    cond = jnp.concatenate([c_ctx[None, :], c, jnp.zeros((8 - 1 - Bd, D_MODEL), F32)], axis=0)
    mods = _ada(cond, ada_w, ada_b).reshape(DEPTH * 8, 1, 6 * D_MODEL)

    w_in_t = jnp.swapaxes(w_in, 1, 2).astype(BF16)
    w_gate = w_in_t[:, N_MAIN:, :]
    gate_b = jnp.broadcast_to(ml_gate_b[:, :, None], (DEPTH, N_GATE, LANES))
    pad_h = LANES - HY_HID
    w1p = jnp.pad(hy_w1, ((0, 0), (0, LANES - HY_EMB), (0, pad_h)))
    b1p = jnp.pad(hy_b1, ((0, 0), (0, pad_h)))
    w2p = jnp.pad(hy_w2, ((0, 0), (0, pad_h), (0, pad_h)))
    b2p = jnp.pad(hy_b2, ((0, 0), (0, pad_h)))
    w3p = jnp.pad(hy_w3, ((0, 0), (0, pad_h), (0, 0)))
    fp = jnp.pad(hy_freq, ((0, 0), (0, pad_h)))
    pos = jnp.asarray(_pos_embed_np(Ld))
    m0_all = jnp.pad(state_m, ((0, 0), (0, 0), (0, 0), (0, ML_H))).reshape(Bd, DEPTH, N_GATE, 1)
    m0_all = jnp.broadcast_to(m0_all, (Bd, DEPTH, N_GATE, LANES))
    n0_all = state_n.reshape(Bd, DEPTH, 2 * ML_H, ML_DH)
    gf = norm_f_g[None, :]
    g1 = norm1_g[:, None, :]
    g2 = norm2_g[:, None, :]
    ml_ng = ml_norm_g[:, None, :]
    filt_args = (w1p, b1p[:, None, :], w2p, b2p[:, None, :], w3p, fp[:, None, :])
    hre_c, him_c = _filter_spectrum(L, *filt_args)
    hre_s, him_s = _filter_spectrum(Ld, *filt_args)
    state = (state_C, n0_all, m0_all)

    xp = x_prompt.reshape(B * L, D_MODEL)
    xs = x_sample.reshape(Bd * Ld, D_MODEL)
    prev = None
    for l in range(DEPTH):
        final = l == DEPTH - 1

        res = _inproj(xp, xs, pos if l == 0 else None, mods, g1, w_in_t, w_gate, (w_out, mlp_w1, mlp_w2), l=l, Ld=Ld)
        proj_c, g_c, proj_s, g_s = res[:4]
        if l == 0:
            xs = res[4]
        w_out_b, w1_b, w2_b = res[-3:]

        ya_c = _schy(proj_c, sc_conv_w, hy_conv_w, hy_skip, hre_c, him_c, l=l, B=B, L=L)
        yb_c, *states = _mlstm(proj_c, g_c, gate_b, ml_conv_w, ml_ng, None, prev, l=l, B=B, L=L)
        if not final:
            prev = [s.reshape((B, 1) + s.shape[1:]) for s in states] if l == 0 else states

        ya_s = _schy(proj_s, sc_conv_w, hy_conv_w, hy_skip, hre_s, him_s, l=l, B=Bd, L=Ld)
        yb_s = _mlstm(proj_s, g_s, gate_b, ml_conv_w, ml_ng, state, None, l=l, B=Bd, L=Ld)

        xp, xs = _outmlp((xp, ya_c, yb_c), (xs, ya_s, yb_s), mods, w_out_b, g2, w1_b, w2_b, gf,
                         l=l, Ld=Ld, final=final)

    y_prompt = xp.reshape(B, L, D_MODEL)
    y_sample = xs.reshape(Bd, Ld, D_MODEL)
    new_c, new_n, new_m = states
    return (y_prompt, y_sample, new_c, new_n.reshape(B, DEPTH, 2, ML_H, ML_DH),
            new_m[..., 0].reshape(B, DEPTH, 2, ML_H))
```

```python
import functools
import math

import numpy as np
import jax
import jax.numpy as jnp
from jax import lax
from jax.experimental import pallas as pl
from jax.experimental.pallas import tpu as pltpu

F32 = jnp.float32
BF16 = jnp.bfloat16

D_MODEL = 1024
DEPTH = 2
GRID_W = 64
SC_W = 256
HY_W = 256
ML_W = 512
ML_H = 4
ML_DH = ML_W // ML_H
D_FF = 4 * D_MODEL
HY_ORDER = 2
HY_BANDS = 16
HY_EMB = 1 + 2 * HY_BANDS
HY_HID = 64
HY_DECAY_SHORT = 0.3
HY_DECAY_LONG = 1.5
HY_DECAY_TARGET = 1e-2
EPS = 1e-6
N_MAIN = 3 * SC_W + 3 * HY_W + 4 * ML_W
N_GATE = 4 * ML_H
LANES = 128
ROW_BLOCK = 512
ATT_BLOCK = 256
HY_BLOCK = 256
HY_STEP_ROWS = 1024
ML_STEP_ROWS = 512
PROJ_DTYPE = BF16
NEG_BIG = -1e30
LOG2E = math.log2(math.e)
VMEM_LIMIT = 60 * 1024 * 1024


def _dot(a, b):
    return jnp.dot(a, b, preferred_element_type=F32)


def _dot_nt(a, b):
    return lax.dot_general(a, b, (((1,), (1,)), ((), ())), preferred_element_type=F32)


def _split2(a):
    hi = a.astype(BF16)
    lo = (a - hi.astype(F32)).astype(BF16)
    return hi, lo


def _split3(a):
    hi = a.astype(BF16)
    r = a - hi.astype(F32)
    mid = r.astype(BF16)
    lo = (r - mid.astype(F32)).astype(BF16)
    return hi, mid, lo


def _dot3(a, b):
    ah, al = _split2(a)
    bh, bl = _split2(b)
    return _dot(ah, bh) + _dot(al, bh) + _dot(ah, bl)


def _dot3_nt(a, b):
    ah, al = _split2(a)
    bh, bl = _split2(b)
    return _dot_nt(ah, bh) + _dot_nt(al, bh) + _dot_nt(ah, bl)


def _dot3_tab(th, tl, b):
    bh, bl = _split2(b)
    return _dot(th, bh) + _dot(tl, bh) + _dot(th, bl)


def _sigmoid(x):
    return 1.0 / (1.0 + jnp.exp(-x))


def _silu(x):
    return x * _sigmoid(x)


def _log_sigmoid(x):
    return jnp.minimum(x, 0.0) - jnp.log(1.0 + jnp.exp(-jnp.abs(x)))


def _conv3(x, w, seq_len=None):
    R = x.shape[0]
    seq_len = R if seq_len is None else seq_len
    pos = lax.broadcasted_iota(jnp.int32, x.shape, 0) & (seq_len - 1)
    xm = jnp.where(pos == 0, 0.0, pltpu.roll(x, 1, axis=0))
    xp = jnp.where(pos == seq_len - 1, 0.0, pltpu.roll(x, R - 1, axis=0))
    return xm * w[0:1, :] + x * w[1:2, :] + xp * w[2:3, :]


def _rms(x):
    return x * lax.rsqrt(jnp.mean(x * x, axis=-1, keepdims=True) + EPS)


def _scan_max(x, *, reverse):
    n = x.shape[1]
    lane = lax.broadcasted_iota(jnp.int32, x.shape, 1)
    d = 1
    while d < n:
        if reverse:
            shifted = jnp.where(lane < n - d, pltpu.roll(x, n - d, axis=1), NEG_BIG)
        else:
            shifted = jnp.where(lane >= d, pltpu.roll(x, d, axis=1), NEG_BIG)
        x = jnp.maximum(x, shifted)
        d *= 2
    return x


def _hi_lo(a64):
    a = a64.astype(np.float32)
    hi = a.astype(BF16)
    lo = (a - hi.astype(np.float32)).astype(BF16)
    return jnp.asarray(hi), jnp.asarray(lo)


def _const_spec(shape):
    return pl.BlockSpec(shape, lambda *_: tuple(0 for _ in shape), pipeline_mode=pl.Buffered(1))


def _layer_spec(shape, l):
    return pl.BlockSpec((None,) + tuple(shape), lambda *_: (l,) + tuple(0 for _ in shape),
                        pipeline_mode=pl.Buffered(1))


@functools.lru_cache(maxsize=None)
def _dft_tables_np(L):
    k = np.arange(L, dtype=np.int64)
    m1 = ((2 * k[:, None] + 1) * k[None, :]) % (4 * L)
    a1 = np.pi * m1.astype(np.float64) / (2 * L)
    m2 = ((2 * k[:, None] + 1) * (2 * k[None, :] + 1)) % (8 * L)
    a2 = np.pi * m2.astype(np.float64) / (4 * L)
    return np.cos(a1), np.sin(a1), np.cos(a2), np.sin(a2)


@functools.lru_cache(maxsize=None)
def _filter_consts_np(L):
    t_idx = np.arange(L, dtype=np.float64)
    t = t_idx / (L - 1)
    bands = np.arange(1, HY_BANDS + 1, dtype=np.float64)
    ang = 2.0 * math.pi * t_idx[:, None] * bands[None, :] / L
    z = np.zeros((L, LANES), np.float64)
    z[:, 0] = t
    z[:, 1:1 + HY_BANDS] = np.cos(ang)
    z[:, 1 + HY_BANDS:1 + 2 * HY_BANDS] = -np.sin(ang)
    lin = np.linspace(math.log(HY_DECAY_TARGET) / HY_DECAY_LONG,
                      math.log(HY_DECAY_TARGET) / HY_DECAY_SHORT, HY_W).astype(np.float32)
    deltas = np.abs(lin).astype(np.float64)
    decay = np.exp(-t[:, None] * deltas[None, :])
    return z.astype(np.float32), decay.astype(np.float32)


@functools.lru_cache(maxsize=None)
def _pos_embed_np(L):
    rows = L // GRID_W
    r, cidx = np.meshgrid(np.arange(rows, dtype=np.float64), np.arange(GRID_W, dtype=np.float64), indexing="ij")
    r = r.reshape(-1)
    cidx = cidx.reshape(-1)
    quarter = D_MODEL // 4
    omega = 1.0 / (10000.0 ** (np.arange(quarter, dtype=np.float64) / quarter))
    ar = r[:, None] * omega[None, :]
    ac = cidx[:, None] * omega[None, :]
    return np.concatenate([np.sin(ar), np.cos(ar), np.sin(ac), np.cos(ac)], axis=-1).astype(np.float32)


def _ada_kernel(c_ref, w_ref, b_ref, o_ref):
    sc = _silu(c_ref[...])
    o_ref[0] = _dot3(sc, w_ref[0]) + b_ref[0]


def _ada(cond, ada_w, ada_b):
    tn = 3072
    nd = 6 * D_MODEL
    return pl.pallas_call(
        _ada_kernel,
        out_shape=jax.ShapeDtypeStruct((DEPTH, 8, nd), F32),
        grid=(DEPTH, nd // tn),
        in_specs=[
            pl.BlockSpec((8, D_MODEL), lambda l, j: (0, 0)),
            pl.BlockSpec((1, D_MODEL, tn), lambda l, j: (l, 0, j)),
            pl.BlockSpec((1, 1, tn), lambda l, j: (l, 0, j)),
        ],
        out_specs=pl.BlockSpec((1, 8, tn), lambda l, j: (l, 0, j)),
        compiler_params=pltpu.CompilerParams(
            dimension_semantics=("arbitrary", "arbitrary"), vmem_limit_bytes=VMEM_LIMIT),
        name="ada_mod",
    )(cond, ada_w, ada_b.reshape(DEPTH, 1, nd))


def _inproj_body(x, mod_ref, g_ref, w_ref, wg_ref, proj_ref, gate_ref):
    mod = mod_ref[0]
    sh1 = mod[:, 0:D_MODEL]
    sc1 = mod[:, D_MODEL:2 * D_MODEL]
    u = _rms(x) * (g_ref[...] * (1.0 + sc1)) + sh1
    ub = u.astype(BF16)
    tn = 512
    for j in range(N_MAIN // tn):
        proj_ref[:, j * tn:(j + 1) * tn] = _dot_nt(ub, w_ref[j * tn:(j + 1) * tn, :]).astype(PROJ_DTYPE)
    gate_ref[...] = _dot_nt(wg_ref[...], ub)


def _inproj_kernel(*refs, with_pos, n_cast, n_ctx):
    it = iter(refs)
    xc_ref, xl_ref = next(it), next(it)
    pos_ref = next(it) if with_pos else None
    mod_ref, g_ref, w_ref, wg_ref = (next(it) for _ in range(4))
    cast_in = [next(it) for _ in range(n_cast)]
    projc_ref, gatec_ref, projl_ref, gatel_ref = (next(it) for _ in range(4))
    xo_ref = next(it) if with_pos else None
    cast_out = [next(it) for _ in range(n_cast)]
    i = pl.program_id(0)

    @pl.when(i < n_ctx)
    def _():
        for src, dst in zip(cast_in, cast_out):
            dst[...] = src[...].astype(BF16)
        _inproj_body(xc_ref[...], mod_ref, g_ref, w_ref, wg_ref, projc_ref, gatec_ref)

    @pl.when(i >= n_ctx)
    def _():
        x = xl_ref[...]
        if with_pos:
            x = x + pos_ref[...]
            xo_ref[...] = x
        _inproj_body(x, mod_ref, g_ref, w_ref, wg_ref, projl_ref, gatel_ref)


def _two_pass_maps(n_ctx):
    ctx = lambda i: jnp.minimum(i, n_ctx - 1)
    lat = lambda i: jnp.maximum(i - n_ctx, 0)
    return ctx, lat


def _mod_spec(l, Ld, tm, n_ctx):
    return pl.BlockSpec((1, 1, 6 * D_MODEL),
                        lambda i: (l * 8 + jnp.where(i < n_ctx, 0, 1 + ((i - n_ctx) * tm) // Ld), 0, 0))


def _inproj(xc, xl, pos, mods, g, w_in_t, w_gate, cast=(), *, l, Ld):
    tm = ROW_BLOCK
    Tc, Tl = xc.shape[0], xl.shape[0]
    n_ctx, n_lat = Tc // tm, Tl // tm
    per_seq = Ld // tm
    ctx, lat = _two_pass_maps(n_ctx)
    with_pos = pos is not None
    in_specs = [pl.BlockSpec((tm, D_MODEL), lambda i: (ctx(i), 0)), pl.BlockSpec((tm, D_MODEL), lambda i: (lat(i), 0))]
    args = [xc, xl]
    if with_pos:
        in_specs.append(pl.BlockSpec((tm, D_MODEL), lambda i: (lat(i) % per_seq, 0)))
        args.append(pos)
    in_specs += [
        _mod_spec(l, Ld, tm, n_ctx),
        _layer_spec((1, D_MODEL), l),
        _layer_spec((N_MAIN, D_MODEL), l),
        _layer_spec((N_GATE, D_MODEL), l),
    ]
    args += [mods, g, w_in_t, w_gate]
    out_shape, out_specs = [], []
    for T, idx in ((Tc, ctx), (Tl, lat)):
        out_shape += [jax.ShapeDtypeStruct((T, N_MAIN), PROJ_DTYPE), jax.ShapeDtypeStruct((N_GATE, T), F32)]
        out_specs += [pl.BlockSpec((tm, N_MAIN), lambda i, idx=idx: (idx(i), 0)),
                      pl.BlockSpec((N_GATE, tm), lambda i, idx=idx: (0, idx(i)))]
    if with_pos:
        out_shape.append(jax.ShapeDtypeStruct((Tl, D_MODEL), F32))
        out_specs.append(pl.BlockSpec((tm, D_MODEL), lambda i: (lat(i), 0)))
    for w in cast:
        _, rows, cols = w.shape
        in_specs.append(pl.BlockSpec((None, rows // n_ctx, cols), lambda i: (l, ctx(i), 0)))
        args.append(w)
        out_shape.append(jax.ShapeDtypeStruct((rows, cols), BF16))
        out_specs.append(pl.BlockSpec((rows // n_ctx, cols), lambda i: (ctx(i), 0)))
    return pl.pallas_call(
        functools.partial(_inproj_kernel, with_pos=with_pos, n_cast=len(cast), n_ctx=n_ctx),
        out_shape=out_shape,
        grid=(n_ctx + n_lat,),
        in_specs=in_specs,
        out_specs=out_specs,
        compiler_params=pltpu.CompilerParams(
            dimension_semantics=("arbitrary",), vmem_limit_bytes=VMEM_LIMIT),
        name="inproj",
    )(*args)


def _filter_kernel(z_ref, dec_ref, w1_ref, b1_ref, w2_ref, b2_ref, w3_ref, f_ref,
                   ch_ref, cl_ref, sh_ref, sl_ref, hre_ref, him_ref, *, L):
    T = HY_BLOCK
    P = L // T
    f = f_ref[...]
    hdn = jnp.sin(f * (_dot3(z_ref[...], w1_ref[...]) + b1_ref[...]))
    hdn = jnp.sin(f * (_dot3(hdn, w2_ref[...]) + b2_ref[...]))
    h = _dot3(hdn, w3_ref[...])
    dec = dec_ref[...]
    row = lax.broadcasted_iota(jnp.int32, (L, HY_W), 0)
    halves = []
    for o in range(HY_ORDER):
        hf = h[:, (2 * o) * HY_W:(2 * o + 1) * HY_W] * dec
        hb = jnp.where(row == 0, 0.0, h[:, (2 * o + 1) * HY_W:(2 * o + 2) * HY_W] * dec)
        nrm = jnp.sum(jnp.abs(hf), axis=0, keepdims=True) + jnp.sum(jnp.abs(hb), axis=0, keepdims=True)
        inv = 1.0 / nrm
        halves.append((hf * inv, hb * inv))
    blocks = [half[e * T:(e + 1) * T] for pair in halves for half in pair for e in range(P)]
    x = jnp.concatenate(blocks, axis=1)
    xre = _dot3_tab(ch_ref[...], cl_ref[...], x)
    xim = -_dot3_tab(sh_ref[...], sl_ref[...], x)
    sgn = jnp.where((lax.broadcasted_iota(jnp.int32, (T, HY_W), 0) & 1) == 0, 1.0, -1.0)
    for o in range(HY_ORDER):
        def half_spec(which, e):
            j = (o * 2 + which) * P + e
            cs = slice(j * HY_W, (j + 1) * HY_W)
            return xre[:, cs], xim[:, cs], halves[o][which][e * T:e * T + 1]
        for d in range(-(P - 1), P):
            if d == 0:
                (fre, fim, _), (gre, gim, _) = half_spec(0, 0), half_spec(1, 0)
                re, im = fre + gre, fim - gim
            elif d > 0:
                (fre, fim, _), (pre, pim, p0) = half_spec(0, d), half_spec(0, d - 1)
                re, im = fre - sgn * pim, fim + sgn * (pre - p0)
            else:
                (gre, gim, _), (pre, pim, p0) = half_spec(1, -d), half_spec(1, -d - 1)
                re, im = gre - sgn * pim, -gim - sgn * (pre - p0)
            hre_ref[d + P - 1, :, o * HY_W:(o + 1) * HY_W] = re
            him_ref[d + P - 1, :, o * HY_W:(o + 1) * HY_W] = im


def _filter_spectrum(L, w1p, b1p, w2p, b2p, w3p, fp):
    T = HY_BLOCK
    z_np, dec_np = _filter_consts_np(L)
    c1, s1, _, _ = _dft_tables_np(T)
    ch, cl = _hi_lo(c1)
    sh, sl = _hi_lo(s1)
    n = HY_ORDER * HY_W
    nd = 2 * (L // T) - 1
    const = _const_spec
    per_layer = lambda shape: pl.BlockSpec((None,) + shape, lambda l: (l,) + tuple(0 for _ in shape))
    return pl.pallas_call(
        functools.partial(_filter_kernel, L=L),
        out_shape=[jax.ShapeDtypeStruct((DEPTH, nd, T, n), F32), jax.ShapeDtypeStruct((DEPTH, nd, T, n), F32)],
        grid=(DEPTH,),
        in_specs=[const((L, LANES)), const((L, HY_W)), per_layer((LANES, LANES)), per_layer((1, LANES)),
                  per_layer((LANES, LANES)), per_layer((1, LANES)), per_layer((LANES, 2 * n)), per_layer((1, LANES)),
                  const((T, T)), const((T, T)), const((T, T)), const((T, T))],
        out_specs=[per_layer((nd, T, n)), per_layer((nd, T, n))],
        compiler_params=pltpu.CompilerParams(
            dimension_semantics=("arbitrary",), vmem_limit_bytes=VMEM_LIMIT),
        name="hyena_filter",
    )(jnp.asarray(z_np), jnp.asarray(dec_np), w1p, b1p, w2p, b2p, w3p, fp, ch, cl, sh, sl)


def _schy_kernel(sc_ref, hy_ref, scw_ref, hyw_ref, skip_ref, fwd_ref, inv_ref, hre_ref, him_ref, out_ref, *, L):
    T = HY_BLOCK
    P = L // T
    NB = sc_ref.shape[0] // T
    sc = sc_ref[...].astype(F32)
    b_g = sc[:, 0:SC_W]
    c_g = sc[:, SC_W:2 * SC_W]
    xin = sc[:, 2 * SC_W:3 * SC_W]
    out_ref[:, 0:SC_W] = b_g * _conv3(c_g * xin, scw_ref[...], L)

    u3 = _conv3(hy_ref[...].astype(F32), hyw_ref[...], L)
    z = u3[:, 0:HY_W]
    gates = (u3[:, HY_W:2 * HY_W], u3[:, 2 * HY_W:3 * HY_W])
    fwd, inv = fwd_ref[...], inv_ref[...]
    lanes = lambda j: slice(j * HY_W, (j + 1) * HY_W)
    for o in range(HY_ORDER):
        zcat = jnp.concatenate([z[j * T:(j + 1) * T] for j in range(NB)], axis=1)
        spec = _dot(fwd, zcat.astype(BF16))
        yre, yim = [], []
        for bi in range(NB):
            s0, i = bi - bi % P, bi % P
            re = im = None
            for j in range(P):
                d = i - j + P - 1
                hre = hre_ref[d, :, lanes(o)]
                him = him_ref[d, :, lanes(o)]
                zc = spec[0:T, lanes(s0 + j)]
                zs = spec[T:2 * T, lanes(s0 + j)]
                pre = zc * hre + zs * him
                pim = zc * him - zs * hre
                re = pre if re is None else re + pre
                im = pim if im is None else im + pim
            yre.append(re)
            yim.append(im)
        prod = jnp.concatenate([jnp.concatenate(yre, axis=1), jnp.concatenate(yim, axis=1)], axis=0)
        ycat = _dot(inv, prod.astype(BF16)) * (1.0 / T)
        y = jnp.concatenate([ycat[:, lanes(bi)] for bi in range(NB)], axis=0)
        z = gates[o] * (y + skip_ref[o:o + 1, :] * z)
    out_ref[:, SC_W:SC_W + HY_W] = z


def _schy(proj, sc_w, hy_w, skip, hre, him, *, l, B, L):
    T = HY_BLOCK
    _, _, c2, s2 = _dft_tables_np(T)
    fwd = jnp.asarray(np.concatenate([c2, s2], axis=0).astype(np.float32).astype(BF16))
    inv = jnp.asarray(np.concatenate([c2, -s2], axis=1).astype(np.float32).astype(BF16))
    n = HY_ORDER * HY_W
    nd = 2 * (L // T) - 1
    rows = max(L, HY_STEP_ROWS)
    return pl.pallas_call(
        functools.partial(_schy_kernel, L=L),
        out_shape=jax.ShapeDtypeStruct((B * L, SC_W + HY_W), F32),
        grid=(B * L // rows,),
        in_specs=[
            pl.BlockSpec((rows, 3 * SC_W), lambda b: (b, 0)),
            pl.BlockSpec((rows, 3 * HY_W), lambda b: (b, 1)),
            _layer_spec((3, SC_W), l), _layer_spec((3, 3 * HY_W), l), _layer_spec((HY_ORDER, HY_W), l),
            _const_spec((2 * T, T)), _const_spec((T, 2 * T)),
            _layer_spec((nd, T, n), l), _layer_spec((nd, T, n), l),
        ],
        out_specs=pl.BlockSpec((rows, SC_W + HY_W), lambda b: (b, 0)),
        compiler_params=pltpu.CompilerParams(
            dimension_semantics=("arbitrary",), vmem_limit_bytes=VMEM_LIMIT),
        name="sconv_hyena",
    )(proj, proj, sc_w, hy_w, skip, fwd, inv, hre, him)


def _gate_prep_kernel(*refs, B, L, with_state):
    if with_state:
        g_ref, gb_ref, m0_ref, rows_ref, cols_ref = refs
    else:
        g_ref, gb_ref, rows_ref, cols_ref = refs
    R = B * N_GATE
    gb = gb_ref[:, 0:1]
    x = jnp.concatenate([g_ref[:, b * L:(b + 1) * L] + gb for b in range(B)], axis=0)
    rr = lax.broadcasted_iota(jnp.int32, (R, L), 0) & (N_GATE - 1)
    is_f = ((rr >= ML_H) & (rr < 2 * ML_H)) | (rr >= 3 * ML_H)
    fwd_rows = rr < 2 * ML_H
    pk = jnp.where(is_f, _log_sigmoid(x), x)
    r_i = lax.broadcasted_iota(jnp.int32, (L, L), 0)
    c_i = lax.broadcasted_iota(jnp.int32, (L, L), 1)
    triu = jnp.where(r_i <= c_i, 1.0, 0.0).astype(BF16)
    cs = _dot(jnp.concatenate(_split3(pk), axis=0), triu)
    cum = cs[0:R] + cs[R:2 * R] + cs[2 * R:3 * R]
    suf = cum[:, L - 1:L] - cum + pk
    bsum = pltpu.roll(jnp.where(fwd_rows, cum, suf), R - ML_H, axis=0)
    r = pk - bsum
    cm = jnp.where(fwd_rows, _scan_max(r, reverse=False), _scan_max(r, reverse=True))
    if with_state:
        m0 = jnp.concatenate([m0_ref[b][:, 0:1] for b in range(B)], axis=0)
        m = jnp.maximum(m0, cm)
    else:
        m = jnp.maximum(cm, 0.0)
    rl = r * LOG2E
    ml = m * LOG2E
    mt = bsum + m
    pieces = [rl, ml, jnp.exp(-mt), mt]
    if not with_state:
        pieces.append(jnp.exp2(rl - jnp.where(fwd_rows, ml[:, L - 1:L], ml[:, 0:1])))
    pad = jnp.zeros((LANES - N_GATE, L), F32)
    for b in range(B):
        bs = slice(b * N_GATE, (b + 1) * N_GATE)
        for j, piece in enumerate(pieces):
            rows_ref[b, j * N_GATE:(j + 1) * N_GATE, :] = piece[bs]
        cols_ref[b * L:(b + 1) * L, :] = jnp.concatenate([rl[bs], pad], axis=0).T


def _gate_prep(g_t, gate_b, m0, *, l, B, L):
    with_state = m0 is not None
    n_rows = (4 if with_state else 5) * N_GATE
    full = lambda shape: pl.BlockSpec(shape, lambda i: tuple(0 for _ in shape))
    args = [g_t, gate_b]
    in_specs = [full((N_GATE, B * L)), pl.BlockSpec((None, N_GATE, LANES), lambda i: (l, 0, 0))]
    if with_state:
        args.append(m0)
        in_specs.append(pl.BlockSpec((B, None, N_GATE, LANES), lambda i: (0, l, 0, 0)))
    return pl.pallas_call(
        functools.partial(_gate_prep_kernel, B=B, L=L, with_state=with_state),
        out_shape=[jax.ShapeDtypeStruct((B, n_rows, L), F32), jax.ShapeDtypeStruct((B * L, LANES), F32)],
        grid=(1,),
        in_specs=in_specs,
        out_specs=[full((B, n_rows, L)), full((B * L, LANES))],
        compiler_params=pltpu.CompilerParams(
            dimension_semantics=("arbitrary",), vmem_limit_bytes=VMEM_LIMIT),
        name="gate_prep",
    )(*args)


def _mlstm_kernel(*refs, L, with_state, n_prev):
    if with_state:
        (q_ref, k_ref, v_ref, o_ref, rows_ref, cols_ref, cw_ref, ng_ref,
         c0_ref, n0_ref, m0_ref, y_ref) = refs
    elif n_prev:
        (q_ref, k_ref, v_ref, o_ref, rows_ref, cols_ref, cw_ref, ng_ref,
         cprev_ref, nprev_ref, mprev_ref, y_ref, cout_ref, nout_ref, mout_ref) = refs
        cout_ref[:, 0:n_prev] = cprev_ref[...]
        nout_ref[:, 0:n_prev] = nprev_ref[...]
        mout_ref[:, 0:n_prev] = mprev_ref[...]
        cout_ref, nout_ref, mout_ref = (r.at[:, n_prev] for r in (cout_ref, nout_ref, mout_ref))
    else:
        (q_ref, k_ref, v_ref, o_ref, rows_ref, cols_ref, cw_ref, ng_ref,
         y_ref, cout_ref, nout_ref, mout_ref) = refs

    S = rows_ref.shape[0]
    tb = min(L // 2, ATT_BLOCK)
    cw = cw_ref[...]
    q_all = _silu(_conv3(q_ref[...].astype(F32), cw[:, 0:ML_W], L))
    k_all = _silu(_conv3(k_ref[...].astype(F32), cw[:, ML_W:2 * ML_W], L)) * (ML_DH ** -0.5)

    s_loc = lax.broadcasted_iota(jnp.int32, (tb, tb), 0)
    t_loc = lax.broadcasted_iota(jnp.int32, (tb, tb), 1)
    masks = (s_loc <= t_loc, s_loc >= t_loc)
    ones_rows = jnp.ones((2 * 8, L), BF16)
    ng = ng_ref[...]
    for h in range(ML_H):
        hs = slice(h * ML_DH, (h + 1) * ML_DH)
        seqs = []
        for s in range(S):
            sl = slice(s * L, (s + 1) * L)
            rl16, ml16, em16, mt16 = (rows_ref[s, j * N_GATE:(j + 1) * N_GATE, :] for j in range(4))
            q = q_all[sl, hs]
            k = k_all[sl, hs]
            v = v_ref[sl, hs].astype(F32)
            vext_t = jnp.concatenate([v.T.astype(BF16), ones_rows], axis=0)
            rlb = [jnp.broadcast_to(cols_ref[sl, 2 * ML_H * d + h:2 * ML_H * d + h + 1], (L, tb)) for d in range(2)]
            seqs.append(dict(q=q, k=k, qb=q.astype(BF16), kb=k.astype(BF16), vb=v.astype(BF16), vext_t=vext_t,
                             rlb=rlb, ml16=ml16, em16=em16, mt16=mt16))
            if with_state:
                seqs[s]["c0_t"] = [c0_ref[s, d, h].T.astype(BF16) for d in range(2)]
        for r0 in range(0, L, tb):
            rs = slice(r0, r0 + tb)
            for s, sq in enumerate(seqs):
                s_t = _dot_nt(sq["kb"], sq["qb"][rs])
                if with_state:
                    qn_rows = _dot3_nt(n0_ref[s], sq["q"][rs])
                hout_t = None
                for d in range(2):
                    o8 = 2 * ML_H * d + h
                    ml_row = sq["ml16"][o8:o8 + 1, rs]
                    e_diag = jnp.exp2(jnp.where(masks[d], sq["rlb"][d][rs] - ml_row, NEG_BIG))
                    acc = _dot(sq["vext_t"][:, rs], (s_t[rs] * e_diag).astype(BF16))
                    side = slice(r0 + tb, L) if d else slice(0, r0)
                    if side.stop > side.start:
                        e_side = jnp.exp2(sq["rlb"][d][side] - ml_row)
                        acc = acc + _dot(sq["vext_t"][:, side], (s_t[side] * e_side).astype(BF16))
                    num_t = acc[0:ML_DH]
                    den = acc[ML_DH:ML_DH + 1]
                    if with_state:
                        wp = jnp.exp2(m0_ref[s, o8:o8 + 1, 0:1] * LOG2E - ml_row)
                        den = den + wp * qn_rows[d * ML_H + h:d * ML_H + h + 1]
                        num_t = num_t + wp * _dot_nt(sq["c0_t"][d], sq["qb"][rs])
                    contrib = num_t * (1.0 / jnp.maximum(jnp.abs(den), sq["em16"][o8:o8 + 1, rs]))
                    hout_t = contrib if hout_t is None else hout_t + contrib
                hn_t = hout_t * lax.rsqrt(jnp.mean(hout_t * hout_t, axis=0, keepdims=True) + EPS)
                rows_out = slice(s * L + r0, s * L + r0 + tb)
                y_ref[rows_out, hs] = _sigmoid(o_ref[rows_out, hs].astype(F32)) * (hn_t.T * ng[:, hs])

        if not with_state:
            for s, sq in enumerate(seqs):
                w16 = rows_ref[s, 4 * N_GATE:5 * N_GATE, :]
                k_t = sq["k"].T
                n_rows = _dot3(w16, sq["k"])
                for d in range(2):
                    o8 = 2 * ML_H * d + h
                    end = 0 if d else L - 1
                    cout_ref[s, d, h] = _dot((k_t * w16[o8:o8 + 1, :]).astype(BF16), sq["vb"])
                    nout_ref[s, d * ML_H + h:d * ML_H + h + 1, :] = n_rows[o8:o8 + 1, :]
                    mout_ref[s, d * ML_H + h:d * ML_H + h + 1, :] = jnp.broadcast_to(
                        sq["mt16"][o8:o8 + 1, end:end + 1], (1, LANES))


def _mlstm(proj, g_t, gate_b, conv_w, norm_g, state, prev, *, l, B, L):
    with_state = state is not None
    n_prev = 0 if prev is None else l
    rows, cols = _gate_prep(g_t, gate_b, state[2] if with_state else None, l=l, B=B, L=L)
    S = max(1, ML_STEP_ROWS // L)
    SL = S * L
    col0 = (3 * SC_W + 3 * HY_W) // ML_W
    in_specs = [
        pl.BlockSpec((SL, ML_W), lambda b: (b, col0)),
        pl.BlockSpec((SL, ML_W), lambda b: (b, col0 + 1)),
        pl.BlockSpec((SL, ML_W), lambda b: (b, col0 + 2)),
        pl.BlockSpec((SL, ML_W), lambda b: (b, col0 + 3)),
        pl.BlockSpec((S, rows.shape[1], L), lambda b: (b, 0, 0)),
        pl.BlockSpec((SL, LANES), lambda b: (b, 0)),
        _layer_spec((3, 2 * ML_W), l), _layer_spec((1, ML_W), l),
    ]
    args = [proj, proj, proj, proj, rows, cols, conv_w, norm_g]
    y_shape = jax.ShapeDtypeStruct((B * L, ML_W), F32)
    y_spec = pl.BlockSpec((SL, ML_W), lambda b: (b, 0))
    c_blk, n_blk, m_blk = (2, ML_H, ML_DH, ML_DH), (2 * ML_H, ML_DH), (2 * ML_H, LANES)
    if with_state:
        c0, n0, m0 = state
        in_specs += [
            pl.BlockSpec((S, None) + c_blk, lambda b: (b, l, 0, 0, 0, 0)),
            pl.BlockSpec((S, None) + n_blk, lambda b: (b, l, 0, 0)),
            pl.BlockSpec((S, None, N_GATE, LANES), lambda b: (b, l, 0, 0)),
        ]
        args += [c0, n0, m0]
        out_shape, out_specs = y_shape, y_spec
    else:
        lead = (l + 1,) if n_prev else ()
        if n_prev:
            in_specs += [pl.BlockSpec((S, n_prev) + blk, lambda b, nz=len(blk) + 1: (b,) + (0,) * nz)
                         for blk in (c_blk, n_blk, m_blk)]
            args += list(prev)
        out_shape = [y_shape] + [jax.ShapeDtypeStruct((B,) + lead + blk, F32) for blk in (c_blk, n_blk, m_blk)]
        out_specs = [y_spec] + [pl.BlockSpec((S,) + lead + blk, lambda b, nz=len(lead) + len(blk): (b,) + (0,) * nz)
                                for blk in (c_blk, n_blk, m_blk)]
    return pl.pallas_call(
        functools.partial(_mlstm_kernel, L=L, with_state=with_state, n_prev=n_prev),
        out_shape=out_shape,
        grid=(B // S,),
        in_specs=in_specs,
        out_specs=out_specs,
        compiler_params=pltpu.CompilerParams(
            dimension_semantics=("arbitrary",), vmem_limit_bytes=VMEM_LIMIT),
        name="mlstm",
    )(*args)


def _outmlp_kernel(xc_ref, yac_ref, ybc_ref, xl_ref, yal_ref, ybl_ref, mod_ref, wo_ref, g2_ref, w1_ref, w2_ref,
                   gf_ref, oc_ref, ol_ref, *, final, n_ctx):
    i = pl.program_id(0)
    shared = (mod_ref, wo_ref, g2_ref, w1_ref, w2_ref, gf_ref)

    @pl.when(i < n_ctx)
    def _():
        _outmlp_body(xc_ref, yac_ref, ybc_ref, *shared, oc_ref, final=final)

    @pl.when(i >= n_ctx)
    def _():
        _outmlp_body(xl_ref, yal_ref, ybl_ref, *shared, ol_ref, final=final)


def _outmlp_body(x_ref, ya_ref, yb_ref, mod_ref, wo_ref, g2_ref, w1_ref, w2_ref, gf_ref, o_ref, *, final):
    mod = mod_ref[0]
    g1 = mod[:, 2 * D_MODEL:3 * D_MODEL]
    sh2 = mod[:, 3 * D_MODEL:4 * D_MODEL]
    sc2 = mod[:, 4 * D_MODEL:5 * D_MODEL]
    g2 = mod[:, 5 * D_MODEL:6 * D_MODEL]
    na = SC_W + HY_W
    attn = _dot(ya_ref[...].astype(BF16), wo_ref[0:na, :]) + _dot(yb_ref[...].astype(BF16), wo_ref[na:, :])
    x = x_ref[...] + g1 * attn
    u = (_rms(x) * (g2_ref[...] * (1.0 + sc2)) + sh2).astype(BF16)
    tf = 1024
    acc = jnp.zeros(x.shape, F32)
    for j in range(D_FF // tf):
        hcol = _dot(u, w1_ref[:, j * tf:(j + 1) * tf])
        hcol = jnp.square(jnp.maximum(hcol, 0.0)).astype(BF16)
        acc = acc + _dot(hcol, w2_ref[j * tf:(j + 1) * tf, :])
    x = x + g2 * acc
    if final:
        x = _rms(x) * gf_ref[...]
    o_ref[...] = x


def _outmlp(ctx_ops, lat_ops, mods, w_out, g2, w1, w2, gf, *, l, Ld, final):
    tm = ROW_BLOCK
    Tc, Tl = ctx_ops[0].shape[0], lat_ops[0].shape[0]
    n_ctx, n_lat = Tc // tm, Tl // tm
    ctx, lat = _two_pass_maps(n_ctx)
    token_specs = lambda idx: [pl.BlockSpec((tm, width), lambda i: (idx(i), 0))
                               for width in (D_MODEL, SC_W + HY_W, ML_W)]
    return pl.pallas_call(
        functools.partial(_outmlp_kernel, final=final, n_ctx=n_ctx),
        out_shape=[jax.ShapeDtypeStruct((Tc, D_MODEL), F32), jax.ShapeDtypeStruct((Tl, D_MODEL), F32)],
        grid=(n_ctx + n_lat,),
        in_specs=token_specs(ctx) + token_specs(lat) + [
            _mod_spec(l, Ld, tm, n_ctx),
            _const_spec((D_MODEL, D_MODEL)), _layer_spec((1, D_MODEL), l),
            _const_spec((D_MODEL, D_FF)), _const_spec((D_FF, D_MODEL)), _const_spec((1, D_MODEL)),
        ],
        out_specs=[pl.BlockSpec((tm, D_MODEL), lambda i: (ctx(i), 0)), pl.BlockSpec((tm, D_MODEL), lambda i: (lat(i), 0))],
        compiler_params=pltpu.CompilerParams(
            dimension_semantics=("arbitrary",), vmem_limit_bytes=VMEM_LIMIT),
        name="outproj_mlp",
    )(*ctx_ops, *lat_ops, mods, w_out, g2, w1, w2, gf)


def kernel(x_prompt, x_sample, state_C, state_n, state_m, c, c_ctx, norm1_g, ada_w, ada_b, w_in, sc_conv_w, hy_conv_w, hy_w1, hy_b1, hy_w2, hy_b2, hy_w3, hy_freq, hy_skip, ml_conv_w, ml_gate_b, ml_norm_g, w_out, norm2_g, mlp_w1, mlp_w2, norm_f_g):
    B, L = x_prompt.shape[0], x_prompt.shape[1]
    Bd, Ld = x_sample.shape[0], x_sample.shape[1]

    cond = jnp.concatenate([c_ctx[None, :], c, jnp.zeros((8 - 1 - Bd, D_MODEL), F32)], axis=0)
    mods = _ada(cond, ada_w, ada_b).reshape(DEPTH * 8, 1, 6 * D_MODEL)

    w_in_t = jnp.swapaxes(w_in, 1, 2).astype(BF16)
    w_gate = w_in_t[:, N_MAIN:, :]
    gate_b = jnp.broadcast_to(ml_gate_b[:, :, None], (DEPTH, N_GATE, LANES))
    pad_h = LANES - HY_HID
    w1p = jnp.pad(hy_w1, ((0, 0), (0, LANES - HY_EMB), (0, pad_h)))
    b1p = jnp.pad(hy_b1, ((0, 0), (0, pad_h)))
    w2p = jnp.pad(hy_w2, ((0, 0), (0, pad_h), (0, pad_h)))
    b2p = jnp.pad(hy_b2, ((0, 0), (0, pad_h)))
    w3p = jnp.pad(hy_w3, ((0, 0), (0, pad_h), (0, 0)))
    fp = jnp.pad(hy_freq, ((0, 0), (0, pad_h)))
    pos = jnp.asarray(_pos_embed_np(Ld))
    m0_all = jnp.pad(state_m, ((0, 0), (0, 0), (0, 0), (0, ML_H))).reshape(Bd, DEPTH, N_GATE, 1)
    m0_all = jnp.broadcast_to(m0_all, (Bd, DEPTH, N_GATE, LANES))
    n0_all = state_n.reshape(Bd, DEPTH, 2 * ML_H, ML_DH)
    gf = norm_f_g[None, :]
    g1 = norm1_g[:, None, :]
    g2 = norm2_g[:, None, :]
    ml_ng = ml_norm_g[:, None, :]
    filt_args = (w1p, b1p[:, None, :], w2p, b2p[:, None, :], w3p, fp[:, None, :])
    hre_c, him_c = _filter_spectrum(L, *filt_args)
    hre_s, him_s = _filter_spectrum(Ld, *filt_args)
    state = (state_C, n0_all, m0_all)

    xp = x_prompt.reshape(B * L, D_MODEL)
    xs = x_sample.reshape(Bd * Ld, D_MODEL)
    prev = None
    for l in range(DEPTH):
        final = l == DEPTH - 1

        res = _inproj(xp, xs, pos if l == 0 else None, mods, g1, w_in_t, w_gate, (w_out, mlp_w1, mlp_w2), l=l, Ld=Ld)
        proj_c, g_c, proj_s, g_s = res[:4]
        if l == 0:
            xs = res[4]
        w_out_b, w1_b, w2_b = res[-3:]

        ya_c = _schy(proj_c, sc_conv_w, hy_conv_w, hy_skip, hre_c, him_c, l=l, B=B, L=L)
        yb_c, *states = _mlstm(proj_c, g_c, gate_b, ml_conv_w, ml_ng, None, prev, l=l, B=B, L=L)
        if not final:
            prev = [s.reshape((B, 1) + s.shape[1:]) for s in states] if l == 0 else states

        ya_s = _schy(proj_s, sc_conv_w, hy_conv_w, hy_skip, hre_s, him_s, l=l, B=Bd, L=Ld)
        yb_s = _mlstm(proj_s, g_s, gate_b, ml_conv_w, ml_ng, state, None, l=l, B=Bd, L=Ld)

        xp, xs = _outmlp((xp, ya_c, yb_c), (xs, ya_s, yb_s), mods, w_out_b, g2, w1_b, w2_b, gf,
                         l=l, Ld=Ld, final=final)

    y_prompt = xp.reshape(B, L, D_MODEL)
    y_sample = xs.reshape(Bd, Ld, D_MODEL)
    new_c, new_n, new_m = states
    return (y_prompt, y_sample, new_c, new_n.reshape(B, DEPTH, 2, ML_H, ML_DH),
            new_m[..., 0].reshape(B, DEPTH, 2, ML_H))
```

```python
import functools
import math

import numpy as np
import jax
import jax.numpy as jnp
from jax import lax
from jax.experimental import pallas as pl
from jax.experimental.pallas import tpu as pltpu

F32 = jnp.float32
BF16 = jnp.bfloat16

D_MODEL = 1024
DEPTH = 2
GRID_W = 64
SC_W = 256
HY_W = 256
ML_W = 512
ML_H = 4
ML_DH = ML_W // ML_H
D_FF = 4 * D_MODEL
HY_ORDER = 2
HY_BANDS = 16
HY_EMB = 1 + 2 * HY_BANDS
HY_HID = 64
HY_DECAY_SHORT = 0.3
HY_DECAY_LONG = 1.5
HY_DECAY_TARGET = 1e-2
EPS = 1e-6
N_MAIN = 3 * SC_W + 3 * HY_W + 4 * ML_W
N_GATE = 4 * ML_H
LANES = 128
ROW_BLOCK = 512
ATT_BLOCK = 512
HY_BLOCK = 256
HY_STEP_ROWS = 1024
ML_STEP_ROWS = 512
PROJ_DTYPE = BF16
NEG_BIG = -1e30
LOG2E = math.log2(math.e)
VMEM_LIMIT = 60 * 1024 * 1024


def _dot(a, b):
    return jnp.dot(a, b, preferred_element_type=F32)


def _dot_nt(a, b):
    return lax.dot_general(a, b, (((1,), (1,)), ((), ())), preferred_element_type=F32)


def _split2(a):
    hi = a.astype(BF16)
    lo = (a - hi.astype(F32)).astype(BF16)
    return hi, lo


def _split3(a):
    hi = a.astype(BF16)
    r = a - hi.astype(F32)
    mid = r.astype(BF16)
    lo = (r - mid.astype(F32)).astype(BF16)
    return hi, mid, lo


def _dot3(a, b):
    ah, al = _split2(a)
    bh, bl = _split2(b)
    return _dot(ah, bh) + _dot(al, bh) + _dot(ah, bl)


def _dot3_nt(a, b):
    ah, al = _split2(a)
    bh, bl = _split2(b)
    return _dot_nt(ah, bh) + _dot_nt(al, bh) + _dot_nt(ah, bl)


def _dot3_tab(th, tl, b):
    bh, bl = _split2(b)
    return _dot(th, bh) + _dot(tl, bh) + _dot(th, bl)


def _sigmoid(x):
    return 1.0 / (1.0 + jnp.exp(-x))


def _silu(x):
    return x * _sigmoid(x)


def _log_sigmoid(x):
    return jnp.minimum(x, 0.0) - jnp.log(1.0 + jnp.exp(-jnp.abs(x)))


def _conv3(x, w, seq_len=None):
    R = x.shape[0]
    seq_len = R if seq_len is None else seq_len
    pos = lax.broadcasted_iota(jnp.int32, x.shape, 0) & (seq_len - 1)
    xm = jnp.where(pos == 0, 0.0, pltpu.roll(x, 1, axis=0))
    xp = jnp.where(pos == seq_len - 1, 0.0, pltpu.roll(x, R - 1, axis=0))
    return xm * w[0:1, :] + x * w[1:2, :] + xp * w[2:3, :]


def _rms(x):
    return x * lax.rsqrt(jnp.mean(x * x, axis=-1, keepdims=True) + EPS)


def _scan_max(x, *, reverse):
    n = x.shape[1]
    lane = lax.broadcasted_iota(jnp.int32, x.shape, 1)
    d = 1
    while d < n:
        if reverse:
            shifted = jnp.where(lane < n - d, pltpu.roll(x, n - d, axis=1), NEG_BIG)
        else:
            shifted = jnp.where(lane >= d, pltpu.roll(x, d, axis=1), NEG_BIG)
        x = jnp.maximum(x, shifted)
        d *= 2
    return x


def _hi_lo(a64):
    a = a64.astype(np.float32)
    hi = a.astype(BF16)
    lo = (a - hi.astype(np.float32)).astype(BF16)
    return jnp.asarray(hi), jnp.asarray(lo)


def _const_spec(shape):
    return pl.BlockSpec(shape, lambda *_: tuple(0 for _ in shape), pipeline_mode=pl.Buffered(1))


def _layer_spec(shape, l):
    return pl.BlockSpec((None,) + tuple(shape), lambda *_: (l,) + tuple(0 for _ in shape),
                        pipeline_mode=pl.Buffered(1))


@functools.lru_cache(maxsize=None)
def _dft_tables_np(L):
    k = np.arange(L, dtype=np.int64)
    m1 = ((2 * k[:, None] + 1) * k[None, :]) % (4 * L)
    a1 = np.pi * m1.astype(np.float64) / (2 * L)
    m2 = ((2 * k[:, None] + 1) * (2 * k[None, :] + 1)) % (8 * L)
    a2 = np.pi * m2.astype(np.float64) / (4 * L)
    return np.cos(a1), np.sin(a1), np.cos(a2), np.sin(a2)


@functools.lru_cache(maxsize=None)
def _filter_consts_np(L):
    t_idx = np.arange(L, dtype=np.float64)
    t = t_idx / (L - 1)
    bands = np.arange(1, HY_BANDS + 1, dtype=np.float64)
    ang = 2.0 * math.pi * t_idx[:, None] * bands[None, :] / L
    z = np.zeros((L, LANES), np.float64)
    z[:, 0] = t
    z[:, 1:1 + HY_BANDS] = np.cos(ang)
    z[:, 1 + HY_BANDS:1 + 2 * HY_BANDS] = -np.sin(ang)
    lin = np.linspace(math.log(HY_DECAY_TARGET) / HY_DECAY_LONG,
                      math.log(HY_DECAY_TARGET) / HY_DECAY_SHORT, HY_W).astype(np.float32)
    deltas = np.abs(lin).astype(np.float64)
    decay = np.exp(-t[:, None] * deltas[None, :])
    return z.astype(np.float32), decay.astype(np.float32)


@functools.lru_cache(maxsize=None)
def _pos_embed_np(L):
    rows = L // GRID_W
    r, cidx = np.meshgrid(np.arange(rows, dtype=np.float64), np.arange(GRID_W, dtype=np.float64), indexing="ij")
    r = r.reshape(-1)
    cidx = cidx.reshape(-1)
    quarter = D_MODEL // 4
    omega = 1.0 / (10000.0 ** (np.arange(quarter, dtype=np.float64) / quarter))
    ar = r[:, None] * omega[None, :]
    ac = cidx[:, None] * omega[None, :]
    return np.concatenate([np.sin(ar), np.cos(ar), np.sin(ac), np.cos(ac)], axis=-1).astype(np.float32)


def _ada_kernel(c_ref, w_ref, b_ref, o_ref):
    sc = _silu(c_ref[...])
    o_ref[0] = _dot3(sc, w_ref[0]) + b_ref[0]


def _ada(cond, ada_w, ada_b):
    tn = 1536
    nd = 6 * D_MODEL
    return pl.pallas_call(
        _ada_kernel,
        out_shape=jax.ShapeDtypeStruct((DEPTH, 8, nd), F32),
        grid=(DEPTH, nd // tn),
        in_specs=[
            pl.BlockSpec((8, D_MODEL), lambda l, j: (0, 0)),
            pl.BlockSpec((1, D_MODEL, tn), lambda l, j: (l, 0, j)),
            pl.BlockSpec((1, 1, tn), lambda l, j: (l, 0, j)),
        ],
        out_specs=pl.BlockSpec((1, 8, tn), lambda l, j: (l, 0, j)),
        compiler_params=pltpu.CompilerParams(
            dimension_semantics=("arbitrary", "arbitrary"), vmem_limit_bytes=VMEM_LIMIT),
        name="ada_mod",
    )(cond, ada_w, ada_b.reshape(DEPTH, 1, nd))


def _inproj_body(x, mod_ref, g_ref, w_ref, wg_ref, proj_ref, gate_ref):
    mod = mod_ref[0]
    sh1 = mod[:, 0:D_MODEL]
    sc1 = mod[:, D_MODEL:2 * D_MODEL]
    u = (_rms(x) * g_ref[...]) * (1.0 + sc1) + sh1
    ub = u.astype(BF16)
    tn = 512
    for j in range(N_MAIN // tn):
        proj_ref[:, j * tn:(j + 1) * tn] = _dot_nt(ub, w_ref[j * tn:(j + 1) * tn, :]).astype(PROJ_DTYPE)
    gate_ref[...] = _dot_nt(wg_ref[...], ub)


def _inproj_kernel(*refs, with_pos, n_cast, n_ctx):
    it = iter(refs)
    xc_ref, xl_ref = next(it), next(it)
    pos_ref = next(it) if with_pos else None
    mod_ref, g_ref, w_ref, wg_ref = (next(it) for _ in range(4))
    cast_in = [next(it) for _ in range(n_cast)]
    projc_ref, gatec_ref, projl_ref, gatel_ref = (next(it) for _ in range(4))
    xo_ref = next(it) if with_pos else None
    cast_out = [next(it) for _ in range(n_cast)]
    i = pl.program_id(0)

    @pl.when(i < n_ctx)
    def _():
        for src, dst in zip(cast_in, cast_out):
            dst[...] = src[...].astype(BF16)
        _inproj_body(xc_ref[...], mod_ref, g_ref, w_ref, wg_ref, projc_ref, gatec_ref)

    @pl.when(i >= n_ctx)
    def _():
        x = xl_ref[...]
        if with_pos:
            x = x + pos_ref[...]
            xo_ref[...] = x
        _inproj_body(x, mod_ref, g_ref, w_ref, wg_ref, projl_ref, gatel_ref)


def _two_pass_maps(n_ctx):
    ctx = lambda i: jnp.minimum(i, n_ctx - 1)
    lat = lambda i: jnp.maximum(i - n_ctx, 0)
    return ctx, lat


def _mod_spec(l, Ld, tm, n_ctx):
    return pl.BlockSpec((1, 1, 6 * D_MODEL),
                        lambda i: (l * 8 + jnp.where(i < n_ctx, 0, 1 + ((i - n_ctx) * tm) // Ld), 0, 0))


def _inproj(xc, xl, pos, mods, g, w_in_t, w_gate, cast=(), *, l, Ld):
    tm = ROW_BLOCK
    Tc, Tl = xc.shape[0], xl.shape[0]
    n_ctx, n_lat = Tc // tm, Tl // tm
    per_seq = Ld // tm
    ctx, lat = _two_pass_maps(n_ctx)
    with_pos = pos is not None
    in_specs = [pl.BlockSpec((tm, D_MODEL), lambda i: (ctx(i), 0)), pl.BlockSpec((tm, D_MODEL), lambda i: (lat(i), 0))]
    args = [xc, xl]
    if with_pos:
        in_specs.append(pl.BlockSpec((tm, D_MODEL), lambda i: (lat(i) % per_seq, 0)))
        args.append(pos)
    in_specs += [
        _mod_spec(l, Ld, tm, n_ctx),
        _layer_spec((1, D_MODEL), l),
        _layer_spec((N_MAIN, D_MODEL), l),
        _layer_spec((N_GATE, D_MODEL), l),
    ]
    args += [mods, g, w_in_t, w_gate]
    out_shape, out_specs = [], []
    for T, idx in ((Tc, ctx), (Tl, lat)):
        out_shape += [jax.ShapeDtypeStruct((T, N_MAIN), PROJ_DTYPE), jax.ShapeDtypeStruct((N_GATE, T), F32)]
        out_specs += [pl.BlockSpec((tm, N_MAIN), lambda i, idx=idx: (idx(i), 0)),
                      pl.BlockSpec((N_GATE, tm), lambda i, idx=idx: (0, idx(i)))]
    if with_pos:
        out_shape.append(jax.ShapeDtypeStruct((Tl, D_MODEL), F32))
        out_specs.append(pl.BlockSpec((tm, D_MODEL), lambda i: (lat(i), 0)))
    for w in cast:
        _, rows, cols = w.shape
        in_specs.append(pl.BlockSpec((None, rows // n_ctx, cols), lambda i: (l, ctx(i), 0)))
        args.append(w)
        out_shape.append(jax.ShapeDtypeStruct((rows, cols), BF16))
        out_specs.append(pl.BlockSpec((rows // n_ctx, cols), lambda i: (ctx(i), 0)))
    return pl.pallas_call(
        functools.partial(_inproj_kernel, with_pos=with_pos, n_cast=len(cast), n_ctx=n_ctx),
        out_shape=out_shape,
        grid=(n_ctx + n_lat,),
        in_specs=in_specs,
        out_specs=out_specs,
        compiler_params=pltpu.CompilerParams(
            dimension_semantics=("arbitrary",), vmem_limit_bytes=VMEM_LIMIT),
        name="inproj",
    )(*args)


def _filter_kernel(z_ref, dec_ref, w1_ref, b1_ref, w2_ref, b2_ref, w3_ref, f_ref,
                   ch_ref, cl_ref, sh_ref, sl_ref, hre_ref, him_ref, *, L):
    T = HY_BLOCK
    P = L // T
    f = f_ref[...]
    hdn = jnp.sin(f * (_dot3(z_ref[...], w1_ref[...]) + b1_ref[...]))
    hdn = jnp.sin(f * (_dot3(hdn, w2_ref[...]) + b2_ref[...]))
    h = _dot3(hdn, w3_ref[...])
    dec = dec_ref[...]
    row = lax.broadcasted_iota(jnp.int32, (L, HY_W), 0)
    halves = []
    for o in range(HY_ORDER):
        hf = h[:, (2 * o) * HY_W:(2 * o + 1) * HY_W] * dec
        hb = jnp.where(row == 0, 0.0, h[:, (2 * o + 1) * HY_W:(2 * o + 2) * HY_W] * dec)
        nrm = jnp.sum(jnp.abs(hf), axis=0, keepdims=True) + jnp.sum(jnp.abs(hb), axis=0, keepdims=True)
        inv = 1.0 / nrm
        halves.append((hf * inv, hb * inv))
    blocks = [half[e * T:(e + 1) * T] for pair in halves for half in pair for e in range(P)]
    x = jnp.concatenate(blocks, axis=1)
    xre = _dot3_tab(ch_ref[...], cl_ref[...], x)
    xim = -_dot3_tab(sh_ref[...], sl_ref[...], x)
    sgn = jnp.where((lax.broadcasted_iota(jnp.int32, (T, HY_W), 0) & 1) == 0, 1.0, -1.0)
    for o in range(HY_ORDER):
        def half_spec(which, e):
            j = (o * 2 + which) * P + e
            cs = slice(j * HY_W, (j + 1) * HY_W)
            return xre[:, cs], xim[:, cs], halves[o][which][e * T:e * T + 1]
        for d in range(-(P - 1), P):
            if d == 0:
                (fre, fim, _), (gre, gim, _) = half_spec(0, 0), half_spec(1, 0)
                re, im = fre + gre, fim - gim
            elif d > 0:
                (fre, fim, _), (pre, pim, p0) = half_spec(0, d), half_spec(0, d - 1)
                re, im = fre - sgn * pim, fim + sgn * (pre - p0)
            else:
                (gre, gim, _), (pre, pim, p0) = half_spec(1, -d), half_spec(1, -d - 1)
                re, im = gre - sgn * pim, -gim - sgn * (pre - p0)
            hre_ref[d + P - 1, :, o * HY_W:(o + 1) * HY_W] = re
            him_ref[d + P - 1, :, o * HY_W:(o + 1) * HY_W] = im


def _filter_spectrum(L, w1p, b1p, w2p, b2p, w3p, fp):
    T = HY_BLOCK
    z_np, dec_np = _filter_consts_np(L)
    c1, s1, _, _ = _dft_tables_np(T)
    ch, cl = _hi_lo(c1)
    sh, sl = _hi_lo(s1)
    n = HY_ORDER * HY_W
    nd = 2 * (L // T) - 1
    const = _const_spec
    per_layer = lambda shape: pl.BlockSpec((None,) + shape, lambda l: (l,) + tuple(0 for _ in shape))
    return pl.pallas_call(
        functools.partial(_filter_kernel, L=L),
        out_shape=[jax.ShapeDtypeStruct((DEPTH, nd, T, n), F32), jax.ShapeDtypeStruct((DEPTH, nd, T, n), F32)],
        grid=(DEPTH,),
        in_specs=[const((L, LANES)), const((L, HY_W)), per_layer((LANES, LANES)), per_layer((1, LANES)),
                  per_layer((LANES, LANES)), per_layer((1, LANES)), per_layer((LANES, 2 * n)), per_layer((1, LANES)),
                  const((T, T)), const((T, T)), const((T, T)), const((T, T))],
        out_specs=[per_layer((nd, T, n)), per_layer((nd, T, n))],
        compiler_params=pltpu.CompilerParams(
            dimension_semantics=("arbitrary",), vmem_limit_bytes=VMEM_LIMIT),
        name="hyena_filter",
    )(jnp.asarray(z_np), jnp.asarray(dec_np), w1p, b1p, w2p, b2p, w3p, fp, ch, cl, sh, sl)


def _schy_kernel(sc_ref, hy_ref, scw_ref, hyw_ref, skip_ref, fwd_ref, inv_ref, hre_ref, him_ref, out_ref, *, L):
    T = HY_BLOCK
    P = L // T
    NB = sc_ref.shape[0] // T
    sc = sc_ref[...].astype(F32)
    b_g = sc[:, 0:SC_W]
    c_g = sc[:, SC_W:2 * SC_W]
    xin = sc[:, 2 * SC_W:3 * SC_W]
    out_ref[:, 0:SC_W] = b_g * _conv3(c_g * xin, scw_ref[...], L)

    u3 = _conv3(hy_ref[...].astype(F32), hyw_ref[...], L)
    z = u3[:, 0:HY_W]
    gates = (u3[:, HY_W:2 * HY_W], u3[:, 2 * HY_W:3 * HY_W])
    fwd, inv = fwd_ref[...], inv_ref[...]
    lanes = lambda j: slice(j * HY_W, (j + 1) * HY_W)
    for o in range(HY_ORDER):
        zcat = jnp.concatenate([z[j * T:(j + 1) * T] for j in range(NB)], axis=1)
        spec = _dot(fwd, zcat.astype(BF16))
        yre, yim = [], []
        for bi in range(NB):
            s0, i = bi - bi % P, bi % P
            re = im = None
            for j in range(P):
                d = i - j + P - 1
                hre = hre_ref[d, :, lanes(o)]
                him = him_ref[d, :, lanes(o)]
                zc = spec[0:T, lanes(s0 + j)]
                zs = spec[T:2 * T, lanes(s0 + j)]
                pre = zc * hre + zs * him
                pim = zc * him - zs * hre
                re = pre if re is None else re + pre
                im = pim if im is None else im + pim
            yre.append(re)
            yim.append(im)
        prod = jnp.concatenate([jnp.concatenate(yre, axis=1), jnp.concatenate(yim, axis=1)], axis=0)
        ycat = _dot(inv, prod.astype(BF16)) * (1.0 / T)
        y = jnp.concatenate([ycat[:, lanes(bi)] for bi in range(NB)], axis=0)
        z = gates[o] * (y + skip_ref[o:o + 1, :] * z)
    out_ref[:, SC_W:SC_W + HY_W] = z


def _schy(proj, sc_w, hy_w, skip, hre, him, *, l, B, L):
    T = HY_BLOCK
    _, _, c2, s2 = _dft_tables_np(T)
    fwd = jnp.asarray(np.concatenate([c2, s2], axis=0).astype(np.float32).astype(BF16))
    inv = jnp.asarray(np.concatenate([c2, -s2], axis=1).astype(np.float32).astype(BF16))
    n = HY_ORDER * HY_W
    nd = 2 * (L // T) - 1
    rows = max(L, HY_STEP_ROWS)
    return pl.pallas_call(
        functools.partial(_schy_kernel, L=L),
        out_shape=jax.ShapeDtypeStruct((B * L, SC_W + HY_W), F32),
        grid=(B * L // rows,),
        in_specs=[
            pl.BlockSpec((rows, 3 * SC_W), lambda b: (b, 0)),
            pl.BlockSpec((rows, 3 * HY_W), lambda b: (b, 1)),
            _layer_spec((3, SC_W), l), _layer_spec((3, 3 * HY_W), l), _layer_spec((HY_ORDER, HY_W), l),
            _const_spec((2 * T, T)), _const_spec((T, 2 * T)),
            _layer_spec((nd, T, n), l), _layer_spec((nd, T, n), l),
        ],
        out_specs=pl.BlockSpec((rows, SC_W + HY_W), lambda b: (b, 0)),
        compiler_params=pltpu.CompilerParams(
            dimension_semantics=("arbitrary",), vmem_limit_bytes=VMEM_LIMIT),
        name="sconv_hyena",
    )(proj, proj, sc_w, hy_w, skip, fwd, inv, hre, him)


def _gate_prep_kernel(*refs, B, L, with_state):
    if with_state:
        g_ref, gb_ref, m0_ref, rows_ref, cols_ref = refs
    else:
        g_ref, gb_ref, rows_ref, cols_ref = refs
    R = B * N_GATE
    gb = gb_ref[:, 0:1]
    x = jnp.concatenate([g_ref[:, b * L:(b + 1) * L] + gb for b in range(B)], axis=0)
    rr = lax.broadcasted_iota(jnp.int32, (R, L), 0) & (N_GATE - 1)
    is_f = ((rr >= ML_H) & (rr < 2 * ML_H)) | (rr >= 3 * ML_H)
    fwd_rows = rr < 2 * ML_H
    pk = jnp.where(is_f, _log_sigmoid(x), x)
    r_i = lax.broadcasted_iota(jnp.int32, (L, L), 0)
    c_i = lax.broadcasted_iota(jnp.int32, (L, L), 1)
    triu = jnp.where(r_i <= c_i, 1.0, 0.0).astype(BF16)
    cs = _dot(jnp.concatenate(_split3(pk), axis=0), triu)
    cum = cs[0:R] + cs[R:2 * R] + cs[2 * R:3 * R]
    suf = cum[:, L - 1:L] - cum + pk
    bsum = pltpu.roll(jnp.where(fwd_rows, cum, suf), R - ML_H, axis=0)
    r = pk - bsum
    cm = jnp.where(fwd_rows, _scan_max(r, reverse=False), _scan_max(r, reverse=True))
    if with_state:
        m0 = jnp.concatenate([m0_ref[b][:, 0:1] for b in range(B)], axis=0)
        m = jnp.maximum(m0, cm)
    else:
        m = jnp.maximum(cm, 0.0)
    rl = r * LOG2E
    ml = m * LOG2E
    mt = bsum + m
    pieces = [rl, ml, jnp.exp(-mt), mt]
    if not with_state:
        pieces.append(jnp.exp2(rl - jnp.where(fwd_rows, ml[:, L - 1:L], ml[:, 0:1])))
    pad = jnp.zeros((LANES - N_GATE, L), F32)
    for b in range(B):
        bs = slice(b * N_GATE, (b + 1) * N_GATE)
        for j, piece in enumerate(pieces):
            rows_ref[b, j * N_GATE:(j + 1) * N_GATE, :] = piece[bs]
        cols_ref[b * L:(b + 1) * L, :] = jnp.concatenate([rl[bs], pad], axis=0).T


def _gate_prep(g_t, gate_b, m0, *, l, B, L):
    with_state = m0 is not None
    n_rows = (4 if with_state else 5) * N_GATE
    full = lambda shape: pl.BlockSpec(shape, lambda i: tuple(0 for _ in shape))
    args = [g_t, gate_b]
    in_specs = [full((N_GATE, B * L)), pl.BlockSpec((None, N_GATE, LANES), lambda i: (l, 0, 0))]
    if with_state:
        args.append(m0)
        in_specs.append(pl.BlockSpec((B, None, N_GATE, LANES), lambda i: (0, l, 0, 0)))
    return pl.pallas_call(
        functools.partial(_gate_prep_kernel, B=B, L=L, with_state=with_state),
        out_shape=[jax.ShapeDtypeStruct((B, n_rows, L), F32), jax.ShapeDtypeStruct((B * L, LANES), F32)],
        grid=(1,),
        in_specs=in_specs,
        out_specs=[full((B, n_rows, L)), full((B * L, LANES))],
        compiler_params=pltpu.CompilerParams(
            dimension_semantics=("arbitrary",), vmem_limit_bytes=VMEM_LIMIT),
        name="gate_prep",
    )(*args)


def _mlstm_kernel(*refs, L, with_state, n_prev):
    if with_state:
        (q_ref, k_ref, v_ref, o_ref, rows_ref, cols_ref, cw_ref, ng_ref,
         c0_ref, n0_ref, m0_ref, y_ref) = refs
    elif n_prev:
        (q_ref, k_ref, v_ref, o_ref, rows_ref, cols_ref, cw_ref, ng_ref,
         cprev_ref, nprev_ref, mprev_ref, y_ref, cout_ref, nout_ref, mout_ref) = refs
        cout_ref[:, 0:n_prev] = cprev_ref[...]
        nout_ref[:, 0:n_prev] = nprev_ref[...]
        mout_ref[:, 0:n_prev] = mprev_ref[...]
        cout_ref, nout_ref, mout_ref = (r.at[:, n_prev] for r in (cout_ref, nout_ref, mout_ref))
    else:
        (q_ref, k_ref, v_ref, o_ref, rows_ref, cols_ref, cw_ref, ng_ref,
         y_ref, cout_ref, nout_ref, mout_ref) = refs

    S = rows_ref.shape[0]
    tb = min(L, ATT_BLOCK)
    cw = cw_ref[...]
    q_all = _silu(_conv3(q_ref[...].astype(F32), cw[:, 0:ML_W], L))
    k_all = _silu(_conv3(k_ref[...].astype(F32), cw[:, ML_W:2 * ML_W], L)) * (ML_DH ** -0.5)

    s_loc = lax.broadcasted_iota(jnp.int32, (tb, tb), 0)
    t_loc = lax.broadcasted_iota(jnp.int32, (tb, tb), 1)
    masks = (s_loc <= t_loc, s_loc >= t_loc)
    ones_rows = jnp.ones((2 * 8, L), BF16)
    ng = ng_ref[...]
    for h in range(ML_H):
        hs = slice(h * ML_DH, (h + 1) * ML_DH)
        seqs = []
        for s in range(S):
            sl = slice(s * L, (s + 1) * L)
            rl16, ml16, em16, mt16 = (rows_ref[s, j * N_GATE:(j + 1) * N_GATE, :] for j in range(4))
            q = q_all[sl, hs]
            k = k_all[sl, hs]
            v = v_ref[sl, hs].astype(F32)
            vext_t = jnp.concatenate([v.T.astype(BF16), ones_rows], axis=0)
            rlb = [jnp.broadcast_to(cols_ref[sl, 2 * ML_H * d + h:2 * ML_H * d + h + 1], (L, tb)) for d in range(2)]
            seqs.append(dict(q=q, k=k, qb=q.astype(BF16), kb=k.astype(BF16), vb=v.astype(BF16), vext_t=vext_t,
                             rlb=rlb, ml16=ml16, em16=em16, mt16=mt16))
            if with_state:
                seqs[s]["c0_t"] = [c0_ref[s, d, h].T.astype(BF16) for d in range(2)]
        for r0 in range(0, L, tb):
            rs = slice(r0, r0 + tb)
            for s, sq in enumerate(seqs):
                s_t = _dot_nt(sq["kb"], sq["qb"][rs])
                if with_state:
                    n0_hi, n0_lo = _split2(n0_ref[s])
                    qb_rs = sq["qb"][rs]
                    qn_rows = _dot_nt(n0_hi, qb_rs) + _dot_nt(n0_lo, qb_rs)
                hout_t = None
                for d in range(2):
                    o8 = 2 * ML_H * d + h
                    ml_row = sq["ml16"][o8:o8 + 1, rs]
                    e_diag = jnp.exp2(jnp.where(masks[d], sq["rlb"][d][rs] - ml_row, NEG_BIG))
                    acc = _dot(sq["vext_t"][:, rs], (s_t[rs] * e_diag).astype(BF16))
                    side = slice(r0 + tb, L) if d else slice(0, r0)
                    if side.stop > side.start:
                        e_side = jnp.exp2(sq["rlb"][d][side] - ml_row)
                        acc = acc + _dot(sq["vext_t"][:, side], (s_t[side] * e_side).astype(BF16))
                    num_t = acc[0:ML_DH]
                    den = acc[ML_DH:ML_DH + 1]
                    if with_state:
                        wp = jnp.exp2(m0_ref[s, o8:o8 + 1, 0:1] * LOG2E - ml_row)
                        den = den + wp * qn_rows[d * ML_H + h:d * ML_H + h + 1]
                        num_t = num_t + wp * _dot_nt(sq["c0_t"][d], sq["qb"][rs])
                    contrib = num_t * (1.0 / jnp.maximum(jnp.abs(den), sq["em16"][o8:o8 + 1, rs]))
                    hout_t = contrib if hout_t is None else hout_t + contrib
                hn_t = hout_t * lax.rsqrt(jnp.mean(hout_t * hout_t, axis=0, keepdims=True) + EPS)
                rows_out = slice(s * L + r0, s * L + r0 + tb)
                y_ref[rows_out, hs] = _sigmoid(o_ref[rows_out, hs].astype(F32)) * (hn_t.T * ng[:, hs])

        if not with_state:
            for s, sq in enumerate(seqs):
                w16 = rows_ref[s, 4 * N_GATE:5 * N_GATE, :]
                k_t = sq["k"].T
                w_hi, w_lo = _split2(w16)
                n_rows = _dot(w_hi, sq["kb"]) + _dot(w_lo, sq["kb"])
                for d in range(2):
                    o8 = 2 * ML_H * d + h
                    end = 0 if d else L - 1
                    cout_ref[s, d, h] = _dot((k_t * w16[o8:o8 + 1, :]).astype(BF16), sq["vb"])
                    nout_ref[s, d * ML_H + h:d * ML_H + h + 1, :] = n_rows[o8:o8 + 1, :]
                    mout_ref[s, d * ML_H + h:d * ML_H + h + 1, :] = jnp.broadcast_to(
                        sq["mt16"][o8:o8 + 1, end:end + 1], (1, LANES))


def _mlstm(proj, g_t, gate_b, conv_w, norm_g, state, prev, *, l, B, L):
    with_state = state is not None
    n_prev = 0 if prev is None else l
    rows, cols = _gate_prep(g_t, gate_b, state[2] if with_state else None, l=l, B=B, L=L)
    S = max(1, ML_STEP_ROWS // L)
    SL = S * L
    col0 = (3 * SC_W + 3 * HY_W) // ML_W
    in_specs = [
        pl.BlockSpec((SL, ML_W), lambda b: (b, col0)),
        pl.BlockSpec((SL, ML_W), lambda b: (b, col0 + 1)),
        pl.BlockSpec((SL, ML_W), lambda b: (b, col0 + 2)),
        pl.BlockSpec((SL, ML_W), lambda b: (b, col0 + 3)),
        pl.BlockSpec((S, rows.shape[1], L), lambda b: (b, 0, 0)),
        pl.BlockSpec((SL, LANES), lambda b: (b, 0)),
        _layer_spec((3, 2 * ML_W), l), _layer_spec((1, ML_W), l),
    ]
    args = [proj, proj, proj, proj, rows, cols, conv_w, norm_g]
    y_shape = jax.ShapeDtypeStruct((B * L, ML_W), F32)
    y_spec = pl.BlockSpec((SL, ML_W), lambda b: (b, 0))
    c_blk, n_blk, m_blk = (2, ML_H, ML_DH, ML_DH), (2 * ML_H, ML_DH), (2 * ML_H, LANES)
    if with_state:
        c0, n0, m0 = state
        in_specs += [
            pl.BlockSpec((S, None) + c_blk, lambda b: (b, l, 0, 0, 0, 0)),
            pl.BlockSpec((S, None) + n_blk, lambda b: (b, l, 0, 0)),
            pl.BlockSpec((S, None, N_GATE, LANES), lambda b: (b, l, 0, 0)),
        ]
        args += [c0, n0, m0]
        out_shape, out_specs = y_shape, y_spec
    else:
        lead = (l + 1,) if n_prev else ()
        if n_prev:
            in_specs += [pl.BlockSpec((S, n_prev) + blk, lambda b, nz=len(blk) + 1: (b,) + (0,) * nz)
                         for blk in (c_blk, n_blk, m_blk)]
            args += list(prev)
        out_shape = [y_shape] + [jax.ShapeDtypeStruct((B,) + lead + blk, F32) for blk in (c_blk, n_blk, m_blk)]
        out_specs = [y_spec] + [pl.BlockSpec((S,) + lead + blk, lambda b, nz=len(lead) + len(blk): (b,) + (0,) * nz)
                                for blk in (c_blk, n_blk, m_blk)]
    return pl.pallas_call(
        functools.partial(_mlstm_kernel, L=L, with_state=with_state, n_prev=n_prev),
        out_shape=out_shape,
        grid=(B // S,),
        in_specs=in_specs,
        out_specs=out_specs,
        compiler_params=pltpu.CompilerParams(
            dimension_semantics=("arbitrary",), vmem_limit_bytes=VMEM_LIMIT),
        name="mlstm",
    )(*args)


def _outmlp_kernel(xc_ref, yac_ref, ybc_ref, xl_ref, yal_ref, ybl_ref, mod_ref, wo_ref, g2_ref, w1_ref, w2_ref,
                   gf_ref, oc_ref, ol_ref, *, final, n_ctx):
    i = pl.program_id(0)
    shared = (mod_ref, wo_ref, g2_ref, w1_ref, w2_ref, gf_ref)

    @pl.when(i < n_ctx)
    def _():
        _outmlp_body(xc_ref, yac_ref, ybc_ref, *shared, oc_ref, final=final)

    @pl.when(i >= n_ctx)
    def _():
        _outmlp_body(xl_ref, yal_ref, ybl_ref, *shared, ol_ref, final=final)


def _outmlp_body(x_ref, ya_ref, yb_ref, mod_ref, wo_ref, g2_ref, w1_ref, w2_ref, gf_ref, o_ref, *, final):
    mod = mod_ref[0]
    g1 = mod[:, 2 * D_MODEL:3 * D_MODEL]
    sh2 = mod[:, 3 * D_MODEL:4 * D_MODEL]
    sc2 = mod[:, 4 * D_MODEL:5 * D_MODEL]
    g2 = mod[:, 5 * D_MODEL:6 * D_MODEL]
    na = SC_W + HY_W
    attn = _dot(ya_ref[...].astype(BF16), wo_ref[0:na, :]) + _dot(yb_ref[...].astype(BF16), wo_ref[na:, :])
    x = x_ref[...] + g1 * attn
    u = ((_rms(x) * g2_ref[...]) * (1.0 + sc2) + sh2).astype(BF16)
    tf = 1024
    acc = jnp.zeros(x.shape, F32)
    for j in range(D_FF // tf):
        hcol = _dot(u, w1_ref[:, j * tf:(j + 1) * tf])
        hcol = jnp.square(jnp.maximum(hcol, 0.0)).astype(BF16)
        acc = acc + _dot(hcol, w2_ref[j * tf:(j + 1) * tf, :])
    x = x + g2 * acc
    if final:
        x = _rms(x) * gf_ref[...]
    o_ref[...] = x


def _outmlp(ctx_ops, lat_ops, mods, w_out, g2, w1, w2, gf, *, l, Ld, final):
    tm = ROW_BLOCK
    Tc, Tl = ctx_ops[0].shape[0], lat_ops[0].shape[0]
    n_ctx, n_lat = Tc // tm, Tl // tm
    ctx, lat = _two_pass_maps(n_ctx)
    token_specs = lambda idx: [pl.BlockSpec((tm, width), lambda i: (idx(i), 0))
                               for width in (D_MODEL, SC_W + HY_W, ML_W)]
    return pl.pallas_call(
        functools.partial(_outmlp_kernel, final=final, n_ctx=n_ctx),
        out_shape=[jax.ShapeDtypeStruct((Tc, D_MODEL), F32), jax.ShapeDtypeStruct((Tl, D_MODEL), F32)],
        grid=(n_ctx + n_lat,),
        in_specs=token_specs(ctx) + token_specs(lat) + [
            _mod_spec(l, Ld, tm, n_ctx),
            _const_spec((D_MODEL, D_MODEL)), _layer_spec((1, D_MODEL), l),
            _const_spec((D_MODEL, D_FF)), _const_spec((D_FF, D_MODEL)), _const_spec((1, D_MODEL)),
        ],
        out_specs=[pl.BlockSpec((tm, D_MODEL), lambda i: (ctx(i), 0)), pl.BlockSpec((tm, D_MODEL), lambda i: (lat(i), 0))],
        compiler_params=pltpu.CompilerParams(
            dimension_semantics=("arbitrary",), vmem_limit_bytes=VMEM_LIMIT),
        name="outproj_mlp",
    )(*ctx_ops, *lat_ops, mods, w_out, g2, w1, w2, gf)


def kernel(x_prompt, x_sample, state_C, state_n, state_m, c, c_ctx, norm1_g, ada_w, ada_b, w_in, sc_conv_w, hy_conv_w, hy_w1, hy_b1, hy_w2, hy_b2, hy_w3, hy_freq, hy_skip, ml_conv_w, ml_gate_b, ml_norm_g, w_out, norm2_g, mlp_w1, mlp_w2, norm_f_g):
    B, L = x_prompt.shape[0], x_prompt.shape[1]
    Bd, Ld = x_sample.shape[0], x_sample.shape[1]

    cond = jnp.concatenate([c_ctx[None, :], c, jnp.zeros((8 - 1 - Bd, D_MODEL), F32)], axis=0)
    mods = _ada(cond, ada_w, ada_b).reshape(DEPTH * 8, 1, 6 * D_MODEL)

    w_in_t = jnp.swapaxes(w_in, 1, 2).astype(BF16)
    w_gate = w_in_t[:, N_MAIN:, :]
    gate_b = jnp.broadcast_to(ml_gate_b[:, :, None], (DEPTH, N_GATE, LANES))
    pad_h = LANES - HY_HID
    w1p = jnp.pad(hy_w1, ((0, 0), (0, LANES - HY_EMB), (0, pad_h)))
    b1p = jnp.pad(hy_b1, ((0, 0), (0, pad_h)))
    w2p = jnp.pad(hy_w2, ((0, 0), (0, pad_h), (0, pad_h)))
    b2p = jnp.pad(hy_b2, ((0, 0), (0, pad_h)))
    w3p = jnp.pad(hy_w3, ((0, 0), (0, pad_h), (0, 0)))
    fp = jnp.pad(hy_freq, ((0, 0), (0, pad_h)))
    pos = jnp.asarray(_pos_embed_np(Ld))
    m0_all = jnp.pad(state_m, ((0, 0), (0, 0), (0, 0), (0, ML_H))).reshape(Bd, DEPTH, N_GATE, 1)
    m0_all = jnp.broadcast_to(m0_all, (Bd, DEPTH, N_GATE, LANES))
    n0_all = state_n.reshape(Bd, DEPTH, 2 * ML_H, ML_DH)
    gf = norm_f_g[None, :]
    g1 = norm1_g[:, None, :]
    g2 = norm2_g[:, None, :]
    ml_ng = ml_norm_g[:, None, :]
    filt_args = (w1p, b1p[:, None, :], w2p, b2p[:, None, :], w3p, fp[:, None, :])
    hre_c, him_c = _filter_spectrum(L, *filt_args)
    hre_s, him_s = _filter_spectrum(Ld, *filt_args)
    state = (state_C, n0_all, m0_all)

    xp = x_prompt.reshape(B * L, D_MODEL)
    xs = x_sample.reshape(Bd * Ld, D_MODEL)
    prev = None
    for l in range(DEPTH):
        final = l == DEPTH - 1

        res = _inproj(xp, xs, pos if l == 0 else None, mods, g1, w_in_t, w_gate, (w_out, mlp_w1, mlp_w2), l=l, Ld=Ld)
        proj_c, g_c, proj_s, g_s = res[:4]
        if l == 0:
            xs = res[4]
        w_out_b, w1_b, w2_b = res[-3:]

        ya_c = _schy(proj_c, sc_conv_w, hy_conv_w, hy_skip, hre_c, him_c, l=l, B=B, L=L)
        yb_c, *states = _mlstm(proj_c, g_c, gate_b, ml_conv_w, ml_ng, None, prev, l=l, B=B, L=L)
        if not final:
            prev = [s.reshape((B, 1) + s.shape[1:]) for s in states] if l == 0 else states

        ya_s = _schy(proj_s, sc_conv_w, hy_conv_w, hy_skip, hre_s, him_s, l=l, B=Bd, L=Ld)
        yb_s = _mlstm(proj_s, g_s, gate_b, ml_conv_w, ml_ng, state, None, l=l, B=Bd, L=Ld)

        xp, xs = _outmlp((xp, ya_c, yb_c), (xs, ya_s, yb_s), mods, w_out_b, g2, w1_b, w2_b, gf,
                         l=l, Ld=Ld, final=final)

    y_prompt = xp.reshape(B, L, D_MODEL)
    y_sample = xs.reshape(Bd, Ld, D_MODEL)
    new_c, new_n, new_m = states
    return (y_prompt, y_sample, new_c, new_n.reshape(B, DEPTH, 2, ML_H, ML_DH),
            new_m[..., 0].reshape(B, DEPTH, 2, ML_H))
```

```python
import functools
import math

import numpy as np
import jax
import jax.numpy as jnp
from jax import lax
from jax.experimental import pallas as pl
from jax.experimental.pallas import tpu as pltpu

F32 = jnp.float32
BF16 = jnp.bfloat16

D_MODEL = 1024
DEPTH = 2
GRID_W = 64
SC_W = 256
HY_W = 256
ML_W = 512
ML_H = 4
ML_DH = ML_W // ML_H
D_FF = 4 * D_MODEL
HY_ORDER = 2
HY_BANDS = 16
HY_EMB = 1 + 2 * HY_BANDS
HY_HID = 64
HY_DECAY_SHORT = 0.3
HY_DECAY_LONG = 1.5
HY_DECAY_TARGET = 1e-2
EPS = 1e-6
N_MAIN = 3 * SC_W + 3 * HY_W + 4 * ML_W
N_GATE = 4 * ML_H
LANES = 128
ROW_BLOCK = 512
ATT_BLOCK = 256
HY_BLOCK = 256
HY_STEP_ROWS = 1024
ML_STEP_ROWS = 512
PROJ_DTYPE = BF16
NEG_BIG = -1e30
LOG2E = math.log2(math.e)
VMEM_LIMIT = 60 * 1024 * 1024


def _dot(a, b):
    return jnp.dot(a, b, preferred_element_type=F32)


def _dot_nt(a, b):
    return lax.dot_general(a, b, (((1,), (1,)), ((), ())), preferred_element_type=F32)


def _split2(a):
    hi = a.astype(BF16)
    lo = (a - hi.astype(F32)).astype(BF16)
    return hi, lo


def _split3(a):
    hi = a.astype(BF16)
    r = a - hi.astype(F32)
    mid = r.astype(BF16)
    lo = (r - mid.astype(F32)).astype(BF16)
    return hi, mid, lo


def _dot3(a, b):
    ah, al = _split2(a)
    bh, bl = _split2(b)
    return _dot(ah, bh) + _dot(al, bh) + _dot(ah, bl)


def _dot3_nt(a, b):
    ah, al = _split2(a)
    bh, bl = _split2(b)
    return _dot_nt(ah, bh) + _dot_nt(al, bh) + _dot_nt(ah, bl)


def _dot3_tab(th, tl, b):
    bh, bl = _split2(b)
    return _dot(th, bh) + _dot(tl, bh) + _dot(th, bl)


def _sigmoid(x):
    return 1.0 / (1.0 + jnp.exp(-x))


def _silu(x):
    return x * _sigmoid(x)


def _log_sigmoid(x):
    return jnp.minimum(x, 0.0) - jnp.log(1.0 + jnp.exp(-jnp.abs(x)))


def _conv3(x, w, seq_len=None):
    R = x.shape[0]
    seq_len = R if seq_len is None else seq_len
    pos = lax.broadcasted_iota(jnp.int32, x.shape, 0) & (seq_len - 1)
    xm = jnp.where(pos == 0, 0.0, pltpu.roll(x, 1, axis=0))
    xp = jnp.where(pos == seq_len - 1, 0.0, pltpu.roll(x, R - 1, axis=0))
    return xm * w[0:1, :] + x * w[1:2, :] + xp * w[2:3, :]


def _rms(x):
    return x * lax.rsqrt(jnp.mean(x * x, axis=-1, keepdims=True) + EPS)


def _scan_max(x, *, reverse):
    n = x.shape[1]
    lane = lax.broadcasted_iota(jnp.int32, x.shape, 1)
    d = 1
    while d < n:
        if reverse:
            shifted = jnp.where(lane < n - d, pltpu.roll(x, n - d, axis=1), NEG_BIG)
        else:
            shifted = jnp.where(lane >= d, pltpu.roll(x, d, axis=1), NEG_BIG)
        x = jnp.maximum(x, shifted)
        d *= 2
    return x


def _hi_lo(a64):
    a = a64.astype(np.float32)
    hi = a.astype(BF16)
    lo = (a - hi.astype(np.float32)).astype(BF16)
    return jnp.asarray(hi), jnp.asarray(lo)


def _const_spec(shape):
    return pl.BlockSpec(shape, lambda *_: tuple(0 for _ in shape), pipeline_mode=pl.Buffered(1))


def _layer_spec(shape, l):
    return pl.BlockSpec((None,) + tuple(shape), lambda *_: (l,) + tuple(0 for _ in shape),
                        pipeline_mode=pl.Buffered(1))


@functools.lru_cache(maxsize=None)
def _dft_tables_np(L):
    k = np.arange(L, dtype=np.int64)
    m1 = ((2 * k[:, None] + 1) * k[None, :]) % (4 * L)
    a1 = np.pi * m1.astype(np.float64) / (2 * L)
    m2 = ((2 * k[:, None] + 1) * (2 * k[None, :] + 1)) % (8 * L)
    a2 = np.pi * m2.astype(np.float64) / (4 * L)
    return np.cos(a1), np.sin(a1), np.cos(a2), np.sin(a2)


@functools.lru_cache(maxsize=None)
def _filter_consts_np(L):
    t_idx = np.arange(L, dtype=np.float64)
    t = t_idx / (L - 1)
    bands = np.arange(1, HY_BANDS + 1, dtype=np.float64)
    ang = 2.0 * math.pi * t_idx[:, None] * bands[None, :] / L
    z = np.zeros((L, LANES), np.float64)
    z[:, 0] = t
    z[:, 1:1 + HY_BANDS] = np.cos(ang)
    z[:, 1 + HY_BANDS:1 + 2 * HY_BANDS] = -np.sin(ang)
    lin = np.linspace(math.log(HY_DECAY_TARGET) / HY_DECAY_LONG,
                      math.log(HY_DECAY_TARGET) / HY_DECAY_SHORT, HY_W).astype(np.float32)
    deltas = np.abs(lin).astype(np.float64)
    decay = np.exp(-t[:, None] * deltas[None, :])
    return z.astype(np.float32), decay.astype(np.float32)


@functools.lru_cache(maxsize=None)
def _pos_embed_np(L):
    rows = L // GRID_W
    r, cidx = np.meshgrid(np.arange(rows, dtype=np.float64), np.arange(GRID_W, dtype=np.float64), indexing="ij")
    r = r.reshape(-1)
    cidx = cidx.reshape(-1)
    quarter = D_MODEL // 4
    omega = 1.0 / (10000.0 ** (np.arange(quarter, dtype=np.float64) / quarter))
    ar = r[:, None] * omega[None, :]
    ac = cidx[:, None] * omega[None, :]
    return np.concatenate([np.sin(ar), np.cos(ar), np.sin(ac), np.cos(ac)], axis=-1).astype(np.float32)


def _ada_kernel(c_ref, w_ref, b_ref, o_ref):
    sc = _silu(c_ref[...])
    o_ref[0] = _dot3(sc, w_ref[0]) + b_ref[0]


def _ada(cond, ada_w, ada_b):
    tn = 1536
    nd = 6 * D_MODEL
    return pl.pallas_call(
        _ada_kernel,
        out_shape=jax.ShapeDtypeStruct((DEPTH, 8, nd), F32),
        grid=(DEPTH, nd // tn),
        in_specs=[
            pl.BlockSpec((8, D_MODEL), lambda l, j: (0, 0)),
            pl.BlockSpec((1, D_MODEL, tn), lambda l, j: (l, 0, j)),
            pl.BlockSpec((1, 1, tn), lambda l, j: (l, 0, j)),
        ],
        out_specs=pl.BlockSpec((1, 8, tn), lambda l, j: (l, 0, j)),
        compiler_params=pltpu.CompilerParams(
            dimension_semantics=("arbitrary", "arbitrary"), vmem_limit_bytes=VMEM_LIMIT),
        name="ada_mod",
    )(cond, ada_w, ada_b.reshape(DEPTH, 1, nd))


def _inproj_body(x, mod_ref, g_ref, w_ref, wg_ref, proj_ref, gate_ref):
    mod = mod_ref[0]
    sh1 = mod[:, 0:D_MODEL]
    sc1 = mod[:, D_MODEL:2 * D_MODEL]
    u = (_rms(x) * g_ref[...]) * (1.0 + sc1) + sh1
    ub = u.astype(BF16)
    tn = 512
    for j in range(N_MAIN // tn):
        proj_ref[:, j * tn:(j + 1) * tn] = _dot_nt(ub, w_ref[j * tn:(j + 1) * tn, :]).astype(PROJ_DTYPE)
    gate_ref[...] = _dot_nt(wg_ref[...], ub)


def _inproj_kernel(*refs, with_pos, n_cast, n_ctx):
    it = iter(refs)
    xc_ref, xl_ref = next(it), next(it)
    pos_ref = next(it) if with_pos else None
    mod_ref, g_ref, w_ref, wg_ref = (next(it) for _ in range(4))
    cast_in = [next(it) for _ in range(n_cast)]
    projc_ref, gatec_ref, projl_ref, gatel_ref = (next(it) for _ in range(4))
    xo_ref = next(it) if with_pos else None
    cast_out = [next(it) for _ in range(n_cast)]
    i = pl.program_id(0)

    @pl.when(i < n_ctx)
    def _():
        for src, dst in zip(cast_in, cast_out):
            dst[...] = src[...].astype(BF16)
        _inproj_body(xc_ref[...], mod_ref, g_ref, w_ref, wg_ref, projc_ref, gatec_ref)

    @pl.when(i >= n_ctx)
    def _():
        x = xl_ref[...]
        if with_pos:
            x = x + pos_ref[...]
            xo_ref[...] = x
        _inproj_body(x, mod_ref, g_ref, w_ref, wg_ref, projl_ref, gatel_ref)


def _two_pass_maps(n_ctx):
    ctx = lambda i: jnp.minimum(i, n_ctx - 1)
    lat = lambda i: jnp.maximum(i - n_ctx, 0)
    return ctx, lat


def _mod_spec(l, Ld, tm, n_ctx):
    return pl.BlockSpec((1, 1, 6 * D_MODEL),
                        lambda i: (l * 8 + jnp.where(i < n_ctx, 0, 1 + ((i - n_ctx) * tm) // Ld), 0, 0))


def _inproj(xc, xl, pos, mods, g, w_in_t, w_gate, cast=(), *, l, Ld):
    tm = ROW_BLOCK
    Tc, Tl = xc.shape[0], xl.shape[0]
    n_ctx, n_lat = Tc // tm, Tl // tm
    per_seq = Ld // tm
    ctx, lat = _two_pass_maps(n_ctx)
    with_pos = pos is not None
    in_specs = [pl.BlockSpec((tm, D_MODEL), lambda i: (ctx(i), 0)), pl.BlockSpec((tm, D_MODEL), lambda i: (lat(i), 0))]
    args = [xc, xl]
    if with_pos:
        in_specs.append(pl.BlockSpec((tm, D_MODEL), lambda i: (lat(i) % per_seq, 0)))
        args.append(pos)
    in_specs += [
        _mod_spec(l, Ld, tm, n_ctx),
        _layer_spec((1, D_MODEL), l),
        _layer_spec((N_MAIN, D_MODEL), l),
        _layer_spec((N_GATE, D_MODEL), l),
    ]
    args += [mods, g, w_in_t, w_gate]
    out_shape, out_specs = [], []
    for T, idx in ((Tc, ctx), (Tl, lat)):
        out_shape += [jax.ShapeDtypeStruct((T, N_MAIN), PROJ_DTYPE), jax.ShapeDtypeStruct((N_GATE, T), F32)]
        out_specs += [pl.BlockSpec((tm, N_MAIN), lambda i, idx=idx: (idx(i), 0)),
                      pl.BlockSpec((N_GATE, tm), lambda i, idx=idx: (0, idx(i)))]
    if with_pos:
        out_shape.append(jax.ShapeDtypeStruct((Tl, D_MODEL), F32))
        out_specs.append(pl.BlockSpec((tm, D_MODEL), lambda i: (lat(i), 0)))
    for w in cast:
        _, rows, cols = w.shape
        in_specs.append(pl.BlockSpec((None, rows // n_ctx, cols), lambda i: (l, ctx(i), 0)))
        args.append(w)
        out_shape.append(jax.ShapeDtypeStruct((rows, cols), BF16))
        out_specs.append(pl.BlockSpec((rows // n_ctx, cols), lambda i: (ctx(i), 0)))
    return pl.pallas_call(
        functools.partial(_inproj_kernel, with_pos=with_pos, n_cast=len(cast), n_ctx=n_ctx),
        out_shape=out_shape,
        grid=(n_ctx + n_lat,),
        in_specs=in_specs,
        out_specs=out_specs,
        compiler_params=pltpu.CompilerParams(
            dimension_semantics=("arbitrary",), vmem_limit_bytes=VMEM_LIMIT),
        name="inproj",
    )(*args)


def _filter_kernel(z_ref, dec_ref, w1_ref, b1_ref, w2_ref, b2_ref, w3_ref, f_ref,
                   ch_ref, cl_ref, sh_ref, sl_ref, hre_ref, him_ref, *, L):
    T = HY_BLOCK
    P = L // T
    f = f_ref[...]
    hdn = jnp.sin(f * (_dot3(z_ref[...], w1_ref[...]) + b1_ref[...]))
    hdn = jnp.sin(f * (_dot3(hdn, w2_ref[...]) + b2_ref[...]))
    h = _dot3(hdn, w3_ref[...])
    dec = dec_ref[...]
    row = lax.broadcasted_iota(jnp.int32, (L, HY_W), 0)
    halves = []
    for o in range(HY_ORDER):
        hf = h[:, (2 * o) * HY_W:(2 * o + 1) * HY_W] * dec
        hb = jnp.where(row == 0, 0.0, h[:, (2 * o + 1) * HY_W:(2 * o + 2) * HY_W] * dec)
        nrm = jnp.sum(jnp.abs(hf), axis=0, keepdims=True) + jnp.sum(jnp.abs(hb), axis=0, keepdims=True)
        inv = 1.0 / nrm
        halves.append((hf * inv, hb * inv))
    blocks = [half[e * T:(e + 1) * T] for pair in halves for half in pair for e in range(P)]
    x = jnp.concatenate(blocks, axis=1)
    xre = _dot3_tab(ch_ref[...], cl_ref[...], x)
    xim = -_dot3_tab(sh_ref[...], sl_ref[...], x)
    sgn = jnp.where((lax.broadcasted_iota(jnp.int32, (T, HY_W), 0) & 1) == 0, 1.0, -1.0)
    for o in range(HY_ORDER):
        def half_spec(which, e):
            j = (o * 2 + which) * P + e
            cs = slice(j * HY_W, (j + 1) * HY_W)
            return xre[:, cs], xim[:, cs], halves[o][which][e * T:e * T + 1]
        for d in range(-(P - 1), P):
            if d == 0:
                (fre, fim, _), (gre, gim, _) = half_spec(0, 0), half_spec(1, 0)
                re, im = fre + gre, fim - gim
            elif d > 0:
                (fre, fim, _), (pre, pim, p0) = half_spec(0, d), half_spec(0, d - 1)
                re, im = fre - sgn * pim, fim + sgn * (pre - p0)
            else:
                (gre, gim, _), (pre, pim, p0) = half_spec(1, -d), half_spec(1, -d - 1)
                re, im = gre - sgn * pim, -gim - sgn * (pre - p0)
            hre_ref[d + P - 1, :, o * HY_W:(o + 1) * HY_W] = re
            him_ref[d + P - 1, :, o * HY_W:(o + 1) * HY_W] = im


def _filter_spectrum(L, w1p, b1p, w2p, b2p, w3p, fp):
    T = HY_BLOCK
    z_np, dec_np = _filter_consts_np(L)
    c1, s1, _, _ = _dft_tables_np(T)
    ch, cl = _hi_lo(c1)
    sh, sl = _hi_lo(s1)
    n = HY_ORDER * HY_W
    nd = 2 * (L // T) - 1
    const = _const_spec
    per_layer = lambda shape: pl.BlockSpec((None,) + shape, lambda l: (l,) + tuple(0 for _ in shape))
    return pl.pallas_call(
        functools.partial(_filter_kernel, L=L),
        out_shape=[jax.ShapeDtypeStruct((DEPTH, nd, T, n), F32), jax.ShapeDtypeStruct((DEPTH, nd, T, n), F32)],
        grid=(DEPTH,),
        in_specs=[const((L, LANES)), const((L, HY_W)), per_layer((LANES, LANES)), per_layer((1, LANES)),
                  per_layer((LANES, LANES)), per_layer((1, LANES)), per_layer((LANES, 2 * n)), per_layer((1, LANES)),
                  const((T, T)), const((T, T)), const((T, T)), const((T, T))],
        out_specs=[per_layer((nd, T, n)), per_layer((nd, T, n))],
        compiler_params=pltpu.CompilerParams(
            dimension_semantics=("arbitrary",), vmem_limit_bytes=VMEM_LIMIT),
        name="hyena_filter",
    )(jnp.asarray(z_np), jnp.asarray(dec_np), w1p, b1p, w2p, b2p, w3p, fp, ch, cl, sh, sl)


def _schy_kernel(sc_ref, hy_ref, scw_ref, hyw_ref, skip_ref, fwd_ref, inv_ref, hre_ref, him_ref, out_ref, *, L):
    T = HY_BLOCK
    P = L // T
    NB = sc_ref.shape[0] // T
    sc = sc_ref[...].astype(F32)
    b_g = sc[:, 0:SC_W]
    c_g = sc[:, SC_W:2 * SC_W]
    xin = sc[:, 2 * SC_W:3 * SC_W]
    out_ref[:, 0:SC_W] = b_g * _conv3(c_g * xin, scw_ref[...], L)

    u3 = _conv3(hy_ref[...].astype(F32), hyw_ref[...], L)
    z = u3[:, 0:HY_W]
    gates = (u3[:, HY_W:2 * HY_W], u3[:, 2 * HY_W:3 * HY_W])
    fwd, inv = fwd_ref[...], inv_ref[...]
    lanes = lambda j: slice(j * HY_W, (j + 1) * HY_W)
    for o in range(HY_ORDER):
        zcat = jnp.concatenate([z[j * T:(j + 1) * T] for j in range(NB)], axis=1)
        spec = _dot(fwd, zcat.astype(BF16))
        yre, yim = [], []
        for bi in range(NB):
            s0, i = bi - bi % P, bi % P
            re = im = None
            for j in range(P):
                d = i - j + P - 1
                hre = hre_ref[d, :, lanes(o)]
                him = him_ref[d, :, lanes(o)]
                zc = spec[0:T, lanes(s0 + j)]
                zs = spec[T:2 * T, lanes(s0 + j)]
                pre = zc * hre + zs * him
                pim = zc * him - zs * hre
                re = pre if re is None else re + pre
                im = pim if im is None else im + pim
            yre.append(re)
            yim.append(im)
        prod = jnp.concatenate([jnp.concatenate(yre, axis=1), jnp.concatenate(yim, axis=1)], axis=0)
        ycat = _dot(inv, prod.astype(BF16)) * (1.0 / T)
        y = jnp.concatenate([ycat[:, lanes(bi)] for bi in range(NB)], axis=0)
        z = gates[o] * (y + skip_ref[o:o + 1, :] * z)
    out_ref[:, SC_W:SC_W + HY_W] = z


def _schy(proj, sc_w, hy_w, skip, hre, him, *, l, B, L):
    T = HY_BLOCK
    _, _, c2, s2 = _dft_tables_np(T)
    fwd = jnp.asarray(np.concatenate([c2, s2], axis=0).astype(np.float32).astype(BF16))
    inv = jnp.asarray(np.concatenate([c2, -s2], axis=1).astype(np.float32).astype(BF16))
    n = HY_ORDER * HY_W
    nd = 2 * (L // T) - 1
    rows = max(L, HY_STEP_ROWS)
    return pl.pallas_call(
        functools.partial(_schy_kernel, L=L),
        out_shape=jax.ShapeDtypeStruct((B * L, SC_W + HY_W), F32),
        grid=(B * L // rows,),
        in_specs=[
            pl.BlockSpec((rows, 3 * SC_W), lambda b: (b, 0)),
            pl.BlockSpec((rows, 3 * HY_W), lambda b: (b, 1)),
            _layer_spec((3, SC_W), l), _layer_spec((3, 3 * HY_W), l), _layer_spec((HY_ORDER, HY_W), l),
            _const_spec((2 * T, T)), _const_spec((T, 2 * T)),
            _layer_spec((nd, T, n), l), _layer_spec((nd, T, n), l),
        ],
        out_specs=pl.BlockSpec((rows, SC_W + HY_W), lambda b: (b, 0)),
        compiler_params=pltpu.CompilerParams(
            dimension_semantics=("arbitrary",), vmem_limit_bytes=VMEM_LIMIT),
        name="sconv_hyena",
    )(proj, proj, sc_w, hy_w, skip, fwd, inv, hre, him)


def _gate_prep_kernel(*refs, B, L, with_state):
    if with_state:
        g_ref, gb_ref, m0_ref, rows_ref, cols_ref = refs
    else:
        g_ref, gb_ref, rows_ref, cols_ref = refs
    R = B * N_GATE
    gb = gb_ref[:, 0:1]
    x = jnp.concatenate([g_ref[:, b * L:(b + 1) * L] + gb for b in range(B)], axis=0)
    rr = lax.broadcasted_iota(jnp.int32, (R, L), 0) & (N_GATE - 1)
    is_f = ((rr >= ML_H) & (rr < 2 * ML_H)) | (rr >= 3 * ML_H)
    fwd_rows = rr < 2 * ML_H
    pk = jnp.where(is_f, _log_sigmoid(x), x)
    r_i = lax.broadcasted_iota(jnp.int32, (L, L), 0)
    c_i = lax.broadcasted_iota(jnp.int32, (L, L), 1)
    triu = jnp.where(r_i <= c_i, 1.0, 0.0).astype(BF16)
    cs = _dot(jnp.concatenate(_split3(pk), axis=0), triu)
    cum = cs[0:R] + cs[R:2 * R] + cs[2 * R:3 * R]
    suf = cum[:, L - 1:L] - cum + pk
    bsum = pltpu.roll(jnp.where(fwd_rows, cum, suf), R - ML_H, axis=0)
    r = pk - bsum
    cm = jnp.where(fwd_rows, _scan_max(r, reverse=False), _scan_max(r, reverse=True))
    if with_state:
        m0 = jnp.concatenate([m0_ref[b][:, 0:1] for b in range(B)], axis=0)
        m = jnp.maximum(m0, cm)
    else:
        m = jnp.maximum(cm, 0.0)
    rl = r * LOG2E
    ml = m * LOG2E
    mt = bsum + m
    pieces = [rl, ml, jnp.exp(-mt), mt]
    if not with_state:
        pieces.append(jnp.exp2(rl - jnp.where(fwd_rows, ml[:, L - 1:L], ml[:, 0:1])))
    pad = jnp.zeros((LANES - N_GATE, L), F32)
    for b in range(B):
        bs = slice(b * N_GATE, (b + 1) * N_GATE)
        for j, piece in enumerate(pieces):
            rows_ref[b, j * N_GATE:(j + 1) * N_GATE, :] = piece[bs]
        cols_ref[b * L:(b + 1) * L, :] = jnp.concatenate([rl[bs], pad], axis=0).T


def _gate_prep(g_t, gate_b, m0, *, l, B, L):
    with_state = m0 is not None
    n_rows = (4 if with_state else 5) * N_GATE
    full = lambda shape: pl.BlockSpec(shape, lambda i: tuple(0 for _ in shape))
    args = [g_t, gate_b]
    in_specs = [full((N_GATE, B * L)), pl.BlockSpec((None, N_GATE, LANES), lambda i: (l, 0, 0))]
    if with_state:
        args.append(m0)
        in_specs.append(pl.BlockSpec((B, None, N_GATE, LANES), lambda i: (0, l, 0, 0)))
    return pl.pallas_call(
        functools.partial(_gate_prep_kernel, B=B, L=L, with_state=with_state),
        out_shape=[jax.ShapeDtypeStruct((B, n_rows, L), F32), jax.ShapeDtypeStruct((B * L, LANES), F32)],
        grid=(1,),
        in_specs=in_specs,
        out_specs=[full((B, n_rows, L)), full((B * L, LANES))],
        compiler_params=pltpu.CompilerParams(
            dimension_semantics=("arbitrary",), vmem_limit_bytes=VMEM_LIMIT),
        name="gate_prep",
    )(*args)


def _mlstm_kernel(*refs, L, with_state, n_prev):
    if with_state:
        (q_ref, k_ref, v_ref, o_ref, rows_ref, cols_ref, cw_ref, ng_ref,
         c0_ref, n0_ref, m0_ref, y_ref) = refs
    elif n_prev:
        (q_ref, k_ref, v_ref, o_ref, rows_ref, cols_ref, cw_ref, ng_ref,
         cprev_ref, nprev_ref, mprev_ref, y_ref, cout_ref, nout_ref, mout_ref) = refs
        cout_ref[:, 0:n_prev] = cprev_ref[...]
        nout_ref[:, 0:n_prev] = nprev_ref[...]
        mout_ref[:, 0:n_prev] = mprev_ref[...]
        cout_ref, nout_ref, mout_ref = (r.at[:, n_prev] for r in (cout_ref, nout_ref, mout_ref))
    else:
        (q_ref, k_ref, v_ref, o_ref, rows_ref, cols_ref, cw_ref, ng_ref,
         y_ref, cout_ref, nout_ref, mout_ref) = refs

    S = rows_ref.shape[0]
    tb = min(L, ATT_BLOCK)
    cw = cw_ref[...]
    q_all = _silu(_conv3(q_ref[...].astype(F32), cw[:, 0:ML_W], L))
    k_all = _silu(_conv3(k_ref[...].astype(F32), cw[:, ML_W:2 * ML_W], L)) * (ML_DH ** -0.5)

    s_loc = lax.broadcasted_iota(jnp.int32, (tb, tb), 0)
    t_loc = lax.broadcasted_iota(jnp.int32, (tb, tb), 1)
    masks = (s_loc <= t_loc, s_loc >= t_loc)
    ones_rows = jnp.ones((2 * 8, L), BF16)
    ng = ng_ref[...]
    for h in range(ML_H):
        hs = slice(h * ML_DH, (h + 1) * ML_DH)
        seqs = []
        for s in range(S):
            sl = slice(s * L, (s + 1) * L)
            rl16, ml16, em16, mt16 = (rows_ref[s, j * N_GATE:(j + 1) * N_GATE, :] for j in range(4))
            q = q_all[sl, hs]
            k = k_all[sl, hs]
            v = v_ref[sl, hs].astype(F32)
            vext_t = jnp.concatenate([v.T.astype(BF16), ones_rows], axis=0)
            rlb = [jnp.broadcast_to(cols_ref[sl, 2 * ML_H * d + h:2 * ML_H * d + h + 1], (L, tb)) for d in range(2)]
            seqs.append(dict(q=q, k=k, qb=q.astype(BF16), kb=k.astype(BF16), vb=v.astype(BF16), vext_t=vext_t,
                             rlb=rlb, ml16=ml16, em16=em16, mt16=mt16))
            if with_state:
                seqs[s]["c0_t"] = [c0_ref[s, d, h].T.astype(BF16) for d in range(2)]
                seqs[s]["n0"] = _split2(n0_ref[s])
        for r0 in range(0, L, tb):
            rs = slice(r0, r0 + tb)
            for s, sq in enumerate(seqs):
                s_t = _dot_nt(sq["kb"], sq["qb"][rs])
                if with_state:
                    qb_rs = sq["qb"][rs]
                    qn_rows = _dot_nt(sq["n0"][0], qb_rs) + _dot_nt(sq["n0"][1], qb_rs)
                hout_t = None
                for d in range(2):
                    o8 = 2 * ML_H * d + h
                    ml_row = sq["ml16"][o8:o8 + 1, rs]
                    e_diag = jnp.exp2(jnp.where(masks[d], sq["rlb"][d][rs] - ml_row, NEG_BIG))
                    acc = _dot(sq["vext_t"][:, rs], (s_t[rs] * e_diag).astype(BF16))
                    side = slice(r0 + tb, L) if d else slice(0, r0)
                    if side.stop > side.start:
                        e_side = jnp.exp2(sq["rlb"][d][side] - ml_row)
                        acc = acc + _dot(sq["vext_t"][:, side], (s_t[side] * e_side).astype(BF16))
                    num_t = acc[0:ML_DH]
                    den = acc[ML_DH:ML_DH + 1]
                    if with_state:
                        wp = jnp.exp2(m0_ref[s, o8:o8 + 1, 0:1] * LOG2E - ml_row)
                        den = den + wp * qn_rows[d * ML_H + h:d * ML_H + h + 1]
                        num_t = num_t + wp * _dot_nt(sq["c0_t"][d], sq["qb"][rs])
                    contrib = num_t * (1.0 / jnp.maximum(jnp.abs(den), sq["em16"][o8:o8 + 1, rs]))
                    hout_t = contrib if hout_t is None else hout_t + contrib
                hn_t = hout_t * lax.rsqrt(jnp.mean(hout_t * hout_t, axis=0, keepdims=True) + EPS)
                rows_out = slice(s * L + r0, s * L + r0 + tb)
                y_ref[rows_out, hs] = _sigmoid(o_ref[rows_out, hs].astype(F32)) * (hn_t.T * ng[:, hs])

        if not with_state:
            for s, sq in enumerate(seqs):
                w16 = rows_ref[s, 4 * N_GATE:5 * N_GATE, :]
                k_t = sq["k"].T
                w_hi, w_lo = _split2(w16)
                n_rows = _dot(w_hi, sq["kb"]) + _dot(w_lo, sq["kb"])
                for d in range(2):
                    o8 = 2 * ML_H * d + h
                    end = 0 if d else L - 1
                    cout_ref[s, d, h] = _dot((k_t * w16[o8:o8 + 1, :]).astype(BF16), sq["vb"])
                    nout_ref[s, d * ML_H + h:d * ML_H + h + 1, :] = n_rows[o8:o8 + 1, :]
                    mout_ref[s, d * ML_H + h:d * ML_H + h + 1, :] = jnp.broadcast_to(
                        sq["mt16"][o8:o8 + 1, end:end + 1], (1, LANES))


def _mlstm(proj, g_t, gate_b, conv_w, norm_g, state, prev, *, l, B, L):
    with_state = state is not None
    n_prev = 0 if prev is None else l
    rows, cols = _gate_prep(g_t, gate_b, state[2] if with_state else None, l=l, B=B, L=L)
    S = max(1, ML_STEP_ROWS // L)
    SL = S * L
    col0 = (3 * SC_W + 3 * HY_W) // ML_W
    in_specs = [
        pl.BlockSpec((SL, ML_W), lambda b: (b, col0)),
        pl.BlockSpec((SL, ML_W), lambda b: (b, col0 + 1)),
        pl.BlockSpec((SL, ML_W), lambda b: (b, col0 + 2)),
        pl.BlockSpec((SL, ML_W), lambda b: (b, col0 + 3)),
        pl.BlockSpec((S, rows.shape[1], L), lambda b: (b, 0, 0)),
        pl.BlockSpec((SL, LANES), lambda b: (b, 0)),
        _layer_spec((3, 2 * ML_W), l), _layer_spec((1, ML_W), l),
    ]
    args = [proj, proj, proj, proj, rows, cols, conv_w, norm_g]
    y_shape = jax.ShapeDtypeStruct((B * L, ML_W), F32)
    y_spec = pl.BlockSpec((SL, ML_W), lambda b: (b, 0))
    c_blk, n_blk, m_blk = (2, ML_H, ML_DH, ML_DH), (2 * ML_H, ML_DH), (2 * ML_H, LANES)
    if with_state:
        c0, n0, m0 = state
        in_specs += [
            pl.BlockSpec((S, None) + c_blk, lambda b: (b, l, 0, 0, 0, 0)),
            pl.BlockSpec((S, None) + n_blk, lambda b: (b, l, 0, 0)),
            pl.BlockSpec((S, None, N_GATE, LANES), lambda b: (b, l, 0, 0)),
        ]
        args += [c0, n0, m0]
        out_shape, out_specs = y_shape, y_spec
    else:
        lead = (l + 1,) if n_prev else ()
        if n_prev:
            in_specs += [pl.BlockSpec((S, n_prev) + blk, lambda b, nz=len(blk) + 1: (b,) + (0,) * nz)
                         for blk in (c_blk, n_blk, m_blk)]
            args += list(prev)
        out_shape = [y_shape] + [jax.ShapeDtypeStruct((B,) + lead + blk, F32) for blk in (c_blk, n_blk, m_blk)]
        out_specs = [y_spec] + [pl.BlockSpec((S,) + lead + blk, lambda b, nz=len(lead) + len(blk): (b,) + (0,) * nz)
                                for blk in (c_blk, n_blk, m_blk)]
    return pl.pallas_call(
        functools.partial(_mlstm_kernel, L=L, with_state=with_state, n_prev=n_prev),
        out_shape=out_shape,
        grid=(B // S,),
        in_specs=in_specs,
        out_specs=out_specs,
        compiler_params=pltpu.CompilerParams(
            dimension_semantics=("arbitrary",), vmem_limit_bytes=VMEM_LIMIT),
        name="mlstm",
    )(*args)


def _outmlp_kernel(xc_ref, yac_ref, ybc_ref, xl_ref, yal_ref, ybl_ref, mod_ref, wo_ref, g2_ref, w1_ref, w2_ref,
                   gf_ref, oc_ref, ol_ref, *, final, n_ctx):
    i = pl.program_id(0)
    shared = (mod_ref, wo_ref, g2_ref, w1_ref, w2_ref, gf_ref)

    @pl.when(i < n_ctx)
    def _():
        _outmlp_body(xc_ref, yac_ref, ybc_ref, *shared, oc_ref, final=final)

    @pl.when(i >= n_ctx)
    def _():
        _outmlp_body(xl_ref, yal_ref, ybl_ref, *shared, ol_ref, final=final)


def _outmlp_body(x_ref, ya_ref, yb_ref, mod_ref, wo_ref, g2_ref, w1_ref, w2_ref, gf_ref, o_ref, *, final):
    mod = mod_ref[0]
    g1 = mod[:, 2 * D_MODEL:3 * D_MODEL]
    sh2 = mod[:, 3 * D_MODEL:4 * D_MODEL]
    sc2 = mod[:, 4 * D_MODEL:5 * D_MODEL]
    g2 = mod[:, 5 * D_MODEL:6 * D_MODEL]
    na = SC_W + HY_W
    attn = _dot(ya_ref[...].astype(BF16), wo_ref[0:na, :]) + _dot(yb_ref[...].astype(BF16), wo_ref[na:, :])
    x = x_ref[...] + g1 * attn
    u = ((_rms(x) * g2_ref[...]) * (1.0 + sc2) + sh2).astype(BF16)
    tf = 1024
    acc = jnp.zeros(x.shape, F32)
    for j in range(D_FF // tf):
        hcol = _dot(u, w1_ref[:, j * tf:(j + 1) * tf])
        hcol = jnp.square(jnp.maximum(hcol, 0.0)).astype(BF16)
        acc = acc + _dot(hcol, w2_ref[j * tf:(j + 1) * tf, :])
    x = x + g2 * acc
    if final:
        x = _rms(x) * gf_ref[...]
    o_ref[...] = x


def _outmlp(ctx_ops, lat_ops, mods, w_out, g2, w1, w2, gf, *, l, Ld, final):
    tm = ROW_BLOCK
    Tc, Tl = ctx_ops[0].shape[0], lat_ops[0].shape[0]
    n_ctx, n_lat = Tc // tm, Tl // tm
    ctx, lat = _two_pass_maps(n_ctx)
    token_specs = lambda idx: [pl.BlockSpec((tm, width), lambda i: (idx(i), 0))
                               for width in (D_MODEL, SC_W + HY_W, ML_W)]
    return pl.pallas_call(
        functools.partial(_outmlp_kernel, final=final, n_ctx=n_ctx),
        out_shape=[jax.ShapeDtypeStruct((Tc, D_MODEL), F32), jax.ShapeDtypeStruct((Tl, D_MODEL), F32)],
        grid=(n_ctx + n_lat,),
        in_specs=token_specs(ctx) + token_specs(lat) + [
            _mod_spec(l, Ld, tm, n_ctx),
            _const_spec((D_MODEL, D_MODEL)), _layer_spec((1, D_MODEL), l),
            _const_spec((D_MODEL, D_FF)), _const_spec((D_FF, D_MODEL)), _const_spec((1, D_MODEL)),
        ],
        out_specs=[pl.BlockSpec((tm, D_MODEL), lambda i: (ctx(i), 0)), pl.BlockSpec((tm, D_MODEL), lambda i: (lat(i), 0))],
        compiler_params=pltpu.CompilerParams(
            dimension_semantics=("arbitrary",), vmem_limit_bytes=VMEM_LIMIT),
        name="outproj_mlp",
    )(*ctx_ops, *lat_ops, mods, w_out, g2, w1, w2, gf)


def kernel(x_prompt, x_sample, state_C, state_n, state_m, c, c_ctx, norm1_g, ada_w, ada_b, w_in, sc_conv_w, hy_conv_w, hy_w1, hy_b1, hy_w2, hy_b2, hy_w3, hy_freq, hy_skip, ml_conv_w, ml_gate_b, ml_norm_g, w_out, norm2_g, mlp_w1, mlp_w2, norm_f_g):
    B, L = x_prompt.shape[0], x_prompt.shape[1]
    Bd, Ld = x_sample.shape[0], x_sample.shape[1]

    cond = jnp.concatenate([c_ctx[None, :], c, jnp.zeros((8 - 1 - Bd, D_MODEL), F32)], axis=0)
    mods = _ada(cond, ada_w, ada_b).reshape(DEPTH * 8, 1, 6 * D_MODEL)

    w_in_t = jnp.swapaxes(w_in, 1, 2).astype(BF16)
    w_gate = w_in_t[:, N_MAIN:, :]
    gate_b = jnp.broadcast_to(ml_gate_b[:, :, None], (DEPTH, N_GATE, LANES))
    pad_h = LANES - HY_HID
    w1p = jnp.pad(hy_w1, ((0, 0), (0, LANES - HY_EMB), (0, pad_h)))
    b1p = jnp.pad(hy_b1, ((0, 0), (0, pad_h)))
    w2p = jnp.pad(hy_w2, ((0, 0), (0, pad_h), (0, pad_h)))
    b2p = jnp.pad(hy_b2, ((0, 0), (0, pad_h)))
    w3p = jnp.pad(hy_w3, ((0, 0), (0, pad_h), (0, 0)))
    fp = jnp.pad(hy_freq, ((0, 0), (0, pad_h)))
    pos = jnp.asarray(_pos_embed_np(Ld))
    m0_all = jnp.pad(state_m, ((0, 0), (0, 0), (0, 0), (0, ML_H))).reshape(Bd, DEPTH, N_GATE, 1)
    m0_all = jnp.broadcast_to(m0_all, (Bd, DEPTH, N_GATE, LANES))
    n0_all = state_n.reshape(Bd, DEPTH, 2 * ML_H, ML_DH)
    gf = norm_f_g[None, :]
    g1 = norm1_g[:, None, :]
    g2 = norm2_g[:, None, :]
    ml_ng = ml_norm_g[:, None, :]
    filt_args = (w1p, b1p[:, None, :], w2p, b2p[:, None, :], w3p, fp[:, None, :])
    hre_c, him_c = _filter_spectrum(L, *filt_args)
    hre_s, him_s = _filter_spectrum(Ld, *filt_args)
    state = (state_C, n0_all, m0_all)

    xp = x_prompt.reshape(B * L, D_MODEL)
    xs = x_sample.reshape(Bd * Ld, D_MODEL)
    prev = None
    for l in range(DEPTH):
        final = l == DEPTH - 1

        res = _inproj(xp, xs, pos if l == 0 else None, mods, g1, w_in_t, w_gate, (w_out, mlp_w1, mlp_w2), l=l, Ld=Ld)
        proj_c, g_c, proj_s, g_s = res[:4]
        if l == 0:
            xs = res[4]
        w_out_b, w1_b, w2_b = res[-3:]

        ya_c = _schy(proj_c, sc_conv_w, hy_conv_w, hy_skip, hre_c, him_c, l=l, B=B, L=L)
        yb_c, *states = _mlstm(proj_c, g_c, gate_b, ml_conv_w, ml_ng, None, prev, l=l, B=B, L=L)
        if not final:
            prev = [s.reshape((B, 1) + s.shape[1:]) for s in states] if l == 0 else states

        ya_s = _schy(proj_s, sc_conv_w, hy_conv_w, hy_skip, hre_s, him_s, l=l, B=Bd, L=Ld)
        yb_s = _mlstm(proj_s, g_s, gate_b, ml_conv_w, ml_ng, state, None, l=l, B=Bd, L=Ld)

        xp, xs = _outmlp((xp, ya_c, yb_c), (xs, ya_s, yb_s), mods, w_out_b, g2, w1_b, w2_b, gf,
                         l=l, Ld=Ld, final=final)

    y_prompt = xp.reshape(B, L, D_MODEL)
    y_sample = xs.reshape(Bd, Ld, D_MODEL)
    new_c, new_n, new_m = states
    return (y_prompt, y_sample, new_c, new_n.reshape(B, DEPTH, 2, ML_H, ML_DH),
            new_m[..., 0].reshape(B, DEPTH, 2, ML_H))
```

```python
import functools
import math

import numpy as np
import jax
import jax.numpy as jnp
from jax import lax
from jax.experimental import pallas as pl
from jax.experimental.pallas import tpu as pltpu

F32 = jnp.float32
BF16 = jnp.bfloat16

D_MODEL = 1024
DEPTH = 2
GRID_W = 64
SC_W = 256
HY_W = 256
ML_W = 512
ML_H = 4
ML_DH = ML_W // ML_H
D_FF = 4 * D_MODEL
HY_ORDER = 2
HY_BANDS = 16
HY_EMB = 1 + 2 * HY_BANDS
HY_HID = 64
HY_DECAY_SHORT = 0.3
HY_DECAY_LONG = 1.5
HY_DECAY_TARGET = 1e-2
EPS = 1e-6
N_MAIN = 3 * SC_W + 3 * HY_W + 4 * ML_W
N_GATE = 4 * ML_H
LANES = 128
ROW_BLOCK = 512
ATT_BLOCK = 256
HY_BLOCK = 256
HY_STEP_ROWS = 1024
ML_STEP_ROWS = 512
PROJ_DTYPE = BF16
NEG_BIG = -1e30
LOG2E = math.log2(math.e)
VMEM_LIMIT = 60 * 1024 * 1024


def _dot(a, b):
    return jnp.dot(a, b, preferred_element_type=F32)


def _dot_nt(a, b):
    return lax.dot_general(a, b, (((1,), (1,)), ((), ())), preferred_element_type=F32)


def _split2(a):
    hi = a.astype(BF16)
    lo = (a - hi.astype(F32)).astype(BF16)
    return hi, lo


def _split3(a):
    hi = a.astype(BF16)
    r = a - hi.astype(F32)
    mid = r.astype(BF16)
    lo = (r - mid.astype(F32)).astype(BF16)
    return hi, mid, lo


def _dot3(a, b):
    ah, al = _split2(a)
    bh, bl = _split2(b)
    return _dot(ah, bh) + _dot(al, bh) + _dot(ah, bl)


def _dot3_nt(a, b):
    ah, al = _split2(a)
    bh, bl = _split2(b)
    return _dot_nt(ah, bh) + _dot_nt(al, bh) + _dot_nt(ah, bl)


def _dot3_tab(th, tl, b):
    bh, bl = _split2(b)
    return _dot(th, bh) + _dot(tl, bh) + _dot(th, bl)


def _sigmoid(x):
    return 0.5 * jnp.tanh(0.5 * x) + 0.5


def _silu(x):
    return x * _sigmoid(x)


def _log_sigmoid(x):
    return jnp.minimum(x, 0.0) - jnp.log(1.0 + jnp.exp(-jnp.abs(x)))


def _conv3(x, w, seq_len=None):
    R = x.shape[0]
    seq_len = R if seq_len is None else seq_len
    pos = lax.broadcasted_iota(jnp.int32, x.shape, 0) & (seq_len - 1)
    xm = jnp.where(pos == 0, 0.0, pltpu.roll(x, 1, axis=0))
    xp = jnp.where(pos == seq_len - 1, 0.0, pltpu.roll(x, R - 1, axis=0))
    return xm * w[0:1, :] + x * w[1:2, :] + xp * w[2:3, :]


def _rms(x):
    return x * lax.rsqrt(jnp.mean(x * x, axis=-1, keepdims=True) + EPS)


def _scan_max(x, *, reverse):
    n = x.shape[1]
    lane = lax.broadcasted_iota(jnp.int32, x.shape, 1)
    d = 1
    while d < n:
        if reverse:
            shifted = jnp.where(lane < n - d, pltpu.roll(x, n - d, axis=1), NEG_BIG)
        else:
            shifted = jnp.where(lane >= d, pltpu.roll(x, d, axis=1), NEG_BIG)
        x = jnp.maximum(x, shifted)
        d *= 2
    return x


def _hi_lo(a64):
    a = a64.astype(np.float32)
    hi = a.astype(BF16)
    lo = (a - hi.astype(np.float32)).astype(BF16)
    return jnp.asarray(hi), jnp.asarray(lo)


def _const_spec(shape):
    return pl.BlockSpec(shape, lambda *_: tuple(0 for _ in shape), pipeline_mode=pl.Buffered(1))


def _layer_spec(shape, l):
    return pl.BlockSpec((None,) + tuple(shape), lambda *_: (l,) + tuple(0 for _ in shape),
                        pipeline_mode=pl.Buffered(1))


@functools.lru_cache(maxsize=None)
def _dft_tables_np(L):
    k = np.arange(L, dtype=np.int64)
    m1 = ((2 * k[:, None] + 1) * k[None, :]) % (4 * L)
    a1 = np.pi * m1.astype(np.float64) / (2 * L)
    m2 = ((2 * k[:, None] + 1) * (2 * k[None, :] + 1)) % (8 * L)
    a2 = np.pi * m2.astype(np.float64) / (4 * L)
    return np.cos(a1), np.sin(a1), np.cos(a2), np.sin(a2)


@functools.lru_cache(maxsize=None)
def _filter_consts_np(L):
    t_idx = np.arange(L, dtype=np.float64)
    t = t_idx / (L - 1)
    bands = np.arange(1, HY_BANDS + 1, dtype=np.float64)
    ang = 2.0 * math.pi * t_idx[:, None] * bands[None, :] / L
    z = np.zeros((L, LANES), np.float64)
    z[:, 0] = t
    z[:, 1:1 + HY_BANDS] = np.cos(ang)
    z[:, 1 + HY_BANDS:1 + 2 * HY_BANDS] = -np.sin(ang)
    lin = np.linspace(math.log(HY_DECAY_TARGET) / HY_DECAY_LONG,
                      math.log(HY_DECAY_TARGET) / HY_DECAY_SHORT, HY_W).astype(np.float32)
    deltas = np.abs(lin).astype(np.float64)
    decay = np.exp(-t[:, None] * deltas[None, :])
    return z.astype(np.float32), decay.astype(np.float32)


@functools.lru_cache(maxsize=None)
def _pos_embed_np(L):
    rows = L // GRID_W
    r, cidx = np.meshgrid(np.arange(rows, dtype=np.float64), np.arange(GRID_W, dtype=np.float64), indexing="ij")
    r = r.reshape(-1)
    cidx = cidx.reshape(-1)
    quarter = D_MODEL // 4
    omega = 1.0 / (10000.0 ** (np.arange(quarter, dtype=np.float64) / quarter))
    ar = r[:, None] * omega[None, :]
    ac = cidx[:, None] * omega[None, :]
    return np.concatenate([np.sin(ar), np.cos(ar), np.sin(ac), np.cos(ac)], axis=-1).astype(np.float32)


def _ada_kernel(c_ref, w_ref, b_ref, o_ref):
    sc = _silu(c_ref[...])
    o_ref[0] = _dot3(sc, w_ref[0]) + b_ref[0]


def _ada(cond, ada_w, ada_b):
    tn = 1536
    nd = 6 * D_MODEL
    return pl.pallas_call(
        _ada_kernel,
        out_shape=jax.ShapeDtypeStruct((DEPTH, 8, nd), F32),
        grid=(DEPTH, nd // tn),
        in_specs=[
            pl.BlockSpec((8, D_MODEL), lambda l, j: (0, 0)),
            pl.BlockSpec((1, D_MODEL, tn), lambda l, j: (l, 0, j)),
            pl.BlockSpec((1, 1, tn), lambda l, j: (l, 0, j)),
        ],
        out_specs=pl.BlockSpec((1, 8, tn), lambda l, j: (l, 0, j)),
        compiler_params=pltpu.CompilerParams(
            dimension_semantics=("arbitrary", "arbitrary"), vmem_limit_bytes=VMEM_LIMIT),
        name="ada_mod",
    )(cond, ada_w, ada_b.reshape(DEPTH, 1, nd))


def _inproj_body(x, mod_ref, g_ref, w_ref, wg_ref, proj_ref, gate_ref):
    mod = mod_ref[0]
    sh1 = mod[:, 0:D_MODEL]
    sc1 = mod[:, D_MODEL:2 * D_MODEL]
    u = (_rms(x) * g_ref[...]) * (1.0 + sc1) + sh1
    ub = u.astype(BF16)
    tn = 512
    for j in range(N_MAIN // tn):
        proj_ref[:, j * tn:(j + 1) * tn] = _dot_nt(ub, w_ref[j * tn:(j + 1) * tn, :]).astype(PROJ_DTYPE)
    gate_ref[...] = _dot_nt(wg_ref[...], ub)


def _inproj_kernel(*refs, with_pos, n_cast, n_ctx):
    it = iter(refs)
    xc_ref, xl_ref = next(it), next(it)
    pos_ref = next(it) if with_pos else None
    mod_ref, g_ref, w_ref, wg_ref = (next(it) for _ in range(4))
    cast_in = [next(it) for _ in range(n_cast)]
    projc_ref, gatec_ref, projl_ref, gatel_ref = (next(it) for _ in range(4))
    xo_ref = next(it) if with_pos else None
    cast_out = [next(it) for _ in range(n_cast)]
    i = pl.program_id(0)

    @pl.when(i < n_ctx)
    def _():
        for src, dst in zip(cast_in, cast_out):
            dst[...] = src[...].astype(BF16)
        _inproj_body(xc_ref[...], mod_ref, g_ref, w_ref, wg_ref, projc_ref, gatec_ref)

    @pl.when(i >= n_ctx)
    def _():
        x = xl_ref[...]
        if with_pos:
            x = x + pos_ref[...]
            xo_ref[...] = x
        _inproj_body(x, mod_ref, g_ref, w_ref, wg_ref, projl_ref, gatel_ref)


def _two_pass_maps(n_ctx):
    ctx = lambda i: jnp.minimum(i, n_ctx - 1)
    lat = lambda i: jnp.maximum(i - n_ctx, 0)
    return ctx, lat


def _mod_spec(l, Ld, tm, n_ctx):
    return pl.BlockSpec((1, 1, 6 * D_MODEL),
                        lambda i: (l * 8 + jnp.where(i < n_ctx, 0, 1 + ((i - n_ctx) * tm) // Ld), 0, 0))


def _inproj(xc, xl, pos, mods, g, w_in_t, w_gate, cast=(), *, l, Ld):
    tm = ROW_BLOCK
    Tc, Tl = xc.shape[0], xl.shape[0]
    n_ctx, n_lat = Tc // tm, Tl // tm
    per_seq = Ld // tm
    ctx, lat = _two_pass_maps(n_ctx)
    with_pos = pos is not None
    in_specs = [pl.BlockSpec((tm, D_MODEL), lambda i: (ctx(i), 0)), pl.BlockSpec((tm, D_MODEL), lambda i: (lat(i), 0))]
    args = [xc, xl]
    if with_pos:
        in_specs.append(pl.BlockSpec((tm, D_MODEL), lambda i: (lat(i) % per_seq, 0)))
        args.append(pos)
    in_specs += [
        _mod_spec(l, Ld, tm, n_ctx),
        _layer_spec((1, D_MODEL), l),
        _layer_spec((N_MAIN, D_MODEL), l),
        _layer_spec((N_GATE, D_MODEL), l),
    ]
    args += [mods, g, w_in_t, w_gate]
    out_shape, out_specs = [], []
    for T, idx in ((Tc, ctx), (Tl, lat)):
        out_shape += [jax.ShapeDtypeStruct((T, N_MAIN), PROJ_DTYPE), jax.ShapeDtypeStruct((N_GATE, T), F32)]
        out_specs += [pl.BlockSpec((tm, N_MAIN), lambda i, idx=idx: (idx(i), 0)),
                      pl.BlockSpec((N_GATE, tm), lambda i, idx=idx: (0, idx(i)))]
    if with_pos:
        out_shape.append(jax.ShapeDtypeStruct((Tl, D_MODEL), F32))
        out_specs.append(pl.BlockSpec((tm, D_MODEL), lambda i: (lat(i), 0)))
    for w in cast:
        _, rows, cols = w.shape
        in_specs.append(pl.BlockSpec((None, rows // n_ctx, cols), lambda i: (l, ctx(i), 0)))
        args.append(w)
        out_shape.append(jax.ShapeDtypeStruct((rows, cols), BF16))
        out_specs.append(pl.BlockSpec((rows // n_ctx, cols), lambda i: (ctx(i), 0)))
    return pl.pallas_call(
        functools.partial(_inproj_kernel, with_pos=with_pos, n_cast=len(cast), n_ctx=n_ctx),
        out_shape=out_shape,
        grid=(n_ctx + n_lat,),
        in_specs=in_specs,
        out_specs=out_specs,
        compiler_params=pltpu.CompilerParams(
            dimension_semantics=("arbitrary",), vmem_limit_bytes=VMEM_LIMIT),
        name="inproj",
    )(*args)


def _filter_kernel(z_ref, dec_ref, w1_ref, b1_ref, w2_ref, b2_ref, w3_ref, f_ref,
                   ch_ref, cl_ref, sh_ref, sl_ref, hre_ref, him_ref, *, L):
    T = HY_BLOCK
    P = L // T
    f = f_ref[...]
    hdn = jnp.sin(f * (_dot3(z_ref[...], w1_ref[...]) + b1_ref[...]))
    hdn = jnp.sin(f * (_dot3(hdn, w2_ref[...]) + b2_ref[...]))
    h = _dot3(hdn, w3_ref[...])
    dec = dec_ref[...]
    row = lax.broadcasted_iota(jnp.int32, (L, HY_W), 0)
    halves = []
    for o in range(HY_ORDER):
        hf = h[:, (2 * o) * HY_W:(2 * o + 1) * HY_W] * dec
        hb = jnp.where(row == 0, 0.0, h[:, (2 * o + 1) * HY_W:(2 * o + 2) * HY_W] * dec)
        nrm = jnp.sum(jnp.abs(hf), axis=0, keepdims=True) + jnp.sum(jnp.abs(hb), axis=0, keepdims=True)
        inv = 1.0 / nrm
        halves.append((hf * inv, hb * inv))
    blocks = [half[e * T:(e + 1) * T] for pair in halves for half in pair for e in range(P)]
    x = jnp.concatenate(blocks, axis=1)
    xre = _dot3_tab(ch_ref[...], cl_ref[...], x)
    xim = -_dot3_tab(sh_ref[...], sl_ref[...], x)
    sgn = jnp.where((lax.broadcasted_iota(jnp.int32, (T, HY_W), 0) & 1) == 0, 1.0, -1.0)
    for o in range(HY_ORDER):
        def half_spec(which, e):
            j = (o * 2 + which) * P + e
            cs = slice(j * HY_W, (j + 1) * HY_W)
            return xre[:, cs], xim[:, cs], halves[o][which][e * T:e * T + 1]
        for d in range(-(P - 1), P):
            if d == 0:
                (fre, fim, _), (gre, gim, _) = half_spec(0, 0), half_spec(1, 0)
                re, im = fre + gre, fim - gim
            elif d > 0:
                (fre, fim, _), (pre, pim, p0) = half_spec(0, d), half_spec(0, d - 1)
                re, im = fre - sgn * pim, fim + sgn * (pre - p0)
            else:
                (gre, gim, _), (pre, pim, p0) = half_spec(1, -d), half_spec(1, -d - 1)
                re, im = gre - sgn * pim, -gim - sgn * (pre - p0)
            hre_ref[d + P - 1, :, o * HY_W:(o + 1) * HY_W] = re
            him_ref[d + P - 1, :, o * HY_W:(o + 1) * HY_W] = im


def _filter_spectrum(L, w1p, b1p, w2p, b2p, w3p, fp):
    T = HY_BLOCK
    z_np, dec_np = _filter_consts_np(L)
    c1, s1, _, _ = _dft_tables_np(T)
    ch, cl = _hi_lo(c1)
    sh, sl = _hi_lo(s1)
    n = HY_ORDER * HY_W
    nd = 2 * (L // T) - 1
    const = _const_spec
    per_layer = lambda shape: pl.BlockSpec((None,) + shape, lambda l: (l,) + tuple(0 for _ in shape))
    return pl.pallas_call(
        functools.partial(_filter_kernel, L=L),
        out_shape=[jax.ShapeDtypeStruct((DEPTH, nd, T, n), F32), jax.ShapeDtypeStruct((DEPTH, nd, T, n), F32)],
        grid=(DEPTH,),
        in_specs=[const((L, LANES)), const((L, HY_W)), per_layer((LANES, LANES)), per_layer((1, LANES)),
                  per_layer((LANES, LANES)), per_layer((1, LANES)), per_layer((LANES, 2 * n)), per_layer((1, LANES)),
                  const((T, T)), const((T, T)), const((T, T)), const((T, T))],
        out_specs=[per_layer((nd, T, n)), per_layer((nd, T, n))],
        compiler_params=pltpu.CompilerParams(
            dimension_semantics=("arbitrary",), vmem_limit_bytes=VMEM_LIMIT),
        name="hyena_filter",
    )(jnp.asarray(z_np), jnp.asarray(dec_np), w1p, b1p, w2p, b2p, w3p, fp, ch, cl, sh, sl)


def _schy_kernel(sc_ref, hy_ref, scw_ref, hyw_ref, skip_ref, fwd_ref, inv_ref, hre_ref, him_ref, out_ref, *, L):
    T = HY_BLOCK
    P = L // T
    NB = sc_ref.shape[0] // T
    sc = sc_ref[...].astype(F32)
    b_g = sc[:, 0:SC_W]
    c_g = sc[:, SC_W:2 * SC_W]
    xin = sc[:, 2 * SC_W:3 * SC_W]
    out_ref[:, 0:SC_W] = b_g * _conv3(c_g * xin, scw_ref[...], L)

    u3 = _conv3(hy_ref[...].astype(F32), hyw_ref[...], L)
    z = u3[:, 0:HY_W]
    gates = (u3[:, HY_W:2 * HY_W], u3[:, 2 * HY_W:3 * HY_W])
    fwd, inv = fwd_ref[...], inv_ref[...]
    lanes = lambda j: slice(j * HY_W, (j + 1) * HY_W)
    for o in range(HY_ORDER):
        zcat = jnp.concatenate([z[j * T:(j + 1) * T] for j in range(NB)], axis=1)
        spec = _dot(fwd, zcat.astype(BF16))
        yre, yim = [], []
        for bi in range(NB):
            s0, i = bi - bi % P, bi % P
            re = im = None
            for j in range(P):
                d = i - j + P - 1
                hre = hre_ref[d, :, lanes(o)]
                him = him_ref[d, :, lanes(o)]
                zc = spec[0:T, lanes(s0 + j)]
                zs = spec[T:2 * T, lanes(s0 + j)]
                pre = zc * hre + zs * him
                pim = zc * him - zs * hre
                re = pre if re is None else re + pre
                im = pim if im is None else im + pim
            yre.append(re)
            yim.append(im)
        prod = jnp.concatenate([jnp.concatenate(yre, axis=1), jnp.concatenate(yim, axis=1)], axis=0)
        ycat = _dot(inv, prod.astype(BF16)) * (1.0 / T)
        y = jnp.concatenate([ycat[:, lanes(bi)] for bi in range(NB)], axis=0)
        z = gates[o] * (y + skip_ref[o:o + 1, :] * z)
    out_ref[:, SC_W:SC_W + HY_W] = z


def _schy(proj, sc_w, hy_w, skip, hre, him, *, l, B, L):
    T = HY_BLOCK
    _, _, c2, s2 = _dft_tables_np(T)
    fwd = jnp.asarray(np.concatenate([c2, s2], axis=0).astype(np.float32).astype(BF16))
    inv = jnp.asarray(np.concatenate([c2, -s2], axis=1).astype(np.float32).astype(BF16))
    n = HY_ORDER * HY_W
    nd = 2 * (L // T) - 1
    rows = max(L, HY_STEP_ROWS)
    return pl.pallas_call(
        functools.partial(_schy_kernel, L=L),
        out_shape=jax.ShapeDtypeStruct((B * L, SC_W + HY_W), F32),
        grid=(B * L // rows,),
        in_specs=[
            pl.BlockSpec((rows, 3 * SC_W), lambda b: (b, 0)),
            pl.BlockSpec((rows, 3 * HY_W), lambda b: (b, 1)),
            _layer_spec((3, SC_W), l), _layer_spec((3, 3 * HY_W), l), _layer_spec((HY_ORDER, HY_W), l),
            _const_spec((2 * T, T)), _const_spec((T, 2 * T)),
            _layer_spec((nd, T, n), l), _layer_spec((nd, T, n), l),
        ],
        out_specs=pl.BlockSpec((rows, SC_W + HY_W), lambda b: (b, 0)),
        compiler_params=pltpu.CompilerParams(
            dimension_semantics=("arbitrary",), vmem_limit_bytes=VMEM_LIMIT),
        name="sconv_hyena",
    )(proj, proj, sc_w, hy_w, skip, fwd, inv, hre, him)


def _gate_prep_kernel(*refs, B, L, with_state):
    if with_state:
        g_ref, gb_ref, m0_ref, rows_ref, cols_ref = refs
    else:
        g_ref, gb_ref, rows_ref, cols_ref = refs
    R = B * N_GATE
    gb = gb_ref[:, 0:1]
    x = jnp.concatenate([g_ref[:, b * L:(b + 1) * L] + gb for b in range(B)], axis=0)
    rr = lax.broadcasted_iota(jnp.int32, (R, L), 0) & (N_GATE - 1)
    is_f = ((rr >= ML_H) & (rr < 2 * ML_H)) | (rr >= 3 * ML_H)
    fwd_rows = rr < 2 * ML_H
    pk = jnp.where(is_f, _log_sigmoid(x), x)
    r_i = lax.broadcasted_iota(jnp.int32, (L, L), 0)
    c_i = lax.broadcasted_iota(jnp.int32, (L, L), 1)
    triu = jnp.where(r_i <= c_i, 1.0, 0.0).astype(BF16)
    cs = _dot(jnp.concatenate(_split3(pk), axis=0), triu)
    cum = cs[0:R] + cs[R:2 * R] + cs[2 * R:3 * R]
    suf = cum[:, L - 1:L] - cum + pk
    bsum = pltpu.roll(jnp.where(fwd_rows, cum, suf), R - ML_H, axis=0)
    r = pk - bsum
    cm = jnp.where(fwd_rows, _scan_max(r, reverse=False), _scan_max(r, reverse=True))
    if with_state:
        m0 = jnp.concatenate([m0_ref[b][:, 0:1] for b in range(B)], axis=0)
        m = jnp.maximum(m0, cm)
    else:
        m = jnp.maximum(cm, 0.0)
    rl = r * LOG2E
    ml = m * LOG2E
    mt = bsum + m
    pieces = [rl, ml, jnp.exp(-mt), mt]
    if not with_state:
        pieces.append(jnp.exp2(rl - jnp.where(fwd_rows, ml[:, L - 1:L], ml[:, 0:1])))
    pad = jnp.zeros((LANES - N_GATE, L), F32)
    for b in range(B):
        bs = slice(b * N_GATE, (b + 1) * N_GATE)
        for j, piece in enumerate(pieces):
            rows_ref[b, j * N_GATE:(j + 1) * N_GATE, :] = piece[bs]
        cols_ref[b * L:(b + 1) * L, :] = jnp.concatenate([rl[bs], pad], axis=0).T


def _gate_prep(g_t, gate_b, m0, *, l, B, L):
    with_state = m0 is not None
    n_rows = (4 if with_state else 5) * N_GATE
    full = lambda shape: pl.BlockSpec(shape, lambda i: tuple(0 for _ in shape))
    args = [g_t, gate_b]
    in_specs = [full((N_GATE, B * L)), pl.BlockSpec((None, N_GATE, LANES), lambda i: (l, 0, 0))]
    if with_state:
        args.append(m0)
        in_specs.append(pl.BlockSpec((B, None, N_GATE, LANES), lambda i: (0, l, 0, 0)))
    return pl.pallas_call(
        functools.partial(_gate_prep_kernel, B=B, L=L, with_state=with_state),
        out_shape=[jax.ShapeDtypeStruct((B, n_rows, L), F32), jax.ShapeDtypeStruct((B * L, LANES), F32)],
        grid=(1,),
        in_specs=in_specs,
        out_specs=[full((B, n_rows, L)), full((B * L, LANES))],
        compiler_params=pltpu.CompilerParams(
            dimension_semantics=("arbitrary",), vmem_limit_bytes=VMEM_LIMIT),
        name="gate_prep",
    )(*args)


def _mlstm_kernel(*refs, L, with_state, n_prev):
    if with_state:
        (q_ref, k_ref, v_ref, o_ref, rows_ref, cols_ref, cw_ref, ng_ref,
         c0_ref, n0_ref, m0_ref, y_ref) = refs
    elif n_prev:
        (q_ref, k_ref, v_ref, o_ref, rows_ref, cols_ref, cw_ref, ng_ref,
         cprev_ref, nprev_ref, mprev_ref, y_ref, cout_ref, nout_ref, mout_ref) = refs
        cout_ref[:, 0:n_prev] = cprev_ref[...]
        nout_ref[:, 0:n_prev] = nprev_ref[...]
        mout_ref[:, 0:n_prev] = mprev_ref[...]
        cout_ref, nout_ref, mout_ref = (r.at[:, n_prev] for r in (cout_ref, nout_ref, mout_ref))
    else:
        (q_ref, k_ref, v_ref, o_ref, rows_ref, cols_ref, cw_ref, ng_ref,
         y_ref, cout_ref, nout_ref, mout_ref) = refs

    S = rows_ref.shape[0]
    tb = min(L, ATT_BLOCK)
    cw = cw_ref[...]
    q_all = _silu(_conv3(q_ref[...].astype(F32), cw[:, 0:ML_W], L))
    k_all = _silu(_conv3(k_ref[...].astype(F32), cw[:, ML_W:2 * ML_W], L)) * (ML_DH ** -0.5)

    s_loc = lax.broadcasted_iota(jnp.int32, (tb, tb), 0)
    t_loc = lax.broadcasted_iota(jnp.int32, (tb, tb), 1)
    masks = (s_loc <= t_loc, s_loc >= t_loc)
    ones_rows = jnp.ones((2 * 8, L), BF16)
    ng = ng_ref[...]
    for h in range(ML_H):
        hs = slice(h * ML_DH, (h + 1) * ML_DH)
        seqs = []
        for s in range(S):
            sl = slice(s * L, (s + 1) * L)
            rl16, ml16, em16, mt16 = (rows_ref[s, j * N_GATE:(j + 1) * N_GATE, :] for j in range(4))
            q = q_all[sl, hs]
            k = k_all[sl, hs]
            v = v_ref[sl, hs].astype(F32)
            vext_t = jnp.concatenate([v.T.astype(BF16), ones_rows], axis=0)
            rlb = [jnp.broadcast_to(cols_ref[sl, 2 * ML_H * d + h:2 * ML_H * d + h + 1], (L, tb)) for d in range(2)]
            seqs.append(dict(q=q, k=k, qb=q.astype(BF16), kb=k.astype(BF16), vb=v.astype(BF16), vext_t=vext_t,
                             rlb=rlb, ml16=ml16, em16=em16, mt16=mt16))
            if with_state:
                seqs[s]["c0_t"] = [c0_ref[s, d, h].T.astype(BF16) for d in range(2)]
                seqs[s]["n0"] = _split2(n0_ref[s])
        for r0 in range(0, L, tb):
            rs = slice(r0, r0 + tb)
            for s, sq in enumerate(seqs):
                s_t = _dot_nt(sq["kb"], sq["qb"][rs])
                if with_state:
                    qb_rs = sq["qb"][rs]
                    qn_rows = _dot_nt(sq["n0"][0], qb_rs) + _dot_nt(sq["n0"][1], qb_rs)
                hout_t = None
                for d in range(2):
                    o8 = 2 * ML_H * d + h
                    ml_row = sq["ml16"][o8:o8 + 1, rs]
                    e_diag = jnp.exp2(jnp.where(masks[d], sq["rlb"][d][rs] - ml_row, NEG_BIG))
                    acc = _dot(sq["vext_t"][:, rs], (s_t[rs] * e_diag).astype(BF16))
                    side = slice(r0 + tb, L) if d else slice(0, r0)
                    if side.stop > side.start:
                        e_side = jnp.exp2(sq["rlb"][d][side] - ml_row)
                        acc = acc + _dot(sq["vext_t"][:, side], (s_t[side] * e_side).astype(BF16))
                    num_t = acc[0:ML_DH]
                    den = acc[ML_DH:ML_DH + 1]
                    if with_state:
                        wp = jnp.exp2(m0_ref[s, o8:o8 + 1, 0:1] * LOG2E - ml_row)
                        den = den + wp * qn_rows[d * ML_H + h:d * ML_H + h + 1]
                        num_t = num_t + wp * _dot_nt(sq["c0_t"][d], sq["qb"][rs])
                    contrib = num_t * (1.0 / jnp.maximum(jnp.abs(den), sq["em16"][o8:o8 + 1, rs]))
                    hout_t = contrib if hout_t is None else hout_t + contrib
                hn_t = hout_t * lax.rsqrt(jnp.mean(hout_t * hout_t, axis=0, keepdims=True) + EPS)
                rows_out = slice(s * L + r0, s * L + r0 + tb)
                y_ref[rows_out, hs] = _sigmoid(o_ref[rows_out, hs].astype(F32)) * (hn_t.T * ng[:, hs])

        if not with_state:
            for s, sq in enumerate(seqs):
                w16 = rows_ref[s, 4 * N_GATE:5 * N_GATE, :]
                k_t = sq["k"].T
                w_hi, w_lo = _split2(w16)
                n_rows = _dot(w_hi, sq["kb"]) + _dot(w_lo, sq["kb"])
                for d in range(2):
                    o8 = 2 * ML_H * d + h
                    end = 0 if d else L - 1
                    cout_ref[s, d, h] = _dot((k_t * w16[o8:o8 + 1, :]).astype(BF16), sq["vb"])
                    nout_ref[s, d * ML_H + h:d * ML_H + h + 1, :] = n_rows[o8:o8 + 1, :]
                    mout_ref[s, d * ML_H + h:d * ML_H + h + 1, :] = jnp.broadcast_to(
                        sq["mt16"][o8:o8 + 1, end:end + 1], (1, LANES))


def _mlstm(proj, g_t, gate_b, conv_w, norm_g, state, prev, *, l, B, L):
    with_state = state is not None
    n_prev = 0 if prev is None else l
    rows, cols = _gate_prep(g_t, gate_b, state[2] if with_state else None, l=l, B=B, L=L)
    S = max(1, ML_STEP_ROWS // L)
    SL = S * L
    col0 = (3 * SC_W + 3 * HY_W) // ML_W
    in_specs = [
        pl.BlockSpec((SL, ML_W), lambda b: (b, col0)),
        pl.BlockSpec((SL, ML_W), lambda b: (b, col0 + 1)),
        pl.BlockSpec((SL, ML_W), lambda b: (b, col0 + 2)),
        pl.BlockSpec((SL, ML_W), lambda b: (b, col0 + 3)),
        pl.BlockSpec((S, rows.shape[1], L), lambda b: (b, 0, 0)),
        pl.BlockSpec((SL, LANES), lambda b: (b, 0)),
        _layer_spec((3, 2 * ML_W), l), _layer_spec((1, ML_W), l),
    ]
    args = [proj, proj, proj, proj, rows, cols, conv_w, norm_g]
    y_shape = jax.ShapeDtypeStruct((B * L, ML_W), F32)
    y_spec = pl.BlockSpec((SL, ML_W), lambda b: (b, 0))
    c_blk, n_blk, m_blk = (2, ML_H, ML_DH, ML_DH), (2 * ML_H, ML_DH), (2 * ML_H, LANES)
    if with_state:
        c0, n0, m0 = state
        in_specs += [
            pl.BlockSpec((S, None) + c_blk, lambda b: (b, l, 0, 0, 0, 0)),
            pl.BlockSpec((S, None) + n_blk, lambda b: (b, l, 0, 0)),
            pl.BlockSpec((S, None, N_GATE, LANES), lambda b: (b, l, 0, 0)),
        ]
        args += [c0, n0, m0]
        out_shape, out_specs = y_shape, y_spec
    else:
        lead = (l + 1,) if n_prev else ()
        if n_prev:
            in_specs += [pl.BlockSpec((S, n_prev) + blk, lambda b, nz=len(blk) + 1: (b,) + (0,) * nz)
                         for blk in (c_blk, n_blk, m_blk)]
            args += list(prev)
        out_shape = [y_shape] + [jax.ShapeDtypeStruct((B,) + lead + blk, F32) for blk in (c_blk, n_blk, m_blk)]
        out_specs = [y_spec] + [pl.BlockSpec((S,) + lead + blk, lambda b, nz=len(lead) + len(blk): (b,) + (0,) * nz)
                                for blk in (c_blk, n_blk, m_blk)]
    return pl.pallas_call(
        functools.partial(_mlstm_kernel, L=L, with_state=with_state, n_prev=n_prev),
        out_shape=out_shape,
        grid=(B // S,),
        in_specs=in_specs,
        out_specs=out_specs,
        compiler_params=pltpu.CompilerParams(
            dimension_semantics=("arbitrary",), vmem_limit_bytes=VMEM_LIMIT),
        name="mlstm",
    )(*args)


def _outmlp_kernel(xc_ref, yac_ref, ybc_ref, xl_ref, yal_ref, ybl_ref, mod_ref, wo_ref, g2_ref, w1_ref, w2_ref,
                   gf_ref, oc_ref, ol_ref, *, final, n_ctx):
    i = pl.program_id(0)
    shared = (mod_ref, wo_ref, g2_ref, w1_ref, w2_ref, gf_ref)

    @pl.when(i < n_ctx)
    def _():
        _outmlp_body(xc_ref, yac_ref, ybc_ref, *shared, oc_ref, final=final)

    @pl.when(i >= n_ctx)
    def _():
        _outmlp_body(xl_ref, yal_ref, ybl_ref, *shared, ol_ref, final=final)


def _outmlp_body(x_ref, ya_ref, yb_ref, mod_ref, wo_ref, g2_ref, w1_ref, w2_ref, gf_ref, o_ref, *, final):
    mod = mod_ref[0]
    g1 = mod[:, 2 * D_MODEL:3 * D_MODEL]
    sh2 = mod[:, 3 * D_MODEL:4 * D_MODEL]
    sc2 = mod[:, 4 * D_MODEL:5 * D_MODEL]
    g2 = mod[:, 5 * D_MODEL:6 * D_MODEL]
    na = SC_W + HY_W
    attn = _dot(ya_ref[...].astype(BF16), wo_ref[0:na, :]) + _dot(yb_ref[...].astype(BF16), wo_ref[na:, :])
    x = x_ref[...] + g1 * attn
    u = ((_rms(x) * g2_ref[...]) * (1.0 + sc2) + sh2).astype(BF16)
    tf = 1024
    acc = jnp.zeros(x.shape, F32)
    for j in range(D_FF // tf):
        hcol = _dot(u, w1_ref[:, j * tf:(j + 1) * tf])
        hcol = jnp.square(jnp.maximum(hcol, 0.0)).astype(BF16)
        acc = acc + _dot(hcol, w2_ref[j * tf:(j + 1) * tf, :])
    x = x + g2 * acc
    if final:
        x = _rms(x) * gf_ref[...]
    o_ref[...] = x


def _outmlp(ctx_ops, lat_ops, mods, w_out, g2, w1, w2, gf, *, l, Ld, final):
    tm = ROW_BLOCK
    Tc, Tl = ctx_ops[0].shape[0], lat_ops[0].shape[0]
    n_ctx, n_lat = Tc // tm, Tl // tm
    ctx, lat = _two_pass_maps(n_ctx)
    token_specs = lambda idx: [pl.BlockSpec((tm, width), lambda i: (idx(i), 0))
                               for width in (D_MODEL, SC_W + HY_W, ML_W)]
    return pl.pallas_call(
        functools.partial(_outmlp_kernel, final=final, n_ctx=n_ctx),
        out_shape=[jax.ShapeDtypeStruct((Tc, D_MODEL), F32), jax.ShapeDtypeStruct((Tl, D_MODEL), F32)],
        grid=(n_ctx + n_lat,),
        in_specs=token_specs(ctx) + token_specs(lat) + [
            _mod_spec(l, Ld, tm, n_ctx),
            _const_spec((D_MODEL, D_MODEL)), _layer_spec((1, D_MODEL), l),
            _const_spec((D_MODEL, D_FF)), _const_spec((D_FF, D_MODEL)), _const_spec((1, D_MODEL)),
        ],
        out_specs=[pl.BlockSpec((tm, D_MODEL), lambda i: (ctx(i), 0)), pl.BlockSpec((tm, D_MODEL), lambda i: (lat(i), 0))],
        compiler_params=pltpu.CompilerParams(
            dimension_semantics=("arbitrary",), vmem_limit_bytes=VMEM_LIMIT),
        name="outproj_mlp",
    )(*ctx_ops, *lat_ops, mods, w_out, g2, w1, w2, gf)


def kernel(x_prompt, x_sample, state_C, state_n, state_m, c, c_ctx, norm1_g, ada_w, ada_b, w_in, sc_conv_w, hy_conv_w, hy_w1, hy_b1, hy_w2, hy_b2, hy_w3, hy_freq, hy_skip, ml_conv_w, ml_gate_b, ml_norm_g, w_out, norm2_g, mlp_w1, mlp_w2, norm_f_g):
    B, L = x_prompt.shape[0], x_prompt.shape[1]
    Bd, Ld = x_sample.shape[0], x_sample.shape[1]

    cond = jnp.concatenate([c_ctx[None, :], c, jnp.zeros((8 - 1 - Bd, D_MODEL), F32)], axis=0)
    mods = _ada(cond, ada_w, ada_b).reshape(DEPTH * 8, 1, 6 * D_MODEL)

    w_in_t = jnp.swapaxes(w_in, 1, 2).astype(BF16)
    w_gate = w_in_t[:, N_MAIN:, :]
    gate_b = jnp.broadcast_to(ml_gate_b[:, :, None], (DEPTH, N_GATE, LANES))
    pad_h = LANES - HY_HID
    w1p = jnp.pad(hy_w1, ((0, 0), (0, LANES - HY_EMB), (0, pad_h)))
    b1p = jnp.pad(hy_b1, ((0, 0), (0, pad_h)))
    w2p = jnp.pad(hy_w2, ((0, 0), (0, pad_h), (0, pad_h)))
    b2p = jnp.pad(hy_b2, ((0, 0), (0, pad_h)))
    w3p = jnp.pad(hy_w3, ((0, 0), (0, pad_h), (0, 0)))
    fp = jnp.pad(hy_freq, ((0, 0), (0, pad_h)))
    pos = jnp.asarray(_pos_embed_np(Ld))
    m0_all = jnp.pad(state_m, ((0, 0), (0, 0), (0, 0), (0, ML_H))).reshape(Bd, DEPTH, N_GATE, 1)
    m0_all = jnp.broadcast_to(m0_all, (Bd, DEPTH, N_GATE, LANES))
    n0_all = state_n.reshape(Bd, DEPTH, 2 * ML_H, ML_DH)
    gf = norm_f_g[None, :]
    g1 = norm1_g[:, None, :]
    g2 = norm2_g[:, None, :]
    ml_ng = ml_norm_g[:, None, :]
    filt_args = (w1p, b1p[:, None, :], w2p, b2p[:, None, :], w3p, fp[:, None, :])
    hre_c, him_c = _filter_spectrum(L, *filt_args)
    hre_s, him_s = _filter_spectrum(Ld, *filt_args)
    state = (state_C, n0_all, m0_all)

    xp = x_prompt.reshape(B * L, D_MODEL)
    xs = x_sample.reshape(Bd * Ld, D_MODEL)
    prev = None
    for l in range(DEPTH):
        final = l == DEPTH - 1

        res = _inproj(xp, xs, pos if l == 0 else None, mods, g1, w_in_t, w_gate, (w_out, mlp_w1, mlp_w2), l=l, Ld=Ld)
        proj_c, g_c, proj_s, g_s = res[:4]
        if l == 0:
            xs = res[4]
        w_out_b, w1_b, w2_b = res[-3:]

        ya_c = _schy(proj_c, sc_conv_w, hy_conv_w, hy_skip, hre_c, him_c, l=l, B=B, L=L)
        yb_c, *states = _mlstm(proj_c, g_c, gate_b, ml_conv_w, ml_ng, None, prev, l=l, B=B, L=L)
        if not final:
            prev = [s.reshape((B, 1) + s.shape[1:]) for s in states] if l == 0 else states

        ya_s = _schy(proj_s, sc_conv_w, hy_conv_w, hy_skip, hre_s, him_s, l=l, B=Bd, L=Ld)
        yb_s = _mlstm(proj_s, g_s, gate_b, ml_conv_w, ml_ng, state, None, l=l, B=Bd, L=Ld)

        xp, xs = _outmlp((xp, ya_c, yb_c), (xs, ya_s, yb_s), mods, w_out_b, g2, w1_b, w2_b, gf,
                         l=l, Ld=Ld, final=final)

    y_prompt = xp.reshape(B, L, D_MODEL)
    y_sample = xs.reshape(Bd, Ld, D_MODEL)
    new_c, new_n, new_m = states
    return (y_prompt, y_sample, new_c, new_n.reshape(B, DEPTH, 2, ML_H, ML_DH),
            new_m[..., 0].reshape(B, DEPTH, 2, ML_H))
```

```python
import functools
import math

import numpy as np
import jax
import jax.numpy as jnp
from jax import lax
from jax.experimental import pallas as pl
from jax.experimental.pallas import tpu as pltpu

F32 = jnp.float32
BF16 = jnp.bfloat16

D_MODEL = 1024
DEPTH = 2
GRID_W = 64
SC_W = 256
HY_W = 256
ML_W = 512
ML_H = 4
ML_DH = ML_W // ML_H
D_FF = 4 * D_MODEL
HY_ORDER = 2
HY_BANDS = 16
HY_EMB = 1 + 2 * HY_BANDS
HY_HID = 64
HY_DECAY_SHORT = 0.3
HY_DECAY_LONG = 1.5
HY_DECAY_TARGET = 1e-2
EPS = 1e-6
N_MAIN = 3 * SC_W + 3 * HY_W + 4 * ML_W
N_GATE = 4 * ML_H
LANES = 128
ROW_BLOCK = 512
ATT_BLOCK = 256
HY_BLOCK = 256
HY_STEP_ROWS = 1024
ML_STEP_ROWS = 512
PROJ_DTYPE = BF16
NEG_BIG = -1e30
LOG2E = math.log2(math.e)
VMEM_LIMIT = 60 * 1024 * 1024


def _dot(a, b):
    return jnp.dot(a, b, preferred_element_type=F32)


def _dot_nt(a, b):
    return lax.dot_general(a, b, (((1,), (1,)), ((), ())), preferred_element_type=F32)


def _split2(a):
    hi = a.astype(BF16)
    lo = (a - hi.astype(F32)).astype(BF16)
    return hi, lo


def _split3(a):
    hi = a.astype(BF16)
    r = a - hi.astype(F32)
    mid = r.astype(BF16)
    lo = (r - mid.astype(F32)).astype(BF16)
    return hi, mid, lo


def _dot3(a, b):
    ah, al = _split2(a)
    bh, bl = _split2(b)
    return _dot(ah, bh) + _dot(al, bh) + _dot(ah, bl)


def _dot3_nt(a, b):
    ah, al = _split2(a)
    bh, bl = _split2(b)
    return _dot_nt(ah, bh) + _dot_nt(al, bh) + _dot_nt(ah, bl)


def _dot3_tab(th, tl, b):
    bh, bl = _split2(b)
    return _dot(th, bh) + _dot(tl, bh) + _dot(th, bl)


def _sigmoid(x):
    return 0.5 * jnp.tanh(0.5 * x) + 0.5


def _silu(x):
    return x * _sigmoid(x)


def _log_sigmoid(x):
    return jnp.minimum(x, 0.0) - jnp.log(1.0 + jnp.exp(-jnp.abs(x)))


def _conv3(x, w, seq_len=None):
    R = x.shape[0]
    seq_len = R if seq_len is None else seq_len
    pos = lax.broadcasted_iota(jnp.int32, x.shape, 0) & (seq_len - 1)
    xm = jnp.where(pos == 0, 0.0, pltpu.roll(x, 1, axis=0))
    xp = jnp.where(pos == seq_len - 1, 0.0, pltpu.roll(x, R - 1, axis=0))
    return xm * w[0:1, :] + x * w[1:2, :] + xp * w[2:3, :]


def _rms(x):
    return x * lax.rsqrt(jnp.mean(x * x, axis=-1, keepdims=True) + EPS)


def _scan_max(x, *, reverse):
    n = x.shape[1]
    lane = lax.broadcasted_iota(jnp.int32, x.shape, 1)
    d = 1
    while d < n:
        if reverse:
            shifted = jnp.where(lane < n - d, pltpu.roll(x, n - d, axis=1), NEG_BIG)
        else:
            shifted = jnp.where(lane >= d, pltpu.roll(x, d, axis=1), NEG_BIG)
        x = jnp.maximum(x, shifted)
        d *= 2
    return x


def _hi_lo(a64):
    a = a64.astype(np.float32)
    hi = a.astype(BF16)
    lo = (a - hi.astype(np.float32)).astype(BF16)
    return jnp.asarray(hi), jnp.asarray(lo)


def _const_spec(shape):
    return pl.BlockSpec(shape, lambda *_: tuple(0 for _ in shape), pipeline_mode=pl.Buffered(1))


def _layer_spec(shape, l):
    return pl.BlockSpec((None,) + tuple(shape), lambda *_: (l,) + tuple(0 for _ in shape),
                        pipeline_mode=pl.Buffered(1))


@functools.lru_cache(maxsize=None)
def _dft_tables_np(L):
    k = np.arange(L, dtype=np.int64)
    m1 = ((2 * k[:, None] + 1) * k[None, :]) % (4 * L)
    a1 = np.pi * m1.astype(np.float64) / (2 * L)
    m2 = ((2 * k[:, None] + 1) * (2 * k[None, :] + 1)) % (8 * L)
    a2 = np.pi * m2.astype(np.float64) / (4 * L)
    return np.cos(a1), np.sin(a1), np.cos(a2), np.sin(a2)


@functools.lru_cache(maxsize=None)
def _filter_consts_np(L):
    t_idx = np.arange(L, dtype=np.float64)
    t = t_idx / (L - 1)
    bands = np.arange(1, HY_BANDS + 1, dtype=np.float64)
    ang = 2.0 * math.pi * t_idx[:, None] * bands[None, :] / L
    z = np.zeros((L, LANES), np.float64)
    z[:, 0] = t
    z[:, 1:1 + HY_BANDS] = np.cos(ang)
    z[:, 1 + HY_BANDS:1 + 2 * HY_BANDS] = -np.sin(ang)
    lin = np.linspace(math.log(HY_DECAY_TARGET) / HY_DECAY_LONG,
                      math.log(HY_DECAY_TARGET) / HY_DECAY_SHORT, HY_W).astype(np.float32)
    deltas = np.abs(lin).astype(np.float64)
    decay = np.exp(-t[:, None] * deltas[None, :])
    return z.astype(np.float32), decay.astype(np.float32)


@functools.lru_cache(maxsize=None)
def _pos_embed_np(L):
    rows = L // GRID_W
    r, cidx = np.meshgrid(np.arange(rows, dtype=np.float64), np.arange(GRID_W, dtype=np.float64), indexing="ij")
    r = r.reshape(-1)
    cidx = cidx.reshape(-1)
    quarter = D_MODEL // 4
    omega = 1.0 / (10000.0 ** (np.arange(quarter, dtype=np.float64) / quarter))
    ar = r[:, None] * omega[None, :]
    ac = cidx[:, None] * omega[None, :]
    return np.concatenate([np.sin(ar), np.cos(ar), np.sin(ac), np.cos(ac)], axis=-1).astype(np.float32)


def _ada_kernel(c_ref, w_ref, b_ref, o_ref):
    sc = _silu(c_ref[...])
    o_ref[0] = _dot3(sc, w_ref[0]) + b_ref[0]


def _ada(cond, ada_w, ada_b):
    tn = 1536
    nd = 6 * D_MODEL
    return pl.pallas_call(
        _ada_kernel,
        out_shape=jax.ShapeDtypeStruct((DEPTH, 8, nd), F32),
        grid=(DEPTH, nd // tn),
        in_specs=[
            pl.BlockSpec((8, D_MODEL), lambda l, j: (0, 0)),
            pl.BlockSpec((1, D_MODEL, tn), lambda l, j: (l, 0, j)),
            pl.BlockSpec((1, 1, tn), lambda l, j: (l, 0, j)),
        ],
        out_specs=pl.BlockSpec((1, 8, tn), lambda l, j: (l, 0, j)),
        compiler_params=pltpu.CompilerParams(
            dimension_semantics=("arbitrary", "arbitrary"), vmem_limit_bytes=VMEM_LIMIT),
        name="ada_mod",
    )(cond, ada_w, ada_b.reshape(DEPTH, 1, nd))


def _inproj_body(x, mod_ref, g_ref, w_ref, wg_ref, proj_ref, gate_ref):
    mod = mod_ref[0]
    sh1 = mod[:, 0:D_MODEL]
    sc1 = mod[:, D_MODEL:2 * D_MODEL]
    u = (_rms(x) * g_ref[...]) * (1.0 + sc1) + sh1
    ub = u.astype(BF16)
    tn = 512
    for j in range(N_MAIN // tn):
        proj_ref[:, j * tn:(j + 1) * tn] = _dot_nt(ub, w_ref[j * tn:(j + 1) * tn, :]).astype(PROJ_DTYPE)
    gate_ref[...] = _dot_nt(wg_ref[...], ub)


def _inproj_kernel(*refs, with_pos, n_cast, n_ctx):
    it = iter(refs)
    xc_ref, xl_ref = next(it), next(it)
    pos_ref = next(it) if with_pos else None
    mod_ref, g_ref, w_ref, wg_ref = (next(it) for _ in range(4))
    cast_in = [next(it) for _ in range(n_cast)]
    projc_ref, gatec_ref, projl_ref, gatel_ref = (next(it) for _ in range(4))
    xo_ref = next(it) if with_pos else None
    cast_out = [next(it) for _ in range(n_cast)]
    i = pl.program_id(0)

    @pl.when(i < n_ctx)
    def _():
        for src, dst in zip(cast_in, cast_out):
            dst[...] = src[...].astype(BF16)
        _inproj_body(xc_ref[...], mod_ref, g_ref, w_ref, wg_ref, projc_ref, gatec_ref)

    @pl.when(i >= n_ctx)
    def _():
        x = xl_ref[...]
        if with_pos:
            x = x + pos_ref[...]
            xo_ref[...] = x
        _inproj_body(x, mod_ref, g_ref, w_ref, wg_ref, projl_ref, gatel_ref)


def _two_pass_maps(n_ctx):
    ctx = lambda i: jnp.minimum(i, n_ctx - 1)
    lat = lambda i: jnp.maximum(i - n_ctx, 0)
    return ctx, lat


def _mod_spec(l, Ld, tm, n_ctx):
    return pl.BlockSpec((1, 1, 6 * D_MODEL),
                        lambda i: (l * 8 + jnp.where(i < n_ctx, 0, 1 + ((i - n_ctx) * tm) // Ld), 0, 0))


def _inproj(xc, xl, pos, mods, g, w_in_t, w_gate, cast=(), *, l, Ld):
    tm = ROW_BLOCK
    Tc, Tl = xc.shape[0], xl.shape[0]
    n_ctx, n_lat = Tc // tm, Tl // tm
    per_seq = Ld // tm
    ctx, lat = _two_pass_maps(n_ctx)
    with_pos = pos is not None
    in_specs = [pl.BlockSpec((tm, D_MODEL), lambda i: (ctx(i), 0)), pl.BlockSpec((tm, D_MODEL), lambda i: (lat(i), 0))]
    args = [xc, xl]
    if with_pos:
        in_specs.append(pl.BlockSpec((tm, D_MODEL), lambda i: (lat(i) % per_seq, 0)))
        args.append(pos)
    in_specs += [
        _mod_spec(l, Ld, tm, n_ctx),
        _layer_spec((1, D_MODEL), l),
        _layer_spec((N_MAIN, D_MODEL), l),
        _layer_spec((N_GATE, D_MODEL), l),
    ]
    args += [mods, g, w_in_t, w_gate]
    out_shape, out_specs = [], []
    for T, idx in ((Tc, ctx), (Tl, lat)):
        out_shape += [jax.ShapeDtypeStruct((T, N_MAIN), PROJ_DTYPE), jax.ShapeDtypeStruct((N_GATE, T), F32)]
        out_specs += [pl.BlockSpec((tm, N_MAIN), lambda i, idx=idx: (idx(i), 0)),
                      pl.BlockSpec((N_GATE, tm), lambda i, idx=idx: (0, idx(i)))]
    if with_pos:
        out_shape.append(jax.ShapeDtypeStruct((Tl, D_MODEL), F32))
        out_specs.append(pl.BlockSpec((tm, D_MODEL), lambda i: (lat(i), 0)))
    for w in cast:
        _, rows, cols = w.shape
        in_specs.append(pl.BlockSpec((None, rows // n_ctx, cols), lambda i: (l, ctx(i), 0)))
        args.append(w)
        out_shape.append(jax.ShapeDtypeStruct((rows, cols), BF16))
        out_specs.append(pl.BlockSpec((rows // n_ctx, cols), lambda i: (ctx(i), 0)))
    return pl.pallas_call(
        functools.partial(_inproj_kernel, with_pos=with_pos, n_cast=len(cast), n_ctx=n_ctx),
        out_shape=out_shape,
        grid=(n_ctx + n_lat,),
        in_specs=in_specs,
        out_specs=out_specs,
        compiler_params=pltpu.CompilerParams(
            dimension_semantics=("arbitrary",), vmem_limit_bytes=VMEM_LIMIT),
        name="inproj",
    )(*args)


def _filter_kernel(z_ref, dec_ref, w1_ref, b1_ref, w2_ref, b2_ref, w3_ref, f_ref,
                   ch_ref, cl_ref, sh_ref, sl_ref, hre_ref, him_ref, *, L):
    T = HY_BLOCK
    P = L // T
    f = f_ref[...]
    hdn = jnp.sin(f * (_dot3(z_ref[...], w1_ref[...]) + b1_ref[...]))
    hdn = jnp.sin(f * (_dot3(hdn, w2_ref[...]) + b2_ref[...]))
    h = _dot3(hdn, w3_ref[...])
    dec = dec_ref[...]
    row = lax.broadcasted_iota(jnp.int32, (L, HY_W), 0)
    halves = []
    for o in range(HY_ORDER):
        hf = h[:, (2 * o) * HY_W:(2 * o + 1) * HY_W] * dec
        hb = jnp.where(row == 0, 0.0, h[:, (2 * o + 1) * HY_W:(2 * o + 2) * HY_W] * dec)
        nrm = jnp.sum(jnp.abs(hf), axis=0, keepdims=True) + jnp.sum(jnp.abs(hb), axis=0, keepdims=True)
        inv = 1.0 / nrm
        halves.append((hf * inv, hb * inv))
    blocks = [half[e * T:(e + 1) * T] for pair in halves for half in pair for e in range(P)]
    x = jnp.concatenate(blocks, axis=1)
    xre = _dot3_tab(ch_ref[...], cl_ref[...], x)
    xim = -_dot3_tab(sh_ref[...], sl_ref[...], x)
    sgn = jnp.where((lax.broadcasted_iota(jnp.int32, (T, HY_W), 0) & 1) == 0, 1.0, -1.0)
    for o in range(HY_ORDER):
        def half_spec(which, e):
            j = (o * 2 + which) * P + e
            cs = slice(j * HY_W, (j + 1) * HY_W)
            return xre[:, cs], xim[:, cs], halves[o][which][e * T:e * T + 1]
        for d in range(-(P - 1), P):
            if d == 0:
                (fre, fim, _), (gre, gim, _) = half_spec(0, 0), half_spec(1, 0)
                re, im = fre + gre, fim - gim
            elif d > 0:
                (fre, fim, _), (pre, pim, p0) = half_spec(0, d), half_spec(0, d - 1)
                re, im = fre - sgn * pim, fim + sgn * (pre - p0)
            else:
                (gre, gim, _), (pre, pim, p0) = half_spec(1, -d), half_spec(1, -d - 1)
                re, im = gre - sgn * pim, -gim - sgn * (pre - p0)
            hre_ref[d + P - 1, :, o * HY_W:(o + 1) * HY_W] = re
            him_ref[d + P - 1, :, o * HY_W:(o + 1) * HY_W] = im


def _filter_spectrum(L, w1p, b1p, w2p, b2p, w3p, fp):
    T = HY_BLOCK
    z_np, dec_np = _filter_consts_np(L)
    c1, s1, _, _ = _dft_tables_np(T)
    ch, cl = _hi_lo(c1)
    sh, sl = _hi_lo(s1)
    n = HY_ORDER * HY_W
    nd = 2 * (L // T) - 1
    const = _const_spec
    per_layer = lambda shape: pl.BlockSpec((None,) + shape, lambda l: (l,) + tuple(0 for _ in shape))
    return pl.pallas_call(
        functools.partial(_filter_kernel, L=L),
        out_shape=[jax.ShapeDtypeStruct((DEPTH, nd, T, n), F32), jax.ShapeDtypeStruct((DEPTH, nd, T, n), F32)],
        grid=(DEPTH,),
        in_specs=[const((L, LANES)), const((L, HY_W)), per_layer((LANES, LANES)), per_layer((1, LANES)),
                  per_layer((LANES, LANES)), per_layer((1, LANES)), per_layer((LANES, 2 * n)), per_layer((1, LANES)),
                  const((T, T)), const((T, T)), const((T, T)), const((T, T))],
        out_specs=[per_layer((nd, T, n)), per_layer((nd, T, n))],
        compiler_params=pltpu.CompilerParams(
            dimension_semantics=("arbitrary",), vmem_limit_bytes=VMEM_LIMIT),
        name="hyena_filter",
    )(jnp.asarray(z_np), jnp.asarray(dec_np), w1p, b1p, w2p, b2p, w3p, fp, ch, cl, sh, sl)


def _schy_kernel(sc_ref, hy_ref, scw_ref, hyw_ref, skip_ref, fwd_ref, inv_ref, hre_ref, him_ref, out_ref, *, L):
    T = HY_BLOCK
    P = L // T
    NB = sc_ref.shape[0] // T
    sc = sc_ref[...].astype(F32)
    b_g = sc[:, 0:SC_W]
    c_g = sc[:, SC_W:2 * SC_W]
    xin = sc[:, 2 * SC_W:3 * SC_W]
    out_ref[:, 0:SC_W] = b_g * _conv3(c_g * xin, scw_ref[...], L)

    u3 = _conv3(hy_ref[...].astype(F32), hyw_ref[...], L)
    z = u3[:, 0:HY_W]
    gates = (u3[:, HY_W:2 * HY_W], u3[:, 2 * HY_W:3 * HY_W])
    fwd, inv = fwd_ref[...], inv_ref[...]
    lanes = lambda j: slice(j * HY_W, (j + 1) * HY_W)
    for o in range(HY_ORDER):
        zcat = jnp.concatenate([z[j * T:(j + 1) * T] for j in range(NB)], axis=1)
        spec = _dot(fwd, zcat.astype(BF16))
        yre, yim = [], []
        for bi in range(NB):
            s0, i = bi - bi % P, bi % P
            re = im = None
            for j in range(P):
                d = i - j + P - 1
                hre = hre_ref[d, :, lanes(o)]
                him = him_ref[d, :, lanes(o)]
                zc = spec[0:T, lanes(s0 + j)]
                zs = spec[T:2 * T, lanes(s0 + j)]
                pre = zc * hre + zs * him
                pim = zc * him - zs * hre
                re = pre if re is None else re + pre
                im = pim if im is None else im + pim
            yre.append(re)
            yim.append(im)
        prod = jnp.concatenate([jnp.concatenate(yre, axis=1), jnp.concatenate(yim, axis=1)], axis=0)
        ycat = _dot(inv, prod.astype(BF16)) * (1.0 / T)
        y = jnp.concatenate([ycat[:, lanes(bi)] for bi in range(NB)], axis=0)
        z = gates[o] * (y + skip_ref[o:o + 1, :] * z)
    out_ref[:, SC_W:SC_W + HY_W] = z


def _schy2_kernel(scc_ref, hyc_ref, scs_ref, hys_ref, scw_ref, hyw_ref, skip_ref, fwd_ref, inv_ref,
                  hrec_ref, himc_ref, hres_ref, hims_ref, outc_ref, outs_ref, *, Lc, Ls, n_ctx):
    i = pl.program_id(0)
    shared = (scw_ref, hyw_ref, skip_ref, fwd_ref, inv_ref)

    @pl.when(i < n_ctx)
    def _():
        _schy_kernel(scc_ref, hyc_ref, *shared, hrec_ref, himc_ref, outc_ref, L=Lc)

    @pl.when(i >= n_ctx)
    def _():
        _schy_kernel(scs_ref, hys_ref, *shared, hres_ref, hims_ref, outs_ref, L=Ls)


def _schy(proj_c, proj_s, sc_w, hy_w, skip, spec_c, spec_s, *, l, Lc, Ls):
    T = HY_BLOCK
    _, _, c2, s2 = _dft_tables_np(T)
    fwd = jnp.asarray(np.concatenate([c2, s2], axis=0).astype(np.float32).astype(BF16))
    inv = jnp.asarray(np.concatenate([c2, -s2], axis=1).astype(np.float32).astype(BF16))
    n = HY_ORDER * HY_W
    rows = HY_STEP_ROWS
    assert rows % Lc == 0 and rows % Ls == 0
    n_ctx, n_lat = proj_c.shape[0] // rows, proj_s.shape[0] // rows
    ctx, lat = _two_pass_maps(n_ctx)
    windows = lambda idx: [pl.BlockSpec((rows, 3 * SC_W), lambda i: (idx(i), 0)),
                           pl.BlockSpec((rows, 3 * HY_W), lambda i: (idx(i), 1))]
    spectra = lambda L: [_layer_spec((2 * (L // T) - 1, T, n), l)] * 2
    return pl.pallas_call(
        functools.partial(_schy2_kernel, Lc=Lc, Ls=Ls, n_ctx=n_ctx),
        out_shape=[jax.ShapeDtypeStruct((p.shape[0], SC_W + HY_W), F32) for p in (proj_c, proj_s)],
        grid=(n_ctx + n_lat,),
        in_specs=windows(ctx) + windows(lat) + [
            _layer_spec((3, SC_W), l), _layer_spec((3, 3 * HY_W), l), _layer_spec((HY_ORDER, HY_W), l),
            _const_spec((2 * T, T)), _const_spec((T, 2 * T)),
        ] + spectra(Lc) + spectra(Ls),
        out_specs=[pl.BlockSpec((rows, SC_W + HY_W), lambda i: (ctx(i), 0)),
                   pl.BlockSpec((rows, SC_W + HY_W), lambda i: (lat(i), 0))],
        compiler_params=pltpu.CompilerParams(
            dimension_semantics=("arbitrary",), vmem_limit_bytes=VMEM_LIMIT),
        name="sconv_hyena",
    )(proj_c, proj_c, proj_s, proj_s, sc_w, hy_w, skip, fwd, inv, *spec_c, *spec_s)


def _gate_prep_kernel(*refs, B, L, with_state):
    if with_state:
        g_ref, gb_ref, m0_ref, rows_ref, cols_ref = refs
    else:
        g_ref, gb_ref, rows_ref, cols_ref = refs
    R = B * N_GATE
    gb = gb_ref[:, 0:1]
    x = jnp.concatenate([g_ref[:, b * L:(b + 1) * L] + gb for b in range(B)], axis=0)
    rr = lax.broadcasted_iota(jnp.int32, (R, L), 0) & (N_GATE - 1)
    is_f = ((rr >= ML_H) & (rr < 2 * ML_H)) | (rr >= 3 * ML_H)
    fwd_rows = rr < 2 * ML_H
    pk = jnp.where(is_f, _log_sigmoid(x), x)
    r_i = lax.broadcasted_iota(jnp.int32, (L, L), 0)
    c_i = lax.broadcasted_iota(jnp.int32, (L, L), 1)
    triu = jnp.where(r_i <= c_i, 1.0, 0.0).astype(BF16)
    cs = _dot(jnp.concatenate(_split3(pk), axis=0), triu)
    cum = cs[0:R] + cs[R:2 * R] + cs[2 * R:3 * R]
    suf = cum[:, L - 1:L] - cum + pk
    bsum = pltpu.roll(jnp.where(fwd_rows, cum, suf), R - ML_H, axis=0)
    r = pk - bsum
    cm = jnp.where(fwd_rows, _scan_max(r, reverse=False), _scan_max(r, reverse=True))
    if with_state:
        m0 = jnp.concatenate([m0_ref[b][:, 0:1] for b in range(B)], axis=0)
        m = jnp.maximum(m0, cm)
    else:
        m = jnp.maximum(cm, 0.0)
    rl = r * LOG2E
    ml = m * LOG2E
    mt = bsum + m
    pieces = [rl, ml, jnp.exp(-mt), mt]
    if not with_state:
        pieces.append(jnp.exp2(rl - jnp.where(fwd_rows, ml[:, L - 1:L], ml[:, 0:1])))
    pad = jnp.zeros((LANES - N_GATE, L), F32)
    for b in range(B):
        bs = slice(b * N_GATE, (b + 1) * N_GATE)
        for j, piece in enumerate(pieces):
            rows_ref[b, j * N_GATE:(j + 1) * N_GATE, :] = piece[bs]
        cols_ref[b * L:(b + 1) * L, :] = jnp.concatenate([rl[bs], pad], axis=0).T


def _gate_prep(g_t, gate_b, m0, *, l, B, L):
    with_state = m0 is not None
    n_rows = (4 if with_state else 5) * N_GATE
    full = lambda shape: pl.BlockSpec(shape, lambda i: tuple(0 for _ in shape))
    args = [g_t, gate_b]
    in_specs = [full((N_GATE, B * L)), pl.BlockSpec((None, N_GATE, LANES), lambda i: (l, 0, 0))]
    if with_state:
        args.append(m0)
        in_specs.append(pl.BlockSpec((B, None, N_GATE, LANES), lambda i: (0, l, 0, 0)))
    return pl.pallas_call(
        functools.partial(_gate_prep_kernel, B=B, L=L, with_state=with_state),
        out_shape=[jax.ShapeDtypeStruct((B, n_rows, L), F32), jax.ShapeDtypeStruct((B * L, LANES), F32)],
        grid=(1,),
        in_specs=in_specs,
        out_specs=[full((B, n_rows, L)), full((B * L, LANES))],
        compiler_params=pltpu.CompilerParams(
            dimension_semantics=("arbitrary",), vmem_limit_bytes=VMEM_LIMIT),
        name="gate_prep",
    )(*args)


def _mlstm_kernel(*refs, L, with_state, n_prev):
    if with_state:
        (q_ref, k_ref, v_ref, o_ref, rows_ref, cols_ref, cw_ref, ng_ref,
         c0_ref, n0_ref, m0_ref, y_ref) = refs
    elif n_prev:
        (q_ref, k_ref, v_ref, o_ref, rows_ref, cols_ref, cw_ref, ng_ref,
         cprev_ref, nprev_ref, mprev_ref, y_ref, cout_ref, nout_ref, mout_ref) = refs
        cout_ref[:, 0:n_prev] = cprev_ref[...]
        nout_ref[:, 0:n_prev] = nprev_ref[...]
        mout_ref[:, 0:n_prev] = mprev_ref[...]
        cout_ref, nout_ref, mout_ref = (r.at[:, n_prev] for r in (cout_ref, nout_ref, mout_ref))
    else:
        (q_ref, k_ref, v_ref, o_ref, rows_ref, cols_ref, cw_ref, ng_ref,
         y_ref, cout_ref, nout_ref, mout_ref) = refs

    S = rows_ref.shape[0]
    tb = min(L, ATT_BLOCK)
    cw = cw_ref[...]
    q_all = _silu(_conv3(q_ref[...].astype(F32), cw[:, 0:ML_W], L))
    k_all = _silu(_conv3(k_ref[...].astype(F32), cw[:, ML_W:2 * ML_W], L)) * (ML_DH ** -0.5)

    s_loc = lax.broadcasted_iota(jnp.int32, (tb, tb), 0)
    t_loc = lax.broadcasted_iota(jnp.int32, (tb, tb), 1)
    masks = (s_loc <= t_loc, s_loc >= t_loc)
    ones_rows = jnp.ones((2 * 8, L), BF16)
    ng = ng_ref[...]
    for h in range(ML_H):
        hs = slice(h * ML_DH, (h + 1) * ML_DH)
        seqs = []
        for s in range(S):
            sl = slice(s * L, (s + 1) * L)
            rl16, ml16, em16, mt16 = (rows_ref[s, j * N_GATE:(j + 1) * N_GATE, :] for j in range(4))
            q = q_all[sl, hs]
            k = k_all[sl, hs]
            v = v_ref[sl, hs].astype(F32)
            vext_t = jnp.concatenate([v.T.astype(BF16), ones_rows], axis=0)
            rlb = [jnp.broadcast_to(cols_ref[sl, 2 * ML_H * d + h:2 * ML_H * d + h + 1], (L, tb)) for d in range(2)]
            seqs.append(dict(q=q, k=k, qb=q.astype(BF16), kb=k.astype(BF16), vb=v.astype(BF16), vext_t=vext_t,
                             rlb=rlb, ml16=ml16, em16=em16, mt16=mt16))
            if with_state:
                seqs[s]["c0_t"] = [c0_ref[s, d, h].T.astype(BF16) for d in range(2)]
                seqs[s]["n0"] = _split2(n0_ref[s])
        for r0 in range(0, L, tb):
            rs = slice(r0, r0 + tb)
            for s, sq in enumerate(seqs):
                s_t = _dot_nt(sq["kb"], sq["qb"][rs])
                if with_state:
                    qb_rs = sq["qb"][rs]
                    qn_rows = _dot_nt(sq["n0"][0], qb_rs) + _dot_nt(sq["n0"][1], qb_rs)
                hout_t = None
                for d in range(2):
                    o8 = 2 * ML_H * d + h
                    ml_row = sq["ml16"][o8:o8 + 1, rs]
                    e_diag = jnp.exp2(jnp.where(masks[d], sq["rlb"][d][rs] - ml_row, NEG_BIG))
                    acc = _dot(sq["vext_t"][:, rs], (s_t[rs] * e_diag).astype(BF16))
                    side = slice(r0 + tb, L) if d else slice(0, r0)
                    if side.stop > side.start:
                        e_side = jnp.exp2(sq["rlb"][d][side] - ml_row)
                        acc = acc + _dot(sq["vext_t"][:, side], (s_t[side] * e_side).astype(BF16))
                    num_t = acc[0:ML_DH]
                    den = acc[ML_DH:ML_DH + 1]
                    if with_state:
                        wp = jnp.exp2(m0_ref[s, o8:o8 + 1, 0:1] * LOG2E - ml_row)
                        den = den + wp * qn_rows[d * ML_H + h:d * ML_H + h + 1]
                        num_t = num_t + wp * _dot_nt(sq["c0_t"][d], sq["qb"][rs])
                    contrib = num_t * (1.0 / jnp.maximum(jnp.abs(den), sq["em16"][o8:o8 + 1, rs]))
                    hout_t = contrib if hout_t is None else hout_t + contrib
                hn_t = hout_t * lax.rsqrt(jnp.mean(hout_t * hout_t, axis=0, keepdims=True) + EPS)
                rows_out = slice(s * L + r0, s * L + r0 + tb)
                y_ref[rows_out, hs] = _sigmoid(o_ref[rows_out, hs].astype(F32)) * (hn_t.T * ng[:, hs])

        if not with_state:
            for s, sq in enumerate(seqs):
                w16 = rows_ref[s, 4 * N_GATE:5 * N_GATE, :]
                k_t = sq["k"].T
                w_hi, w_lo = _split2(w16)
                n_rows = _dot(w_hi, sq["kb"]) + _dot(w_lo, sq["kb"])
                for d in range(2):
                    o8 = 2 * ML_H * d + h
                    end = 0 if d else L - 1
                    cout_ref[s, d, h] = _dot((k_t * w16[o8:o8 + 1, :]).astype(BF16), sq["vb"])
                    nout_ref[s, d * ML_H + h:d * ML_H + h + 1, :] = n_rows[o8:o8 + 1, :]
                    mout_ref[s, d * ML_H + h:d * ML_H + h + 1, :] = jnp.broadcast_to(
                        sq["mt16"][o8:o8 + 1, end:end + 1], (1, LANES))


def _mlstm(proj, g_t, gate_b, conv_w, norm_g, state, prev, *, l, B, L):
    with_state = state is not None
    n_prev = 0 if prev is None else l
    rows, cols = _gate_prep(g_t, gate_b, state[2] if with_state else None, l=l, B=B, L=L)
    S = max(1, ML_STEP_ROWS // L)
    SL = S * L
    col0 = (3 * SC_W + 3 * HY_W) // ML_W
    in_specs = [
        pl.BlockSpec((SL, ML_W), lambda b: (b, col0)),
        pl.BlockSpec((SL, ML_W), lambda b: (b, col0 + 1)),
        pl.BlockSpec((SL, ML_W), lambda b: (b, col0 + 2)),
        pl.BlockSpec((SL, ML_W), lambda b: (b, col0 + 3)),
        pl.BlockSpec((S, rows.shape[1], L), lambda b: (b, 0, 0)),
        pl.BlockSpec((SL, LANES), lambda b: (b, 0)),
        _layer_spec((3, 2 * ML_W), l), _layer_spec((1, ML_W), l),
    ]
    args = [proj, proj, proj, proj, rows, cols, conv_w, norm_g]
    y_shape = jax.ShapeDtypeStruct((B * L, ML_W), F32)
    y_spec = pl.BlockSpec((SL, ML_W), lambda b: (b, 0))
    c_blk, n_blk, m_blk = (2, ML_H, ML_DH, ML_DH), (2 * ML_H, ML_DH), (2 * ML_H, LANES)
    if with_state:
        c0, n0, m0 = state
        in_specs += [
            pl.BlockSpec((S, None) + c_blk, lambda b: (b, l, 0, 0, 0, 0)),
            pl.BlockSpec((S, None) + n_blk, lambda b: (b, l, 0, 0)),
            pl.BlockSpec((S, None, N_GATE, LANES), lambda b: (b, l, 0, 0)),
        ]
        args += [c0, n0, m0]
        out_shape, out_specs = y_shape, y_spec
    else:
        lead = (l + 1,) if n_prev else ()
        if n_prev:
            in_specs += [pl.BlockSpec((S, n_prev) + blk, lambda b, nz=len(blk) + 1: (b,) + (0,) * nz)
                         for blk in (c_blk, n_blk, m_blk)]
            args += list(prev)
        out_shape = [y_shape] + [jax.ShapeDtypeStruct((B,) + lead + blk, F32) for blk in (c_blk, n_blk, m_blk)]
        out_specs = [y_spec] + [pl.BlockSpec((S,) + lead + blk, lambda b, nz=len(lead) + len(blk): (b,) + (0,) * nz)
                                for blk in (c_blk, n_blk, m_blk)]
    return pl.pallas_call(
        functools.partial(_mlstm_kernel, L=L, with_state=with_state, n_prev=n_prev),
        out_shape=out_shape,
        grid=(B // S,),
        in_specs=in_specs,
        out_specs=out_specs,
        compiler_params=pltpu.CompilerParams(
            dimension_semantics=("arbitrary",), vmem_limit_bytes=VMEM_LIMIT),
        name="mlstm",
    )(*args)


def _outmlp_kernel(xc_ref, yac_ref, ybc_ref, xl_ref, yal_ref, ybl_ref, mod_ref, wo_ref, g2_ref, w1_ref, w2_ref,
                   gf_ref, oc_ref, ol_ref, *, final, n_ctx):
    i = pl.program_id(0)
    shared = (mod_ref, wo_ref, g2_ref, w1_ref, w2_ref, gf_ref)

    @pl.when(i < n_ctx)
    def _():
        _outmlp_body(xc_ref, yac_ref, ybc_ref, *shared, oc_ref, final=final)

    @pl.when(i >= n_ctx)
    def _():
        _outmlp_body(xl_ref, yal_ref, ybl_ref, *shared, ol_ref, final=final)


def _outmlp_body(x_ref, ya_ref, yb_ref, mod_ref, wo_ref, g2_ref, w1_ref, w2_ref, gf_ref, o_ref, *, final):
    mod = mod_ref[0]
    g1 = mod[:, 2 * D_MODEL:3 * D_MODEL]
    sh2 = mod[:, 3 * D_MODEL:4 * D_MODEL]
    sc2 = mod[:, 4 * D_MODEL:5 * D_MODEL]
    g2 = mod[:, 5 * D_MODEL:6 * D_MODEL]
    na = SC_W + HY_W
    attn = _dot(ya_ref[...].astype(BF16), wo_ref[0:na, :]) + _dot(yb_ref[...].astype(BF16), wo_ref[na:, :])
    x = x_ref[...] + g1 * attn
    u = ((_rms(x) * g2_ref[...]) * (1.0 + sc2) + sh2).astype(BF16)
    tf = 1024
    acc = jnp.zeros(x.shape, F32)
    for j in range(D_FF // tf):
        hcol = _dot(u, w1_ref[:, j * tf:(j + 1) * tf])
        hcol = jnp.square(jnp.maximum(hcol, 0.0)).astype(BF16)
        acc = acc + _dot(hcol, w2_ref[j * tf:(j + 1) * tf, :])
    x = x + g2 * acc
    if final:
        x = _rms(x) * gf_ref[...]
    o_ref[...] = x


def _outmlp(ctx_ops, lat_ops, mods, w_out, g2, w1, w2, gf, *, l, Ld, final):
    tm = ROW_BLOCK
    Tc, Tl = ctx_ops[0].shape[0], lat_ops[0].shape[0]
    n_ctx, n_lat = Tc // tm, Tl // tm
    ctx, lat = _two_pass_maps(n_ctx)
    token_specs = lambda idx: [pl.BlockSpec((tm, width), lambda i: (idx(i), 0))
                               for width in (D_MODEL, SC_W + HY_W, ML_W)]
    return pl.pallas_call(
        functools.partial(_outmlp_kernel, final=final, n_ctx=n_ctx),
        out_shape=[jax.ShapeDtypeStruct((Tc, D_MODEL), F32), jax.ShapeDtypeStruct((Tl, D_MODEL), F32)],
        grid=(n_ctx + n_lat,),
        in_specs=token_specs(ctx) + token_specs(lat) + [
            _mod_spec(l, Ld, tm, n_ctx),
            _const_spec((D_MODEL, D_MODEL)), _layer_spec((1, D_MODEL), l),
            _const_spec((D_MODEL, D_FF)), _const_spec((D_FF, D_MODEL)), _const_spec((1, D_MODEL)),
        ],
        out_specs=[pl.BlockSpec((tm, D_MODEL), lambda i: (ctx(i), 0)), pl.BlockSpec((tm, D_MODEL), lambda i: (lat(i), 0))],
        compiler_params=pltpu.CompilerParams(
            dimension_semantics=("arbitrary",), vmem_limit_bytes=VMEM_LIMIT),
        name="outproj_mlp",
    )(*ctx_ops, *lat_ops, mods, w_out, g2, w1, w2, gf)


def kernel(x_prompt, x_sample, state_C, state_n, state_m, c, c_ctx, norm1_g, ada_w, ada_b, w_in, sc_conv_w, hy_conv_w, hy_w1, hy_b1, hy_w2, hy_b2, hy_w3, hy_freq, hy_skip, ml_conv_w, ml_gate_b, ml_norm_g, w_out, norm2_g, mlp_w1, mlp_w2, norm_f_g):
    B, L = x_prompt.shape[0], x_prompt.shape[1]
    Bd, Ld = x_sample.shape[0], x_sample.shape[1]

    cond = jnp.concatenate([c_ctx[None, :], c, jnp.zeros((8 - 1 - Bd, D_MODEL), F32)], axis=0)
    mods = _ada(cond, ada_w, ada_b).reshape(DEPTH * 8, 1, 6 * D_MODEL)

    w_in_t = jnp.swapaxes(w_in, 1, 2).astype(BF16)
    w_gate = w_in_t[:, N_MAIN:, :]
    gate_b = jnp.broadcast_to(ml_gate_b[:, :, None], (DEPTH, N_GATE, LANES))
    pad_h = LANES - HY_HID
    w1p = jnp.pad(hy_w1, ((0, 0), (0, LANES - HY_EMB), (0, pad_h)))
    b1p = jnp.pad(hy_b1, ((0, 0), (0, pad_h)))
    w2p = jnp.pad(hy_w2, ((0, 0), (0, pad_h), (0, pad_h)))
    b2p = jnp.pad(hy_b2, ((0, 0), (0, pad_h)))
    w3p = jnp.pad(hy_w3, ((0, 0), (0, pad_h), (0, 0)))
    fp = jnp.pad(hy_freq, ((0, 0), (0, pad_h)))
    pos = jnp.asarray(_pos_embed_np(Ld))
    m0_all = jnp.pad(state_m, ((0, 0), (0, 0), (0, 0), (0, ML_H))).reshape(Bd, DEPTH, N_GATE, 1)
    m0_all = jnp.broadcast_to(m0_all, (Bd, DEPTH, N_GATE, LANES))
    n0_all = state_n.reshape(Bd, DEPTH, 2 * ML_H, ML_DH)
    gf = norm_f_g[None, :]
    g1 = norm1_g[:, None, :]
    g2 = norm2_g[:, None, :]
    ml_ng = ml_norm_g[:, None, :]
    filt_args = (w1p, b1p[:, None, :], w2p, b2p[:, None, :], w3p, fp[:, None, :])
    hre_c, him_c = _filter_spectrum(L, *filt_args)
    hre_s, him_s = _filter_spectrum(Ld, *filt_args)
    state = (state_C, n0_all, m0_all)

    xp = x_prompt.reshape(B * L, D_MODEL)
    xs = x_sample.reshape(Bd * Ld, D_MODEL)
    prev = None
    for l in range(DEPTH):
        final = l == DEPTH - 1

        res = _inproj(xp, xs, pos if l == 0 else None, mods, g1, w_in_t, w_gate, (w_out, mlp_w1, mlp_w2), l=l, Ld=Ld)
        proj_c, g_c, proj_s, g_s = res[:4]
        if l == 0:
            xs = res[4]
        w_out_b, w1_b, w2_b = res[-3:]

        ya_c, ya_s = _schy(proj_c, proj_s, sc_conv_w, hy_conv_w, hy_skip, (hre_c, him_c), (hre_s, him_s),
                           l=l, Lc=L, Ls=Ld)

        yb_c, *states = _mlstm(proj_c, g_c, gate_b, ml_conv_w, ml_ng, None, prev, l=l, B=B, L=L)
        if not final:
            prev = [s.reshape((B, 1) + s.shape[1:]) for s in states] if l == 0 else states

        yb_s = _mlstm(proj_s, g_s, gate_b, ml_conv_w, ml_ng, state, None, l=l, B=Bd, L=Ld)

        xp, xs = _outmlp((xp, ya_c, yb_c), (xs, ya_s, yb_s), mods, w_out_b, g2, w1_b, w2_b, gf,
                         l=l, Ld=Ld, final=final)

    y_prompt = xp.reshape(B, L, D_MODEL)
    y_sample = xs.reshape(Bd, Ld, D_MODEL)
    new_c, new_n, new_m = states
    return (y_prompt, y_sample, new_c, new_n.reshape(B, DEPTH, 2, ML_H, ML_DH),
            new_m[..., 0].reshape(B, DEPTH, 2, ML_H))
```

```python
import functools
import math

import numpy as np
import jax
import jax.numpy as jnp
from jax import lax
from jax.experimental import pallas as pl
from jax.experimental.pallas import tpu as pltpu

F32 = jnp.float32
BF16 = jnp.bfloat16

D_MODEL = 1024
DEPTH = 2
GRID_W = 64
SC_W = 256
HY_W = 256
ML_W = 512
ML_H = 4
ML_DH = ML_W // ML_H
D_FF = 4 * D_MODEL
HY_ORDER = 2
HY_BANDS = 16
HY_EMB = 1 + 2 * HY_BANDS
HY_HID = 64
HY_DECAY_SHORT = 0.3
HY_DECAY_LONG = 1.5
HY_DECAY_TARGET = 1e-2
EPS = 1e-6
N_MAIN = 3 * SC_W + 3 * HY_W + 4 * ML_W
N_GATE = 4 * ML_H
LANES = 128
ROW_BLOCK = 512
ATT_BLOCK = 256
HY_BLOCK = 256
HY_STEP_ROWS = 1024
ML_STEP_ROWS = 512
PROJ_DTYPE = BF16
NEG_BIG = -1e30
LOG2E = math.log2(math.e)
VMEM_LIMIT = 60 * 1024 * 1024


def _dot(a, b):
    return jnp.dot(a, b, preferred_element_type=F32)


def _dot_nt(a, b):
    return lax.dot_general(a, b, (((1,), (1,)), ((), ())), preferred_element_type=F32)


def _split2(a):
    hi = a.astype(BF16)
    lo = (a - hi.astype(F32)).astype(BF16)
    return hi, lo


def _split3(a):
    hi = a.astype(BF16)
    r = a - hi.astype(F32)
    mid = r.astype(BF16)
    lo = (r - mid.astype(F32)).astype(BF16)
    return hi, mid, lo


def _dot3(a, b):
    ah, al = _split2(a)
    bh, bl = _split2(b)
    return _dot(ah, bh) + _dot(al, bh) + _dot(ah, bl)


def _dot3_nt(a, b):
    ah, al = _split2(a)
    bh, bl = _split2(b)
    return _dot_nt(ah, bh) + _dot_nt(al, bh) + _dot_nt(ah, bl)


def _dot3_tab(th, tl, b):
    bh, bl = _split2(b)
    return _dot(th, bh) + _dot(tl, bh) + _dot(th, bl)


def _sigmoid(x):
    return 0.5 * jnp.tanh(0.5 * x) + 0.5


def _silu(x):
    return x * _sigmoid(x)


def _log_sigmoid(x):
    return jnp.minimum(x, 0.0) - jnp.log(1.0 + jnp.exp(-jnp.abs(x)))


def _conv3(x, w, seq_len=None):
    R = x.shape[0]
    seq_len = R if seq_len is None else seq_len
    pos = lax.broadcasted_iota(jnp.int32, x.shape, 0) & (seq_len - 1)
    xm = jnp.where(pos == 0, 0.0, pltpu.roll(x, 1, axis=0))
    xp = jnp.where(pos == seq_len - 1, 0.0, pltpu.roll(x, R - 1, axis=0))
    return xm * w[0:1, :] + x * w[1:2, :] + xp * w[2:3, :]


def _conv3_tiles(load, w, n_rows, seq_len):
    C = w.shape[1]
    sub = lax.broadcasted_iota(jnp.int32, (8, C), 0)
    w0, w1, w2 = w[0:1, :], w[1:2, :], w[2:3, :]
    zero = jnp.zeros((8, C), F32)
    out = []
    for r0 in range(0, n_rows, 8):
        cur = load(r0)
        prev = zero if r0 % seq_len == 0 else load(r0 - 8)
        nxt = zero if (r0 + 8) % seq_len == 0 else load(r0 + 8)
        xm = jnp.where(sub == 0, pltpu.roll(prev, 1, axis=0), pltpu.roll(cur, 1, axis=0))
        xp = jnp.where(sub == 7, pltpu.roll(nxt, 7, axis=0), pltpu.roll(cur, 7, axis=0))
        out.append(xm * w0 + cur * w1 + xp * w2)
    return jnp.concatenate(out, axis=0)


def _rms(x):
    return x * lax.rsqrt(jnp.mean(x * x, axis=-1, keepdims=True) + EPS)


def _scan_max(x, *, reverse):
    n = x.shape[1]
    lane = lax.broadcasted_iota(jnp.int32, x.shape, 1)
    d = 1
    while d < n:
        if reverse:
            shifted = jnp.where(lane < n - d, pltpu.roll(x, n - d, axis=1), NEG_BIG)
        else:
            shifted = jnp.where(lane >= d, pltpu.roll(x, d, axis=1), NEG_BIG)
        x = jnp.maximum(x, shifted)
        d *= 2
    return x


def _hi_lo(a64):
    a = a64.astype(np.float32)
    hi = a.astype(BF16)
    lo = (a - hi.astype(np.float32)).astype(BF16)
    return jnp.asarray(hi), jnp.asarray(lo)


def _const_spec(shape):
    return pl.BlockSpec(shape, lambda *_: tuple(0 for _ in shape), pipeline_mode=pl.Buffered(1))


def _layer_spec(shape, l):
    return pl.BlockSpec((None,) + tuple(shape), lambda *_: (l,) + tuple(0 for _ in shape),
                        pipeline_mode=pl.Buffered(1))


@functools.lru_cache(maxsize=None)
def _dft_tables_np(L):
    k = np.arange(L, dtype=np.int64)
    m1 = ((2 * k[:, None] + 1) * k[None, :]) % (4 * L)
    a1 = np.pi * m1.astype(np.float64) / (2 * L)
    m2 = ((2 * k[:, None] + 1) * (2 * k[None, :] + 1)) % (8 * L)
    a2 = np.pi * m2.astype(np.float64) / (4 * L)
    return np.cos(a1), np.sin(a1), np.cos(a2), np.sin(a2)


@functools.lru_cache(maxsize=None)
def _filter_consts_np(L):
    t_idx = np.arange(L, dtype=np.float64)
    t = t_idx / (L - 1)
    bands = np.arange(1, HY_BANDS + 1, dtype=np.float64)
    ang = 2.0 * math.pi * t_idx[:, None] * bands[None, :] / L
    z = np.zeros((L, LANES), np.float64)
    z[:, 0] = t
    z[:, 1:1 + HY_BANDS] = np.cos(ang)
    z[:, 1 + HY_BANDS:1 + 2 * HY_BANDS] = -np.sin(ang)
    lin = np.linspace(math.log(HY_DECAY_TARGET) / HY_DECAY_LONG,
                      math.log(HY_DECAY_TARGET) / HY_DECAY_SHORT, HY_W).astype(np.float32)
    deltas = np.abs(lin).astype(np.float64)
    decay = np.exp(-t[:, None] * deltas[None, :])
    return z.astype(np.float32), decay.astype(np.float32)


@functools.lru_cache(maxsize=None)
def _pos_embed_np(L):
    rows = L // GRID_W
    r, cidx = np.meshgrid(np.arange(rows, dtype=np.float64), np.arange(GRID_W, dtype=np.float64), indexing="ij")
    r = r.reshape(-1)
    cidx = cidx.reshape(-1)
    quarter = D_MODEL // 4
    omega = 1.0 / (10000.0 ** (np.arange(quarter, dtype=np.float64) / quarter))
    ar = r[:, None] * omega[None, :]
    ac = cidx[:, None] * omega[None, :]
    return np.concatenate([np.sin(ar), np.cos(ar), np.sin(ac), np.cos(ac)], axis=-1).astype(np.float32)


def _ada_kernel(c_ref, w_ref, b_ref, o_ref):
    sc = _silu(c_ref[...])
    o_ref[0] = _dot3(sc, w_ref[0]) + b_ref[0]


def _ada(cond, ada_w, ada_b):
    tn = 1536
    nd = 6 * D_MODEL
    return pl.pallas_call(
        _ada_kernel,
        out_shape=jax.ShapeDtypeStruct((DEPTH, 8, nd), F32),
        grid=(DEPTH, nd // tn),
        in_specs=[
            pl.BlockSpec((8, D_MODEL), lambda l, j: (0, 0)),
            pl.BlockSpec((1, D_MODEL, tn), lambda l, j: (l, 0, j)),
            pl.BlockSpec((1, 1, tn), lambda l, j: (l, 0, j)),
        ],
        out_specs=pl.BlockSpec((1, 8, tn), lambda l, j: (l, 0, j)),
        compiler_params=pltpu.CompilerParams(
            dimension_semantics=("arbitrary", "arbitrary"), vmem_limit_bytes=VMEM_LIMIT),
        name="ada_mod",
    )(cond, ada_w, ada_b.reshape(DEPTH, 1, nd))


def _inproj_body(x, mod_ref, g_ref, w_ref, wg_ref, proj_ref, gate_ref):
    mod = mod_ref[0]
    sh1 = mod[:, 0:D_MODEL]
    sc1 = mod[:, D_MODEL:2 * D_MODEL]
    u = (_rms(x) * g_ref[...]) * (1.0 + sc1) + sh1
    ub = u.astype(BF16)
    tn = 512
    for j in range(N_MAIN // tn):
        proj_ref[:, j * tn:(j + 1) * tn] = _dot_nt(ub, w_ref[j * tn:(j + 1) * tn, :]).astype(PROJ_DTYPE)
    gate_ref[...] = _dot_nt(wg_ref[...], ub)


def _inproj_kernel(*refs, with_pos, n_cast, n_ctx):
    it = iter(refs)
    xc_ref, xl_ref = next(it), next(it)
    pos_ref = next(it) if with_pos else None
    mod_ref, g_ref, w_ref, wg_ref = (next(it) for _ in range(4))
    cast_in = [next(it) for _ in range(n_cast)]
    projc_ref, gatec_ref, projl_ref, gatel_ref = (next(it) for _ in range(4))
    xo_ref = next(it) if with_pos else None
    cast_out = [next(it) for _ in range(n_cast)]
    i = pl.program_id(0)

    @pl.when(i < n_ctx)
    def _():
        for src, dst in zip(cast_in, cast_out):
            dst[...] = src[...].astype(BF16)
        _inproj_body(xc_ref[...], mod_ref, g_ref, w_ref, wg_ref, projc_ref, gatec_ref)

    @pl.when(i >= n_ctx)
    def _():
        x = xl_ref[...]
        if with_pos:
            x = x + pos_ref[...]
            xo_ref[...] = x
        _inproj_body(x, mod_ref, g_ref, w_ref, wg_ref, projl_ref, gatel_ref)


def _two_pass_maps(n_ctx):
    ctx = lambda i: jnp.minimum(i, n_ctx - 1)
    lat = lambda i: jnp.maximum(i - n_ctx, 0)
    return ctx, lat


def _mod_spec(l, Ld, tm, n_ctx):
    return pl.BlockSpec((1, 1, 6 * D_MODEL),
                        lambda i: (l * 8 + jnp.where(i < n_ctx, 0, 1 + ((i - n_ctx) * tm) // Ld), 0, 0))


def _inproj(xc, xl, pos, mods, g, w_in_t, w_gate, cast=(), *, l, Ld):
    tm = ROW_BLOCK
    Tc, Tl = xc.shape[0], xl.shape[0]
    n_ctx, n_lat = Tc // tm, Tl // tm
    per_seq = Ld // tm
    ctx, lat = _two_pass_maps(n_ctx)
    with_pos = pos is not None
    in_specs = [pl.BlockSpec((tm, D_MODEL), lambda i: (ctx(i), 0)), pl.BlockSpec((tm, D_MODEL), lambda i: (lat(i), 0))]
    args = [xc, xl]
    if with_pos:
        in_specs.append(pl.BlockSpec((tm, D_MODEL), lambda i: (lat(i) % per_seq, 0)))
        args.append(pos)
    in_specs += [
        _mod_spec(l, Ld, tm, n_ctx),
        _layer_spec((1, D_MODEL), l),
        _layer_spec((N_MAIN, D_MODEL), l),
        _layer_spec((N_GATE, D_MODEL), l),
    ]
    args += [mods, g, w_in_t, w_gate]
    out_shape, out_specs = [], []
    for T, idx in ((Tc, ctx), (Tl, lat)):
        out_shape += [jax.ShapeDtypeStruct((T, N_MAIN), PROJ_DTYPE), jax.ShapeDtypeStruct((N_GATE, T), F32)]
        out_specs += [pl.BlockSpec((tm, N_MAIN), lambda i, idx=idx: (idx(i), 0)),
                      pl.BlockSpec((N_GATE, tm), lambda i, idx=idx: (0, idx(i)))]
    if with_pos:
        out_shape.append(jax.ShapeDtypeStruct((Tl, D_MODEL), F32))
        out_specs.append(pl.BlockSpec((tm, D_MODEL), lambda i: (lat(i), 0)))
    for w in cast:
        _, rows, cols = w.shape
        in_specs.append(pl.BlockSpec((None, rows // n_ctx, cols), lambda i: (l, ctx(i), 0)))
        args.append(w)
        out_shape.append(jax.ShapeDtypeStruct((rows, cols), BF16))
        out_specs.append(pl.BlockSpec((rows // n_ctx, cols), lambda i: (ctx(i), 0)))
    return pl.pallas_call(
        functools.partial(_inproj_kernel, with_pos=with_pos, n_cast=len(cast), n_ctx=n_ctx),
        out_shape=out_shape,
        grid=(n_ctx + n_lat,),
        in_specs=in_specs,
        out_specs=out_specs,
        compiler_params=pltpu.CompilerParams(
            dimension_semantics=("arbitrary",), vmem_limit_bytes=VMEM_LIMIT),
        name="inproj",
    )(*args)


def _filter_kernel(z_ref, dec_ref, w1_ref, b1_ref, w2_ref, b2_ref, w3_ref, f_ref,
                   ch_ref, cl_ref, sh_ref, sl_ref, hre_ref, him_ref, *, L):
    T = HY_BLOCK
    P = L // T
    f = f_ref[...]
    hdn = jnp.sin(f * (_dot3(z_ref[...], w1_ref[...]) + b1_ref[...]))
    hdn = jnp.sin(f * (_dot3(hdn, w2_ref[...]) + b2_ref[...]))
    h = _dot3(hdn, w3_ref[...])
    dec = dec_ref[...]
    row = lax.broadcasted_iota(jnp.int32, (L, HY_W), 0)
    halves = []
    for o in range(HY_ORDER):
        hf = h[:, (2 * o) * HY_W:(2 * o + 1) * HY_W] * dec
        hb = jnp.where(row == 0, 0.0, h[:, (2 * o + 1) * HY_W:(2 * o + 2) * HY_W] * dec)
        nrm = jnp.sum(jnp.abs(hf), axis=0, keepdims=True) + jnp.sum(jnp.abs(hb), axis=0, keepdims=True)
        inv = 1.0 / nrm
        halves.append((hf * inv, hb * inv))
    blocks = [half[e * T:(e + 1) * T] for pair in halves for half in pair for e in range(P)]
    x = jnp.concatenate(blocks, axis=1)
    xre = _dot3_tab(ch_ref[...], cl_ref[...], x)
    xim = -_dot3_tab(sh_ref[...], sl_ref[...], x)
    sgn = jnp.where((lax.broadcasted_iota(jnp.int32, (T, HY_W), 0) & 1) == 0, 1.0, -1.0)
    for o in range(HY_ORDER):
        def half_spec(which, e):
            j = (o * 2 + which) * P + e
            cs = slice(j * HY_W, (j + 1) * HY_W)
            return xre[:, cs], xim[:, cs], halves[o][which][e * T:e * T + 1]
        for d in range(-(P - 1), P):
            if d == 0:
                (fre, fim, _), (gre, gim, _) = half_spec(0, 0), half_spec(1, 0)
                re, im = fre + gre, fim - gim
            elif d > 0:
                (fre, fim, _), (pre, pim, p0) = half_spec(0, d), half_spec(0, d - 1)
                re, im = fre - sgn * pim, fim + sgn * (pre - p0)
            else:
                (gre, gim, _), (pre, pim, p0) = half_spec(1, -d), half_spec(1, -d - 1)
                re, im = gre - sgn * pim, -gim - sgn * (pre - p0)
            hre_ref[d + P - 1, :, o * HY_W:(o + 1) * HY_W] = re
            him_ref[d + P - 1, :, o * HY_W:(o + 1) * HY_W] = im


def _filter_spectrum(L, w1p, b1p, w2p, b2p, w3p, fp):
    T = HY_BLOCK
    z_np, dec_np = _filter_consts_np(L)
    c1, s1, _, _ = _dft_tables_np(T)
    ch, cl = _hi_lo(c1)
    sh, sl = _hi_lo(s1)
    n = HY_ORDER * HY_W
    nd = 2 * (L // T) - 1
    const = _const_spec
    per_layer = lambda shape: pl.BlockSpec((None,) + shape, lambda l: (l,) + tuple(0 for _ in shape))
    return pl.pallas_call(
        functools.partial(_filter_kernel, L=L),
        out_shape=[jax.ShapeDtypeStruct((DEPTH, nd, T, n), F32), jax.ShapeDtypeStruct((DEPTH, nd, T, n), F32)],
        grid=(DEPTH,),
        in_specs=[const((L, LANES)), const((L, HY_W)), per_layer((LANES, LANES)), per_layer((1, LANES)),
                  per_layer((LANES, LANES)), per_layer((1, LANES)), per_layer((LANES, 2 * n)), per_layer((1, LANES)),
                  const((T, T)), const((T, T)), const((T, T)), const((T, T))],
        out_specs=[per_layer((nd, T, n)), per_layer((nd, T, n))],
        compiler_params=pltpu.CompilerParams(
            dimension_semantics=("arbitrary",), vmem_limit_bytes=VMEM_LIMIT),
        name="hyena_filter",
    )(jnp.asarray(z_np), jnp.asarray(dec_np), w1p, b1p, w2p, b2p, w3p, fp, ch, cl, sh, sl)


def _schy_kernel(sc_ref, hy_ref, scw_ref, hyw_ref, skip_ref, fwd_ref, inv_ref, hre_ref, him_ref, out_ref, *, L):
    T = HY_BLOCK
    P = L // T
    NB = sc_ref.shape[0] // T
    R = sc_ref.shape[0]

    def gated_input(r0):
        t = sc_ref[r0:r0 + 8, SC_W:3 * SC_W].astype(F32)
        return t[:, 0:SC_W] * t[:, SC_W:2 * SC_W]
    out_ref[:, 0:SC_W] = sc_ref[:, 0:SC_W].astype(F32) * _conv3_tiles(gated_input, scw_ref[...], R, L)

    u3 = _conv3_tiles(lambda r0: hy_ref[r0:r0 + 8, :].astype(F32), hyw_ref[...], R, L)
    z = u3[:, 0:HY_W]
    gates = (u3[:, HY_W:2 * HY_W], u3[:, 2 * HY_W:3 * HY_W])
    fwd, inv = fwd_ref[...], inv_ref[...]
    lanes = lambda j: slice(j * HY_W, (j + 1) * HY_W)
    for o in range(HY_ORDER):
        zcat = jnp.concatenate([z[j * T:(j + 1) * T] for j in range(NB)], axis=1)
        spec = _dot(fwd, zcat.astype(BF16))
        yre, yim = [], []
        for bi in range(NB):
            s0, i = bi - bi % P, bi % P
            re = im = None
            for j in range(P):
                d = i - j + P - 1
                hre = hre_ref[d, :, lanes(o)]
                him = him_ref[d, :, lanes(o)]
                zc = spec[0:T, lanes(s0 + j)]
                zs = spec[T:2 * T, lanes(s0 + j)]
                pre = zc * hre + zs * him
                pim = zc * him - zs * hre
                re = pre if re is None else re + pre
                im = pim if im is None else im + pim
            yre.append(re)
            yim.append(im)
        prod = jnp.concatenate([jnp.concatenate(yre, axis=1), jnp.concatenate(yim, axis=1)], axis=0)
        ycat = _dot(inv, prod.astype(BF16)) * (1.0 / T)
        y = jnp.concatenate([ycat[:, lanes(bi)] for bi in range(NB)], axis=0)
        z = gates[o] * (y + skip_ref[o:o + 1, :] * z)
    out_ref[:, SC_W:SC_W + HY_W] = z


def _schy2_kernel(scc_ref, hyc_ref, scs_ref, hys_ref, scw_ref, hyw_ref, skip_ref, fwd_ref, inv_ref,
                  hrec_ref, himc_ref, hres_ref, hims_ref, outc_ref, outs_ref, *, Lc, Ls, n_ctx):
    i = pl.program_id(0)
    shared = (scw_ref, hyw_ref, skip_ref, fwd_ref, inv_ref)

    @pl.when(i < n_ctx)
    def _():
        _schy_kernel(scc_ref, hyc_ref, *shared, hrec_ref, himc_ref, outc_ref, L=Lc)

    @pl.when(i >= n_ctx)
    def _():
        _schy_kernel(scs_ref, hys_ref, *shared, hres_ref, hims_ref, outs_ref, L=Ls)


def _schy(proj_c, proj_s, sc_w, hy_w, skip, spec_c, spec_s, *, l, Lc, Ls):
    T = HY_BLOCK
    _, _, c2, s2 = _dft_tables_np(T)
    fwd = jnp.asarray(np.concatenate([c2, s2], axis=0).astype(np.float32).astype(BF16))
    inv = jnp.asarray(np.concatenate([c2, -s2], axis=1).astype(np.float32).astype(BF16))
    n = HY_ORDER * HY_W
    rows = HY_STEP_ROWS
    assert rows % Lc == 0 and rows % Ls == 0
    n_ctx, n_lat = proj_c.shape[0] // rows, proj_s.shape[0] // rows
    ctx, lat = _two_pass_maps(n_ctx)
    windows = lambda idx: [pl.BlockSpec((rows, 3 * SC_W), lambda i: (idx(i), 0)),
                           pl.BlockSpec((rows, 3 * HY_W), lambda i: (idx(i), 1))]
    spectra = lambda L: [_layer_spec((2 * (L // T) - 1, T, n), l)] * 2
    return pl.pallas_call(
        functools.partial(_schy2_kernel, Lc=Lc, Ls=Ls, n_ctx=n_ctx),
        out_shape=[jax.ShapeDtypeStruct((p.shape[0], SC_W + HY_W), F32) for p in (proj_c, proj_s)],
        grid=(n_ctx + n_lat,),
        in_specs=windows(ctx) + windows(lat) + [
            _layer_spec((3, SC_W), l), _layer_spec((3, 3 * HY_W), l), _layer_spec((HY_ORDER, HY_W), l),
            _const_spec((2 * T, T)), _const_spec((T, 2 * T)),
        ] + spectra(Lc) + spectra(Ls),
        out_specs=[pl.BlockSpec((rows, SC_W + HY_W), lambda i: (ctx(i), 0)),
                   pl.BlockSpec((rows, SC_W + HY_W), lambda i: (lat(i), 0))],
        compiler_params=pltpu.CompilerParams(
            dimension_semantics=("arbitrary",), vmem_limit_bytes=VMEM_LIMIT),
        name="sconv_hyena",
    )(proj_c, proj_c, proj_s, proj_s, sc_w, hy_w, skip, fwd, inv, *spec_c, *spec_s)


def _gate_prep_kernel(*refs, B, L, with_state):
    if with_state:
        g_ref, gb_ref, m0_ref, rows_ref, cols_ref = refs
    else:
        g_ref, gb_ref, rows_ref, cols_ref = refs
    R = B * N_GATE
    gb = gb_ref[:, 0:1]
    x = jnp.concatenate([g_ref[:, b * L:(b + 1) * L] + gb for b in range(B)], axis=0)
    rr = lax.broadcasted_iota(jnp.int32, (R, L), 0) & (N_GATE - 1)
    is_f = ((rr >= ML_H) & (rr < 2 * ML_H)) | (rr >= 3 * ML_H)
    fwd_rows = rr < 2 * ML_H
    pk = jnp.where(is_f, _log_sigmoid(x), x)
    r_i = lax.broadcasted_iota(jnp.int32, (L, L), 0)
    c_i = lax.broadcasted_iota(jnp.int32, (L, L), 1)
    triu = jnp.where(r_i <= c_i, 1.0, 0.0).astype(BF16)
    cs = _dot(jnp.concatenate(_split3(pk), axis=0), triu)
    cum = cs[0:R] + cs[R:2 * R] + cs[2 * R:3 * R]
    suf = cum[:, L - 1:L] - cum + pk
    bsum = pltpu.roll(jnp.where(fwd_rows, cum, suf), R - ML_H, axis=0)
    r = pk - bsum
    cm = jnp.where(fwd_rows, _scan_max(r, reverse=False), _scan_max(r, reverse=True))
    if with_state:
        m0 = jnp.concatenate([m0_ref[b][:, 0:1] for b in range(B)], axis=0)
        m = jnp.maximum(m0, cm)
    else:
        m = jnp.maximum(cm, 0.0)
    rl = r * LOG2E
    ml = m * LOG2E
    mt = bsum + m
    pieces = [rl, ml, jnp.exp(-mt), mt]
    if not with_state:
        pieces.append(jnp.exp2(rl - jnp.where(fwd_rows, ml[:, L - 1:L], ml[:, 0:1])))
    pad = jnp.zeros((LANES - N_GATE, L), F32)
    for b in range(B):
        bs = slice(b * N_GATE, (b + 1) * N_GATE)
        for j, piece in enumerate(pieces):
            rows_ref[b, j * N_GATE:(j + 1) * N_GATE, :] = piece[bs]
        cols_ref[b * L:(b + 1) * L, :] = jnp.concatenate([rl[bs], pad], axis=0).T


def _gate_prep(g_t, gate_b, m0, *, l, B, L):
    with_state = m0 is not None
    n_rows = (4 if with_state else 5) * N_GATE
    full = lambda shape: pl.BlockSpec(shape, lambda i: tuple(0 for _ in shape))
    args = [g_t, gate_b]
    in_specs = [full((N_GATE, B * L)), pl.BlockSpec((None, N_GATE, LANES), lambda i: (l, 0, 0))]
    if with_state:
        args.append(m0)
        in_specs.append(pl.BlockSpec((B, None, N_GATE, LANES), lambda i: (0, l, 0, 0)))
    return pl.pallas_call(
        functools.partial(_gate_prep_kernel, B=B, L=L, with_state=with_state),
        out_shape=[jax.ShapeDtypeStruct((B, n_rows, L), F32), jax.ShapeDtypeStruct((B * L, LANES), F32)],
        grid=(1,),
        in_specs=in_specs,
        out_specs=[full((B, n_rows, L)), full((B * L, LANES))],
        compiler_params=pltpu.CompilerParams(
            dimension_semantics=("arbitrary",), vmem_limit_bytes=VMEM_LIMIT),
        name="gate_prep",
    )(*args)


def _mlstm_kernel(*refs, L, with_state, n_prev):
    if with_state:
        (q_ref, k_ref, v_ref, o_ref, rows_ref, cols_ref, cw_ref, ng_ref,
         c0_ref, n0_ref, m0_ref, y_ref) = refs
    elif n_prev:
        (q_ref, k_ref, v_ref, o_ref, rows_ref, cols_ref, cw_ref, ng_ref,
         cprev_ref, nprev_ref, mprev_ref, y_ref, cout_ref, nout_ref, mout_ref) = refs
        cout_ref[:, 0:n_prev] = cprev_ref[...]
        nout_ref[:, 0:n_prev] = nprev_ref[...]
        mout_ref[:, 0:n_prev] = mprev_ref[...]
        cout_ref, nout_ref, mout_ref = (r.at[:, n_prev] for r in (cout_ref, nout_ref, mout_ref))
    else:
        (q_ref, k_ref, v_ref, o_ref, rows_ref, cols_ref, cw_ref, ng_ref,
         y_ref, cout_ref, nout_ref, mout_ref) = refs

    S = rows_ref.shape[0]
    tb = min(L, ATT_BLOCK)
    cw = cw_ref[...]
    q_all = _silu(_conv3(q_ref[...].astype(F32), cw[:, 0:ML_W], L))
    k_all = _silu(_conv3(k_ref[...].astype(F32), cw[:, ML_W:2 * ML_W], L)) * (ML_DH ** -0.5)

    s_loc = lax.broadcasted_iota(jnp.int32, (tb, tb), 0)
    t_loc = lax.broadcasted_iota(jnp.int32, (tb, tb), 1)
    masks = (s_loc <= t_loc, s_loc >= t_loc)
    ones_rows = jnp.ones((2 * 8, L), BF16)
    ng = ng_ref[...]
    for h in range(ML_H):
        hs = slice(h * ML_DH, (h + 1) * ML_DH)
        seqs = []
        for s in range(S):
            sl = slice(s * L, (s + 1) * L)
            rl16, ml16, em16, mt16 = (rows_ref[s, j * N_GATE:(j + 1) * N_GATE, :] for j in range(4))
            q = q_all[sl, hs]
            k = k_all[sl, hs]
            v = v_ref[sl, hs].astype(F32)
            vext_t = jnp.concatenate([v.T.astype(BF16), ones_rows], axis=0)
            rlb = [jnp.broadcast_to(cols_ref[sl, 2 * ML_H * d + h:2 * ML_H * d + h + 1], (L, tb)) for d in range(2)]
            seqs.append(dict(q=q, k=k, qb=q.astype(BF16), kb=k.astype(BF16), vb=v.astype(BF16), vext_t=vext_t,
                             rlb=rlb, ml16=ml16, em16=em16, mt16=mt16))
            if with_state:
                seqs[s]["c0_t"] = [c0_ref[s, d, h].T.astype(BF16) for d in range(2)]
                seqs[s]["n0"] = _split2(n0_ref[s])
        for r0 in range(0, L, tb):
            rs = slice(r0, r0 + tb)
            for s, sq in enumerate(seqs):
                s_t = _dot_nt(sq["kb"], sq["qb"][rs])
                if with_state:
                    qb_rs = sq["qb"][rs]
                    qn_rows = _dot_nt(sq["n0"][0], qb_rs) + _dot_nt(sq["n0"][1], qb_rs)
                hout_t = None
                for d in range(2):
                    o8 = 2 * ML_H * d + h
                    ml_row = sq["ml16"][o8:o8 + 1, rs]
                    e_diag = jnp.exp2(jnp.where(masks[d], sq["rlb"][d][rs] - ml_row, NEG_BIG))
                    acc = _dot(sq["vext_t"][:, rs], (s_t[rs] * e_diag).astype(BF16))
                    side = slice(r0 + tb, L) if d else slice(0, r0)
                    if side.stop > side.start:
                        e_side = jnp.exp2(sq["rlb"][d][side] - ml_row)
                        acc = acc + _dot(sq["vext_t"][:, side], (s_t[side] * e_side).astype(BF16))
                    num_t = acc[0:ML_DH]
                    den = acc[ML_DH:ML_DH + 1]
                    if with_state:
                        wp = jnp.exp2(m0_ref[s, o8:o8 + 1, 0:1] * LOG2E - ml_row)
                        den = den + wp * qn_rows[d * ML_H + h:d * ML_H + h + 1]
                        num_t = num_t + wp * _dot_nt(sq["c0_t"][d], sq["qb"][rs])
                    contrib = num_t * (1.0 / jnp.maximum(jnp.abs(den), sq["em16"][o8:o8 + 1, rs]))
                    hout_t = contrib if hout_t is None else hout_t + contrib
                hn_t = hout_t * lax.rsqrt(jnp.mean(hout_t * hout_t, axis=0, keepdims=True) + EPS)
                rows_out = slice(s * L + r0, s * L + r0 + tb)
                y_ref[rows_out, hs] = _sigmoid(o_ref[rows_out, hs].astype(F32)) * (hn_t.T * ng[:, hs])

        if not with_state:
            for s, sq in enumerate(seqs):
                w16 = rows_ref[s, 4 * N_GATE:5 * N_GATE, :]
                k_t = sq["k"].T
                w_hi, w_lo = _split2(w16)
                n_rows = _dot(w_hi, sq["kb"]) + _dot(w_lo, sq["kb"])
                for d in range(2):
                    o8 = 2 * ML_H * d + h
                    end = 0 if d else L - 1
                    cout_ref[s, d, h] = _dot((k_t * w16[o8:o8 + 1, :]).astype(BF16), sq["vb"])
                    nout_ref[s, d * ML_H + h:d * ML_H + h + 1, :] = n_rows[o8:o8 + 1, :]
                    mout_ref[s, d * ML_H + h:d * ML_H + h + 1, :] = jnp.broadcast_to(
                        sq["mt16"][o8:o8 + 1, end:end + 1], (1, LANES))


def _mlstm(proj, g_t, gate_b, conv_w, norm_g, state, prev, *, l, B, L):
    with_state = state is not None
    n_prev = 0 if prev is None else l
    rows, cols = _gate_prep(g_t, gate_b, state[2] if with_state else None, l=l, B=B, L=L)
    S = max(1, ML_STEP_ROWS // L)
    SL = S * L
    col0 = (3 * SC_W + 3 * HY_W) // ML_W
    in_specs = [
        pl.BlockSpec((SL, ML_W), lambda b: (b, col0)),
        pl.BlockSpec((SL, ML_W), lambda b: (b, col0 + 1)),
        pl.BlockSpec((SL, ML_W), lambda b: (b, col0 + 2)),
        pl.BlockSpec((SL, ML_W), lambda b: (b, col0 + 3)),
        pl.BlockSpec((S, rows.shape[1], L), lambda b: (b, 0, 0)),
        pl.BlockSpec((SL, LANES), lambda b: (b, 0)),
        _layer_spec((3, 2 * ML_W), l), _layer_spec((1, ML_W), l),
    ]
    args = [proj, proj, proj, proj, rows, cols, conv_w, norm_g]
    y_shape = jax.ShapeDtypeStruct((B * L, ML_W), F32)
    y_spec = pl.BlockSpec((SL, ML_W), lambda b: (b, 0))
    c_blk, n_blk, m_blk = (2, ML_H, ML_DH, ML_DH), (2 * ML_H, ML_DH), (2 * ML_H, LANES)
    if with_state:
        c0, n0, m0 = state
        in_specs += [
            pl.BlockSpec((S, None) + c_blk, lambda b: (b, l, 0, 0, 0, 0)),
            pl.BlockSpec((S, None) + n_blk, lambda b: (b, l, 0, 0)),
            pl.BlockSpec((S, None, N_GATE, LANES), lambda b: (b, l, 0, 0)),
        ]
        args += [c0, n0, m0]
        out_shape, out_specs = y_shape, y_spec
    else:
        lead = (l + 1,) if n_prev else ()
        if n_prev:
            in_specs += [pl.BlockSpec((S, n_prev) + blk, lambda b, nz=len(blk) + 1: (b,) + (0,) * nz)
                         for blk in (c_blk, n_blk, m_blk)]
            args += list(prev)
        out_shape = [y_shape] + [jax.ShapeDtypeStruct((B,) + lead + blk, F32) for blk in (c_blk, n_blk, m_blk)]
        out_specs = [y_spec] + [pl.BlockSpec((S,) + lead + blk, lambda b, nz=len(lead) + len(blk): (b,) + (0,) * nz)
                                for blk in (c_blk, n_blk, m_blk)]
    return pl.pallas_call(
        functools.partial(_mlstm_kernel, L=L, with_state=with_state, n_prev=n_prev),
        out_shape=out_shape,
        grid=(B // S,),
        in_specs=in_specs,
        out_specs=out_specs,
        compiler_params=pltpu.CompilerParams(
            dimension_semantics=("arbitrary",), vmem_limit_bytes=VMEM_LIMIT),
        name="mlstm",
    )(*args)


def _outmlp_kernel(xc_ref, yac_ref, ybc_ref, xl_ref, yal_ref, ybl_ref, mod_ref, wo_ref, g2_ref, w1_ref, w2_ref,
                   gf_ref, oc_ref, ol_ref, *, final, n_ctx):
    i = pl.program_id(0)
    shared = (mod_ref, wo_ref, g2_ref, w1_ref, w2_ref, gf_ref)

    @pl.when(i < n_ctx)
    def _():
        _outmlp_body(xc_ref, yac_ref, ybc_ref, *shared, oc_ref, final=final)

    @pl.when(i >= n_ctx)
    def _():
        _outmlp_body(xl_ref, yal_ref, ybl_ref, *shared, ol_ref, final=final)


def _outmlp_body(x_ref, ya_ref, yb_ref, mod_ref, wo_ref, g2_ref, w1_ref, w2_ref, gf_ref, o_ref, *, final):
    mod = mod_ref[0]
    g1 = mod[:, 2 * D_MODEL:3 * D_MODEL]
    sh2 = mod[:, 3 * D_MODEL:4 * D_MODEL]
    sc2 = mod[:, 4 * D_MODEL:5 * D_MODEL]
    g2 = mod[:, 5 * D_MODEL:6 * D_MODEL]
    na = SC_W + HY_W
    attn = _dot(ya_ref[...].astype(BF16), wo_ref[0:na, :]) + _dot(yb_ref[...].astype(BF16), wo_ref[na:, :])
    x = x_ref[...] + g1 * attn
    u = ((_rms(x) * g2_ref[...]) * (1.0 + sc2) + sh2).astype(BF16)
    tf = 1024
    acc = jnp.zeros(x.shape, F32)
    for j in range(D_FF // tf):
        hcol = _dot(u, w1_ref[:, j * tf:(j + 1) * tf])
        hcol = jnp.square(jnp.maximum(hcol, 0.0)).astype(BF16)
        acc = acc + _dot(hcol, w2_ref[j * tf:(j + 1) * tf, :])
    x = x + g2 * acc
    if final:
        x = _rms(x) * gf_ref[...]
    o_ref[...] = x


def _outmlp(ctx_ops, lat_ops, mods, w_out, g2, w1, w2, gf, *, l, Ld, final):
    tm = ROW_BLOCK
    Tc, Tl = ctx_ops[0].shape[0], lat_ops[0].shape[0]
    n_ctx, n_lat = Tc // tm, Tl // tm
    ctx, lat = _two_pass_maps(n_ctx)
    token_specs = lambda idx: [pl.BlockSpec((tm, width), lambda i: (idx(i), 0))
                               for width in (D_MODEL, SC_W + HY_W, ML_W)]
    return pl.pallas_call(
        functools.partial(_outmlp_kernel, final=final, n_ctx=n_ctx),
        out_shape=[jax.ShapeDtypeStruct((Tc, D_MODEL), F32), jax.ShapeDtypeStruct((Tl, D_MODEL), F32)],
        grid=(n_ctx + n_lat,),
        in_specs=token_specs(ctx) + token_specs(lat) + [
            _mod_spec(l, Ld, tm, n_ctx),
            _const_spec((D_MODEL, D_MODEL)), _layer_spec((1, D_MODEL), l),
            _const_spec((D_MODEL, D_FF)), _const_spec((D_FF, D_MODEL)), _const_spec((1, D_MODEL)),
        ],
        out_specs=[pl.BlockSpec((tm, D_MODEL), lambda i: (ctx(i), 0)), pl.BlockSpec((tm, D_MODEL), lambda i: (lat(i), 0))],
        compiler_params=pltpu.CompilerParams(
            dimension_semantics=("arbitrary",), vmem_limit_bytes=VMEM_LIMIT),
        name="outproj_mlp",
    )(*ctx_ops, *lat_ops, mods, w_out, g2, w1, w2, gf)


def kernel(x_prompt, x_sample, state_C, state_n, state_m, c, c_ctx, norm1_g, ada_w, ada_b, w_in, sc_conv_w, hy_conv_w, hy_w1, hy_b1, hy_w2, hy_b2, hy_w3, hy_freq, hy_skip, ml_conv_w, ml_gate_b, ml_norm_g, w_out, norm2_g, mlp_w1, mlp_w2, norm_f_g):
    B, L = x_prompt.shape[0], x_prompt.shape[1]
    Bd, Ld = x_sample.shape[0], x_sample.shape[1]

    cond = jnp.concatenate([c_ctx[None, :], c, jnp.zeros((8 - 1 - Bd, D_MODEL), F32)], axis=0)
    mods = _ada(cond, ada_w, ada_b).reshape(DEPTH * 8, 1, 6 * D_MODEL)

    w_in_t = jnp.swapaxes(w_in, 1, 2).astype(BF16)
    w_gate = w_in_t[:, N_MAIN:, :]
    gate_b = jnp.broadcast_to(ml_gate_b[:, :, None], (DEPTH, N_GATE, LANES))
    pad_h = LANES - HY_HID
    w1p = jnp.pad(hy_w1, ((0, 0), (0, LANES - HY_EMB), (0, pad_h)))
    b1p = jnp.pad(hy_b1, ((0, 0), (0, pad_h)))
    w2p = jnp.pad(hy_w2, ((0, 0), (0, pad_h), (0, pad_h)))
    b2p = jnp.pad(hy_b2, ((0, 0), (0, pad_h)))
    w3p = jnp.pad(hy_w3, ((0, 0), (0, pad_h), (0, 0)))
    fp = jnp.pad(hy_freq, ((0, 0), (0, pad_h)))
    pos = jnp.asarray(_pos_embed_np(Ld))
    m0_all = jnp.pad(state_m, ((0, 0), (0, 0), (0, 0), (0, ML_H))).reshape(Bd, DEPTH, N_GATE, 1)
    m0_all = jnp.broadcast_to(m0_all, (Bd, DEPTH, N_GATE, LANES))
    n0_all = state_n.reshape(Bd, DEPTH, 2 * ML_H, ML_DH)
    gf = norm_f_g[None, :]
    g1 = norm1_g[:, None, :]
    g2 = norm2_g[:, None, :]
    ml_ng = ml_norm_g[:, None, :]
    filt_args = (w1p, b1p[:, None, :], w2p, b2p[:, None, :], w3p, fp[:, None, :])
    hre_c, him_c = _filter_spectrum(L, *filt_args)
    hre_s, him_s = _filter_spectrum(Ld, *filt_args)
    state = (state_C, n0_all, m0_all)

    xp = x_prompt.reshape(B * L, D_MODEL)
    xs = x_sample.reshape(Bd * Ld, D_MODEL)
    prev = None
    for l in range(DEPTH):
        final = l == DEPTH - 1

        res = _inproj(xp, xs, pos if l == 0 else None, mods, g1, w_in_t, w_gate, (w_out, mlp_w1, mlp_w2), l=l, Ld=Ld)
        proj_c, g_c, proj_s, g_s = res[:4]
        if l == 0:
            xs = res[4]
        w_out_b, w1_b, w2_b = res[-3:]

        ya_c, ya_s = _schy(proj_c, proj_s, sc_conv_w, hy_conv_w, hy_skip, (hre_c, him_c), (hre_s, him_s),
                           l=l, Lc=L, Ls=Ld)

        yb_c, *states = _mlstm(proj_c, g_c, gate_b, ml_conv_w, ml_ng, None, prev, l=l, B=B, L=L)
        if not final:
            prev = [s.reshape((B, 1) + s.shape[1:]) for s in states] if l == 0 else states

        yb_s = _mlstm(proj_s, g_s, gate_b, ml_conv_w, ml_ng, state, None, l=l, B=Bd, L=Ld)

        xp, xs = _outmlp((xp, ya_c, yb_c), (xs, ya_s, yb_s), mods, w_out_b, g2, w1_b, w2_b, gf,
                         l=l, Ld=Ld, final=final)

    y_prompt = xp.reshape(B, L, D_MODEL)
    y_sample = xs.reshape(Bd, Ld, D_MODEL)
    new_c, new_n, new_m = states
    return (y_prompt, y_sample, new_c, new_n.reshape(B, DEPTH, 2, ML_H, ML_DH),
            new_m[..., 0].reshape(B, DEPTH, 2, ML_H))
```

```python
import functools
import math

import numpy as np
import jax
import jax.numpy as jnp
from jax import lax
from jax.experimental import pallas as pl
from jax.experimental.pallas import tpu as pltpu

F32 = jnp.float32
BF16 = jnp.bfloat16

D_MODEL = 1024
DEPTH = 2
GRID_W = 64
SC_W = 256
HY_W = 256
ML_W = 512
ML_H = 4
ML_DH = ML_W // ML_H
D_FF = 4 * D_MODEL
HY_ORDER = 2
HY_BANDS = 16
HY_EMB = 1 + 2 * HY_BANDS
HY_HID = 64
HY_DECAY_SHORT = 0.3
HY_DECAY_LONG = 1.5
HY_DECAY_TARGET = 1e-2
EPS = 1e-6
N_MAIN = 3 * SC_W + 3 * HY_W + 4 * ML_W
N_GATE = 4 * ML_H
LANES = 128
ROW_BLOCK = 512
ATT_BLOCK = 256
HY_BLOCK = 256
HY_STEP_ROWS = 1024
ML_STEP_ROWS = 1024
PROJ_DTYPE = BF16
NEG_BIG = -1e30
LOG2E = math.log2(math.e)
VMEM_LIMIT = 60 * 1024 * 1024


def _dot(a, b):
    return jnp.dot(a, b, preferred_element_type=F32)


def _dot_nt(a, b):
    return lax.dot_general(a, b, (((1,), (1,)), ((), ())), preferred_element_type=F32)


def _split2(a):
    hi = a.astype(BF16)
    lo = (a - hi.astype(F32)).astype(BF16)
    return hi, lo


def _split3(a):
    hi = a.astype(BF16)
    r = a - hi.astype(F32)
    mid = r.astype(BF16)
    lo = (r - mid.astype(F32)).astype(BF16)
    return hi, mid, lo


def _dot3(a, b):
    ah, al = _split2(a)
    bh, bl = _split2(b)
    return _dot(ah, bh) + _dot(al, bh) + _dot(ah, bl)


def _dot3_nt(a, b):
    ah, al = _split2(a)
    bh, bl = _split2(b)
    return _dot_nt(ah, bh) + _dot_nt(al, bh) + _dot_nt(ah, bl)


def _dot3_tab(th, tl, b):
    bh, bl = _split2(b)
    return _dot(th, bh) + _dot(tl, bh) + _dot(th, bl)


def _sigmoid(x):
    return 0.5 * jnp.tanh(0.5 * x) + 0.5


def _silu(x):
    return x * _sigmoid(x)


def _log_sigmoid(x):
    return jnp.minimum(x, 0.0) - jnp.log(1.0 + jnp.exp(-jnp.abs(x)))


def _conv3(x, w, seq_len=None):
    R = x.shape[0]
    seq_len = R if seq_len is None else seq_len
    pos = lax.broadcasted_iota(jnp.int32, x.shape, 0) & (seq_len - 1)
    xm = jnp.where(pos == 0, 0.0, pltpu.roll(x, 1, axis=0))
    xp = jnp.where(pos == seq_len - 1, 0.0, pltpu.roll(x, R - 1, axis=0))
    return xm * w[0:1, :] + x * w[1:2, :] + xp * w[2:3, :]


def _conv3_tiles(load, w, n_rows, seq_len):
    C = w.shape[1]
    sub = lax.broadcasted_iota(jnp.int32, (8, C), 0)
    w0, w1, w2 = w[0:1, :], w[1:2, :], w[2:3, :]
    zero = jnp.zeros((8, C), F32)
    out = []
    for r0 in range(0, n_rows, 8):
        cur = load(r0)
        prev = zero if r0 % seq_len == 0 else load(r0 - 8)
        nxt = zero if (r0 + 8) % seq_len == 0 else load(r0 + 8)
        xm = jnp.where(sub == 0, pltpu.roll(prev, 1, axis=0), pltpu.roll(cur, 1, axis=0))
        xp = jnp.where(sub == 7, pltpu.roll(nxt, 7, axis=0), pltpu.roll(cur, 7, axis=0))
        out.append(xm * w0 + cur * w1 + xp * w2)
    return jnp.concatenate(out, axis=0)


def _rms(x):
    return x * lax.rsqrt(jnp.mean(x * x, axis=-1, keepdims=True) + EPS)


def _scan_max(x, *, reverse):
    n = x.shape[1]
    lane = lax.broadcasted_iota(jnp.int32, x.shape, 1)
    d = 1
    while d < n:
        if reverse:
            shifted = jnp.where(lane < n - d, pltpu.roll(x, n - d, axis=1), NEG_BIG)
        else:
            shifted = jnp.where(lane >= d, pltpu.roll(x, d, axis=1), NEG_BIG)
        x = jnp.maximum(x, shifted)
        d *= 2
    return x


def _hi_lo(a64):
    a = a64.astype(np.float32)
    hi = a.astype(BF16)
    lo = (a - hi.astype(np.float32)).astype(BF16)
    return jnp.asarray(hi), jnp.asarray(lo)


def _const_spec(shape):
    return pl.BlockSpec(shape, lambda *_: tuple(0 for _ in shape), pipeline_mode=pl.Buffered(1))


def _layer_spec(shape, l):
    return pl.BlockSpec((None,) + tuple(shape), lambda *_: (l,) + tuple(0 for _ in shape),
                        pipeline_mode=pl.Buffered(1))


@functools.lru_cache(maxsize=None)
def _dft_tables_np(L):
    k = np.arange(L, dtype=np.int64)
    m1 = ((2 * k[:, None] + 1) * k[None, :]) % (4 * L)
    a1 = np.pi * m1.astype(np.float64) / (2 * L)
    m2 = ((2 * k[:, None] + 1) * (2 * k[None, :] + 1)) % (8 * L)
    a2 = np.pi * m2.astype(np.float64) / (4 * L)
    return np.cos(a1), np.sin(a1), np.cos(a2), np.sin(a2)


@functools.lru_cache(maxsize=None)
def _filter_consts_np(L):
    t_idx = np.arange(L, dtype=np.float64)
    t = t_idx / (L - 1)
    bands = np.arange(1, HY_BANDS + 1, dtype=np.float64)
    ang = 2.0 * math.pi * t_idx[:, None] * bands[None, :] / L
    z = np.zeros((L, LANES), np.float64)
    z[:, 0] = t
    z[:, 1:1 + HY_BANDS] = np.cos(ang)
    z[:, 1 + HY_BANDS:1 + 2 * HY_BANDS] = -np.sin(ang)
    lin = np.linspace(math.log(HY_DECAY_TARGET) / HY_DECAY_LONG,
                      math.log(HY_DECAY_TARGET) / HY_DECAY_SHORT, HY_W).astype(np.float32)
    deltas = np.abs(lin).astype(np.float64)
    decay = np.exp(-t[:, None] * deltas[None, :])
    return z.astype(np.float32), decay.astype(np.float32)


@functools.lru_cache(maxsize=None)
def _pos_embed_np(L):
    rows = L // GRID_W
    r, cidx = np.meshgrid(np.arange(rows, dtype=np.float64), np.arange(GRID_W, dtype=np.float64), indexing="ij")
    r = r.reshape(-1)
    cidx = cidx.reshape(-1)
    quarter = D_MODEL // 4
    omega = 1.0 / (10000.0 ** (np.arange(quarter, dtype=np.float64) / quarter))
    ar = r[:, None] * omega[None, :]
    ac = cidx[:, None] * omega[None, :]
    return np.concatenate([np.sin(ar), np.cos(ar), np.sin(ac), np.cos(ac)], axis=-1).astype(np.float32)


def _ada_kernel(c_ref, w_ref, b_ref, o_ref):
    sc = _silu(c_ref[...])
    o_ref[0] = _dot3(sc, w_ref[0]) + b_ref[0]


def _ada(cond, ada_w, ada_b):
    tn = 1536
    nd = 6 * D_MODEL
    return pl.pallas_call(
        _ada_kernel,
        out_shape=jax.ShapeDtypeStruct((DEPTH, 8, nd), F32),
        grid=(DEPTH, nd // tn),
        in_specs=[
            pl.BlockSpec((8, D_MODEL), lambda l, j: (0, 0)),
            pl.BlockSpec((1, D_MODEL, tn), lambda l, j: (l, 0, j)),
            pl.BlockSpec((1, 1, tn), lambda l, j: (l, 0, j)),
        ],
        out_specs=pl.BlockSpec((1, 8, tn), lambda l, j: (l, 0, j)),
        compiler_params=pltpu.CompilerParams(
            dimension_semantics=("arbitrary", "arbitrary"), vmem_limit_bytes=VMEM_LIMIT),
        name="ada_mod",
    )(cond, ada_w, ada_b.reshape(DEPTH, 1, nd))


def _inproj_body(x, mod_ref, g_ref, w_ref, wg_ref, proj_ref, gate_ref):
    mod = mod_ref[0]
    sh1 = mod[:, 0:D_MODEL]
    sc1 = mod[:, D_MODEL:2 * D_MODEL]
    u = (_rms(x) * g_ref[...]) * (1.0 + sc1) + sh1
    ub = u.astype(BF16)
    tn = 512
    for j in range(N_MAIN // tn):
        proj_ref[:, j * tn:(j + 1) * tn] = _dot_nt(ub, w_ref[j * tn:(j + 1) * tn, :]).astype(PROJ_DTYPE)
    gate_ref[...] = _dot_nt(wg_ref[...], ub)


def _inproj_kernel(*refs, with_pos, n_cast, n_ctx):
    it = iter(refs)
    xc_ref, xl_ref = next(it), next(it)
    pos_ref = next(it) if with_pos else None
    mod_ref, g_ref, w_ref, wg_ref = (next(it) for _ in range(4))
    cast_in = [next(it) for _ in range(n_cast)]
    projc_ref, gatec_ref, projl_ref, gatel_ref = (next(it) for _ in range(4))
    xo_ref = next(it) if with_pos else None
    cast_out = [next(it) for _ in range(n_cast)]
    i = pl.program_id(0)

    @pl.when(i < n_ctx)
    def _():
        for src, dst in zip(cast_in, cast_out):
            dst[...] = src[...].astype(BF16)
        _inproj_body(xc_ref[...], mod_ref, g_ref, w_ref, wg_ref, projc_ref, gatec_ref)

    @pl.when(i >= n_ctx)
    def _():
        x = xl_ref[...]
        if with_pos:
            x = x + pos_ref[...]
            xo_ref[...] = x
        _inproj_body(x, mod_ref, g_ref, w_ref, wg_ref, projl_ref, gatel_ref)


def _two_pass_maps(n_ctx):
    ctx = lambda i: jnp.minimum(i, n_ctx - 1)
    lat = lambda i: jnp.maximum(i - n_ctx, 0)
    return ctx, lat


def _mod_spec(l, Ld, tm, n_ctx):
    return pl.BlockSpec((1, 1, 6 * D_MODEL),
                        lambda i: (l * 8 + jnp.where(i < n_ctx, 0, 1 + ((i - n_ctx) * tm) // Ld), 0, 0))


def _inproj(xc, xl, pos, mods, g, w_in_t, w_gate, cast=(), *, l, Ld):
    tm = ROW_BLOCK
    Tc, Tl = xc.shape[0], xl.shape[0]
    n_ctx, n_lat = Tc // tm, Tl // tm
    per_seq = Ld // tm
    ctx, lat = _two_pass_maps(n_ctx)
    with_pos = pos is not None
    in_specs = [pl.BlockSpec((tm, D_MODEL), lambda i: (ctx(i), 0)), pl.BlockSpec((tm, D_MODEL), lambda i: (lat(i), 0))]
    args = [xc, xl]
    if with_pos:
        in_specs.append(pl.BlockSpec((tm, D_MODEL), lambda i: (lat(i) % per_seq, 0)))
        args.append(pos)
    in_specs += [
        _mod_spec(l, Ld, tm, n_ctx),
        _layer_spec((1, D_MODEL), l),
        _layer_spec((N_MAIN, D_MODEL), l),
        _layer_spec((N_GATE, D_MODEL), l),
    ]
    args += [mods, g, w_in_t, w_gate]
    out_shape, out_specs = [], []
    for T, idx in ((Tc, ctx), (Tl, lat)):
        out_shape += [jax.ShapeDtypeStruct((T, N_MAIN), PROJ_DTYPE), jax.ShapeDtypeStruct((N_GATE, T), F32)]
        out_specs += [pl.BlockSpec((tm, N_MAIN), lambda i, idx=idx: (idx(i), 0)),
                      pl.BlockSpec((N_GATE, tm), lambda i, idx=idx: (0, idx(i)))]
    if with_pos:
        out_shape.append(jax.ShapeDtypeStruct((Tl, D_MODEL), F32))
        out_specs.append(pl.BlockSpec((tm, D_MODEL), lambda i: (lat(i), 0)))
    for w in cast:
        _, rows, cols = w.shape
        in_specs.append(pl.BlockSpec((None, rows // n_ctx, cols), lambda i: (l, ctx(i), 0)))
        args.append(w)
        out_shape.append(jax.ShapeDtypeStruct((rows, cols), BF16))
        out_specs.append(pl.BlockSpec((rows // n_ctx, cols), lambda i: (ctx(i), 0)))
    return pl.pallas_call(
        functools.partial(_inproj_kernel, with_pos=with_pos, n_cast=len(cast), n_ctx=n_ctx),
        out_shape=out_shape,
        grid=(n_ctx + n_lat,),
        in_specs=in_specs,
        out_specs=out_specs,
        compiler_params=pltpu.CompilerParams(
            dimension_semantics=("arbitrary",), vmem_limit_bytes=VMEM_LIMIT),
        name="inproj",
    )(*args)


def _filter_kernel(z_ref, dec_ref, w1_ref, b1_ref, w2_ref, b2_ref, w3_ref, f_ref,
                   ch_ref, cl_ref, sh_ref, sl_ref, hre_ref, him_ref, *, L):
    T = HY_BLOCK
    P = L // T
    f = f_ref[...]
    hdn = jnp.sin(f * (_dot3(z_ref[...], w1_ref[...]) + b1_ref[...]))
    hdn = jnp.sin(f * (_dot3(hdn, w2_ref[...]) + b2_ref[...]))
    h = _dot3(hdn, w3_ref[...])
    dec = dec_ref[...]
    row = lax.broadcasted_iota(jnp.int32, (L, HY_W), 0)
    halves = []
    for o in range(HY_ORDER):
        hf = h[:, (2 * o) * HY_W:(2 * o + 1) * HY_W] * dec
        hb = jnp.where(row == 0, 0.0, h[:, (2 * o + 1) * HY_W:(2 * o + 2) * HY_W] * dec)
        nrm = jnp.sum(jnp.abs(hf), axis=0, keepdims=True) + jnp.sum(jnp.abs(hb), axis=0, keepdims=True)
        inv = 1.0 / nrm
        halves.append((hf * inv, hb * inv))
    blocks = [half[e * T:(e + 1) * T] for pair in halves for half in pair for e in range(P)]
    x = jnp.concatenate(blocks, axis=1)
    xre = _dot3_tab(ch_ref[...], cl_ref[...], x)
    xim = -_dot3_tab(sh_ref[...], sl_ref[...], x)
    sgn = jnp.where((lax.broadcasted_iota(jnp.int32, (T, HY_W), 0) & 1) == 0, 1.0, -1.0)
    for o in range(HY_ORDER):
        def half_spec(which, e):
            j = (o * 2 + which) * P + e
            cs = slice(j * HY_W, (j + 1) * HY_W)
            return xre[:, cs], xim[:, cs], halves[o][which][e * T:e * T + 1]
        for d in range(-(P - 1), P):
            if d == 0:
                (fre, fim, _), (gre, gim, _) = half_spec(0, 0), half_spec(1, 0)
                re, im = fre + gre, fim - gim
            elif d > 0:
                (fre, fim, _), (pre, pim, p0) = half_spec(0, d), half_spec(0, d - 1)
                re, im = fre - sgn * pim, fim + sgn * (pre - p0)
            else:
                (gre, gim, _), (pre, pim, p0) = half_spec(1, -d), half_spec(1, -d - 1)
                re, im = gre - sgn * pim, -gim - sgn * (pre - p0)
            hre_ref[d + P - 1, :, o * HY_W:(o + 1) * HY_W] = re
            him_ref[d + P - 1, :, o * HY_W:(o + 1) * HY_W] = im


def _filter_spectrum(L, w1p, b1p, w2p, b2p, w3p, fp):
    T = HY_BLOCK
    z_np, dec_np = _filter_consts_np(L)
    c1, s1, _, _ = _dft_tables_np(T)
    ch, cl = _hi_lo(c1)
    sh, sl = _hi_lo(s1)
    n = HY_ORDER * HY_W
    nd = 2 * (L // T) - 1
    const = _const_spec
    per_layer = lambda shape: pl.BlockSpec((None,) + shape, lambda l: (l,) + tuple(0 for _ in shape))
    return pl.pallas_call(
        functools.partial(_filter_kernel, L=L),
        out_shape=[jax.ShapeDtypeStruct((DEPTH, nd, T, n), F32), jax.ShapeDtypeStruct((DEPTH, nd, T, n), F32)],
        grid=(DEPTH,),
        in_specs=[const((L, LANES)), const((L, HY_W)), per_layer((LANES, LANES)), per_layer((1, LANES)),
                  per_layer((LANES, LANES)), per_layer((1, LANES)), per_layer((LANES, 2 * n)), per_layer((1, LANES)),
                  const((T, T)), const((T, T)), const((T, T)), const((T, T))],
        out_specs=[per_layer((nd, T, n)), per_layer((nd, T, n))],
        compiler_params=pltpu.CompilerParams(
            dimension_semantics=("arbitrary",), vmem_limit_bytes=VMEM_LIMIT),
        name="hyena_filter",
    )(jnp.asarray(z_np), jnp.asarray(dec_np), w1p, b1p, w2p, b2p, w3p, fp, ch, cl, sh, sl)


def _schy_kernel(sc_ref, hy_ref, scw_ref, hyw_ref, skip_ref, fwd_ref, inv_ref, hre_ref, him_ref, out_ref, *, L):
    T = HY_BLOCK
    P = L // T
    NB = sc_ref.shape[0] // T
    R = sc_ref.shape[0]

    def gated_input(r0):
        t = sc_ref[r0:r0 + 8, SC_W:3 * SC_W].astype(F32)
        return t[:, 0:SC_W] * t[:, SC_W:2 * SC_W]
    out_ref[:, 0:SC_W] = sc_ref[:, 0:SC_W].astype(F32) * _conv3_tiles(gated_input, scw_ref[...], R, L)

    u3 = _conv3_tiles(lambda r0: hy_ref[r0:r0 + 8, :].astype(F32), hyw_ref[...], R, L)
    z = u3[:, 0:HY_W]
    gates = (u3[:, HY_W:2 * HY_W], u3[:, 2 * HY_W:3 * HY_W])
    fwd, inv = fwd_ref[...], inv_ref[...]
    lanes = lambda j: slice(j * HY_W, (j + 1) * HY_W)
    for o in range(HY_ORDER):
        zcat = jnp.concatenate([z[j * T:(j + 1) * T] for j in range(NB)], axis=1)
        spec = _dot(fwd, zcat.astype(BF16))
        yre, yim = [], []
        for bi in range(NB):
            s0, i = bi - bi % P, bi % P
            re = im = None
            for j in range(P):
                d = i - j + P - 1
                hre = hre_ref[d, :, lanes(o)]
                him = him_ref[d, :, lanes(o)]
                zc = spec[0:T, lanes(s0 + j)]
                zs = spec[T:2 * T, lanes(s0 + j)]
                pre = zc * hre + zs * him
                pim = zc * him - zs * hre
                re = pre if re is None else re + pre
                im = pim if im is None else im + pim
            yre.append(re)
            yim.append(im)
        prod = jnp.concatenate([jnp.concatenate(yre, axis=1), jnp.concatenate(yim, axis=1)], axis=0)
        ycat = _dot(inv, prod.astype(BF16)) * (1.0 / T)
        y = jnp.concatenate([ycat[:, lanes(bi)] for bi in range(NB)], axis=0)
        z = gates[o] * (y + skip_ref[o:o + 1, :] * z)
    out_ref[:, SC_W:SC_W + HY_W] = z


def _schy2_kernel(scc_ref, hyc_ref, scs_ref, hys_ref, scw_ref, hyw_ref, skip_ref, fwd_ref, inv_ref,
                  hrec_ref, himc_ref, hres_ref, hims_ref, outc_ref, outs_ref, *, Lc, Ls, n_ctx):
    i = pl.program_id(0)
    shared = (scw_ref, hyw_ref, skip_ref, fwd_ref, inv_ref)

    @pl.when(i < n_ctx)
    def _():
        _schy_kernel(scc_ref, hyc_ref, *shared, hrec_ref, himc_ref, outc_ref, L=Lc)

    @pl.when(i >= n_ctx)
    def _():
        _schy_kernel(scs_ref, hys_ref, *shared, hres_ref, hims_ref, outs_ref, L=Ls)


def _schy(proj_c, proj_s, sc_w, hy_w, skip, spec_c, spec_s, *, l, Lc, Ls):
    T = HY_BLOCK
    _, _, c2, s2 = _dft_tables_np(T)
    fwd = jnp.asarray(np.concatenate([c2, s2], axis=0).astype(np.float32).astype(BF16))
    inv = jnp.asarray(np.concatenate([c2, -s2], axis=1).astype(np.float32).astype(BF16))
    n = HY_ORDER * HY_W
    rows = HY_STEP_ROWS
    assert rows % Lc == 0 and rows % Ls == 0
    n_ctx, n_lat = proj_c.shape[0] // rows, proj_s.shape[0] // rows
    ctx, lat = _two_pass_maps(n_ctx)
    windows = lambda idx: [pl.BlockSpec((rows, 3 * SC_W), lambda i: (idx(i), 0)),
                           pl.BlockSpec((rows, 3 * HY_W), lambda i: (idx(i), 1))]
    spectra = lambda L: [_layer_spec((2 * (L // T) - 1, T, n), l)] * 2
    return pl.pallas_call(
        functools.partial(_schy2_kernel, Lc=Lc, Ls=Ls, n_ctx=n_ctx),
        out_shape=[jax.ShapeDtypeStruct((p.shape[0], SC_W + HY_W), F32) for p in (proj_c, proj_s)],
        grid=(n_ctx + n_lat,),
        in_specs=windows(ctx) + windows(lat) + [
            _layer_spec((3, SC_W), l), _layer_spec((3, 3 * HY_W), l), _layer_spec((HY_ORDER, HY_W), l),
            _const_spec((2 * T, T)), _const_spec((T, 2 * T)),
        ] + spectra(Lc) + spectra(Ls),
        out_specs=[pl.BlockSpec((rows, SC_W + HY_W), lambda i: (ctx(i), 0)),
                   pl.BlockSpec((rows, SC_W + HY_W), lambda i: (lat(i), 0))],
        compiler_params=pltpu.CompilerParams(
            dimension_semantics=("arbitrary",), vmem_limit_bytes=VMEM_LIMIT),
        name="sconv_hyena",
    )(proj_c, proj_c, proj_s, proj_s, sc_w, hy_w, skip, fwd, inv, *spec_c, *spec_s)


def _gate_prep_kernel(*refs, B, L, with_state):
    if with_state:
        g_ref, gb_ref, m0_ref, rows_ref, cols_ref = refs
    else:
        g_ref, gb_ref, rows_ref, cols_ref = refs
    R = B * N_GATE
    gb = gb_ref[:, 0:1]
    x = jnp.concatenate([g_ref[:, b * L:(b + 1) * L] + gb for b in range(B)], axis=0)
    rr = lax.broadcasted_iota(jnp.int32, (R, L), 0) & (N_GATE - 1)
    is_f = ((rr >= ML_H) & (rr < 2 * ML_H)) | (rr >= 3 * ML_H)
    fwd_rows = rr < 2 * ML_H
    pk = jnp.where(is_f, _log_sigmoid(x), x)
    r_i = lax.broadcasted_iota(jnp.int32, (L, L), 0)
    c_i = lax.broadcasted_iota(jnp.int32, (L, L), 1)
    triu = jnp.where(r_i <= c_i, 1.0, 0.0).astype(BF16)
    cs = _dot(jnp.concatenate(_split3(pk), axis=0), triu)
    cum = cs[0:R] + cs[R:2 * R] + cs[2 * R:3 * R]
    suf = cum[:, L - 1:L] - cum + pk
    bsum = pltpu.roll(jnp.where(fwd_rows, cum, suf), R - ML_H, axis=0)
    r = pk - bsum
    cm = jnp.where(fwd_rows, _scan_max(r, reverse=False), _scan_max(r, reverse=True))
    if with_state:
        m0 = jnp.concatenate([m0_ref[b][:, 0:1] for b in range(B)], axis=0)
        m = jnp.maximum(m0, cm)
    else:
        m = jnp.maximum(cm, 0.0)
    rl = r * LOG2E
    ml = m * LOG2E
    mt = bsum + m
    pieces = [rl, ml, jnp.exp(-mt), mt]
    if not with_state:
        pieces.append(jnp.exp2(rl - jnp.where(fwd_rows, ml[:, L - 1:L], ml[:, 0:1])))
    pad = jnp.zeros((LANES - N_GATE, L), F32)
    for b in range(B):
        bs = slice(b * N_GATE, (b + 1) * N_GATE)
        for j, piece in enumerate(pieces):
            rows_ref[b, j * N_GATE:(j + 1) * N_GATE, :] = piece[bs]
        cols_ref[b * L:(b + 1) * L, :] = jnp.concatenate([rl[bs], pad], axis=0).T


def _gate_prep(g_t, gate_b, m0, *, l, B, L):
    with_state = m0 is not None
    n_rows = (4 if with_state else 5) * N_GATE
    full = lambda shape: pl.BlockSpec(shape, lambda i: tuple(0 for _ in shape))
    args = [g_t, gate_b]
    in_specs = [full((N_GATE, B * L)), pl.BlockSpec((None, N_GATE, LANES), lambda i: (l, 0, 0))]
    if with_state:
        args.append(m0)
        in_specs.append(pl.BlockSpec((B, None, N_GATE, LANES), lambda i: (0, l, 0, 0)))
    return pl.pallas_call(
        functools.partial(_gate_prep_kernel, B=B, L=L, with_state=with_state),
        out_shape=[jax.ShapeDtypeStruct((B, n_rows, L), F32), jax.ShapeDtypeStruct((B * L, LANES), F32)],
        grid=(1,),
        in_specs=in_specs,
        out_specs=[full((B, n_rows, L)), full((B * L, LANES))],
        compiler_params=pltpu.CompilerParams(
            dimension_semantics=("arbitrary",), vmem_limit_bytes=VMEM_LIMIT),
        name="gate_prep",
    )(*args)


def _mlstm_kernel(*refs, L, with_state, n_prev):
    if with_state:
        (q_ref, k_ref, v_ref, o_ref, rows_ref, cols_ref, cw_ref, ng_ref,
         c0_ref, n0_ref, m0_ref, y_ref) = refs
    elif n_prev:
        (q_ref, k_ref, v_ref, o_ref, rows_ref, cols_ref, cw_ref, ng_ref,
         cprev_ref, nprev_ref, mprev_ref, y_ref, cout_ref, nout_ref, mout_ref) = refs
        cout_ref[:, 0:n_prev] = cprev_ref[...]
        nout_ref[:, 0:n_prev] = nprev_ref[...]
        mout_ref[:, 0:n_prev] = mprev_ref[...]
        cout_ref, nout_ref, mout_ref = (r.at[:, n_prev] for r in (cout_ref, nout_ref, mout_ref))
    else:
        (q_ref, k_ref, v_ref, o_ref, rows_ref, cols_ref, cw_ref, ng_ref,
         y_ref, cout_ref, nout_ref, mout_ref) = refs

    S = rows_ref.shape[0]
    tb = min(L, ATT_BLOCK)
    cw = cw_ref[...]
    q_all = _silu(_conv3(q_ref[...].astype(F32), cw[:, 0:ML_W], L))
    k_all = _silu(_conv3(k_ref[...].astype(F32), cw[:, ML_W:2 * ML_W], L)) * (ML_DH ** -0.5)

    s_loc = lax.broadcasted_iota(jnp.int32, (tb, tb), 0)
    t_loc = lax.broadcasted_iota(jnp.int32, (tb, tb), 1)
    masks = (s_loc <= t_loc, s_loc >= t_loc)
    ones_rows = jnp.ones((2 * 8, L), BF16)
    ng = ng_ref[...]
    for h in range(ML_H):
        hs = slice(h * ML_DH, (h + 1) * ML_DH)
        seqs = []
        for s in range(S):
            sl = slice(s * L, (s + 1) * L)
            rl16, ml16, em16, mt16 = (rows_ref[s, j * N_GATE:(j + 1) * N_GATE, :] for j in range(4))
            q = q_all[sl, hs]
            k = k_all[sl, hs]
            v = v_ref[sl, hs].astype(F32)
            vext_t = jnp.concatenate([v.T.astype(BF16), ones_rows], axis=0)
            rlb = [jnp.broadcast_to(cols_ref[sl, 2 * ML_H * d + h:2 * ML_H * d + h + 1], (L, tb)) for d in range(2)]
            seqs.append(dict(q=q, k=k, qb=q.astype(BF16), kb=k.astype(BF16), vb=v.astype(BF16), vext_t=vext_t,
                             rlb=rlb, ml16=ml16, em16=em16, mt16=mt16))
            if with_state:
                seqs[s]["c0_t"] = [c0_ref[s, d, h].T.astype(BF16) for d in range(2)]
                seqs[s]["n0"] = _split2(n0_ref[s])
        for r0 in range(0, L, tb):
            rs = slice(r0, r0 + tb)
            for s, sq in enumerate(seqs):
                s_t = _dot_nt(sq["kb"], sq["qb"][rs])
                if with_state:
                    qb_rs = sq["qb"][rs]
                    qn_rows = _dot_nt(sq["n0"][0], qb_rs) + _dot_nt(sq["n0"][1], qb_rs)
                hout_t = None
                for d in range(2):
                    o8 = 2 * ML_H * d + h
                    ml_row = sq["ml16"][o8:o8 + 1, rs]
                    e_diag = jnp.exp2(jnp.where(masks[d], sq["rlb"][d][rs] - ml_row, NEG_BIG))
                    acc = _dot(sq["vext_t"][:, rs], (s_t[rs] * e_diag).astype(BF16))
                    side = slice(r0 + tb, L) if d else slice(0, r0)
                    if side.stop > side.start:
                        e_side = jnp.exp2(sq["rlb"][d][side] - ml_row)
                        acc = acc + _dot(sq["vext_t"][:, side], (s_t[side] * e_side).astype(BF16))
                    num_t = acc[0:ML_DH]
                    den = acc[ML_DH:ML_DH + 1]
                    if with_state:
                        wp = jnp.exp2(m0_ref[s, o8:o8 + 1, 0:1] * LOG2E - ml_row)
                        den = den + wp * qn_rows[d * ML_H + h:d * ML_H + h + 1]
                        num_t = num_t + wp * _dot_nt(sq["c0_t"][d], sq["qb"][rs])
                    contrib = num_t * (1.0 / jnp.maximum(jnp.abs(den), sq["em16"][o8:o8 + 1, rs]))
                    hout_t = contrib if hout_t is None else hout_t + contrib
                hn_t = hout_t * lax.rsqrt(jnp.mean(hout_t * hout_t, axis=0, keepdims=True) + EPS)
                rows_out = slice(s * L + r0, s * L + r0 + tb)
                y_ref[rows_out, hs] = _sigmoid(o_ref[rows_out, hs].astype(F32)) * (hn_t.T * ng[:, hs])

        if not with_state:
            for s, sq in enumerate(seqs):
                w16 = rows_ref[s, 4 * N_GATE:5 * N_GATE, :]
                k_t = sq["k"].T
                w_hi, w_lo = _split2(w16)
                n_rows = _dot(w_hi, sq["kb"]) + _dot(w_lo, sq["kb"])
                for d in range(2):
                    o8 = 2 * ML_H * d + h
                    end = 0 if d else L - 1
                    cout_ref[s, d, h] = _dot((k_t * w16[o8:o8 + 1, :]).astype(BF16), sq["vb"])
                    nout_ref[s, d * ML_H + h:d * ML_H + h + 1, :] = n_rows[o8:o8 + 1, :]
                    mout_ref[s, d * ML_H + h:d * ML_H + h + 1, :] = jnp.broadcast_to(
                        sq["mt16"][o8:o8 + 1, end:end + 1], (1, LANES))


def _mlstm(proj, g_t, gate_b, conv_w, norm_g, state, prev, *, l, B, L):
    with_state = state is not None
    n_prev = 0 if prev is None else l
    rows, cols = _gate_prep(g_t, gate_b, state[2] if with_state else None, l=l, B=B, L=L)
    S = max(1, ML_STEP_ROWS // L)
    SL = S * L
    col0 = (3 * SC_W + 3 * HY_W) // ML_W
    in_specs = [
        pl.BlockSpec((SL, ML_W), lambda b: (b, col0)),
        pl.BlockSpec((SL, ML_W), lambda b: (b, col0 + 1)),
        pl.BlockSpec((SL, ML_W), lambda b: (b, col0 + 2)),
        pl.BlockSpec((SL, ML_W), lambda b: (b, col0 + 3)),
        pl.BlockSpec((S, rows.shape[1], L), lambda b: (b, 0, 0)),
        pl.BlockSpec((SL, LANES), lambda b: (b, 0)),
        _layer_spec((3, 2 * ML_W), l), _layer_spec((1, ML_W), l),
    ]
    args = [proj, proj, proj, proj, rows, cols, conv_w, norm_g]
    y_shape = jax.ShapeDtypeStruct((B * L, ML_W), F32)
    y_spec = pl.BlockSpec((SL, ML_W), lambda b: (b, 0))
    c_blk, n_blk, m_blk = (2, ML_H, ML_DH, ML_DH), (2 * ML_H, ML_DH), (2 * ML_H, LANES)
    if with_state:
        c0, n0, m0 = state
        in_specs += [
            pl.BlockSpec((S, None) + c_blk, lambda b: (b, l, 0, 0, 0, 0)),
            pl.BlockSpec((S, None) + n_blk, lambda b: (b, l, 0, 0)),
            pl.BlockSpec((S, None, N_GATE, LANES), lambda b: (b, l, 0, 0)),
        ]
        args += [c0, n0, m0]
        out_shape, out_specs = y_shape, y_spec
    else:
        lead = (l + 1,) if n_prev else ()
        if n_prev:
            in_specs += [pl.BlockSpec((S, n_prev) + blk, lambda b, nz=len(blk) + 1: (b,) + (0,) * nz)
                         for blk in (c_blk, n_blk, m_blk)]
            args += list(prev)
        out_shape = [y_shape] + [jax.ShapeDtypeStruct((B,) + lead + blk, F32) for blk in (c_blk, n_blk, m_blk)]
        out_specs = [y_spec] + [pl.BlockSpec((S,) + lead + blk, lambda b, nz=len(lead) + len(blk): (b,) + (0,) * nz)
                                for blk in (c_blk, n_blk, m_blk)]
    return pl.pallas_call(
        functools.partial(_mlstm_kernel, L=L, with_state=with_state, n_prev=n_prev),
        out_shape=out_shape,
        grid=(B // S,),
        in_specs=in_specs,
        out_specs=out_specs,
        compiler_params=pltpu.CompilerParams(
            dimension_semantics=("arbitrary",), vmem_limit_bytes=VMEM_LIMIT),
        name="mlstm",
    )(*args)


def _outmlp_kernel(xc_ref, yac_ref, ybc_ref, xl_ref, yal_ref, ybl_ref, mod_ref, wo_ref, g2_ref, w1_ref, w2_ref,
                   gf_ref, oc_ref, ol_ref, *, final, n_ctx):
    i = pl.program_id(0)
    shared = (mod_ref, wo_ref, g2_ref, w1_ref, w2_ref, gf_ref)

    @pl.when(i < n_ctx)
    def _():
        _outmlp_body(xc_ref, yac_ref, ybc_ref, *shared, oc_ref, final=final)

    @pl.when(i >= n_ctx)
    def _():
        _outmlp_body(xl_ref, yal_ref, ybl_ref, *shared, ol_ref, final=final)


def _outmlp_body(x_ref, ya_ref, yb_ref, mod_ref, wo_ref, g2_ref, w1_ref, w2_ref, gf_ref, o_ref, *, final):
    mod = mod_ref[0]
    g1 = mod[:, 2 * D_MODEL:3 * D_MODEL]
    sh2 = mod[:, 3 * D_MODEL:4 * D_MODEL]
    sc2 = mod[:, 4 * D_MODEL:5 * D_MODEL]
    g2 = mod[:, 5 * D_MODEL:6 * D_MODEL]
    na = SC_W + HY_W
    attn = _dot(ya_ref[...].astype(BF16), wo_ref[0:na, :]) + _dot(yb_ref[...].astype(BF16), wo_ref[na:, :])
    x = x_ref[...] + g1 * attn
    u = ((_rms(x) * g2_ref[...]) * (1.0 + sc2) + sh2).astype(BF16)
    tf = 1024
    acc = jnp.zeros(x.shape, F32)
    for j in range(D_FF // tf):
        hcol = _dot(u, w1_ref[:, j * tf:(j + 1) * tf])
        hcol = jnp.square(jnp.maximum(hcol, 0.0)).astype(BF16)
        acc = acc + _dot(hcol, w2_ref[j * tf:(j + 1) * tf, :])
    x = x + g2 * acc
    if final:
        x = _rms(x) * gf_ref[...]
    o_ref[...] = x


def _outmlp(ctx_ops, lat_ops, mods, w_out, g2, w1, w2, gf, *, l, Ld, final):
    tm = ROW_BLOCK
    Tc, Tl = ctx_ops[0].shape[0], lat_ops[0].shape[0]
    n_ctx, n_lat = Tc // tm, Tl // tm
    ctx, lat = _two_pass_maps(n_ctx)
    token_specs = lambda idx: [pl.BlockSpec((tm, width), lambda i: (idx(i), 0))
                               for width in (D_MODEL, SC_W + HY_W, ML_W)]
    return pl.pallas_call(
        functools.partial(_outmlp_kernel, final=final, n_ctx=n_ctx),
        out_shape=[jax.ShapeDtypeStruct((Tc, D_MODEL), F32), jax.ShapeDtypeStruct((Tl, D_MODEL), F32)],
        grid=(n_ctx + n_lat,),
        in_specs=token_specs(ctx) + token_specs(lat) + [
            _mod_spec(l, Ld, tm, n_ctx),
            _const_spec((D_MODEL, D_MODEL)), _layer_spec((1, D_MODEL), l),
            _const_spec((D_MODEL, D_FF)), _const_spec((D_FF, D_MODEL)), _const_spec((1, D_MODEL)),
        ],
        out_specs=[pl.BlockSpec((tm, D_MODEL), lambda i: (ctx(i), 0)), pl.BlockSpec((tm, D_MODEL), lambda i: (lat(i), 0))],
        compiler_params=pltpu.CompilerParams(
            dimension_semantics=("arbitrary",), vmem_limit_bytes=VMEM_LIMIT),
        name="outproj_mlp",
    )(*ctx_ops, *lat_ops, mods, w_out, g2, w1, w2, gf)


def kernel(x_prompt, x_sample, state_C, state_n, state_m, c, c_ctx, norm1_g, ada_w, ada_b, w_in, sc_conv_w, hy_conv_w, hy_w1, hy_b1, hy_w2, hy_b2, hy_w3, hy_freq, hy_skip, ml_conv_w, ml_gate_b, ml_norm_g, w_out, norm2_g, mlp_w1, mlp_w2, norm_f_g):
    B, L = x_prompt.shape[0], x_prompt.shape[1]
    Bd, Ld = x_sample.shape[0], x_sample.shape[1]

    cond = jnp.concatenate([c_ctx[None, :], c, jnp.zeros((8 - 1 - Bd, D_MODEL), F32)], axis=0)
    mods = _ada(cond, ada_w, ada_b).reshape(DEPTH * 8, 1, 6 * D_MODEL)

    w_in_t = jnp.swapaxes(w_in, 1, 2).astype(BF16)
    w_gate = w_in_t[:, N_MAIN:, :]
    gate_b = jnp.broadcast_to(ml_gate_b[:, :, None], (DEPTH, N_GATE, LANES))
    pad_h = LANES - HY_HID
    w1p = jnp.pad(hy_w1, ((0, 0), (0, LANES - HY_EMB), (0, pad_h)))
    b1p = jnp.pad(hy_b1, ((0, 0), (0, pad_h)))
    w2p = jnp.pad(hy_w2, ((0, 0), (0, pad_h), (0, pad_h)))
    b2p = jnp.pad(hy_b2, ((0, 0), (0, pad_h)))
    w3p = jnp.pad(hy_w3, ((0, 0), (0, pad_h), (0, 0)))
    fp = jnp.pad(hy_freq, ((0, 0), (0, pad_h)))
    pos = jnp.asarray(_pos_embed_np(Ld))
    m0_all = jnp.pad(state_m, ((0, 0), (0, 0), (0, 0), (0, ML_H))).reshape(Bd, DEPTH, N_GATE, 1)
    m0_all = jnp.broadcast_to(m0_all, (Bd, DEPTH, N_GATE, LANES))
    n0_all = state_n.reshape(Bd, DEPTH, 2 * ML_H, ML_DH)
    gf = norm_f_g[None, :]
    g1 = norm1_g[:, None, :]
    g2 = norm2_g[:, None, :]
    ml_ng = ml_norm_g[:, None, :]
    filt_args = (w1p, b1p[:, None, :], w2p, b2p[:, None, :], w3p, fp[:, None, :])
    hre_c, him_c = _filter_spectrum(L, *filt_args)
    hre_s, him_s = _filter_spectrum(Ld, *filt_args)
    state = (state_C, n0_all, m0_all)

    xp = x_prompt.reshape(B * L, D_MODEL)
    xs = x_sample.reshape(Bd * Ld, D_MODEL)
    prev = None
    for l in range(DEPTH):
        final = l == DEPTH - 1

        res = _inproj(xp, xs, pos if l == 0 else None, mods, g1, w_in_t, w_gate, (w_out, mlp_w1, mlp_w2), l=l, Ld=Ld)
        proj_c, g_c, proj_s, g_s = res[:4]
        if l == 0:
            xs = res[4]
        w_out_b, w1_b, w2_b = res[-3:]

        ya_c, ya_s = _schy(proj_c, proj_s, sc_conv_w, hy_conv_w, hy_skip, (hre_c, him_c), (hre_s, him_s),
                           l=l, Lc=L, Ls=Ld)

        yb_c, *states = _mlstm(proj_c, g_c, gate_b, ml_conv_w, ml_ng, None, prev, l=l, B=B, L=L)
        if not final:
            prev = [s.reshape((B, 1) + s.shape[1:]) for s in states] if l == 0 else states

        yb_s = _mlstm(proj_s, g_s, gate_b, ml_conv_w, ml_ng, state, None, l=l, B=Bd, L=Ld)

        xp, xs = _outmlp((xp, ya_c, yb_c), (xs, ya_s, yb_s), mods, w_out_b, g2, w1_b, w2_b, gf,
                         l=l, Ld=Ld, final=final)

    y_prompt = xp.reshape(B, L, D_MODEL)
    y_sample = xs.reshape(Bd, Ld, D_MODEL)
    new_c, new_n, new_m = states
    return (y_prompt, y_sample, new_c, new_n.reshape(B, DEPTH, 2, ML_H, ML_DH),
            new_m[..., 0].reshape(B, DEPTH, 2, ML_H))
```

```python
import functools
import math

import numpy as np
import jax
import jax.numpy as jnp
from jax import lax
from jax.experimental import pallas as pl
from jax.experimental.pallas import tpu as pltpu

F32 = jnp.float32
BF16 = jnp.bfloat16

D_MODEL = 1024
DEPTH = 2
GRID_W = 64
SC_W = 256
HY_W = 256
ML_W = 512
ML_H = 4
ML_DH = ML_W // ML_H
D_FF = 4 * D_MODEL
HY_ORDER = 2
HY_BANDS = 16
HY_EMB = 1 + 2 * HY_BANDS
HY_HID = 64
HY_DECAY_SHORT = 0.3
HY_DECAY_LONG = 1.5
HY_DECAY_TARGET = 1e-2
EPS = 1e-6
N_MAIN = 3 * SC_W + 3 * HY_W + 4 * ML_W
N_GATE = 4 * ML_H
LANES = 128
ROW_BLOCK = 512
ATT_BLOCK = 256
HY_BLOCK = 256
HY_STEP_ROWS = 1024
ML_STEP_ROWS = 512
PROJ_DTYPE = BF16
NEG_BIG = -1e30
LOG2E = math.log2(math.e)
VMEM_LIMIT = 60 * 1024 * 1024


def _dot(a, b):
    return jnp.dot(a, b, preferred_element_type=F32)


def _dot_nt(a, b):
    return lax.dot_general(a, b, (((1,), (1,)), ((), ())), preferred_element_type=F32)


def _split2(a):
    hi = a.astype(BF16)
    lo = (a - hi.astype(F32)).astype(BF16)
    return hi, lo


def _split3(a):
    hi = a.astype(BF16)
    r = a - hi.astype(F32)
    mid = r.astype(BF16)
    lo = (r - mid.astype(F32)).astype(BF16)
    return hi, mid, lo


def _dot3(a, b):
    ah, al = _split2(a)
    bh, bl = _split2(b)
    return _dot(ah, bh) + _dot(al, bh) + _dot(ah, bl)


def _dot3_tab(th, tl, b):
    bh, bl = _split2(b)
    return _dot(th, bh) + _dot(tl, bh) + _dot(th, bl)


def _sigmoid(x):
    return 0.5 * jnp.tanh(0.5 * x) + 0.5


def _silu(x):
    return x * _sigmoid(x)


def _log_sigmoid(x):
    return jnp.minimum(x, 0.0) - jnp.log(1.0 + jnp.exp(-jnp.abs(x)))


def _conv3(x, w, seq_len=None):
    R = x.shape[0]
    seq_len = R if seq_len is None else seq_len
    pos = lax.broadcasted_iota(jnp.int32, x.shape, 0) & (seq_len - 1)
    xm = jnp.where(pos == 0, 0.0, pltpu.roll(x, 1, axis=0))
    xp = jnp.where(pos == seq_len - 1, 0.0, pltpu.roll(x, R - 1, axis=0))
    return xm * w[0:1, :] + x * w[1:2, :] + xp * w[2:3, :]


def _conv3_tiles(load, w, n_rows, seq_len):
    C = w.shape[1]
    sub = lax.broadcasted_iota(jnp.int32, (8, C), 0)
    w0, w1, w2 = w[0:1, :], w[1:2, :], w[2:3, :]
    zero = jnp.zeros((8, C), F32)
    out = []
    for r0 in range(0, n_rows, 8):
        cur = load(r0)
        prev = zero if r0 % seq_len == 0 else load(r0 - 8)
        nxt = zero if (r0 + 8) % seq_len == 0 else load(r0 + 8)
        xm = jnp.where(sub == 0, pltpu.roll(prev, 1, axis=0), pltpu.roll(cur, 1, axis=0))
        xp = jnp.where(sub == 7, pltpu.roll(nxt, 7, axis=0), pltpu.roll(cur, 7, axis=0))
        out.append(xm * w0 + cur * w1 + xp * w2)
    return jnp.concatenate(out, axis=0)


def _rms(x):
    return x * lax.rsqrt(jnp.mean(x * x, axis=-1, keepdims=True) + EPS)


def _scan_max(x, *, reverse):
    n = x.shape[1]
    lane = lax.broadcasted_iota(jnp.int32, x.shape, 1)
    d = 1
    while d < n:
        if reverse:
            shifted = jnp.where(lane < n - d, pltpu.roll(x, n - d, axis=1), NEG_BIG)
        else:
            shifted = jnp.where(lane >= d, pltpu.roll(x, d, axis=1), NEG_BIG)
        x = jnp.maximum(x, shifted)
        d *= 2
    return x


def _hi_lo(a64):
    a = a64.astype(np.float32)
    hi = a.astype(BF16)
    lo = (a - hi.astype(np.float32)).astype(BF16)
    return jnp.asarray(hi), jnp.asarray(lo)


def _const_spec(shape):
    return pl.BlockSpec(shape, lambda *_: tuple(0 for _ in shape), pipeline_mode=pl.Buffered(1))


def _layer_spec(shape, l):
    return pl.BlockSpec((None,) + tuple(shape), lambda *_: (l,) + tuple(0 for _ in shape),
                        pipeline_mode=pl.Buffered(1))


@functools.lru_cache(maxsize=None)
def _dft_tables_np(L):
    k = np.arange(L, dtype=np.int64)
    m1 = ((2 * k[:, None] + 1) * k[None, :]) % (4 * L)
    a1 = np.pi * m1.astype(np.float64) / (2 * L)
    m2 = ((2 * k[:, None] + 1) * (2 * k[None, :] + 1)) % (8 * L)
    a2 = np.pi * m2.astype(np.float64) / (4 * L)
    return np.cos(a1), np.sin(a1), np.cos(a2), np.sin(a2)


@functools.lru_cache(maxsize=None)
def _filter_consts_np(L):
    t_idx = np.arange(L, dtype=np.float64)
    t = t_idx / (L - 1)
    bands = np.arange(1, HY_BANDS + 1, dtype=np.float64)
    ang = 2.0 * math.pi * t_idx[:, None] * bands[None, :] / L
    z = np.zeros((L, LANES), np.float64)
    z[:, 0] = t
    z[:, 1:1 + HY_BANDS] = np.cos(ang)
    z[:, 1 + HY_BANDS:1 + 2 * HY_BANDS] = -np.sin(ang)
    lin = np.linspace(math.log(HY_DECAY_TARGET) / HY_DECAY_LONG,
                      math.log(HY_DECAY_TARGET) / HY_DECAY_SHORT, HY_W).astype(np.float32)
    deltas = np.abs(lin).astype(np.float64)
    decay = np.exp(-t[:, None] * deltas[None, :])
    return z.astype(np.float32), decay.astype(np.float32)


@functools.lru_cache(maxsize=None)
def _pos_embed_np(L):
    rows = L // GRID_W
    r, cidx = np.meshgrid(np.arange(rows, dtype=np.float64), np.arange(GRID_W, dtype=np.float64), indexing="ij")
    r = r.reshape(-1)
    cidx = cidx.reshape(-1)
    quarter = D_MODEL // 4
    omega = 1.0 / (10000.0 ** (np.arange(quarter, dtype=np.float64) / quarter))
    ar = r[:, None] * omega[None, :]
    ac = cidx[:, None] * omega[None, :]
    return np.concatenate([np.sin(ar), np.cos(ar), np.sin(ac), np.cos(ac)], axis=-1).astype(np.float32)


def _ada_kernel(c_ref, w_ref, b_ref, o_ref):
    sc = _silu(c_ref[...])
    o_ref[0] = _dot3(sc, w_ref[0]) + b_ref[0]


def _ada(cond, ada_w, ada_b):
    tn = 1536
    nd = 6 * D_MODEL
    return pl.pallas_call(
        _ada_kernel,
        out_shape=jax.ShapeDtypeStruct((DEPTH, 8, nd), F32),
        grid=(DEPTH, nd // tn),
        in_specs=[
            pl.BlockSpec((8, D_MODEL), lambda l, j: (0, 0)),
            pl.BlockSpec((1, D_MODEL, tn), lambda l, j: (l, 0, j)),
            pl.BlockSpec((1, 1, tn), lambda l, j: (l, 0, j)),
        ],
        out_specs=pl.BlockSpec((1, 8, tn), lambda l, j: (l, 0, j)),
        compiler_params=pltpu.CompilerParams(
            dimension_semantics=("arbitrary", "arbitrary"), vmem_limit_bytes=VMEM_LIMIT),
        name="ada_mod",
    )(cond, ada_w, ada_b.reshape(DEPTH, 1, nd))


def _inproj_body(x, mod_ref, g_ref, w_ref, wg_ref, proj_ref, gate_ref):
    mod = mod_ref[0]
    sh1 = mod[:, 0:D_MODEL]
    sc1 = mod[:, D_MODEL:2 * D_MODEL]
    u = (_rms(x) * g_ref[...]) * (1.0 + sc1) + sh1
    ub = u.astype(BF16)
    tn = 512
    for j in range(N_MAIN // tn):
        proj_ref[:, j * tn:(j + 1) * tn] = _dot_nt(ub, w_ref[j * tn:(j + 1) * tn, :]).astype(PROJ_DTYPE)
    gate_ref[...] = _dot_nt(wg_ref[...], ub)


def _inproj_kernel(*refs, with_pos, n_cast, n_ctx):
    it = iter(refs)
    xc_ref, xl_ref = next(it), next(it)
    pos_ref = next(it) if with_pos else None
    mod_ref, g_ref, w_ref, wg_ref = (next(it) for _ in range(4))
    cast_in = [next(it) for _ in range(n_cast)]
    projc_ref, gatec_ref, projl_ref, gatel_ref = (next(it) for _ in range(4))
    xo_ref = next(it) if with_pos else None
    cast_out = [next(it) for _ in range(n_cast)]
    i = pl.program_id(0)

    @pl.when(i < n_ctx)
    def _():
        for src, dst in zip(cast_in, cast_out):
            dst[...] = src[...].astype(BF16)
        _inproj_body(xc_ref[...], mod_ref, g_ref, w_ref, wg_ref, projc_ref, gatec_ref)

    @pl.when(i >= n_ctx)
    def _():
        x = xl_ref[...]
        if with_pos:
            x = x + pos_ref[...]
            xo_ref[...] = x
        _inproj_body(x, mod_ref, g_ref, w_ref, wg_ref, projl_ref, gatel_ref)


def _two_pass_maps(n_ctx):
    ctx = lambda i: jnp.minimum(i, n_ctx - 1)
    lat = lambda i: jnp.maximum(i - n_ctx, 0)
    return ctx, lat


def _mod_spec(l, Ld, tm, n_ctx):
    return pl.BlockSpec((1, 1, 6 * D_MODEL),
                        lambda i: (l * 8 + jnp.where(i < n_ctx, 0, 1 + ((i - n_ctx) * tm) // Ld), 0, 0))


def _inproj(xc, xl, pos, mods, g, w_in_t, w_gate, cast=(), *, l, Ld):
    tm = ROW_BLOCK
    Tc, Tl = xc.shape[0], xl.shape[0]
    n_ctx, n_lat = Tc // tm, Tl // tm
    per_seq = Ld // tm
    ctx, lat = _two_pass_maps(n_ctx)
    with_pos = pos is not None
    in_specs = [pl.BlockSpec((tm, D_MODEL), lambda i: (ctx(i), 0)), pl.BlockSpec((tm, D_MODEL), lambda i: (lat(i), 0))]
    args = [xc, xl]
    if with_pos:
        in_specs.append(pl.BlockSpec((tm, D_MODEL), lambda i: (lat(i) % per_seq, 0)))
        args.append(pos)
    in_specs += [
        _mod_spec(l, Ld, tm, n_ctx),
        _layer_spec((1, D_MODEL), l),
        _layer_spec((N_MAIN, D_MODEL), l),
        _layer_spec((N_GATE, D_MODEL), l),
    ]
    args += [mods, g, w_in_t, w_gate]
    out_shape, out_specs = [], []
    for T, idx in ((Tc, ctx), (Tl, lat)):
        out_shape += [jax.ShapeDtypeStruct((T, N_MAIN), PROJ_DTYPE), jax.ShapeDtypeStruct((N_GATE, T), F32)]
        out_specs += [pl.BlockSpec((tm, N_MAIN), lambda i, idx=idx: (idx(i), 0)),
                      pl.BlockSpec((N_GATE, tm), lambda i, idx=idx: (0, idx(i)))]
    if with_pos:
        out_shape.append(jax.ShapeDtypeStruct((Tl, D_MODEL), F32))
        out_specs.append(pl.BlockSpec((tm, D_MODEL), lambda i: (lat(i), 0)))
    for w in cast:
        _, rows, cols = w.shape
        in_specs.append(pl.BlockSpec((None, rows // n_ctx, cols), lambda i: (l, ctx(i), 0)))
        args.append(w)
        out_shape.append(jax.ShapeDtypeStruct((rows, cols), BF16))
        out_specs.append(pl.BlockSpec((rows // n_ctx, cols), lambda i: (ctx(i), 0)))
    return pl.pallas_call(
        functools.partial(_inproj_kernel, with_pos=with_pos, n_cast=len(cast), n_ctx=n_ctx),
        out_shape=out_shape,
        grid=(n_ctx + n_lat,),
        in_specs=in_specs,
        out_specs=out_specs,
        compiler_params=pltpu.CompilerParams(
            dimension_semantics=("arbitrary",), vmem_limit_bytes=VMEM_LIMIT),
        name="inproj",
    )(*args)


def _filter_kernel(z_ref, dec_ref, w1_ref, b1_ref, w2_ref, b2_ref, w3_ref, f_ref,
                   ch_ref, cl_ref, sh_ref, sl_ref, hre_ref, him_ref, *, L):
    T = HY_BLOCK
    P = L // T
    f = f_ref[...]
    hdn = jnp.sin(f * (_dot3(z_ref[...], w1_ref[...]) + b1_ref[...]))
    hdn = jnp.sin(f * (_dot3(hdn, w2_ref[...]) + b2_ref[...]))
    h = _dot3(hdn, w3_ref[...])
    dec = dec_ref[...]
    row = lax.broadcasted_iota(jnp.int32, (L, HY_W), 0)
    halves = []
    for o in range(HY_ORDER):
        hf = h[:, (2 * o) * HY_W:(2 * o + 1) * HY_W] * dec
        hb = jnp.where(row == 0, 0.0, h[:, (2 * o + 1) * HY_W:(2 * o + 2) * HY_W] * dec)
        nrm = jnp.sum(jnp.abs(hf), axis=0, keepdims=True) + jnp.sum(jnp.abs(hb), axis=0, keepdims=True)
        inv = 1.0 / nrm
        halves.append((hf * inv, hb * inv))
    blocks = [half[e * T:(e + 1) * T] for pair in halves for half in pair for e in range(P)]
    x = jnp.concatenate(blocks, axis=1)
    xre = _dot3_tab(ch_ref[...], cl_ref[...], x)
    xim = -_dot3_tab(sh_ref[...], sl_ref[...], x)
    sgn = jnp.where((lax.broadcasted_iota(jnp.int32, (T, HY_W), 0) & 1) == 0, 1.0, -1.0)
    for o in range(HY_ORDER):
        def half_spec(which, e):
            j = (o * 2 + which) * P + e
            cs = slice(j * HY_W, (j + 1) * HY_W)
            return xre[:, cs], xim[:, cs], halves[o][which][e * T:e * T + 1]
        for d in range(-(P - 1), P):
            if d == 0:
                (fre, fim, _), (gre, gim, _) = half_spec(0, 0), half_spec(1, 0)
                re, im = fre + gre, fim - gim
            elif d > 0:
                (fre, fim, _), (pre, pim, p0) = half_spec(0, d), half_spec(0, d - 1)
                re, im = fre - sgn * pim, fim + sgn * (pre - p0)
            else:
                (gre, gim, _), (pre, pim, p0) = half_spec(1, -d), half_spec(1, -d - 1)
                re, im = gre - sgn * pim, -gim - sgn * (pre - p0)
            hre_ref[d + P - 1, :, o * HY_W:(o + 1) * HY_W] = re
            him_ref[d + P - 1, :, o * HY_W:(o + 1) * HY_W] = im


def _filter_spectrum(L, w1p, b1p, w2p, b2p, w3p, fp):
    T = HY_BLOCK
    z_np, dec_np = _filter_consts_np(L)
    c1, s1, _, _ = _dft_tables_np(T)
    ch, cl = _hi_lo(c1)
    sh, sl = _hi_lo(s1)
    n = HY_ORDER * HY_W
    nd = 2 * (L // T) - 1
    const = _const_spec
    per_layer = lambda shape: pl.BlockSpec((None,) + shape, lambda l: (l,) + tuple(0 for _ in shape))
    return pl.pallas_call(
        functools.partial(_filter_kernel, L=L),
        out_shape=[jax.ShapeDtypeStruct((DEPTH, nd, T, n), F32), jax.ShapeDtypeStruct((DEPTH, nd, T, n), F32)],
        grid=(DEPTH,),
        in_specs=[const((L, LANES)), const((L, HY_W)), per_layer((LANES, LANES)), per_layer((1, LANES)),
                  per_layer((LANES, LANES)), per_layer((1, LANES)), per_layer((LANES, 2 * n)), per_layer((1, LANES)),
                  const((T, T)), const((T, T)), const((T, T)), const((T, T))],
        out_specs=[per_layer((nd, T, n)), per_layer((nd, T, n))],
        compiler_params=pltpu.CompilerParams(
            dimension_semantics=("arbitrary",), vmem_limit_bytes=VMEM_LIMIT),
        name="hyena_filter",
    )(jnp.asarray(z_np), jnp.asarray(dec_np), w1p, b1p, w2p, b2p, w3p, fp, ch, cl, sh, sl)


def _schy_kernel(sc_ref, hy_ref, scw_ref, hyw_ref, skip_ref, fwd_ref, inv_ref, hre_ref, him_ref, out_ref, *, L):
    T = HY_BLOCK
    P = L // T
    NB = sc_ref.shape[0] // T
    R = sc_ref.shape[0]

    def gated_input(r0):
        t = sc_ref[r0:r0 + 8, SC_W:3 * SC_W].astype(F32)
        return t[:, 0:SC_W] * t[:, SC_W:2 * SC_W]
    out_ref[:, 0:SC_W] = sc_ref[:, 0:SC_W].astype(F32) * _conv3_tiles(gated_input, scw_ref[...], R, L)

    u3 = _conv3_tiles(lambda r0: hy_ref[r0:r0 + 8, :].astype(F32), hyw_ref[...], R, L)
    z = u3[:, 0:HY_W]
    gates = (u3[:, HY_W:2 * HY_W], u3[:, 2 * HY_W:3 * HY_W])
    fwd, inv = fwd_ref[...], inv_ref[...]
    lanes = lambda j: slice(j * HY_W, (j + 1) * HY_W)
    for o in range(HY_ORDER):
        zcat = jnp.concatenate([z[j * T:(j + 1) * T] for j in range(NB)], axis=1)
        spec = _dot(fwd, zcat.astype(BF16))
        yre, yim = [], []
        for bi in range(NB):
            s0, i = bi - bi % P, bi % P
            re = im = None
            for j in range(P):
                d = i - j + P - 1
                hre = hre_ref[d, :, lanes(o)]
                him = him_ref[d, :, lanes(o)]
                zc = spec[0:T, lanes(s0 + j)]
                zs = spec[T:2 * T, lanes(s0 + j)]
                pre = zc * hre + zs * him
                pim = zc * him - zs * hre
                re = pre if re is None else re + pre
                im = pim if im is None else im + pim
            yre.append(re)
            yim.append(im)
        prod = jnp.concatenate([jnp.concatenate(yre, axis=1), jnp.concatenate(yim, axis=1)], axis=0)
        ycat = _dot(inv, prod.astype(BF16)) * (1.0 / T)
        y = jnp.concatenate([ycat[:, lanes(bi)] for bi in range(NB)], axis=0)
        z = gates[o] * (y + skip_ref[o:o + 1, :] * z)
    out_ref[:, SC_W:SC_W + HY_W] = z


def _schy2_kernel(scc_ref, hyc_ref, scs_ref, hys_ref, scw_ref, hyw_ref, skip_ref, fwd_ref, inv_ref,
                  hrec_ref, himc_ref, hres_ref, hims_ref, outc_ref, outs_ref, *, Lc, Ls, n_ctx):
    i = pl.program_id(0)
    shared = (scw_ref, hyw_ref, skip_ref, fwd_ref, inv_ref)

    @pl.when(i < n_ctx)
    def _():
        _schy_kernel(scc_ref, hyc_ref, *shared, hrec_ref, himc_ref, outc_ref, L=Lc)

    @pl.when(i >= n_ctx)
    def _():
        _schy_kernel(scs_ref, hys_ref, *shared, hres_ref, hims_ref, outs_ref, L=Ls)


def _schy(proj_c, proj_s, sc_w, hy_w, skip, spec_c, spec_s, *, l, Lc, Ls):
    T = HY_BLOCK
    _, _, c2, s2 = _dft_tables_np(T)
    fwd = jnp.asarray(np.concatenate([c2, s2], axis=0).astype(np.float32).astype(BF16))
    inv = jnp.asarray(np.concatenate([c2, -s2], axis=1).astype(np.float32).astype(BF16))
    n = HY_ORDER * HY_W
    rows = HY_STEP_ROWS
    assert rows % Lc == 0 and rows % Ls == 0
    n_ctx, n_lat = proj_c.shape[0] // rows, proj_s.shape[0] // rows
    ctx, lat = _two_pass_maps(n_ctx)
    windows = lambda idx: [pl.BlockSpec((rows, 3 * SC_W), lambda i: (idx(i), 0)),
                           pl.BlockSpec((rows, 3 * HY_W), lambda i: (idx(i), 1))]
    spectra = lambda L: [_layer_spec((2 * (L // T) - 1, T, n), l)] * 2
    return pl.pallas_call(
        functools.partial(_schy2_kernel, Lc=Lc, Ls=Ls, n_ctx=n_ctx),
        out_shape=[jax.ShapeDtypeStruct((p.shape[0], SC_W + HY_W), F32) for p in (proj_c, proj_s)],
        grid=(n_ctx + n_lat,),
        in_specs=windows(ctx) + windows(lat) + [
            _layer_spec((3, SC_W), l), _layer_spec((3, 3 * HY_W), l), _layer_spec((HY_ORDER, HY_W), l),
            _const_spec((2 * T, T)), _const_spec((T, 2 * T)),
        ] + spectra(Lc) + spectra(Ls),
        out_specs=[pl.BlockSpec((rows, SC_W + HY_W), lambda i: (ctx(i), 0)),
                   pl.BlockSpec((rows, SC_W + HY_W), lambda i: (lat(i), 0))],
        compiler_params=pltpu.CompilerParams(
            dimension_semantics=("arbitrary",), vmem_limit_bytes=VMEM_LIMIT),
        name="sconv_hyena",
    )(proj_c, proj_c, proj_s, proj_s, sc_w, hy_w, skip, fwd, inv, *spec_c, *spec_s)


def _gate_prep_kernel(*refs, B, L, with_state):
    if with_state:
        g_ref, gb_ref, m0_ref, rows_ref, cols_ref = refs
    else:
        g_ref, gb_ref, rows_ref, cols_ref = refs
    R = B * N_GATE
    gb = gb_ref[:, 0:1]
    x = jnp.concatenate([g_ref[:, b * L:(b + 1) * L] + gb for b in range(B)], axis=0)
    rr = lax.broadcasted_iota(jnp.int32, (R, L), 0) & (N_GATE - 1)
    is_f = ((rr >= ML_H) & (rr < 2 * ML_H)) | (rr >= 3 * ML_H)
    fwd_rows = rr < 2 * ML_H
    pk = jnp.where(is_f, _log_sigmoid(x), x)
    r_i = lax.broadcasted_iota(jnp.int32, (L, L), 0)
    c_i = lax.broadcasted_iota(jnp.int32, (L, L), 1)
    triu = jnp.where(r_i <= c_i, 1.0, 0.0).astype(BF16)
    cs = _dot(jnp.concatenate(_split3(pk), axis=0), triu)
    cum = cs[0:R] + cs[R:2 * R] + cs[2 * R:3 * R]
    suf = cum[:, L - 1:L] - cum + pk
    bsum = pltpu.roll(jnp.where(fwd_rows, cum, suf), R - ML_H, axis=0)
    r = pk - bsum
    cm = jnp.where(fwd_rows, _scan_max(r, reverse=False), _scan_max(r, reverse=True))
    if with_state:
        m0 = jnp.concatenate([m0_ref[b][:, 0:1] for b in range(B)], axis=0)
        m = jnp.maximum(m0, cm)
    else:
        m = jnp.maximum(cm, 0.0)
    rl = r * LOG2E
    ml = m * LOG2E
    mt = bsum + m
    pieces = [rl, ml, jnp.exp(-mt), mt]
    if not with_state:
        pieces.append(jnp.exp2(rl - jnp.where(fwd_rows, ml[:, L - 1:L], ml[:, 0:1])))
    pad = jnp.zeros((LANES - N_GATE, L), F32)
    for b in range(B):
        bs = slice(b * N_GATE, (b + 1) * N_GATE)
        for j, piece in enumerate(pieces):
            rows_ref[b, j * N_GATE:(j + 1) * N_GATE, :] = piece[bs]
        cols_ref[b * L:(b + 1) * L, :] = jnp.concatenate([rl[bs], pad], axis=0).T


def _gate_prep(g_t, gate_b, m0, *, l, B, L):
    with_state = m0 is not None
    n_rows = (4 if with_state else 5) * N_GATE
    full = lambda shape: pl.BlockSpec(shape, lambda i: tuple(0 for _ in shape))
    args = [g_t, gate_b]
    in_specs = [full((N_GATE, B * L)), pl.BlockSpec((None, N_GATE, LANES), lambda i: (l, 0, 0))]
    if with_state:
        args.append(m0)
        in_specs.append(pl.BlockSpec((B, None, N_GATE, LANES), lambda i: (0, l, 0, 0)))
    return pl.pallas_call(
        functools.partial(_gate_prep_kernel, B=B, L=L, with_state=with_state),
        out_shape=[jax.ShapeDtypeStruct((B, n_rows, L), F32), jax.ShapeDtypeStruct((B * L, LANES), F32)],
        grid=(1,),
        in_specs=in_specs,
        out_specs=[full((B, n_rows, L)), full((B * L, LANES))],
        compiler_params=pltpu.CompilerParams(
            dimension_semantics=("arbitrary",), vmem_limit_bytes=VMEM_LIMIT),
        name="gate_prep",
    )(*args)


def _mlstm_kernel(*refs, L, with_state, n_prev):
    if with_state:
        (q_ref, k_ref, v_ref, o_ref, rows_ref, cols_ref, cw_ref, ng_ref,
         c0_ref, n0_ref, m0_ref, y_ref) = refs
    elif n_prev:
        (q_ref, k_ref, v_ref, o_ref, rows_ref, cols_ref, cw_ref, ng_ref,
         cprev_ref, nprev_ref, mprev_ref, y_ref, cout_ref, nout_ref, mout_ref) = refs
        cout_ref[:, 0:n_prev] = cprev_ref[...]
        nout_ref[:, 0:n_prev] = nprev_ref[...]
        mout_ref[:, 0:n_prev] = mprev_ref[...]
        cout_ref, nout_ref, mout_ref = (r.at[:, n_prev] for r in (cout_ref, nout_ref, mout_ref))
    else:
        (q_ref, k_ref, v_ref, o_ref, rows_ref, cols_ref, cw_ref, ng_ref,
         y_ref, cout_ref, nout_ref, mout_ref) = refs

    S = rows_ref.shape[0]
    tb = min(L, ATT_BLOCK)
    cw = cw_ref[...]
    q_all = _silu(_conv3(q_ref[...].astype(F32), cw[:, 0:ML_W], L))
    k_all = _silu(_conv3(k_ref[...].astype(F32), cw[:, ML_W:2 * ML_W], L)) * (ML_DH ** -0.5)

    s_loc = lax.broadcasted_iota(jnp.int32, (tb, tb), 0)
    t_loc = lax.broadcasted_iota(jnp.int32, (tb, tb), 1)
    masks = (s_loc <= t_loc, s_loc >= t_loc)
    ones_rows = jnp.ones((2 * 8, L), BF16)
    ng = ng_ref[...]
    for h in range(ML_H):
        hs = slice(h * ML_DH, (h + 1) * ML_DH)
        seqs = []
        for s in range(S):
            sl = slice(s * L, (s + 1) * L)
            rl16, ml16, em16, mt16 = (rows_ref[s, j * N_GATE:(j + 1) * N_GATE, :] for j in range(4))
            q = q_all[sl, hs]
            k = k_all[sl, hs]
            v = v_ref[sl, hs].astype(F32)
            vext_t = jnp.concatenate([v.T.astype(BF16), ones_rows], axis=0)
            rlb = [jnp.broadcast_to(cols_ref[sl, 2 * ML_H * d + h:2 * ML_H * d + h + 1], (L, tb)) for d in range(2)]
            seqs.append(dict(q=q, k=k, qb=q.astype(BF16), kb=k.astype(BF16), vb=v.astype(BF16), vext_t=vext_t,
                             rlb=rlb, ml16=ml16, em16=em16, mt16=mt16))
            if with_state:
                seqs[s]["c0_t"] = [c0_ref[s, d, h].T.astype(BF16) for d in range(2)]
                seqs[s]["n0"] = _split2(n0_ref[s])
        for r0 in range(0, L, tb):
            rs = slice(r0, r0 + tb)
            for s, sq in enumerate(seqs):
                s_t = _dot_nt(sq["kb"], sq["qb"][rs])
                if with_state:
                    qb_rs = sq["qb"][rs]
                    qn_rows = _dot_nt(sq["n0"][0], qb_rs) + _dot_nt(sq["n0"][1], qb_rs)
                hout_t = None
                for d in range(2):
                    o8 = 2 * ML_H * d + h
                    ml_row = sq["ml16"][o8:o8 + 1, rs]
                    e_diag = jnp.exp2(jnp.where(masks[d], sq["rlb"][d][rs] - ml_row, NEG_BIG))
                    acc = _dot(sq["vext_t"][:, rs], (s_t[rs] * e_diag).astype(BF16))
                    side = slice(r0 + tb, L) if d else slice(0, r0)
                    if side.stop > side.start:
                        e_side = jnp.exp2(sq["rlb"][d][side] - ml_row)
                        acc = acc + _dot(sq["vext_t"][:, side], (s_t[side] * e_side).astype(BF16))
                    num_t = acc[0:ML_DH]
                    den = acc[ML_DH:ML_DH + 1]
                    if with_state:
                        wp = jnp.exp2(m0_ref[s, o8:o8 + 1, 0:1] * LOG2E - ml_row)
                        den = den + wp * qn_rows[d * ML_H + h:d * ML_H + h + 1]
                        num_t = num_t + wp * _dot_nt(sq["c0_t"][d], sq["qb"][rs])
                    contrib = num_t * (1.0 / jnp.maximum(jnp.abs(den), sq["em16"][o8:o8 + 1, rs]))
                    hout_t = contrib if hout_t is None else hout_t + contrib
                hn_t = hout_t * lax.rsqrt(jnp.mean(hout_t * hout_t, axis=0, keepdims=True) + EPS)
                rows_out = slice(s * L + r0, s * L + r0 + tb)
                y_ref[rows_out, hs] = _sigmoid(o_ref[rows_out, hs].astype(F32)) * (hn_t.T * ng[:, hs])

        if not with_state:
            for s, sq in enumerate(seqs):
                w16 = rows_ref[s, 4 * N_GATE:5 * N_GATE, :]
                k_t = sq["k"].T
                w_hi, w_lo = _split2(w16)
                n_rows = _dot(w_hi, sq["kb"]) + _dot(w_lo, sq["kb"])
                for d in range(2):
                    o8 = 2 * ML_H * d + h
                    end = 0 if d else L - 1
                    cout_ref[s, d, h] = _dot((k_t * w16[o8:o8 + 1, :]).astype(BF16), sq["vb"])
                    nout_ref[s, d * ML_H + h:d * ML_H + h + 1, :] = n_rows[o8:o8 + 1, :]
                    mout_ref[s, d * ML_H + h:d * ML_H + h + 1, :] = jnp.broadcast_to(
                        sq["mt16"][o8:o8 + 1, end:end + 1], (1, LANES))


def _mlstm(proj, g_t, gate_b, conv_w, norm_g, state, prev, *, l, B, L):
    with_state = state is not None
    n_prev = 0 if prev is None else l
    rows, cols = _gate_prep(g_t, gate_b, state[2] if with_state else None, l=l, B=B, L=L)
    S = max(1, ML_STEP_ROWS // L)
    SL = S * L
    col0 = (3 * SC_W + 3 * HY_W) // ML_W
    in_specs = [
        pl.BlockSpec((SL, ML_W), lambda b: (b, col0)),
        pl.BlockSpec((SL, ML_W), lambda b: (b, col0 + 1)),
        pl.BlockSpec((SL, ML_W), lambda b: (b, col0 + 2)),
        pl.BlockSpec((SL, ML_W), lambda b: (b, col0 + 3)),
        pl.BlockSpec((S, rows.shape[1], L), lambda b: (b, 0, 0)),
        pl.BlockSpec((SL, LANES), lambda b: (b, 0)),
        _layer_spec((3, 2 * ML_W), l), _layer_spec((1, ML_W), l),
    ]
    args = [proj, proj, proj, proj, rows, cols, conv_w, norm_g]
    y_shape = jax.ShapeDtypeStruct((B * L, ML_W), F32)
    y_spec = pl.BlockSpec((SL, ML_W), lambda b: (b, 0))
    c_blk, n_blk, m_blk = (2, ML_H, ML_DH, ML_DH), (2 * ML_H, ML_DH), (2 * ML_H, LANES)
    if with_state:
        c0, n0, m0 = state
        in_specs += [
            pl.BlockSpec((S, None) + c_blk, lambda b: (b, l, 0, 0, 0, 0)),
            pl.BlockSpec((S, None) + n_blk, lambda b: (b, l, 0, 0)),
            pl.BlockSpec((S, None, N_GATE, LANES), lambda b: (b, l, 0, 0)),
        ]
        args += [c0, n0, m0]
        out_shape, out_specs = y_shape, y_spec
    else:
        lead = (l + 1,) if n_prev else ()
        if n_prev:
            in_specs += [pl.BlockSpec((S, n_prev) + blk, lambda b, nz=len(blk) + 1: (b,) + (0,) * nz)
                         for blk in (c_blk, n_blk, m_blk)]
            args += list(prev)
        out_shape = [y_shape] + [jax.ShapeDtypeStruct((B,) + lead + blk, F32) for blk in (c_blk, n_blk, m_blk)]
        out_specs = [y_spec] + [pl.BlockSpec((S,) + lead + blk, lambda b, nz=len(lead) + len(blk): (b,) + (0,) * nz)
                                for blk in (c_blk, n_blk, m_blk)]
    return pl.pallas_call(
        functools.partial(_mlstm_kernel, L=L, with_state=with_state, n_prev=n_prev),
        out_shape=out_shape,
        grid=(B // S,),
        in_specs=in_specs,
        out_specs=out_specs,
        compiler_params=pltpu.CompilerParams(
            dimension_semantics=("arbitrary",), vmem_limit_bytes=VMEM_LIMIT),
        name="mlstm",
    )(*args)


def _outmlp_kernel(xc_ref, yac_ref, ybc_ref, xl_ref, yal_ref, ybl_ref, mod_ref, wo_ref, g2_ref, w1_ref, w2_ref,
                   gf_ref, oc_ref, ol_ref, *, final, n_ctx):
    i = pl.program_id(0)
    shared = (mod_ref, wo_ref, g2_ref, w1_ref, w2_ref, gf_ref)

    @pl.when(i < n_ctx)
    def _():
        _outmlp_body(xc_ref, yac_ref, ybc_ref, *shared, oc_ref, final=final)

    @pl.when(i >= n_ctx)
    def _():
        _outmlp_body(xl_ref, yal_ref, ybl_ref, *shared, ol_ref, final=final)


def _outmlp_body(x_ref, ya_ref, yb_ref, mod_ref, wo_ref, g2_ref, w1_ref, w2_ref, gf_ref, o_ref, *, final):
    mod = mod_ref[0]
    g1 = mod[:, 2 * D_MODEL:3 * D_MODEL]
    sh2 = mod[:, 3 * D_MODEL:4 * D_MODEL]
    sc2 = mod[:, 4 * D_MODEL:5 * D_MODEL]
    g2 = mod[:, 5 * D_MODEL:6 * D_MODEL]
    na = SC_W + HY_W
    attn = _dot(ya_ref[...].astype(BF16), wo_ref[0:na, :]) + _dot(yb_ref[...].astype(BF16), wo_ref[na:, :])
    x = x_ref[...] + g1 * attn
    u = ((_rms(x) * g2_ref[...]) * (1.0 + sc2) + sh2).astype(BF16)
    tf = 1024
    acc = jnp.zeros(x.shape, F32)
    for j in range(D_FF // tf):
        hcol = _dot(u, w1_ref[:, j * tf:(j + 1) * tf])
        hcol = jnp.square(jnp.maximum(hcol, 0.0)).astype(BF16)
        acc = acc + _dot(hcol, w2_ref[j * tf:(j + 1) * tf, :])
    x = x + g2 * acc
    if final:
        x = _rms(x) * gf_ref[...]
    o_ref[...] = x


def _outmlp(ctx_ops, lat_ops, mods, w_out, g2, w1, w2, gf, *, l, Ld, final):
    tm = ROW_BLOCK
    Tc, Tl = ctx_ops[0].shape[0], lat_ops[0].shape[0]
    n_ctx, n_lat = Tc // tm, Tl // tm
    ctx, lat = _two_pass_maps(n_ctx)
    token_specs = lambda idx: [pl.BlockSpec((tm, width), lambda i: (idx(i), 0))
                               for width in (D_MODEL, SC_W + HY_W, ML_W)]
    return pl.pallas_call(
        functools.partial(_outmlp_kernel, final=final, n_ctx=n_ctx),
        out_shape=[jax.ShapeDtypeStruct((Tc, D_MODEL), F32), jax.ShapeDtypeStruct((Tl, D_MODEL), F32)],
        grid=(n_ctx + n_lat,),
        in_specs=token_specs(ctx) + token_specs(lat) + [
            _mod_spec(l, Ld, tm, n_ctx),
            _const_spec((D_MODEL, D_MODEL)), _layer_spec((1, D_MODEL), l),
            _const_spec((D_MODEL, D_FF)), _const_spec((D_FF, D_MODEL)), _const_spec((1, D_MODEL)),
        ],
        out_specs=[pl.BlockSpec((tm, D_MODEL), lambda i: (ctx(i), 0)), pl.BlockSpec((tm, D_MODEL), lambda i: (lat(i), 0))],
        compiler_params=pltpu.CompilerParams(
            dimension_semantics=("arbitrary",), vmem_limit_bytes=VMEM_LIMIT),
        name="outproj_mlp",
    )(*ctx_ops, *lat_ops, mods, w_out, g2, w1, w2, gf)


def kernel(x_prompt, x_sample, state_C, state_n, state_m, c, c_ctx, norm1_g, ada_w, ada_b, w_in, sc_conv_w, hy_conv_w, hy_w1, hy_b1, hy_w2, hy_b2, hy_w3, hy_freq, hy_skip, ml_conv_w, ml_gate_b, ml_norm_g, w_out, norm2_g, mlp_w1, mlp_w2, norm_f_g):
    B, L = x_prompt.shape[0], x_prompt.shape[1]
    Bd, Ld = x_sample.shape[0], x_sample.shape[1]

    cond = jnp.concatenate([c_ctx[None, :], c, jnp.zeros((8 - 1 - Bd, D_MODEL), F32)], axis=0)
    mods = _ada(cond, ada_w, ada_b).reshape(DEPTH * 8, 1, 6 * D_MODEL)

    w_in_t = jnp.swapaxes(w_in, 1, 2).astype(BF16)
    w_gate = w_in_t[:, N_MAIN:, :]
    gate_b = jnp.broadcast_to(ml_gate_b[:, :, None], (DEPTH, N_GATE, LANES))
    pad_h = LANES - HY_HID
    w1p = jnp.pad(hy_w1, ((0, 0), (0, LANES - HY_EMB), (0, pad_h)))
    b1p = jnp.pad(hy_b1, ((0, 0), (0, pad_h)))
    w2p = jnp.pad(hy_w2, ((0, 0), (0, pad_h), (0, pad_h)))
    b2p = jnp.pad(hy_b2, ((0, 0), (0, pad_h)))
    w3p = jnp.pad(hy_w3, ((0, 0), (0, pad_h), (0, 0)))
    fp = jnp.pad(hy_freq, ((0, 0), (0, pad_h)))
    pos = jnp.asarray(_pos_embed_np(Ld))
    m0_all = jnp.pad(state_m, ((0, 0), (0, 0), (0, 0), (0, ML_H))).reshape(Bd, DEPTH, N_GATE, 1)
    m0_all = jnp.broadcast_to(m0_all, (Bd, DEPTH, N_GATE, LANES))
    n0_all = state_n.reshape(Bd, DEPTH, 2 * ML_H, ML_DH)
    gf = norm_f_g[None, :]
    g1 = norm1_g[:, None, :]
    g2 = norm2_g[:, None, :]
    ml_ng = ml_norm_g[:, None, :]
    filt_args = (w1p, b1p[:, None, :], w2p, b2p[:, None, :], w3p, fp[:, None, :])
    hre_c, him_c = _filter_spectrum(L, *filt_args)
    hre_s, him_s = _filter_spectrum(Ld, *filt_args)
    state = (state_C, n0_all, m0_all)

    xp = x_prompt.reshape(B * L, D_MODEL)
    xs = x_sample.reshape(Bd * Ld, D_MODEL)
    prev = None
    for l in range(DEPTH):
        final = l == DEPTH - 1

        res = _inproj(xp, xs, pos if l == 0 else None, mods, g1, w_in_t, w_gate, (w_out, mlp_w1, mlp_w2), l=l, Ld=Ld)
        proj_c, g_c, proj_s, g_s = res[:4]
        if l == 0:
            xs = res[4]
        w_out_b, w1_b, w2_b = res[-3:]

        ya_c, ya_s = _schy(proj_c, proj_s, sc_conv_w, hy_conv_w, hy_skip, (hre_c, him_c), (hre_s, him_s),
                           l=l, Lc=L, Ls=Ld)

        yb_c, *states = _mlstm(proj_c, g_c, gate_b, ml_conv_w, ml_ng, None, prev, l=l, B=B, L=L)
        if not final:
            prev = [s.reshape((B, 1) + s.shape[1:]) for s in states] if l == 0 else states

        yb_s = _mlstm(proj_s, g_s, gate_b, ml_conv_w, ml_ng, state, None, l=l, B=Bd, L=Ld)

        xp, xs = _outmlp((xp, ya_c, yb_c), (xs, ya_s, yb_s), mods, w_out_b, g2, w1_b, w2_b, gf,
                         l=l, Ld=Ld, final=final)

    y_prompt = xp.reshape(B, L, D_MODEL)
    y_sample = xs.reshape(Bd, Ld, D_MODEL)
    new_c, new_n, new_m = states
    return (y_prompt, y_sample, new_c, new_n.reshape(B, DEPTH, 2, ML_H, ML_DH),
            new_m[..., 0].reshape(B, DEPTH, 2, ML_H))
```

```python
import functools
import math

import numpy as np
import jax
import jax.numpy as jnp
from jax import lax
from jax.experimental import pallas as pl
from jax.experimental.pallas import tpu as pltpu

F32 = jnp.float32
BF16 = jnp.bfloat16

D_MODEL = 1024
DEPTH = 2
GRID_W = 64
SC_W = 256
HY_W = 256
ML_W = 512
ML_H = 4
ML_DH = ML_W // ML_H
D_FF = 4 * D_MODEL
HY_ORDER = 2
HY_BANDS = 16
HY_EMB = 1 + 2 * HY_BANDS
HY_HID = 64
HY_DECAY_SHORT = 0.3
HY_DECAY_LONG = 1.5
HY_DECAY_TARGET = 1e-2
EPS = 1e-6
N_MAIN = 3 * SC_W + 3 * HY_W + 4 * ML_W
N_GATE = 4 * ML_H
LANES = 128
ROW_BLOCK = 512
ATT_BLOCK = 256
HY_BLOCK = 256
HY_STEP_ROWS = 1024
ML_STEP_ROWS = 512
PROJ_DTYPE = BF16
NEG_BIG = -1e30
LOG2E = math.log2(math.e)
VMEM_LIMIT = 60 * 1024 * 1024


def _dot(a, b):
    return jnp.dot(a, b, preferred_element_type=F32)


def _dot_nt(a, b):
    return lax.dot_general(a, b, (((1,), (1,)), ((), ())), preferred_element_type=F32)


def _split2(a):
    hi = a.astype(BF16)
    lo = (a - hi.astype(F32)).astype(BF16)
    return hi, lo


def _split3(a):
    hi = a.astype(BF16)
    r = a - hi.astype(F32)
    mid = r.astype(BF16)
    lo = (r - mid.astype(F32)).astype(BF16)
    return hi, mid, lo


def _dot3(a, b):
    ah, al = _split2(a)
    bh, bl = _split2(b)
    return _dot(ah, bh) + _dot(al, bh) + _dot(ah, bl)


def _dot3_tab(th, tl, b):
    bh, bl = _split2(b)
    return _dot(th, bh) + _dot(tl, bh) + _dot(th, bl)


def _sigmoid(x):
    return 0.5 * jnp.tanh(0.5 * x) + 0.5


def _silu(x):
    return x * _sigmoid(x)


def _log_sigmoid(x):
    return jnp.minimum(x, 0.0) - jnp.log(1.0 + jnp.exp(-jnp.abs(x)))


def _conv3(x, w, seq_len=None):
    R = x.shape[0]
    seq_len = R if seq_len is None else seq_len
    pos = lax.broadcasted_iota(jnp.int32, x.shape, 0) & (seq_len - 1)
    xm = jnp.where(pos == 0, 0.0, pltpu.roll(x, 1, axis=0))
    xp = jnp.where(pos == seq_len - 1, 0.0, pltpu.roll(x, R - 1, axis=0))
    return xm * w[0:1, :] + x * w[1:2, :] + xp * w[2:3, :]


def _conv3_tiles(load, w, n_rows, seq_len):
    C = w.shape[1]
    sub = lax.broadcasted_iota(jnp.int32, (8, C), 0)
    w0, w1, w2 = w[0:1, :], w[1:2, :], w[2:3, :]
    zero = jnp.zeros((8, C), F32)
    out = []
    for r0 in range(0, n_rows, 8):
        cur = load(r0)
        prev = zero if r0 % seq_len == 0 else load(r0 - 8)
        nxt = zero if (r0 + 8) % seq_len == 0 else load(r0 + 8)
        xm = jnp.where(sub == 0, pltpu.roll(prev, 1, axis=0), pltpu.roll(cur, 1, axis=0))
        xp = jnp.where(sub == 7, pltpu.roll(nxt, 7, axis=0), pltpu.roll(cur, 7, axis=0))
        out.append(xm * w0 + cur * w1 + xp * w2)
    return jnp.concatenate(out, axis=0)


def _rms(x):
    return x * lax.rsqrt(jnp.mean(x * x, axis=-1, keepdims=True) + EPS)


def _scan_max(x, *, reverse):
    n = x.shape[1]
    lane = lax.broadcasted_iota(jnp.int32, x.shape, 1)
    d = 1
    while d < n:
        if reverse:
            shifted = jnp.where(lane < n - d, pltpu.roll(x, n - d, axis=1), NEG_BIG)
        else:
            shifted = jnp.where(lane >= d, pltpu.roll(x, d, axis=1), NEG_BIG)
        x = jnp.maximum(x, shifted)
        d *= 2
    return x


def _hi_lo(a64):
    a = a64.astype(np.float32)
    hi = a.astype(BF16)
    lo = (a - hi.astype(np.float32)).astype(BF16)
    return jnp.asarray(hi), jnp.asarray(lo)


def _const_spec(shape):
    return pl.BlockSpec(shape, lambda *_: tuple(0 for _ in shape), pipeline_mode=pl.Buffered(1))


def _layer_spec(shape, l):
    return pl.BlockSpec((None,) + tuple(shape), lambda *_: (l,) + tuple(0 for _ in shape),
                        pipeline_mode=pl.Buffered(1))


@functools.lru_cache(maxsize=None)
def _dft_tables_np(L):
    k = np.arange(L, dtype=np.int64)
    m1 = ((2 * k[:, None] + 1) * k[None, :]) % (4 * L)
    a1 = np.pi * m1.astype(np.float64) / (2 * L)
    m2 = ((2 * k[:, None] + 1) * (2 * k[None, :] + 1)) % (8 * L)
    a2 = np.pi * m2.astype(np.float64) / (4 * L)
    return np.cos(a1), np.sin(a1), np.cos(a2), np.sin(a2)


@functools.lru_cache(maxsize=None)
def _filter_consts_np(L):
    t_idx = np.arange(L, dtype=np.float64)
    t = t_idx / (L - 1)
    bands = np.arange(1, HY_BANDS + 1, dtype=np.float64)
    ang = 2.0 * math.pi * t_idx[:, None] * bands[None, :] / L
    z = np.zeros((L, LANES), np.float64)
    z[:, 0] = t
    z[:, 1:1 + HY_BANDS] = np.cos(ang)
    z[:, 1 + HY_BANDS:1 + 2 * HY_BANDS] = -np.sin(ang)
    lin = np.linspace(math.log(HY_DECAY_TARGET) / HY_DECAY_LONG,
                      math.log(HY_DECAY_TARGET) / HY_DECAY_SHORT, HY_W).astype(np.float32)
    deltas = np.abs(lin).astype(np.float64)
    decay = np.exp(-t[:, None] * deltas[None, :])
    return z.astype(np.float32), decay.astype(np.float32)


@functools.lru_cache(maxsize=None)
def _pos_embed_np(L):
    rows = L // GRID_W
    r, cidx = np.meshgrid(np.arange(rows, dtype=np.float64), np.arange(GRID_W, dtype=np.float64), indexing="ij")
    r = r.reshape(-1)
    cidx = cidx.reshape(-1)
    quarter = D_MODEL // 4
    omega = 1.0 / (10000.0 ** (np.arange(quarter, dtype=np.float64) / quarter))
    ar = r[:, None] * omega[None, :]
    ac = cidx[:, None] * omega[None, :]
    return np.concatenate([np.sin(ar), np.cos(ar), np.sin(ac), np.cos(ac)], axis=-1).astype(np.float32)


def _ada_kernel(c_ref, w_ref, b_ref, o_ref):
    sc = _silu(c_ref[...])
    res = _dot3(sc, w_ref[0]) + b_ref[0]
    for r in range(8):
        o_ref[r] = res[r:r + 1, :]


def _ada(cond, ada_w, ada_b):
    tn = 1536
    nd = 6 * D_MODEL
    return pl.pallas_call(
        _ada_kernel,
        out_shape=jax.ShapeDtypeStruct((DEPTH * 8, 1, nd), F32),
        grid=(DEPTH, nd // tn),
        in_specs=[
            pl.BlockSpec((8, D_MODEL), lambda l, j: (0, 0)),
            pl.BlockSpec((1, D_MODEL, tn), lambda l, j: (l, 0, j)),
            pl.BlockSpec((1, 1, tn), lambda l, j: (l, 0, j)),
        ],
        out_specs=pl.BlockSpec((8, 1, tn), lambda l, j: (l, 0, j)),
        compiler_params=pltpu.CompilerParams(
            dimension_semantics=("arbitrary", "arbitrary"), vmem_limit_bytes=VMEM_LIMIT),
        name="ada_mod",
    )(cond, ada_w, ada_b.reshape(DEPTH, 1, nd))


def _inproj_body(x, mod_ref, g_ref, w_ref, wg_ref, proj_ref, gate_ref):
    mod = mod_ref[0]
    sh1 = mod[:, 0:D_MODEL]
    sc1 = mod[:, D_MODEL:2 * D_MODEL]
    u = (_rms(x) * g_ref[...]) * (1.0 + sc1) + sh1
    ub = u.astype(BF16)
    tn = 512
    for j in range(N_MAIN // tn):
        proj_ref[:, j * tn:(j + 1) * tn] = _dot_nt(ub, w_ref[j * tn:(j + 1) * tn, :]).astype(PROJ_DTYPE)
    gate_ref[...] = _dot_nt(wg_ref[...], ub)


def _inproj_kernel(*refs, with_pos, n_cast, n_ctx):
    it = iter(refs)
    xc_ref, xl_ref = next(it), next(it)
    pos_ref = next(it) if with_pos else None
    mod_ref, g_ref, w_ref, wg_ref = (next(it) for _ in range(4))
    cast_in = [next(it) for _ in range(n_cast)]
    projc_ref, gatec_ref, projl_ref, gatel_ref = (next(it) for _ in range(4))
    xo_ref = next(it) if with_pos else None
    cast_out = [next(it) for _ in range(n_cast)]
    i = pl.program_id(0)

    @pl.when(i < n_ctx)
    def _():
        for src, dst in zip(cast_in, cast_out):
            dst[...] = src[...].astype(BF16)
        _inproj_body(xc_ref[...], mod_ref, g_ref, w_ref, wg_ref, projc_ref, gatec_ref)

    @pl.when(i >= n_ctx)
    def _():
        x = xl_ref[...]
        if with_pos:
            x = x + pos_ref[...]
            xo_ref[...] = x
        _inproj_body(x, mod_ref, g_ref, w_ref, wg_ref, projl_ref, gatel_ref)


def _two_pass_maps(n_ctx):
    ctx = lambda i: jnp.minimum(i, n_ctx - 1)
    lat = lambda i: jnp.maximum(i - n_ctx, 0)
    return ctx, lat


def _mod_spec(l, Ld, tm, n_ctx):
    return pl.BlockSpec((1, 1, 6 * D_MODEL),
                        lambda i: (l * 8 + jnp.where(i < n_ctx, 0, 1 + ((i - n_ctx) * tm) // Ld), 0, 0))


def _inproj(xc, xl, pos, mods, g, w_in_t, w_gate, cast=(), *, l, Ld):
    tm = ROW_BLOCK
    Tc, Tl = xc.shape[0], xl.shape[0]
    n_ctx, n_lat = Tc // tm, Tl // tm
    per_seq = Ld // tm
    ctx, lat = _two_pass_maps(n_ctx)
    with_pos = pos is not None
    in_specs = [pl.BlockSpec((tm, D_MODEL), lambda i: (ctx(i), 0)), pl.BlockSpec((tm, D_MODEL), lambda i: (lat(i), 0))]
    args = [xc, xl]
    if with_pos:
        in_specs.append(pl.BlockSpec((tm, D_MODEL), lambda i: (lat(i) % per_seq, 0)))
        args.append(pos)
    in_specs += [
        _mod_spec(l, Ld, tm, n_ctx),
        _layer_spec((1, D_MODEL), l),
        _layer_spec((N_MAIN, D_MODEL), l),
        _layer_spec((N_GATE, D_MODEL), l),
    ]
    args += [mods, g, w_in_t, w_gate]
    out_shape, out_specs = [], []
    for T, idx in ((Tc, ctx), (Tl, lat)):
        out_shape += [jax.ShapeDtypeStruct((T, N_MAIN), PROJ_DTYPE), jax.ShapeDtypeStruct((N_GATE, T), F32)]
        out_specs += [pl.BlockSpec((tm, N_MAIN), lambda i, idx=idx: (idx(i), 0)),
                      pl.BlockSpec((N_GATE, tm), lambda i, idx=idx: (0, idx(i)))]
    if with_pos:
        out_shape.append(jax.ShapeDtypeStruct((Tl, D_MODEL), F32))
        out_specs.append(pl.BlockSpec((tm, D_MODEL), lambda i: (lat(i), 0)))
    for w in cast:
        _, rows, cols = w.shape
        in_specs.append(pl.BlockSpec((None, rows // n_ctx, cols), lambda i: (l, ctx(i), 0)))
        args.append(w)
        out_shape.append(jax.ShapeDtypeStruct((rows, cols), BF16))
        out_specs.append(pl.BlockSpec((rows // n_ctx, cols), lambda i: (ctx(i), 0)))
    return pl.pallas_call(
        functools.partial(_inproj_kernel, with_pos=with_pos, n_cast=len(cast), n_ctx=n_ctx),
        out_shape=out_shape,
        grid=(n_ctx + n_lat,),
        in_specs=in_specs,
        out_specs=out_specs,
        compiler_params=pltpu.CompilerParams(
            dimension_semantics=("arbitrary",), vmem_limit_bytes=VMEM_LIMIT),
        name="inproj",
    )(*args)


def _filter_cast_kernel(*refs, L):
    *filter_refs, src_ref, hre_ref, him_ref, dst_ref = refs
    dst_ref[...] = src_ref[...].astype(BF16)
    _filter_kernel(*filter_refs, hre_ref, him_ref, L=L)


def _filter_kernel(z_ref, dec_ref, w1_ref, b1_ref, w2_ref, b2_ref, w3_ref, f_ref,
                   ch_ref, cl_ref, sh_ref, sl_ref, hre_ref, him_ref, *, L):
    T = HY_BLOCK
    P = L // T
    f = f_ref[...]
    hdn = jnp.sin(f * (_dot3(z_ref[...], w1_ref[...]) + b1_ref[...]))
    hdn = jnp.sin(f * (_dot3(hdn, w2_ref[...]) + b2_ref[...]))
    h = _dot3(hdn, w3_ref[...])
    dec = dec_ref[...]
    row = lax.broadcasted_iota(jnp.int32, (L, HY_W), 0)
    halves = []
    for o in range(HY_ORDER):
        hf = h[:, (2 * o) * HY_W:(2 * o + 1) * HY_W] * dec
        hb = jnp.where(row == 0, 0.0, h[:, (2 * o + 1) * HY_W:(2 * o + 2) * HY_W] * dec)
        nrm = jnp.sum(jnp.abs(hf), axis=0, keepdims=True) + jnp.sum(jnp.abs(hb), axis=0, keepdims=True)
        inv = 1.0 / nrm
        halves.append((hf * inv, hb * inv))
    blocks = [half[e * T:(e + 1) * T] for pair in halves for half in pair for e in range(P)]
    x = jnp.concatenate(blocks, axis=1)
    xre = _dot3_tab(ch_ref[...], cl_ref[...], x)
    xim = -_dot3_tab(sh_ref[...], sl_ref[...], x)
    sgn = jnp.where((lax.broadcasted_iota(jnp.int32, (T, HY_W), 0) & 1) == 0, 1.0, -1.0)
    for o in range(HY_ORDER):
        def half_spec(which, e):
            j = (o * 2 + which) * P + e
            cs = slice(j * HY_W, (j + 1) * HY_W)
            return xre[:, cs], xim[:, cs], halves[o][which][e * T:e * T + 1]
        for d in range(-(P - 1), P):
            if d == 0:
                (fre, fim, _), (gre, gim, _) = half_spec(0, 0), half_spec(1, 0)
                re, im = fre + gre, fim - gim
            elif d > 0:
                (fre, fim, _), (pre, pim, p0) = half_spec(0, d), half_spec(0, d - 1)
                re, im = fre - sgn * pim, fim + sgn * (pre - p0)
            else:
                (gre, gim, _), (pre, pim, p0) = half_spec(1, -d), half_spec(1, -d - 1)
                re, im = gre - sgn * pim, -gim - sgn * (pre - p0)
            hre_ref[d + P - 1, :, o * HY_W:(o + 1) * HY_W] = re
            him_ref[d + P - 1, :, o * HY_W:(o + 1) * HY_W] = im


def _filter_spectrum(L, w1p, w2p, w3p, rows, cast=None):
    T = HY_BLOCK
    z_np, dec_np = _filter_consts_np(L)
    c1, s1, _, _ = _dft_tables_np(T)
    ch, cl = _hi_lo(c1)
    sh, sl = _hi_lo(s1)
    n = HY_ORDER * HY_W
    nd = 2 * (L // T) - 1
    const = _const_spec
    per_layer = lambda shape: pl.BlockSpec((None,) + shape, lambda l: (l,) + tuple(0 for _ in shape))
    row = lambda r: pl.BlockSpec((None, None, 1, LANES), lambda l: (l, r, 0, 0))
    in_specs = [const((L, LANES)), const((L, HY_W)), per_layer((LANES, LANES)), row(0),
                per_layer((LANES, LANES)), row(1), per_layer((LANES, 2 * n)), row(2),
                const((T, T)), const((T, T)), const((T, T)), const((T, T))]
    args = [jnp.asarray(z_np), jnp.asarray(dec_np), w1p, rows, w2p, rows, w3p, rows, ch, cl, sh, sl]
    out_shape = [jax.ShapeDtypeStruct((DEPTH, nd, T, n), F32), jax.ShapeDtypeStruct((DEPTH, nd, T, n), F32)]
    out_specs = [per_layer((nd, T, n)), per_layer((nd, T, n))]
    body = _filter_kernel
    if cast is not None:
        in_specs.append(per_layer(cast.shape[1:]))
        args.append(cast)
        out_shape.append(jax.ShapeDtypeStruct(cast.shape, BF16))
        out_specs.append(per_layer(cast.shape[1:]))
        body = _filter_cast_kernel
    return pl.pallas_call(
        functools.partial(body, L=L),
        out_shape=out_shape,
        grid=(DEPTH,),
        in_specs=in_specs,
        out_specs=out_specs,
        compiler_params=pltpu.CompilerParams(
            dimension_semantics=("arbitrary",), vmem_limit_bytes=VMEM_LIMIT),
        name="hyena_filter",
    )(*args)


def _schy_kernel(sc_ref, hy_ref, scw_ref, hyw_ref, skip_ref, fwd_ref, inv_ref, hre_ref, him_ref, out_ref, *, L):
    T = HY_BLOCK
    P = L // T
    NB = sc_ref.shape[0] // T
    R = sc_ref.shape[0]

    def gated_input(r0):
        t = sc_ref[r0:r0 + 8, SC_W:3 * SC_W].astype(F32)
        return t[:, 0:SC_W] * t[:, SC_W:2 * SC_W]
    out_ref[:, 0:SC_W] = sc_ref[:, 0:SC_W].astype(F32) * _conv3_tiles(gated_input, scw_ref[...], R, L)

    u3 = _conv3_tiles(lambda r0: hy_ref[r0:r0 + 8, :].astype(F32), hyw_ref[...], R, L)
    z = u3[:, 0:HY_W]
    gates = (u3[:, HY_W:2 * HY_W], u3[:, 2 * HY_W:3 * HY_W])
    fwd, inv = fwd_ref[...], inv_ref[...]
    lanes = lambda j: slice(j * HY_W, (j + 1) * HY_W)
    for o in range(HY_ORDER):
        zcat = jnp.concatenate([z[j * T:(j + 1) * T] for j in range(NB)], axis=1)
        spec = _dot(fwd, zcat.astype(BF16))
        yre, yim = [], []
        for bi in range(NB):
            s0, i = bi - bi % P, bi % P
            re = im = None
            for j in range(P):
                d = i - j + P - 1
                hre = hre_ref[d, :, lanes(o)]
                him = him_ref[d, :, lanes(o)]
                zc = spec[0:T, lanes(s0 + j)]
                zs = spec[T:2 * T, lanes(s0 + j)]
                pre = zc * hre + zs * him
                pim = zc * him - zs * hre
                re = pre if re is None else re + pre
                im = pim if im is None else im + pim
            yre.append(re)
            yim.append(im)
        prod = jnp.concatenate([jnp.concatenate(yre, axis=1), jnp.concatenate(yim, axis=1)], axis=0)
        ycat = _dot(inv, prod.astype(BF16)) * (1.0 / T)
        y = jnp.concatenate([ycat[:, lanes(bi)] for bi in range(NB)], axis=0)
        z = gates[o] * (y + skip_ref[o:o + 1, :] * z)
    out_ref[:, SC_W:SC_W + HY_W] = z


def _schy2_kernel(scc_ref, hyc_ref, scs_ref, hys_ref, scw_ref, hyw_ref, skip_ref, fwd_ref, inv_ref,
                  hrec_ref, himc_ref, hres_ref, hims_ref, outc_ref, outs_ref, *, Lc, Ls, n_ctx):
    i = pl.program_id(0)
    shared = (scw_ref, hyw_ref, skip_ref, fwd_ref, inv_ref)

    @pl.when(i < n_ctx)
    def _():
        _schy_kernel(scc_ref, hyc_ref, *shared, hrec_ref, himc_ref, outc_ref, L=Lc)

    @pl.when(i >= n_ctx)
    def _():
        _schy_kernel(scs_ref, hys_ref, *shared, hres_ref, hims_ref, outs_ref, L=Ls)


def _schy(proj_c, proj_s, sc_w, hy_w, skip, spec_c, spec_s, *, l, Lc, Ls):
    T = HY_BLOCK
    _, _, c2, s2 = _dft_tables_np(T)
    fwd = jnp.asarray(np.concatenate([c2, s2], axis=0).astype(np.float32).astype(BF16))
    inv = jnp.asarray(np.concatenate([c2, -s2], axis=1).astype(np.float32).astype(BF16))
    n = HY_ORDER * HY_W
    rows = HY_STEP_ROWS
    assert rows % Lc == 0 and rows % Ls == 0
    n_ctx, n_lat = proj_c.shape[0] // rows, proj_s.shape[0] // rows
    ctx, lat = _two_pass_maps(n_ctx)
    windows = lambda idx: [pl.BlockSpec((rows, 3 * SC_W), lambda i: (idx(i), 0)),
                           pl.BlockSpec((rows, 3 * HY_W), lambda i: (idx(i), 1))]
    spectra = lambda L: [_layer_spec((2 * (L // T) - 1, T, n), l)] * 2
    return pl.pallas_call(
        functools.partial(_schy2_kernel, Lc=Lc, Ls=Ls, n_ctx=n_ctx),
        out_shape=[jax.ShapeDtypeStruct((p.shape[0], SC_W + HY_W), F32) for p in (proj_c, proj_s)],
        grid=(n_ctx + n_lat,),
        in_specs=windows(ctx) + windows(lat) + [
            _layer_spec((3, SC_W), l), _layer_spec((3, 3 * HY_W), l), _layer_spec((HY_ORDER, HY_W), l),
            _const_spec((2 * T, T)), _const_spec((T, 2 * T)),
        ] + spectra(Lc) + spectra(Ls),
        out_specs=[pl.BlockSpec((rows, SC_W + HY_W), lambda i: (ctx(i), 0)),
                   pl.BlockSpec((rows, SC_W + HY_W), lambda i: (lat(i), 0))],
        compiler_params=pltpu.CompilerParams(
            dimension_semantics=("arbitrary",), vmem_limit_bytes=VMEM_LIMIT),
        name="sconv_hyena",
    )(proj_c, proj_c, proj_s, proj_s, sc_w, hy_w, skip, fwd, inv, *spec_c, *spec_s)


def _gate_prep_kernel(*refs, B, L, with_state):
    if with_state:
        g_ref, gb_ref, m0_ref, rows_ref, cols_ref = refs
    else:
        g_ref, gb_ref, rows_ref, cols_ref = refs
    R = B * N_GATE
    gb = gb_ref[:, 0:1]
    x = jnp.concatenate([g_ref[:, b * L:(b + 1) * L] + gb for b in range(B)], axis=0)
    rr = lax.broadcasted_iota(jnp.int32, (R, L), 0) & (N_GATE - 1)
    is_f = ((rr >= ML_H) & (rr < 2 * ML_H)) | (rr >= 3 * ML_H)
    fwd_rows = rr < 2 * ML_H
    pk = jnp.where(is_f, _log_sigmoid(x), x)
    r_i = lax.broadcasted_iota(jnp.int32, (L, L), 0)
    c_i = lax.broadcasted_iota(jnp.int32, (L, L), 1)
    triu = jnp.where(r_i <= c_i, 1.0, 0.0).astype(BF16)
    cs = _dot(jnp.concatenate(_split3(pk), axis=0), triu)
    cum = cs[0:R] + cs[R:2 * R] + cs[2 * R:3 * R]
    suf = cum[:, L - 1:L] - cum + pk
    bsum = pltpu.roll(jnp.where(fwd_rows, cum, suf), R - ML_H, axis=0)
    r = pk - bsum
    cm = jnp.where(fwd_rows, _scan_max(r, reverse=False), _scan_max(r, reverse=True))
    if with_state:
        m0 = jnp.concatenate([m0_ref[b][:, 0:1] for b in range(B)], axis=0)
        m = jnp.maximum(m0, cm)
    else:
        m = jnp.maximum(cm, 0.0)
    rl = r * LOG2E
    ml = m * LOG2E
    mt = bsum + m
    pieces = [rl, ml, jnp.exp(-mt), mt]
    if not with_state:
        pieces.append(jnp.exp2(rl - jnp.where(fwd_rows, ml[:, L - 1:L], ml[:, 0:1])))
    pad = jnp.zeros((LANES - N_GATE, L), F32)
    for b in range(B):
        bs = slice(b * N_GATE, (b + 1) * N_GATE)
        for j, piece in enumerate(pieces):
            rows_ref[b, j * N_GATE:(j + 1) * N_GATE, :] = piece[bs]
        cols_ref[b * L:(b + 1) * L, :] = jnp.concatenate([rl[bs], pad], axis=0).T


def _gate_prep(g_t, gate_b, m0, *, l, B, L):
    with_state = m0 is not None
    n_rows = (4 if with_state else 5) * N_GATE
    full = lambda shape: pl.BlockSpec(shape, lambda i: tuple(0 for _ in shape))
    args = [g_t, gate_b]
    in_specs = [full((N_GATE, B * L)), pl.BlockSpec((None, N_GATE, LANES), lambda i: (l, 0, 0))]
    if with_state:
        args.append(m0)
        in_specs.append(pl.BlockSpec((B, None, N_GATE, LANES), lambda i: (0, l, 0, 0)))
    return pl.pallas_call(
        functools.partial(_gate_prep_kernel, B=B, L=L, with_state=with_state),
        out_shape=[jax.ShapeDtypeStruct((B, n_rows, L), F32), jax.ShapeDtypeStruct((B * L, LANES), F32)],
        grid=(1,),
        in_specs=in_specs,
        out_specs=[full((B, n_rows, L)), full((B * L, LANES))],
        compiler_params=pltpu.CompilerParams(
            dimension_semantics=("arbitrary",), vmem_limit_bytes=VMEM_LIMIT),
        name="gate_prep",
    )(*args)


def _mlstm_kernel(*refs, L, with_state, n_prev):
    if with_state:
        (q_ref, k_ref, v_ref, o_ref, rows_ref, cols_ref, cw_ref, ng_ref,
         c0_ref, n0_ref, m0_ref, y_ref) = refs
    elif n_prev:
        (q_ref, k_ref, v_ref, o_ref, rows_ref, cols_ref, cw_ref, ng_ref,
         cprev_ref, nprev_ref, mprev_ref, y_ref, cout_ref, nout_ref, mout_ref) = refs
        cout_ref[:, 0:n_prev] = cprev_ref[...]
        nout_ref[:, 0:n_prev] = nprev_ref[...]
        mout_ref[:, 0:n_prev] = mprev_ref[...]
        cout_ref, nout_ref, mout_ref = (r.at[:, n_prev] for r in (cout_ref, nout_ref, mout_ref))
    else:
        (q_ref, k_ref, v_ref, o_ref, rows_ref, cols_ref, cw_ref, ng_ref,
         y_ref, cout_ref, nout_ref, mout_ref) = refs

    S = rows_ref.shape[0]
    tb = min(L, ATT_BLOCK)
    cw = cw_ref[...]
    q_all = _silu(_conv3(q_ref[...].astype(F32), cw[:, 0:ML_W], L))
    k_all = _silu(_conv3(k_ref[...].astype(F32), cw[:, ML_W:2 * ML_W], L)) * (ML_DH ** -0.5)

    s_loc = lax.broadcasted_iota(jnp.int32, (tb, tb), 0)
    t_loc = lax.broadcasted_iota(jnp.int32, (tb, tb), 1)
    masks = (s_loc <= t_loc, s_loc >= t_loc)
    ones_rows = jnp.ones((2 * 8, L), BF16)
    ng = ng_ref[...]
    for h in range(ML_H):
        hs = slice(h * ML_DH, (h + 1) * ML_DH)
        seqs = []
        for s in range(S):
            sl = slice(s * L, (s + 1) * L)
            rl16, ml16, em16, mt16 = (rows_ref[s, j * N_GATE:(j + 1) * N_GATE, :] for j in range(4))
            q = q_all[sl, hs]
            k = k_all[sl, hs]
            v = v_ref[sl, hs].astype(F32)
            vext_t = jnp.concatenate([v.T.astype(BF16), ones_rows], axis=0)
            rlb = [jnp.broadcast_to(cols_ref[sl, 2 * ML_H * d + h:2 * ML_H * d + h + 1], (L, tb)) for d in range(2)]
            seqs.append(dict(q=q, k=k, qb=q.astype(BF16), kb=k.astype(BF16), vb=v.astype(BF16), vext_t=vext_t,
                             rlb=rlb, ml16=ml16, em16=em16, mt16=mt16))
            if with_state:
                seqs[s]["c0_t"] = [c0_ref[s, d, h].T.astype(BF16) for d in range(2)]
                seqs[s]["n0"] = _split2(n0_ref[s])
        for r0 in range(0, L, tb):
            rs = slice(r0, r0 + tb)
            for s, sq in enumerate(seqs):
                s_t = _dot_nt(sq["kb"], sq["qb"][rs])
                if with_state:
                    qb_rs = sq["qb"][rs]
                    qn_rows = _dot_nt(sq["n0"][0], qb_rs) + _dot_nt(sq["n0"][1], qb_rs)
                hout_t = None
                for d in range(2):
                    o8 = 2 * ML_H * d + h
                    ml_row = sq["ml16"][o8:o8 + 1, rs]
                    e_diag = jnp.exp2(jnp.where(masks[d], sq["rlb"][d][rs] - ml_row, NEG_BIG))
                    acc = _dot(sq["vext_t"][:, rs], (s_t[rs] * e_diag).astype(BF16))
                    side = slice(r0 + tb, L) if d else slice(0, r0)
                    if side.stop > side.start:
                        e_side = jnp.exp2(sq["rlb"][d][side] - ml_row)
                        acc = acc + _dot(sq["vext_t"][:, side], (s_t[side] * e_side).astype(BF16))
                    num_t = acc[0:ML_DH]
                    den = acc[ML_DH:ML_DH + 1]
                    if with_state:
                        wp = jnp.exp2(m0_ref[s, o8:o8 + 1, 0:1] * LOG2E - ml_row)
                        den = den + wp * qn_rows[d * ML_H + h:d * ML_H + h + 1]
                        num_t = num_t + wp * _dot_nt(sq["c0_t"][d], sq["qb"][rs])
                    contrib = num_t * (1.0 / jnp.maximum(jnp.abs(den), sq["em16"][o8:o8 + 1, rs]))
                    hout_t = contrib if hout_t is None else hout_t + contrib
                hn_t = hout_t * lax.rsqrt(jnp.mean(hout_t * hout_t, axis=0, keepdims=True) + EPS)
                rows_out = slice(s * L + r0, s * L + r0 + tb)
                y_ref[rows_out, hs] = _sigmoid(o_ref[rows_out, hs].astype(F32)) * (hn_t.T * ng[:, hs])

        if not with_state:
            for s, sq in enumerate(seqs):
                w16 = rows_ref[s, 4 * N_GATE:5 * N_GATE, :]
                k_t = sq["k"].T
                w_hi, w_lo = _split2(w16)
                n_rows = _dot(w_hi, sq["kb"]) + _dot(w_lo, sq["kb"])
                for d in range(2):
                    o8 = 2 * ML_H * d + h
                    end = 0 if d else L - 1
                    cout_ref[s, d, h] = _dot((k_t * w16[o8:o8 + 1, :]).astype(BF16), sq["vb"])
                    nout_ref[s, d * ML_H + h:d * ML_H + h + 1, :] = n_rows[o8:o8 + 1, :]
                    mout_ref[s, d * ML_H + h:d * ML_H + h + 1, :] = jnp.broadcast_to(
                        sq["mt16"][o8:o8 + 1, end:end + 1], (1, LANES))


def _mlstm(proj, g_t, gate_b, conv_w, norm_g, state, prev, *, l, B, L):
    with_state = state is not None
    n_prev = 0 if prev is None else l
    rows, cols = _gate_prep(g_t, gate_b, state[2] if with_state else None, l=l, B=B, L=L)
    S = max(1, ML_STEP_ROWS // L)
    SL = S * L
    col0 = (3 * SC_W + 3 * HY_W) // ML_W
    in_specs = [
        pl.BlockSpec((SL, ML_W), lambda b: (b, col0)),
        pl.BlockSpec((SL, ML_W), lambda b: (b, col0 + 1)),
        pl.BlockSpec((SL, ML_W), lambda b: (b, col0 + 2)),
        pl.BlockSpec((SL, ML_W), lambda b: (b, col0 + 3)),
        pl.BlockSpec((S, rows.shape[1], L), lambda b: (b, 0, 0)),
        pl.BlockSpec((SL, LANES), lambda b: (b, 0)),
        _layer_spec((3, 2 * ML_W), l), _layer_spec((1, ML_W), l),
    ]
    args = [proj, proj, proj, proj, rows, cols, conv_w, norm_g]
    y_shape = jax.ShapeDtypeStruct((B * L, ML_W), F32)
    y_spec = pl.BlockSpec((SL, ML_W), lambda b: (b, 0))
    c_blk, n_blk, m_blk = (2, ML_H, ML_DH, ML_DH), (2 * ML_H, ML_DH), (2 * ML_H, LANES)
    if with_state:
        c0, n0, m0 = state
        in_specs += [
            pl.BlockSpec((S, None) + c_blk, lambda b: (b, l, 0, 0, 0, 0)),
            pl.BlockSpec((S, None) + n_blk, lambda b: (b, l, 0, 0)),
            pl.BlockSpec((S, None, N_GATE, LANES), lambda b: (b, l, 0, 0)),
        ]
        args += [c0, n0, m0]
        out_shape, out_specs = y_shape, y_spec
    else:
        lead = (l + 1,) if n_prev else ()
        if n_prev:
            in_specs += [pl.BlockSpec((S, n_prev) + blk, lambda b, nz=len(blk) + 1: (b,) + (0,) * nz)
                         for blk in (c_blk, n_blk, m_blk)]
            args += list(prev)
        out_shape = [y_shape] + [jax.ShapeDtypeStruct((B,) + lead + blk, F32) for blk in (c_blk, n_blk, m_blk)]
        out_specs = [y_spec] + [pl.BlockSpec((S,) + lead + blk, lambda b, nz=len(lead) + len(blk): (b,) + (0,) * nz)
                                for blk in (c_blk, n_blk, m_blk)]
    return pl.pallas_call(
        functools.partial(_mlstm_kernel, L=L, with_state=with_state, n_prev=n_prev),
        out_shape=out_shape,
        grid=(B // S,),
        in_specs=in_specs,
        out_specs=out_specs,
        compiler_params=pltpu.CompilerParams(
            dimension_semantics=("arbitrary",), vmem_limit_bytes=VMEM_LIMIT),
        name="mlstm",
    )(*args)


def _outmlp_kernel(xc_ref, yac_ref, ybc_ref, xl_ref, yal_ref, ybl_ref, mod_ref, wo_ref, g2_ref, w1_ref, w2_ref,
                   gf_ref, oc_ref, ol_ref, *, final, n_ctx):
    i = pl.program_id(0)
    shared = (mod_ref, wo_ref, g2_ref, w1_ref, w2_ref, gf_ref)

    @pl.when(i < n_ctx)
    def _():
        _outmlp_body(xc_ref, yac_ref, ybc_ref, *shared, oc_ref, final=final)

    @pl.when(i >= n_ctx)
    def _():
        _outmlp_body(xl_ref, yal_ref, ybl_ref, *shared, ol_ref, final=final)


def _outmlp_body(x_ref, ya_ref, yb_ref, mod_ref, wo_ref, g2_ref, w1_ref, w2_ref, gf_ref, o_ref, *, final):
    mod = mod_ref[0]
    g1 = mod[:, 2 * D_MODEL:3 * D_MODEL]
    sh2 = mod[:, 3 * D_MODEL:4 * D_MODEL]
    sc2 = mod[:, 4 * D_MODEL:5 * D_MODEL]
    g2 = mod[:, 5 * D_MODEL:6 * D_MODEL]
    na = SC_W + HY_W
    attn = _dot(ya_ref[...].astype(BF16), wo_ref[0:na, :]) + _dot(yb_ref[...].astype(BF16), wo_ref[na:, :])
    x = x_ref[...] + g1 * attn
    u = ((_rms(x) * g2_ref[...]) * (1.0 + sc2) + sh2).astype(BF16)
    tf = 1024
    acc = jnp.zeros(x.shape, F32)
    for j in range(D_FF // tf):
        hcol = _dot(u, w1_ref[:, j * tf:(j + 1) * tf])
        hcol = jnp.square(jnp.maximum(hcol, 0.0)).astype(BF16)
        acc = acc + _dot(hcol, w2_ref[j * tf:(j + 1) * tf, :])
    x = x + g2 * acc
    if final:
        x = _rms(x) * gf_ref[...]
    o_ref[...] = x


def _outmlp(ctx_ops, lat_ops, mods, w_out, g2, w1, w2, gf, *, l, Ld, final):
    tm = ROW_BLOCK
    Tc, Tl = ctx_ops[0].shape[0], lat_ops[0].shape[0]
    n_ctx, n_lat = Tc // tm, Tl // tm
    ctx, lat = _two_pass_maps(n_ctx)
    token_specs = lambda idx: [pl.BlockSpec((tm, width), lambda i: (idx(i), 0))
                               for width in (D_MODEL, SC_W + HY_W, ML_W)]
    return pl.pallas_call(
        functools.partial(_outmlp_kernel, final=final, n_ctx=n_ctx),
        out_shape=[jax.ShapeDtypeStruct((Tc, D_MODEL), F32), jax.ShapeDtypeStruct((Tl, D_MODEL), F32)],
        grid=(n_ctx + n_lat,),
        in_specs=token_specs(ctx) + token_specs(lat) + [
            _mod_spec(l, Ld, tm, n_ctx),
            _const_spec((D_MODEL, D_MODEL)), _layer_spec((1, D_MODEL), l),
            _const_spec((D_MODEL, D_FF)), _const_spec((D_FF, D_MODEL)), _const_spec((1, D_MODEL)),
        ],
        out_specs=[pl.BlockSpec((tm, D_MODEL), lambda i: (ctx(i), 0)), pl.BlockSpec((tm, D_MODEL), lambda i: (lat(i), 0))],
        compiler_params=pltpu.CompilerParams(
            dimension_semantics=("arbitrary",), vmem_limit_bytes=VMEM_LIMIT),
        name="outproj_mlp",
    )(*ctx_ops, *lat_ops, mods, w_out, g2, w1, w2, gf)


def kernel(x_prompt, x_sample, state_C, state_n, state_m, c, c_ctx, norm1_g, ada_w, ada_b, w_in, sc_conv_w, hy_conv_w, hy_w1, hy_b1, hy_w2, hy_b2, hy_w3, hy_freq, hy_skip, ml_conv_w, ml_gate_b, ml_norm_g, w_out, norm2_g, mlp_w1, mlp_w2, norm_f_g):
    B, L = x_prompt.shape[0], x_prompt.shape[1]
    Bd, Ld = x_sample.shape[0], x_sample.shape[1]

    cond = jnp.concatenate([c_ctx[None, :], c, jnp.zeros((8 - 1 - Bd, D_MODEL), F32)], axis=0)
    mods = _ada(cond, ada_w, ada_b)

    gate_b = jnp.broadcast_to(ml_gate_b[:, :, None], (DEPTH, N_GATE, LANES))
    pad_h = LANES - HY_HID
    w1p = jnp.pad(hy_w1, ((0, 0), (0, LANES - HY_EMB), (0, pad_h)))
    w2p = jnp.pad(hy_w2, ((0, 0), (0, pad_h), (0, pad_h)))
    w3p = jnp.pad(hy_w3, ((0, 0), (0, pad_h), (0, 0)))
    hy_rows = jnp.pad(jnp.stack([hy_b1, hy_b2, hy_freq], axis=1), ((0, 0), (0, 0), (0, pad_h)))[:, :, None, :]
    pos = jnp.asarray(_pos_embed_np(Ld))
    m0_all = jnp.pad(state_m, ((0, 0), (0, 0), (0, 0), (0, ML_H))).reshape(Bd, DEPTH, N_GATE, 1)
    m0_all = jnp.broadcast_to(m0_all, (Bd, DEPTH, N_GATE, LANES))
    n0_all = state_n.reshape(Bd, DEPTH, 2 * ML_H, ML_DH)
    gf = norm_f_g[None, :]
    g1 = norm1_g[:, None, :]
    g2 = norm2_g[:, None, :]
    ml_ng = ml_norm_g[:, None, :]
    filt_args = (w1p, w2p, w3p, hy_rows)
    hre_c, him_c, w_in_t = _filter_spectrum(L, *filt_args, cast=jnp.swapaxes(w_in, 1, 2))
    w_gate = w_in_t[:, N_MAIN:, :]
    hre_s, him_s = _filter_spectrum(Ld, *filt_args)
    state = (state_C, n0_all, m0_all)

    xp = x_prompt.reshape(B * L, D_MODEL)
    xs = x_sample.reshape(Bd * Ld, D_MODEL)
    prev = None
    for l in range(DEPTH):
        final = l == DEPTH - 1

        res = _inproj(xp, xs, pos if l == 0 else None, mods, g1, w_in_t, w_gate, (w_out, mlp_w1, mlp_w2), l=l, Ld=Ld)
        proj_c, g_c, proj_s, g_s = res[:4]
        if l == 0:
            xs = res[4]
        w_out_b, w1_b, w2_b = res[-3:]

        ya_c, ya_s = _schy(proj_c, proj_s, sc_conv_w, hy_conv_w, hy_skip, (hre_c, him_c), (hre_s, him_s),
                           l=l, Lc=L, Ls=Ld)

        yb_c, *states = _mlstm(proj_c, g_c, gate_b, ml_conv_w, ml_ng, None, prev, l=l, B=B, L=L)
        if not final:
            prev = [s.reshape((B, 1) + s.shape[1:]) for s in states] if l == 0 else states

        yb_s = _mlstm(proj_s, g_s, gate_b, ml_conv_w, ml_ng, state, None, l=l, B=Bd, L=Ld)

        xp, xs = _outmlp((xp, ya_c, yb_c), (xs, ya_s, yb_s), mods, w_out_b, g2, w1_b, w2_b, gf,
                         l=l, Ld=Ld, final=final)

    y_prompt = xp.reshape(B, L, D_MODEL)
    y_sample = xs.reshape(Bd, Ld, D_MODEL)
    new_c, new_n, new_m = states
    return (y_prompt, y_sample, new_c, new_n.reshape(B, DEPTH, 2, ML_H, ML_DH),
            new_m[..., 0].reshape(B, DEPTH, 2, ML_H))
```
